```python
import math
import jax
import jax.numpy as jnp
from jax import lax
import numpy as np

D_MODEL = 1024
BATCH = 32
SEQ = 256
DEPTH = 2
DEC_BATCH = 8
DEC_SEQ = 4096
PAST_LEN = 512

GRID_W = 64
HEAD_DIM = 64
D_MIX = D_MODEL
A_HEADS = 8
A_KV_HEADS = 2
A_GROUP = A_HEADS // A_KV_HEADS
A_WIDTH = A_HEADS * HEAD_DIM
WINDOW = 128
BLK = 128
ROPE_BASE = 10000.0
HY_CH = 256
HY_ORDER = 2
HY_BANDS = 16
HY_EMB = 1 + 2 * HY_BANDS
HY_HID = 64
C_HEADS = 4
C_DK = 64
C_DV = 64
C_WIDTH = C_HEADS * C_DV
GLA_RANK = 16
GLA_TAU = 16.0
GLA_CHUNK = 64
D_FF = 2816
N_EXPERTS = 8
TOP_K = 2
D_FF_EXPERT = 3584
N_DENSE = (DEPTH + 1) // 2
N_MOE = DEPTH // 2

PROJ_SIZES = (A_WIDTH, A_KV_HEADS * HEAD_DIM, A_KV_HEADS * HEAD_DIM, 3 * HY_CH,
              C_HEADS * C_DK, C_HEADS * C_DK, C_WIDTH, C_WIDTH, 2 * GLA_RANK)
PROJ_SPLITS = tuple(int(s) for s in np.cumsum(PROJ_SIZES)[:-1])
D_PROJ = int(sum(PROJ_SIZES))

F32 = jnp.float32
ATT_SCALE = HEAD_DIM ** -0.5
NEG_INF = -1e30
EPS = 1e-6

kernel_name = 'hybrid_flow_parallel_heads_step'


def rms_norm(x, g):
    xf = x.astype(F32)
    y = xf * lax.rsqrt(jnp.mean(xf * xf, axis=-1, keepdims=True) + EPS)
    return (y * g.astype(F32)).astype(x.dtype)


def ada_params(cvec, w_ada, b_ada):
    m = jax.nn.silu(cvec) @ w_ada + b_ada
    return [t[:, None, :] for t in jnp.split(m, 6, axis=-1)]


def adaln(x, g, shift, scale):
    return rms_norm(x, g) * (1 + scale) + shift


def axial_rope(L):
    rows = L // GRID_W
    r = jnp.repeat(jnp.arange(rows, dtype=F32), GRID_W)
    col = jnp.tile(jnp.arange(GRID_W, dtype=F32), rows)
    n = HEAD_DIM // 4
    freqs = ROPE_BASE ** (-jnp.arange(n, dtype=F32) / n)
    ang = jnp.concatenate([r[:, None] * freqs, col[:, None] * freqs], axis=-1)
    return jnp.cos(ang), jnp.sin(ang)


def rope_part(x, cos, sin):
    x1, x2 = jnp.split(x, 2, axis=-1)
    return jnp.concatenate([x1 * cos - x2 * sin, x2 * cos + x1 * sin], axis=-1)


def apply_axial_rope(x, cos, sin):
    n = HEAD_DIM // 4
    c = cos[None, :, None, :]
    s = sin[None, :, None, :]
    xf = x.astype(F32)
    xr = rope_part(xf[..., :2 * n], c[..., :n], s[..., :n])
    xc = rope_part(xf[..., 2 * n:], c[..., n:], s[..., n:])
    return jnp.concatenate([xr, xc], axis=-1).astype(x.dtype)


def sink_softmax_combine(scores, values, sink):
    B, _, _, Q, _ = scores[0].shape
    sink_col = jnp.broadcast_to(sink.astype(F32).reshape(1, A_KV_HEADS, A_GROUP, 1, 1),
                                (B, A_KV_HEADS, A_GROUP, Q, 1))
    p = jax.nn.softmax(jnp.concatenate([sink_col] + scores, axis=-1), axis=-1)
    outs = []
    start = 1
    for s, v in zip(scores, values):
        n = s.shape[-1]
        outs.append(jnp.einsum('bkgqs,bskd->bqkgd', p[..., start:start + n], v.astype(F32)))
        start += n
    return sum(outs[1:], outs[0])


def context_attention(q, k, v, sink):
    B, L = q.shape[:2]
    nb = L // BLK
    qb = q.reshape(B, nb, BLK, A_KV_HEADS, A_GROUP, HEAD_DIM).swapaxes(0, 1)
    kf = k.astype(F32)

    def block(qi):
        s = jnp.einsum('bqkgd,bskd->bkgqs', qi.astype(F32), kf) * ATT_SCALE
        return sink_softmax_combine([s], [v], sink)

    o = lax.map(block, qb)
    return o.swapaxes(0, 1).reshape(B, L, A_WIDTH)


def latent_attention(q, k, v, kc, vc, sink):
    B, L = q.shape[:2]
    nb = L // BLK
    qb = q.reshape(B, nb, BLK, A_KV_HEADS, A_GROUP, HEAD_DIM).swapaxes(0, 1)
    pad = ((0, 0), (BLK, BLK), (0, 0), (0, 0))
    kp = jnp.pad(k.astype(F32), pad)
    vp = jnp.pad(v, pad)
    kcf = kc.astype(F32)
    offs_q = jnp.arange(BLK)
    offs_k = jnp.arange(3 * BLK) - BLK

    def block(args):
        qi, i = args
        start = i * BLK
        kl = lax.dynamic_slice_in_dim(kp, start, 3 * BLK, axis=1)
        vl = lax.dynamic_slice_in_dim(vp, start, 3 * BLK, axis=1)
        qf = qi.astype(F32)
        s_loc = jnp.einsum('bqkgd,bskd->bkgqs', qf, kl) * ATT_SCALE
        qpos = start + offs_q
        kpos = start + offs_k
        valid = ((kpos[None, :] >= 0) & (kpos[None, :] < L)
                 & (jnp.abs(qpos[:, None] - kpos[None, :]) <= WINDOW))
        s_loc = jnp.where(valid, s_loc, NEG_INF)
        s_ctx = jnp.einsum('bqkgd,bskd->bkgqs', qf, kcf) * ATT_SCALE
        return sink_softmax_combine([s_ctx, s_loc], [vc, vl], sink)

    o = lax.map(block, (qb, jnp.arange(nb)))
    return o.swapaxes(0, 1).reshape(B, L, A_WIDTH)


def short_conv3(x, w, b):
    xp = jnp.pad(x, ((0, 0), (1, 1), (0, 0)))
    return xp[:, :-2] * w[0] + xp[:, 1:-1] * w[1] + xp[:, 2:] * w[2] + b


def hyena_filters(L, lp):
    t = jnp.arange(L, dtype=F32)
    t_norm = t / max(L - 1, 1)
    w = (2.0 * math.pi / L) * t
    f = jnp.linspace(1e-4, HY_BANDS - 1, HY_BANDS, dtype=F32)
    fw = w[:, None] * f[None, :]
    feat = jnp.concatenate([t_norm[:, None], jnp.cos(fw), -jnp.sin(fw)], axis=-1)
    z = jnp.sin(lp['hy_freq1'] * (feat @ lp['hy_w1'] + lp['hy_b1']))
    z = jnp.sin(lp['hy_freq2'] * (z @ lp['hy_w2'] + lp['hy_b2']))
    hf = (z @ lp['hy_w3']).astype(F32).reshape(L, 2, HY_ORDER, HY_CH)
    hf = hf * jnp.exp(-t_norm[:, None, None, None] * jnp.abs(lp['hy_decay'].astype(F32)))
    hf = hf / (jnp.sum(jnp.abs(hf), axis=(0, 1), keepdims=True) + EPS)
    h_fwd, h_bwd = hf[:, 0], hf[:, 1]
    circ = jnp.concatenate([h_fwd, jnp.zeros((1, HY_ORDER, HY_CH), F32), h_bwd[:0:-1]], axis=0)
    return jnp.fft.rfft(circ, axis=0)


def hyena_mixer(u, lp):
    B, L, _ = u.shape
    u = short_conv3(u, lp['hy_conv_w'], lp['hy_conv_b']).astype(F32)
    v, x1, x2 = jnp.split(u, 3, axis=-1)
    filt = hyena_filters(L, lp)
    d = lp['hy_d'].astype(F32)
    z = v
    for o, gate in enumerate((x1, x2)):
        zf = jnp.fft.rfft(z, n=2 * L, axis=1)
        conv = jnp.fft.irfft(zf * filt[None, :, o], n=2 * L, axis=1)[:, :L]
        z = gate * (conv + d[o] * z)
    return z


def gla_chunked(q, k, v, log_a, s0):
    B, L, H, _ = q.shape
    DV = v.shape[-1]
    n = L // GLA_CHUNK

    def blocks(t):
        return t.astype(F32).reshape(B, n, GLA_CHUNK, H, t.shape[-1]).transpose(1, 0, 3, 2, 4)

    qc, kc, vc, la = blocks(q), blocks(k), blocks(v), blocks(log_a)
    b = jnp.cumsum(la, axis=3)
    b_last = b[:, :, :, -1:, :]
    qg = qc * jnp.exp(b)
    kg = kc * jnp.exp(-b)
    kd = kc * jnp.exp(b_last - b)
    causal = jnp.tril(jnp.ones((GLA_CHUNK, GLA_CHUNK), dtype=bool))
    a = jnp.where(causal, jnp.einsum('nbhtd,nbhsd->nbhts', qg, kg), 0.0)
    o_intra = jnp.einsum('nbhts,nbhsv->nbhtv', a, vc)
    u = jnp.einsum('nbhsd,nbhsv->nbhdv', kd, vc)
    decay = jnp.exp(b_last[:, :, :, 0, :])

    def step(s, inp):
        dec, uu = inp
        return dec[..., None] * s + uu, s

    s_final, s_start = lax.scan(step, s0.astype(F32), (decay, u))
    o = o_intra + jnp.einsum('nbhtd,nbhdv->nbhtv', qg, s_start)
    return o.transpose(1, 0, 3, 2, 4).reshape(B, L, H, DV), s_final


def gla_mixer(cq, ck, cv, cg, cr, s_fwd, s_bwd, lp):
    B, L, _ = cq.shape
    q = cq.reshape(B, L, C_HEADS, C_DK) * (C_DK ** -0.5)
    k = ck.reshape(B, L, C_HEADS, C_DK)
    v = cv.reshape(B, L, C_HEADS, C_DV)
    r_f, r_b = jnp.split(cr, 2, axis=-1)
    gw, gb = lp['gla_gate_w'], lp['gla_gate_b']
    la_f = (jax.nn.log_sigmoid((r_f @ gw[0] + gb[0]).astype(F32)) / GLA_TAU).reshape(B, L, C_HEADS, C_DK)
    la_b = (jax.nn.log_sigmoid((r_b @ gw[1] + gb[1]).astype(F32)) / GLA_TAU).reshape(B, L, C_HEADS, C_DK)
    o_f, st_f = gla_chunked(q, k, v, la_f, s_fwd)
    o_b, st_b = gla_chunked(jnp.flip(q, axis=1), jnp.flip(k, axis=1), jnp.flip(v, axis=1),
                            jnp.flip(la_b, axis=1), s_bwd)
    o = rms_norm(o_f + jnp.flip(o_b, axis=1), lp['gla_norm_g']).reshape(B, L, C_WIDTH)
    return o * jax.nn.silu(cg.astype(F32)), jnp.stack([st_f, st_b], axis=1)


def project(h, lp):
    return jnp.split(h @ lp['w_in'], PROJ_SPLITS, axis=-1)


def attn_heads(aq, ak, av, lp):
    B, L, _ = aq.shape
    q = rms_norm(aq.reshape(B, L, A_HEADS, HEAD_DIM), lp['q_norm_g'])
    k = rms_norm(ak.reshape(B, L, A_KV_HEADS, HEAD_DIM), lp['k_norm_g'])
    v = av.reshape(B, L, A_KV_HEADS, HEAD_DIM)
    return q, k, v


def merge_heads(a_out, h_out, g_out, lp, dtype):
    cat = jnp.concatenate([a_out.astype(F32), h_out.astype(F32), g_out.astype(F32)], axis=-1)
    return cat.astype(dtype) @ lp['w_out']


def mixer_context(h, lp):
    aq, ak, av, hy, cq, ck, cv, cg, cr = project(h, lp)
    q, k, v = attn_heads(aq, ak, av, lp)
    a_out = context_attention(q, k, v, lp['attn_sink'])
    h_out = hyena_mixer(hy, lp)
    zeros = jnp.zeros((h.shape[0], C_HEADS, C_DK, C_DV), F32)
    g_out, st = gla_mixer(cq, ck, cv, cg, cr, zeros, zeros, lp)
    return merge_heads(a_out, h_out, g_out, lp, h.dtype), k, v, st


def mixer_latent(h, kc, vc, st, lp):
    aq, ak, av, hy, cq, ck, cv, cg, cr = project(h, lp)
    q, k, v = attn_heads(aq, ak, av, lp)
    cos, sin = axial_rope(h.shape[1])
    q = apply_axial_rope(q, cos, sin)
    k = apply_axial_rope(k, cos, sin)
    a_out = latent_attention(q, k, v, kc, vc, lp['attn_sink'])
    h_out = hyena_mixer(hy, lp)
    g_out, _ = gla_mixer(cq, ck, cv, cg, cr, st[:, 0], st[:, 1], lp)
    return merge_heads(a_out, h_out, g_out, lp, h.dtype)


def swiglu(h, w1, w3, w2):
    return (jax.nn.silu(h @ w1) * (h @ w3)) @ w2


def moe_swiglu(h, w_router, w1, w3, w2):
    B, L, D = h.shape
    t = h.reshape(B * L, D)
    logits = (t @ w_router).astype(F32)
    top_v, top_i = lax.top_k(logits, TOP_K)
    top_w = jax.nn.softmax(top_v, axis=-1)
    gates = jnp.sum(jax.nn.one_hot(top_i, N_EXPERTS, dtype=F32) * top_w[..., None], axis=1)
    outs = [gates[:, e:e + 1] * swiglu(t, w1[e], w3[e], w2[e]).astype(F32) for e in range(N_EXPERTS)]
    return sum(outs[1:], outs[0]).reshape(B, L, D).astype(h.dtype)


def channel_mixer(h, l, ffn_w1, ffn_w3, ffn_w2, moe_router, moe_w1, moe_w3, moe_w2):
    j = l // 2
    if l % 2 == 0:
        return swiglu(h, ffn_w1[j], ffn_w3[j], ffn_w2[j])
    return moe_swiglu(h, moe_router[j], moe_w1[j], moe_w3[j], moe_w2[j])


def setup_inputs(seed: int = 0) -> dict:
    key = jax.random.key(seed)
    ks = iter(jax.random.split(key, 64))

    def nrm(shape, scale=1.0):
        return jax.random.normal(next(ks), shape, F32) * scale

    def gain(shape):
        return 1.0 + 0.02 * nrm(shape)

    D = D_MODEL
    return {
        'x_prompt': nrm((BATCH, SEQ, D)),
        'x_sample': nrm((DEC_BATCH, DEC_SEQ, D)),
        'cache_k': nrm((DEC_BATCH, DEPTH, PAST_LEN, A_KV_HEADS, HEAD_DIM)),
        'cache_v': nrm((DEC_BATCH, DEPTH, PAST_LEN, A_KV_HEADS, HEAD_DIM)),
        'state_gla': nrm((DEC_BATCH, DEPTH, 2, C_HEADS, C_DK, C_DV), 0.5),
        'c': nrm((DEC_BATCH, D)),
        'c_ctx': nrm((D,)),
        'norm1_g': gain((DEPTH, D)),
        'norm2_g': gain((DEPTH, D)),
        'w_ada': nrm((DEPTH, D, 6 * D), 0.5 * D ** -0.5),
        'b_ada': nrm((DEPTH, 6 * D), 0.02),
        'w_in': nrm((DEPTH, D, D_PROJ), D ** -0.5),
        'w_out': nrm((DEPTH, D_MIX, D), D_MIX ** -0.5),
        'q_norm_g': gain((DEPTH, HEAD_DIM)),
        'k_norm_g': gain((DEPTH, HEAD_DIM)),
        'attn_sink': nrm((DEPTH, A_HEADS)),
        'hy_conv_w': nrm((DEPTH, 3, 3 * HY_CH), 3 ** -0.5),
        'hy_conv_b': nrm((DEPTH, 3 * HY_CH), 0.02),
        'hy_w1': nrm((DEPTH, HY_EMB, HY_HID), HY_EMB ** -0.5),
        'hy_b1': nrm((DEPTH, HY_HID), 0.02),
        'hy_freq1': gain((DEPTH, HY_HID)),
        'hy_w2': nrm((DEPTH, HY_HID, HY_HID), HY_HID ** -0.5),
        'hy_b2': nrm((DEPTH, HY_HID), 0.02),
        'hy_freq2': gain((DEPTH, HY_HID)),
        'hy_w3': nrm((DEPTH, HY_HID, 2 * HY_ORDER * HY_CH), HY_HID ** -0.5),
        'hy_decay': jax.random.uniform(next(ks), (DEPTH, 2, HY_ORDER, HY_CH), F32, 1.0, 15.0),
        'hy_d': nrm((DEPTH, HY_ORDER, HY_CH), 0.5),
        'gla_gate_w': nrm((DEPTH, 2, GLA_RANK, C_HEADS * C_DK), GLA_RANK ** -0.5),
        'gla_gate_b': nrm((DEPTH, 2, C_HEADS * C_DK), 0.02),
        'gla_norm_g': gain((DEPTH, C_DV)),
        'ffn_w1': nrm((N_DENSE, D, D_FF), D ** -0.5),
        'ffn_w3': nrm((N_DENSE, D, D_FF), D ** -0.5),
        'ffn_w2': nrm((N_DENSE, D_FF, D), D_FF ** -0.5),
        'moe_router': nrm((N_MOE, D, N_EXPERTS), D ** -0.5),
        'moe_w1': nrm((N_MOE, N_EXPERTS, D, D_FF_EXPERT), D ** -0.5),
        'moe_w3': nrm((N_MOE, N_EXPERTS, D, D_FF_EXPERT), D ** -0.5),
        'moe_w2': nrm((N_MOE, N_EXPERTS, D_FF_EXPERT, D), D_FF_EXPERT ** -0.5),
    }


def reference(x_prompt, x_sample, cache_k, cache_v, state_gla, c, c_ctx, norm1_g, norm2_g, w_ada, b_ada,
              w_in, w_out, q_norm_g, k_norm_g, attn_sink, hy_conv_w, hy_conv_b, hy_w1, hy_b1, hy_freq1,
              hy_w2, hy_b2, hy_freq2, hy_w3, hy_decay, hy_d, gla_gate_w, gla_gate_b, gla_norm_g,
              ffn_w1, ffn_w3, ffn_w2, moe_router, moe_w1, moe_w3, moe_w2):
    xp = x_prompt
    xs = x_sample
    ks_list, vs_list, st_list = [], [], []
    for l in range(DEPTH):
        lp = {
            'w_in': w_in[l], 'w_out': w_out[l], 'q_norm_g': q_norm_g[l], 'k_norm_g': k_norm_g[l],
            'attn_sink': attn_sink[l], 'hy_conv_w': hy_conv_w[l], 'hy_conv_b': hy_conv_b[l],
            'hy_w1': hy_w1[l], 'hy_b1': hy_b1[l], 'hy_freq1': hy_freq1[l], 'hy_w2': hy_w2[l],
            'hy_b2': hy_b2[l], 'hy_freq2': hy_freq2[l], 'hy_w3': hy_w3[l], 'hy_decay': hy_decay[l],
            'hy_d': hy_d[l], 'gla_gate_w': gla_gate_w[l], 'gla_gate_b': gla_gate_b[l],
            'gla_norm_g': gla_norm_g[l],
        }
        sh1, sc1, g1, sh2, sc2, g2 = ada_params(c_ctx[None, :], w_ada[l], b_ada[l])
        mix, k_ctx, v_ctx, st_ctx = mixer_context(adaln(xp, norm1_g[l], sh1, sc1), lp)
        xp = xp + (g1 * mix).astype(xp.dtype)
        ff = channel_mixer(adaln(xp, norm2_g[l], sh2, sc2), l, ffn_w1, ffn_w3, ffn_w2,
                           moe_router, moe_w1, moe_w3, moe_w2)
        xp = xp + (g2 * ff).astype(xp.dtype)
        ks_list.append(k_ctx)
        vs_list.append(v_ctx)
        st_list.append(st_ctx)
        sh1, sc1, g1, sh2, sc2, g2 = ada_params(c, w_ada[l], b_ada[l])
        mix = mixer_latent(adaln(xs, norm1_g[l], sh1, sc1), cache_k[:, l], cache_v[:, l], state_gla[:, l], lp)
        xs = xs + (g1 * mix).astype(xs.dtype)
        ff = channel_mixer(adaln(xs, norm2_g[l], sh2, sc2), l, ffn_w1, ffn_w3, ffn_w2,
                           moe_router, moe_w1, moe_w3, moe_w2)
        xs = xs + (g2 * ff).astype(xs.dtype)
    y_prompt = xp
    y_sample = xs
    new_cache_k = jnp.stack(ks_list, axis=1)
    new_cache_v = jnp.stack(vs_list, axis=1)
    new_state_gla = jnp.stack(st_list, axis=1)
    return (y_prompt, y_sample, new_cache_k, new_cache_v, new_state_gla)
```

```python
import math
import functools
import jax
import jax.numpy as jnp
from jax import lax
import numpy as np
from jax.experimental import pallas as pl
from jax.experimental.pallas import tpu as pltpu

D_MODEL = 1024
BATCH = 32
SEQ = 256
DEPTH = 2
DEC_BATCH = 8
DEC_SEQ = 4096
PAST_LEN = 512

GRID_W = 64
HEAD_DIM = 64
D_MIX = D_MODEL
A_HEADS = 8
A_KV_HEADS = 2
A_GROUP = A_HEADS // A_KV_HEADS
A_WIDTH = A_HEADS * HEAD_DIM
WINDOW = 128
BLK = 128
ROPE_BASE = 10000.0
HY_CH = 256
HY_ORDER = 2
HY_BANDS = 16
HY_EMB = 1 + 2 * HY_BANDS
HY_HID = 64
C_HEADS = 4
C_DK = 64
C_DV = 64
C_WIDTH = C_HEADS * C_DV
GLA_RANK = 16
GLA_TAU = 16.0
GLA_CHUNK = 64
D_FF = 2816
N_EXPERTS = 8
TOP_K = 2
D_FF_EXPERT = 3584

PROJ_SIZES = (A_WIDTH, A_KV_HEADS * HEAD_DIM, A_KV_HEADS * HEAD_DIM, 3 * HY_CH,
              C_HEADS * C_DK, C_HEADS * C_DK, C_WIDTH, C_WIDTH, 2 * GLA_RANK)
PROJ_SPLITS = tuple(int(s) for s in np.cumsum(PROJ_SIZES)[:-1])
D_PROJ = int(sum(PROJ_SIZES))

F32 = jnp.float32
BF16 = jnp.bfloat16
ATT_SCALE = HEAD_DIM ** -0.5
NEG_INF = -1e30
EPS = 1e-6

LANES = 128
SUBLANES = 8
D_PROJ_PAD = -(-D_PROJ // LANES) * LANES
VMEM_LIMIT_BYTES = 56 * 1024 * 1024

MOD_ROWS = SUBLANES
ROW_SHIFT1, ROW_SCALE1, ROW_GATE1, ROW_SHIFT2, ROW_SCALE2, ROW_GATE2 = range(6)

TM_PROJ = 512
TM_MOE = 1024
TF_MOE = 512
TM_COMB = 512


def _cparams(*sem):
    return pltpu.CompilerParams(dimension_semantics=sem, vmem_limit_bytes=VMEM_LIMIT_BYTES)


def _adaln_rows(x, g, shift, scale):
    ms = jnp.mean(x * x, axis=-1, keepdims=True)
    return (x * lax.rsqrt(ms + EPS) * g) * (1.0 + scale) + shift


def _silu(a):
    return a * jax.nn.sigmoid(a)


def _inproj_body(x_ref, mod_ref, g_ref, w_ref, o_ref):
    h = _adaln_rows(x_ref[...], g_ref[...], mod_ref[0, ROW_SHIFT1:ROW_SHIFT1 + 1, :],
                    mod_ref[0, ROW_SCALE1:ROW_SCALE1 + 1, :])
    o_ref[...] = jnp.dot(h.astype(BF16), w_ref[...], preferred_element_type=F32)


def in_proj(x, mods, g, w, seg_len, tm=TM_PROJ):
    T, D = x.shape
    N = w.shape[1]
    return pl.pallas_call(
        _inproj_body,
        grid=(T // tm,),
        in_specs=[
            pl.BlockSpec((tm, D), lambda i: (i, 0)),
            pl.BlockSpec((1, MOD_ROWS, D), lambda i: (i * tm // seg_len, 0, 0)),
            pl.BlockSpec((1, D), lambda i: (0, 0)),
            pl.BlockSpec((D, N), lambda i: (0, 0)),
        ],
        out_specs=pl.BlockSpec((tm, N), lambda i: (i, 0)),
        out_shape=jax.ShapeDtypeStruct((T, N), F32),
        compiler_params=_cparams("parallel"),
        name="in_proj",
    )(x, mods, g, w)


def _outproj_body(x_ref, cat_ref, mod_ref, w_ref, o_ref):
    mix = jnp.dot(cat_ref[...].astype(BF16), w_ref[...], preferred_element_type=F32)
    o_ref[...] = x_ref[...] + mod_ref[0, ROW_GATE1:ROW_GATE1 + 1, :] * mix


def out_proj(x, cat, mods, w, seg_len, tm=TM_PROJ):
    T, D = x.shape
    return pl.pallas_call(
        _outproj_body,
        grid=(T // tm,),
        in_specs=[
            pl.BlockSpec((tm, D), lambda i: (i, 0)),
            pl.BlockSpec((tm, D), lambda i: (i, 0)),
            pl.BlockSpec((1, MOD_ROWS, D), lambda i: (i * tm // seg_len, 0, 0)),
            pl.BlockSpec((D, D), lambda i: (0, 0)),
        ],
        out_specs=pl.BlockSpec((tm, D), lambda i: (i, 0)),
        out_shape=jax.ShapeDtypeStruct((T, D), F32),
        compiler_params=_cparams("parallel"),
        name="out_proj",
    )(x, cat, mods, w)


def _ffn_body(x_ref, mod_ref, g_ref, w1_ref, w3_ref, w2_ref, o_ref):
    x = x_ref[...]
    h = _adaln_rows(x, g_ref[...], mod_ref[0, ROW_SHIFT2:ROW_SHIFT2 + 1, :],
                    mod_ref[0, ROW_SCALE2:ROW_SCALE2 + 1, :]).astype(BF16)
    a = jnp.dot(h, w1_ref[...], preferred_element_type=F32)
    b = jnp.dot(h, w3_ref[...], preferred_element_type=F32)
    act = (_silu(a) * b).astype(BF16)
    ff = jnp.dot(act, w2_ref[...], preferred_element_type=F32)
    o_ref[...] = x + mod_ref[0, ROW_GATE2:ROW_GATE2 + 1, :] * ff


def ffn_dense(x, mods, g, w1, w3, w2, seg_len, tm=TM_PROJ):
    T, D = x.shape
    F = w1.shape[1]
    resident = functools.partial(pl.BlockSpec, pipeline_mode=pl.Buffered(1))
    return pl.pallas_call(
        _ffn_body,
        grid=(T // tm,),
        in_specs=[
            pl.BlockSpec((tm, D), lambda i: (i, 0)),
            pl.BlockSpec((1, MOD_ROWS, D), lambda i: (i * tm // seg_len, 0, 0)),
            pl.BlockSpec((1, D), lambda i: (0, 0)),
            resident((D, F), lambda i: (0, 0)),
            resident((D, F), lambda i: (0, 0)),
            resident((F, D), lambda i: (0, 0)),
        ],
        out_specs=pl.BlockSpec((tm, D), lambda i: (i, 0)),
        out_shape=jax.ShapeDtypeStruct((T, D), F32),
        compiler_params=_cparams("parallel"),
        name="ffn_dense",
    )(x, mods, g, w1, w3, w2)


def _router_body(x_ref, mod_ref, g_ref, wr_ref, h_ref, idx_ref, gw_ref):
    h = _adaln_rows(x_ref[...], g_ref[...], mod_ref[0, ROW_SHIFT2:ROW_SHIFT2 + 1, :],
                    mod_ref[0, ROW_SCALE2:ROW_SCALE2 + 1, :])
    h_ref[...] = h
    logits = lax.dot_general(wr_ref[...], h, (((1,), (1,)), ((), ())),
                             precision=lax.Precision.HIGHEST, preferred_element_type=F32)
    eidx = lax.broadcasted_iota(jnp.int32, logits.shape, 0)
    m1 = jnp.max(logits, axis=0, keepdims=True)
    i1 = jnp.min(jnp.where(logits == m1, eidx, N_EXPERTS), axis=0, keepdims=True)
    rest = jnp.where(eidx == i1, -jnp.inf, logits)
    m2 = jnp.max(rest, axis=0, keepdims=True)
    i2 = jnp.min(jnp.where(rest == m2, eidx, N_EXPERTS), axis=0, keepdims=True)
    e2 = jnp.exp(m2 - m1)
    den = 1.0 + e2
    row = lax.broadcasted_iota(jnp.int32, logits.shape, 0)
    idx_ref[...] = jnp.where(row == 0, i1, jnp.where(row == 1, i2, 0))
    gw_ref[...] = jnp.where(row == 0, 1.0 / den, jnp.where(row == 1, e2 / den, 0.0))


def moe_router(x, mods, g, wr_t, seg_len, tm=TM_PROJ):
    T, D = x.shape
    return pl.pallas_call(
        _router_body,
        grid=(T // tm,),
        in_specs=[
            pl.BlockSpec((tm, D), lambda i: (i, 0)),
            pl.BlockSpec((1, MOD_ROWS, D), lambda i: (i * tm // seg_len, 0, 0)),
            pl.BlockSpec((1, D), lambda i: (0, 0)),
            pl.BlockSpec((N_EXPERTS, D), lambda i: (0, 0)),
        ],
        out_specs=[
            pl.BlockSpec((tm, D), lambda i: (i, 0)),
            pl.BlockSpec((N_EXPERTS, tm), lambda i: (0, i)),
            pl.BlockSpec((N_EXPERTS, tm), lambda i: (0, i)),
        ],
        out_shape=[
            jax.ShapeDtypeStruct((T, D), F32),
            jax.ShapeDtypeStruct((N_EXPERTS, T), jnp.int32),
            jax.ShapeDtypeStruct((N_EXPERTS, T), F32),
        ],
        compiler_params=_cparams("parallel"),
        name="moe_router",
    )(x, mods, g, wr_t)


def _row_copy(src_hbm, row, dst, r, sem):
    return pltpu.make_async_copy(src_hbm.at[pl.ds(row, 1)], dst.at[pl.ds(r, 1)], sem)


def _experts_body(te_ref, nv_ref, src_ref, gate_ref, h_hbm, w1_ref, w3_ref, w2_ref, y_ref,
                  hrows, hb, acc, sem, *, tm, nf):
    i = pl.program_id(0)
    f = pl.program_id(1)
    valid = i < nv_ref[0]

    @pl.when(jnp.logical_and(valid, f == 0))
    def _gather():
        def issue(r, carry):
            _row_copy(h_hbm, src_ref[0, 0, r], hrows, r, sem).start()
            return carry
        lax.fori_loop(0, tm, issue, 0)

        def wait(r, carry):
            _row_copy(h_hbm, 0, hrows, r, sem).wait()
            return carry
        lax.fori_loop(0, tm, wait, 0)
        hb[...] = hrows[...].astype(BF16)
        acc[...] = jnp.zeros_like(acc)

    @pl.when(valid)
    def _compute():
        h = hb[...]
        a = jnp.dot(h, w1_ref[0].astype(BF16), preferred_element_type=F32)
        b = jnp.dot(h, w3_ref[0].astype(BF16), preferred_element_type=F32)
        act = (_silu(a) * b).astype(BF16)
        acc[...] += jnp.dot(act, w2_ref[0].astype(BF16), preferred_element_type=F32)

    @pl.when(jnp.logical_and(valid, f == nf - 1))
    def _emit():
        y_ref[...] = acc[...] * gate_ref[...]

    @pl.when(jnp.logical_and(jnp.logical_not(valid), f == nf - 1))
    def _emit_unused():
        y_ref[...] = jnp.zeros_like(y_ref)


def moe_experts(h, tile_expert, n_valid, src_rows, gate_rows, w1, w3, w2, tm=TM_MOE, tf=TF_MOE):
    T, D = h.shape
    F = w1.shape[2]
    n_tiles = src_rows.shape[0]
    nf = F // tf

    def wcol(i, f, te, nv):
        return (te[i], 0, jnp.where(i < nv[0], f, nf - 1))

    def wrow(i, f, te, nv):
        return (te[i], jnp.where(i < nv[0], f, nf - 1), 0)

    grid_spec = pltpu.PrefetchScalarGridSpec(
        num_scalar_prefetch=2,
        grid=(n_tiles, nf),
        in_specs=[
            pl.BlockSpec((1, 1, tm), lambda i, f, te, nv: (i, 0, 0), memory_space=pltpu.SMEM),
            pl.BlockSpec((tm, 1), lambda i, f, te, nv: (i, 0)),
            pl.BlockSpec(memory_space=pl.ANY),
            pl.BlockSpec((1, D, tf), wcol),
            pl.BlockSpec((1, D, tf), wcol),
            pl.BlockSpec((1, tf, D), wrow),
        ],
        out_specs=pl.BlockSpec((tm, D), lambda i, f, te, nv: (i, 0)),
        scratch_shapes=[
            pltpu.VMEM((tm, D), F32),
            pltpu.VMEM((tm, D), BF16),
            pltpu.VMEM((tm, D), F32),
            pltpu.SemaphoreType.DMA(()),
        ],
    )
    return pl.pallas_call(
        functools.partial(_experts_body, tm=tm, nf=nf),
        grid_spec=grid_spec,
        out_shape=jax.ShapeDtypeStruct((n_tiles * tm, D), F32),
        compiler_params=_cparams("arbitrary", "arbitrary"),
        name="moe_experts",
    )(tile_expert, n_valid, src_rows, gate_rows, h, w1, w3, w2)


def _combine_body(pos_ref, x_ref, mod_ref, y_hbm, o_ref, buf0, buf1, sem, *, tm):
    def issue(r, carry):
        _row_copy(y_hbm, pos_ref[0, 0, r], buf0, r, sem).start()
        _row_copy(y_hbm, pos_ref[0, 0, tm + r], buf1, r, sem).start()
        return carry
    lax.fori_loop(0, tm, issue, 0)

    def wait(r, carry):
        _row_copy(y_hbm, 0, buf0, r, sem).wait()
        _row_copy(y_hbm, 0, buf1, r, sem).wait()
        return carry
    lax.fori_loop(0, tm, wait, 0)
    o_ref[...] = x_ref[...] + mod_ref[0, ROW_GATE2:ROW_GATE2 + 1, :] * (buf0[...] + buf1[...])


def moe_combine(x, mods, y, pos_tiles, seg_len, tm=TM_COMB):
    T, D = x.shape
    return pl.pallas_call(
        functools.partial(_combine_body, tm=tm),
        grid=(T // tm,),
        in_specs=[
            pl.BlockSpec((1, 1, 2 * tm), lambda i: (i, 0, 0), memory_space=pltpu.SMEM),
            pl.BlockSpec((tm, D), lambda i: (i, 0)),
            pl.BlockSpec((1, MOD_ROWS, D), lambda i: (i * tm // seg_len, 0, 0)),
            pl.BlockSpec(memory_space=pl.ANY),
        ],
        out_specs=pl.BlockSpec((tm, D), lambda i: (i, 0)),
        out_shape=jax.ShapeDtypeStruct((T, D), F32),
        scratch_shapes=[
            pltpu.VMEM((tm, D), F32),
            pltpu.VMEM((tm, D), F32),
            pltpu.SemaphoreType.DMA(()),
        ],
        compiler_params=_cparams("arbitrary"),
        name="moe_combine",
    )(pos_tiles, x, mods, y)


def moe_dispatch_plan(idx, gw, tm=TM_MOE, tmc=TM_COMB):
    T = idx.shape[1]
    n_slots = TOP_K * T
    n_tiles = n_slots // tm + N_EXPERTS
    n_rows = n_tiles * tm
    e_flat = idx[:TOP_K].reshape(n_slots)
    g_flat = gw[:TOP_K].reshape(n_slots)
    onehot = (e_flat[:, None] == jnp.arange(N_EXPERTS, dtype=jnp.int32)[None, :]).astype(jnp.int32)
    csum = jnp.cumsum(onehot, axis=0)
    counts = csum[-1]
    rank = jnp.sum((csum - onehot) * onehot, axis=1)
    padded = (counts + tm - 1) // tm * tm
    ends = jnp.cumsum(padded)
    offs = ends - padded
    pos = offs[e_flat] + rank
    order = jnp.argsort(e_flat, stable=True).astype(jnp.int32)
    cstart = jnp.cumsum(counts) - counts
    rows = jnp.arange(n_rows, dtype=jnp.int32)
    row_e = jnp.minimum(jnp.searchsorted(ends, rows, side='right'), N_EXPERTS - 1).astype(jnp.int32)
    j = rows - offs[row_e]
    live = j < counts[row_e]
    slot = order[jnp.clip(cstart[row_e] + j, 0, n_slots - 1)]
    src_rows = jnp.where(live, slot % T, 0).astype(jnp.int32).reshape(n_tiles, 1, tm)
    gate_rows = jnp.where(live, g_flat[slot], 0.0).reshape(n_rows, 1)
    tile_start = jnp.arange(n_tiles, dtype=jnp.int32) * tm
    n_valid = (ends[-1] // tm).astype(jnp.int32).reshape(1)
    tile_expert = jnp.minimum(jnp.searchsorted(ends, tile_start, side='right'), N_EXPERTS - 1)
    last_e = tile_expert[jnp.maximum(n_valid[0] - 1, 0)]
    tile_expert = jnp.where(tile_start < ends[-1], tile_expert, last_e).astype(jnp.int32)
    pos2 = pos.reshape(TOP_K, T // tmc, 1, tmc)
    pos_tiles = jnp.concatenate([pos2[0], pos2[1]], axis=-1).astype(jnp.int32)
    return tile_expert, n_valid, src_rows, gate_rows, pos_tiles


def ffn_moe(x, mods, g, wr_t, w1, w3, w2, seg_len, tm=TM_PROJ, tme=TM_MOE, tmc=TM_COMB):
    h, idx, gw = moe_router(x, mods, g, wr_t, seg_len, tm)
    tile_expert, n_valid, src_rows, gate_rows, pos_tiles = moe_dispatch_plan(idx, gw, tme, tmc)
    y = moe_experts(h, tile_expert, n_valid, src_rows, gate_rows, w1, w3, w2, tme)
    return moe_combine(x, mods, y, pos_tiles, seg_len, tmc)


def rms_norm(x, g):
    xf = x.astype(F32)
    y = xf * lax.rsqrt(jnp.mean(xf * xf, axis=-1, keepdims=True) + EPS)
    return (y * g.astype(F32)).astype(x.dtype)


def axial_rope(L):
    rows = L // GRID_W
    r = jnp.repeat(jnp.arange(rows, dtype=F32), GRID_W)
    col = jnp.tile(jnp.arange(GRID_W, dtype=F32), rows)
    n = HEAD_DIM // 4
    freqs = ROPE_BASE ** (-jnp.arange(n, dtype=F32) / n)
    ang = jnp.concatenate([r[:, None] * freqs, col[:, None] * freqs], axis=-1)
    return jnp.cos(ang), jnp.sin(ang)


def rope_part(x, cos, sin):
    x1, x2 = jnp.split(x, 2, axis=-1)
    return jnp.concatenate([x1 * cos - x2 * sin, x2 * cos + x1 * sin], axis=-1)


def apply_axial_rope(x, cos, sin):
    n = HEAD_DIM // 4
    c = cos[None, :, None, :]
    s = sin[None, :, None, :]
    xf = x.astype(F32)
    xr = rope_part(xf[..., :2 * n], c[..., :n], s[..., :n])
    xc = rope_part(xf[..., 2 * n:], c[..., n:], s[..., n:])
    return jnp.concatenate([xr, xc], axis=-1).astype(x.dtype)


def sink_softmax_combine(scores, values, sink):
    B, _, _, Q, _ = scores[0].shape
    sink_col = jnp.broadcast_to(sink.astype(F32).reshape(1, A_KV_HEADS, A_GROUP, 1, 1),
                                (B, A_KV_HEADS, A_GROUP, Q, 1))
    p = jax.nn.softmax(jnp.concatenate([sink_col] + scores, axis=-1), axis=-1)
    outs = []
    start = 1
    for s, v in zip(scores, values):
        n = s.shape[-1]
        outs.append(jnp.einsum('bkgqs,bskd->bqkgd', p[..., start:start + n], v.astype(F32)))
        start += n
    return sum(outs[1:], outs[0])


def context_attention(q, k, v, sink):
    B, L = q.shape[:2]
    nb = L // BLK
    qb = q.reshape(B, nb, BLK, A_KV_HEADS, A_GROUP, HEAD_DIM).swapaxes(0, 1)
    kf = k.astype(F32)

    def block(qi):
        s = jnp.einsum('bqkgd,bskd->bkgqs', qi.astype(F32), kf) * ATT_SCALE
        return sink_softmax_combine([s], [v], sink)

    o = lax.map(block, qb)
    return o.swapaxes(0, 1).reshape(B, L, A_WIDTH)


def latent_attention(q, k, v, kc, vc, sink):
    B, L = q.shape[:2]
    nb = L // BLK
    qb = q.reshape(B, nb, BLK, A_KV_HEADS, A_GROUP, HEAD_DIM).swapaxes(0, 1)
    pad = ((0, 0), (BLK, BLK), (0, 0), (0, 0))
    kp = jnp.pad(k.astype(F32), pad)
    vp = jnp.pad(v, pad)
    kcf = kc.astype(F32)
    offs_q = jnp.arange(BLK)
    offs_k = jnp.arange(3 * BLK) - BLK

    def block(args):
        qi, i = args
        start = i * BLK
        kl = lax.dynamic_slice_in_dim(kp, start, 3 * BLK, axis=1)
        vl = lax.dynamic_slice_in_dim(vp, start, 3 * BLK, axis=1)
        qf = qi.astype(F32)
        s_loc = jnp.einsum('bqkgd,bskd->bkgqs', qf, kl) * ATT_SCALE
        qpos = start + offs_q
        kpos = start + offs_k
        valid = ((kpos[None, :] >= 0) & (kpos[None, :] < L)
                 & (jnp.abs(qpos[:, None] - kpos[None, :]) <= WINDOW))
        s_loc = jnp.where(valid, s_loc, NEG_INF)
        s_ctx = jnp.einsum('bqkgd,bskd->bkgqs', qf, kcf) * ATT_SCALE
        return sink_softmax_combine([s_ctx, s_loc], [vc, vl], sink)

    o = lax.map(block, (qb, jnp.arange(nb)))
    return o.swapaxes(0, 1).reshape(B, L, A_WIDTH)


def short_conv3(x, w, b):
    xp = jnp.pad(x, ((0, 0), (1, 1), (0, 0)))
    return xp[:, :-2] * w[0] + xp[:, 1:-1] * w[1] + xp[:, 2:] * w[2] + b


def hyena_filters(L, lp):
    t = jnp.arange(L, dtype=F32)
    t_norm = t / max(L - 1, 1)
    w = (2.0 * math.pi / L) * t
    f = jnp.linspace(1e-4, HY_BANDS - 1, HY_BANDS, dtype=F32)
    fw = w[:, None] * f[None, :]
    feat = jnp.concatenate([t_norm[:, None], jnp.cos(fw), -jnp.sin(fw)], axis=-1)
    z = jnp.sin(lp['hy_freq1'] * (feat @ lp['hy_w1'] + lp['hy_b1']))
    z = jnp.sin(lp['hy_freq2'] * (z @ lp['hy_w2'] + lp['hy_b2']))
    hf = (z @ lp['hy_w3']).astype(F32).reshape(L, 2, HY_ORDER, HY_CH)
    hf = hf * jnp.exp(-t_norm[:, None, None, None] * jnp.abs(lp['hy_decay'].astype(F32)))
    hf = hf / (jnp.sum(jnp.abs(hf), axis=(0, 1), keepdims=True) + EPS)
    h_fwd, h_bwd = hf[:, 0], hf[:, 1]
    circ = jnp.concatenate([h_fwd, jnp.zeros((1, HY_ORDER, HY_CH), F32), h_bwd[:0:-1]], axis=0)
    return jnp.fft.rfft(circ, axis=0)


def hyena_mixer(u, lp):
    B, L, _ = u.shape
    u = short_conv3(u, lp['hy_conv_w'], lp['hy_conv_b']).astype(F32)
    v, x1, x2 = jnp.split(u, 3, axis=-1)
    filt = hyena_filters(L, lp)
    d = lp['hy_d'].astype(F32)
    z = v
    for o, gate in enumerate((x1, x2)):
        zf = jnp.fft.rfft(z, n=2 * L, axis=1)
        conv = jnp.fft.irfft(zf * filt[None, :, o], n=2 * L, axis=1)[:, :L]
        z = gate * (conv + d[o] * z)
    return z


def gla_chunked(q, k, v, log_a, s0):
    B, L, H, _ = q.shape
    DV = v.shape[-1]
    n = L // GLA_CHUNK

    def blocks(t):
        return t.astype(F32).reshape(B, n, GLA_CHUNK, H, t.shape[-1]).transpose(1, 0, 3, 2, 4)

    qc, kc, vc, la = blocks(q), blocks(k), blocks(v), blocks(log_a)
    b = jnp.cumsum(la, axis=3)
    b_last = b[:, :, :, -1:, :]
    qg = qc * jnp.exp(b)
    kg = kc * jnp.exp(-b)
    kd = kc * jnp.exp(b_last - b)
    causal = jnp.tril(jnp.ones((GLA_CHUNK, GLA_CHUNK), dtype=bool))
    a = jnp.where(causal, jnp.einsum('nbhtd,nbhsd->nbhts', qg, kg), 0.0)
    o_intra = jnp.einsum('nbhts,nbhsv->nbhtv', a, vc)
    u = jnp.einsum('nbhsd,nbhsv->nbhdv', kd, vc)
    decay = jnp.exp(b_last[:, :, :, 0, :])

    def step(s, inp):
        dec, uu = inp
        return dec[..., None] * s + uu, s

    s_final, s_start = lax.scan(step, s0.astype(F32), (decay, u))
    o = o_intra + jnp.einsum('nbhtd,nbhdv->nbhtv', qg, s_start)
    return o.transpose(1, 0, 3, 2, 4).reshape(B, L, H, DV), s_final


def gla_mixer(cq, ck, cv, cg, cr, s_fwd, s_bwd, lp):
    B, L, _ = cq.shape
    q = cq.reshape(B, L, C_HEADS, C_DK) * (C_DK ** -0.5)
    k = ck.reshape(B, L, C_HEADS, C_DK)
    v = cv.reshape(B, L, C_HEADS, C_DV)
    r_f, r_b = jnp.split(cr, 2, axis=-1)
    gw, gb = lp['gla_gate_w'], lp['gla_gate_b']
    la_f = (jax.nn.log_sigmoid((r_f @ gw[0] + gb[0]).astype(F32)) / GLA_TAU).reshape(B, L, C_HEADS, C_DK)
    la_b = (jax.nn.log_sigmoid((r_b @ gw[1] + gb[1]).astype(F32)) / GLA_TAU).reshape(B, L, C_HEADS, C_DK)
    o_f, st_f = gla_chunked(q, k, v, la_f, s_fwd)
    o_b, st_b = gla_chunked(jnp.flip(q, axis=1), jnp.flip(k, axis=1), jnp.flip(v, axis=1),
                            jnp.flip(la_b, axis=1), s_bwd)
    o = rms_norm(o_f + jnp.flip(o_b, axis=1), lp['gla_norm_g']).reshape(B, L, C_WIDTH)
    return o * jax.nn.silu(cg.astype(F32)), jnp.stack([st_f, st_b], axis=1)


def attn_heads(aq, ak, av, lp):
    B, L, _ = aq.shape
    q = rms_norm(aq.reshape(B, L, A_HEADS, HEAD_DIM), lp['q_norm_g'])
    k = rms_norm(ak.reshape(B, L, A_KV_HEADS, HEAD_DIM), lp['k_norm_g'])
    v = av.reshape(B, L, A_KV_HEADS, HEAD_DIM)
    return q, k, v


def mixer_heads(proj, lp, latent, kc=None, vc=None, st=None):
    B, L, _ = proj.shape
    aq, ak, av, hy, cq, ck, cv, cg, cr = jnp.split(proj[..., :D_PROJ], PROJ_SPLITS, axis=-1)
    q, k, v = attn_heads(aq, ak, av, lp)
    if latent:
        cos, sin = axial_rope(L)
        q = apply_axial_rope(q, cos, sin)
        k = apply_axial_rope(k, cos, sin)
        a_out = latent_attention(q, k, v, kc, vc, lp['attn_sink'])
        s_f, s_b = st[:, 0], st[:, 1]
    else:
        a_out = context_attention(q, k, v, lp['attn_sink'])
        s_f = s_b = jnp.zeros((B, C_HEADS, C_DK, C_DV), F32)
    h_out = hyena_mixer(hy, lp)
    g_out, st_new = gla_mixer(cq, ck, cv, cg, cr, s_f, s_b, lp)
    cat = jnp.concatenate([a_out.astype(F32), h_out.astype(F32), g_out.astype(F32)], axis=-1)
    return cat, k, v, st_new


def mod_table(cvec, w_ada, b_ada):
    m = jax.nn.silu(cvec) @ w_ada + b_ada
    m = m.reshape(cvec.shape[0], 6, D_MODEL)
    return jnp.pad(m, ((0, 0), (0, MOD_ROWS - 6), (0, 0)))


def kernel(x_prompt, x_sample, cache_k, cache_v, state_gla, c, c_ctx, norm1_g, norm2_g, w_ada, b_ada,
           w_in, w_out, q_norm_g, k_norm_g, attn_sink, hy_conv_w, hy_conv_b, hy_w1, hy_b1, hy_freq1,
           hy_w2, hy_b2, hy_freq2, hy_w3, hy_decay, hy_d, gla_gate_w, gla_gate_b, gla_norm_g,
           ffn_w1, ffn_w3, ffn_w2, moe_router, moe_w1, moe_w3, moe_w2):
    D = D_MODEL
    xp = x_prompt.reshape(BATCH * SEQ, D)
    xs = x_sample.reshape(DEC_BATCH * DEC_SEQ, D)
    streams = [
        dict(x=xp, B=BATCH, L=SEQ, seg_len=BATCH * SEQ, cvec=c_ctx[None, :], latent=False),
        dict(x=xs, B=DEC_BATCH, L=DEC_SEQ, seg_len=DEC_SEQ, cvec=c, latent=True),
    ]
    ks_list, vs_list, st_list = [], [], []
    for l in range(DEPTH):
        lp = {
            'q_norm_g': q_norm_g[l], 'k_norm_g': k_norm_g[l],
            'attn_sink': attn_sink[l], 'hy_conv_w': hy_conv_w[l], 'hy_conv_b': hy_conv_b[l],
            'hy_w1': hy_w1[l], 'hy_b1': hy_b1[l], 'hy_freq1': hy_freq1[l], 'hy_w2': hy_w2[l],
            'hy_b2': hy_b2[l], 'hy_freq2': hy_freq2[l], 'hy_w3': hy_w3[l], 'hy_decay': hy_decay[l],
            'hy_d': hy_d[l], 'gla_gate_w': gla_gate_w[l], 'gla_gate_b': gla_gate_b[l],
            'gla_norm_g': gla_norm_g[l],
        }
        w_in_l = jnp.pad(w_in[l], ((0, 0), (0, D_PROJ_PAD - D_PROJ))).astype(BF16)
        w_out_l = w_out[l].astype(BF16)
        g1 = norm1_g[l][None, :]
        g2 = norm2_g[l][None, :]
        j = l // 2
        for s in streams:
            mods = mod_table(s['cvec'], w_ada[l], b_ada[l])
            proj = in_proj(s['x'], mods, g1, w_in_l, s['seg_len'])
            proj = proj.reshape(s['B'], s['L'], D_PROJ_PAD)
            if s['latent']:
                cat, _, _, _ = mixer_heads(proj, lp, True, cache_k[:, l], cache_v[:, l], state_gla[:, l])
            else:
                cat, k_ctx, v_ctx, st_ctx = mixer_heads(proj, lp, False)
                ks_list.append(k_ctx)
                vs_list.append(v_ctx)
                st_list.append(st_ctx)
            x1 = out_proj(s['x'], cat.reshape(-1, D), mods, w_out_l, s['seg_len'])
            if l % 2 == 0:
                s['x'] = ffn_dense(x1, mods, g2, ffn_w1[j].astype(BF16), ffn_w3[j].astype(BF16),
                                   ffn_w2[j].astype(BF16), s['seg_len'])
            else:
                s['x'] = ffn_moe(x1, mods, g2, moe_router[j].T, moe_w1[j], moe_w3[j], moe_w2[j],
                                 s['seg_len'])
    y_prompt = streams[0]['x'].reshape(BATCH, SEQ, D)
    y_sample = streams[1]['x'].reshape(DEC_BATCH, DEC_SEQ, D)
    new_cache_k = jnp.stack(ks_list, axis=1)
    new_cache_v = jnp.stack(vs_list, axis=1)
    new_state_gla = jnp.stack(st_list, axis=1)
    return (y_prompt, y_sample, new_cache_k, new_cache_v, new_state_gla)
```

```python
import math
import functools
import jax
import jax.numpy as jnp
from jax import lax
import numpy as np
from jax.experimental import pallas as pl
from jax.experimental.pallas import tpu as pltpu

D_MODEL = 1024
BATCH = 32
SEQ = 256
DEPTH = 2
DEC_BATCH = 8
DEC_SEQ = 4096
PAST_LEN = 512

GRID_W = 64
HEAD_DIM = 64
D_MIX = D_MODEL
A_HEADS = 8
A_KV_HEADS = 2
A_GROUP = A_HEADS // A_KV_HEADS
A_WIDTH = A_HEADS * HEAD_DIM
WINDOW = 128
BLK = 128
ROPE_BASE = 10000.0
HY_CH = 256
HY_ORDER = 2
HY_BANDS = 16
HY_EMB = 1 + 2 * HY_BANDS
HY_HID = 64
C_HEADS = 4
C_DK = 64
C_DV = 64
C_WIDTH = C_HEADS * C_DV
GLA_RANK = 16
GLA_TAU = 16.0
GLA_CHUNK = 64
D_FF = 2816
N_EXPERTS = 8
TOP_K = 2
D_FF_EXPERT = 3584

PROJ_SIZES = (A_WIDTH, A_KV_HEADS * HEAD_DIM, A_KV_HEADS * HEAD_DIM, 3 * HY_CH,
              C_HEADS * C_DK, C_HEADS * C_DK, C_WIDTH, C_WIDTH, 2 * GLA_RANK)
PROJ_SPLITS = tuple(int(s) for s in np.cumsum(PROJ_SIZES)[:-1])
D_PROJ = int(sum(PROJ_SIZES))

F32 = jnp.float32
BF16 = jnp.bfloat16
ATT_SCALE = HEAD_DIM ** -0.5
NEG_INF = -1e30
EPS = 1e-6

LANES = 128
SUBLANES = 8
D_PROJ_PAD = -(-D_PROJ // LANES) * LANES
VMEM_LIMIT_BYTES = 56 * 1024 * 1024

MOD_ROWS = SUBLANES
ROW_SHIFT1, ROW_SCALE1, ROW_GATE1, ROW_SHIFT2, ROW_SCALE2, ROW_GATE2 = range(6)

TM_PROJ = 512
TM_MOE = 1024
TF_MOE = 512
TM_COMB = 512


def _cparams(*sem):
    return pltpu.CompilerParams(dimension_semantics=sem, vmem_limit_bytes=VMEM_LIMIT_BYTES)


def _adaln_rows(x, g, shift, scale):
    ms = jnp.mean(x * x, axis=-1, keepdims=True)
    return (x * lax.rsqrt(ms + EPS) * g) * (1.0 + scale) + shift


def _silu(a):
    return a * jax.nn.sigmoid(a)


def _group_mean_sq(x, ones_bd):
    sq = x * x
    hi = sq.astype(BF16)
    lo = (sq - hi.astype(F32)).astype(BF16)
    s = jnp.dot(hi, ones_bd, preferred_element_type=F32) + jnp.dot(lo, ones_bd, preferred_element_type=F32)
    return s * (1.0 / HEAD_DIM)


def _rope_rows(x, cos_t, sin_t):
    q4 = HEAD_DIM // 4
    lane = lax.broadcasted_iota(jnp.int32, x.shape, 1)
    partner = jnp.where((lane % (2 * q4)) < q4, pltpu.roll(x, LANES - q4, 1), pltpu.roll(x, q4, 1))
    return x * cos_t + partner * sin_t


def _log_sigmoid(x):
    return jnp.minimum(x, 0.0) - jnp.log(1.0 + jnp.exp(-jnp.abs(x)))


def _inproj_body(*refs, latent):
    if latent:
        (x_ref, mod_ref, g_ref, w_ref, bd_ref, qg_ref, kg_ref, gw_ref, gb_ref, cos_ref, sin_ref,
         q_ref, k_ref, v_ref, hy_ref, cq_ref, ck_ref, cv_ref, cg_ref, la_ref) = refs
    else:
        (x_ref, mod_ref, g_ref, w_ref, bd_ref, qg_ref, kg_ref, gw_ref, gb_ref,
         q_ref, k_ref, v_ref, hy_ref, cq_ref, ck_ref, cv_ref, cg_ref, la_ref) = refs
    h = _adaln_rows(x_ref[...], g_ref[...], mod_ref[0, ROW_SHIFT1:ROW_SHIFT1 + 1, :],
                    mod_ref[0, ROW_SCALE1:ROW_SCALE1 + 1, :])
    acc = jnp.dot(h.astype(BF16), w_ref[...], preferred_element_type=F32)
    o = 0
    q = acc[:, o:o + A_WIDTH]
    o += A_WIDTH
    k = acc[:, o:o + LANES]
    o += LANES
    v_ref[...] = acc[:, o:o + LANES]
    o += LANES
    hy_ref[...] = acc[:, o:o + 3 * HY_CH]
    o += 3 * HY_CH
    cq_ref[...] = acc[:, o:o + C_WIDTH] * (C_DK ** -0.5)
    o += C_WIDTH
    ck_ref[...] = acc[:, o:o + C_WIDTH]
    o += C_WIDTH
    cv_ref[...] = acc[:, o:o + C_WIDTH]
    o += C_WIDTH
    cg_ref[...] = acc[:, o:o + C_WIDTH]
    o += C_WIDTH
    r = acc[:, o:o + LANES]
    la_ref[...] = _log_sigmoid(jnp.dot(r.astype(BF16), gw_ref[...], preferred_element_type=F32)
                               + gb_ref[...]) * (1.0 / GLA_TAU)
    q = q * lax.rsqrt(_group_mean_sq(q, bd_ref[...]) + EPS) * qg_ref[...]
    k = k * lax.rsqrt(_group_mean_sq(k, bd_ref[0:LANES, 0:LANES]) + EPS) * kg_ref[...]
    if latent:
        cos_t = cos_ref[...]
        sin_t = sin_ref[...]
        q = jnp.concatenate([_rope_rows(q[:, j * LANES:(j + 1) * LANES], cos_t, sin_t)
                             for j in range(A_WIDTH // LANES)], axis=1)
        k = _rope_rows(k, cos_t, sin_t)
    q_ref[...] = (q * ATT_SCALE).astype(BF16)
    k_ref[...] = k


def in_proj(x, mods, g, w, bd, qg, kg, gw, gb, seg_len, rope=None, seq_len=None, tm=TM_PROJ):
    T, D = x.shape
    N = w.shape[1]
    latent = rope is not None
    const = lambda shape: pl.BlockSpec(shape, lambda i: (0,) * len(shape))
    in_specs = [
        pl.BlockSpec((tm, D), lambda i: (i, 0)),
        pl.BlockSpec((1, MOD_ROWS, D), lambda i: (i * tm // seg_len, 0, 0)),
        const((1, D)), const((D, N)), const((A_WIDTH, A_WIDTH)), const((1, A_WIDTH)), const((1, LANES)),
        const((LANES, 2 * C_WIDTH)), const((1, 2 * C_WIDTH)),
    ]
    args = [x, mods, g, w, bd, qg, kg, gw, gb]
    if latent:
        tiles_per_seq = seq_len // tm
        in_specs += [pl.BlockSpec((tm, LANES), lambda i: (i % tiles_per_seq, 0))] * 2
        args += list(rope)
    widths = [A_WIDTH, LANES, LANES, 3 * HY_CH, C_WIDTH, C_WIDTH, C_WIDTH, C_WIDTH, 2 * C_WIDTH]
    dtypes = [BF16] + [F32] * 8
    return pl.pallas_call(
        functools.partial(_inproj_body, latent=latent),
        grid=(T // tm,),
        in_specs=in_specs,
        out_specs=[pl.BlockSpec((tm, wd), lambda i: (i, 0)) for wd in widths],
        out_shape=[jax.ShapeDtypeStruct((T, wd), dt) for wd, dt in zip(widths, dtypes)],
        compiler_params=_cparams("parallel"),
        name="in_proj_latent" if latent else "in_proj_context",
    )(*args)


def _outproj_body(x_ref, a_ref, h_ref, of_ref, ob_ref, cg_ref, mod_ref, w_ref, bd_ref, gg_ref, o_ref):
    o = of_ref[...] + ob_ref[...]
    g_out = o * lax.rsqrt(_group_mean_sq(o, bd_ref[...]) + EPS) * gg_ref[...] * _silu(cg_ref[...])
    h0 = A_WIDTH
    g0 = A_WIDTH + HY_CH
    mix = jnp.dot(a_ref[...], w_ref[0:h0, :], preferred_element_type=F32)
    mix += jnp.dot(h_ref[...].astype(BF16), w_ref[h0:g0, :], preferred_element_type=F32)
    mix += jnp.dot(g_out.astype(BF16), w_ref[g0:, :], preferred_element_type=F32)
    o_ref[...] = x_ref[...] + mod_ref[0, ROW_GATE1:ROW_GATE1 + 1, :] * mix


def out_proj(x, a, h, of, ob, cg, mods, w, bd, gg, seg_len, tm=TM_PROJ):
    T, D = x.shape
    row = lambda wd: pl.BlockSpec((tm, wd), lambda i: (i, 0))
    const = lambda shape: pl.BlockSpec(shape, lambda i: (0,) * len(shape))
    return pl.pallas_call(
        _outproj_body,
        grid=(T // tm,),
        in_specs=[
            row(D), row(A_WIDTH), row(HY_CH), row(C_WIDTH), row(C_WIDTH), row(C_WIDTH),
            pl.BlockSpec((1, MOD_ROWS, D), lambda i: (i * tm // seg_len, 0, 0)),
            const((D, D)), const((C_WIDTH, C_WIDTH)), const((1, C_WIDTH)),
        ],
        out_specs=row(D),
        out_shape=jax.ShapeDtypeStruct((T, D), F32),
        compiler_params=_cparams("parallel"),
        name="out_proj",
    )(x, a, h, of, ob, cg, mods, w, bd, gg)


TQ_ATT = 256
KWIN_ATT = TQ_ATT + 2 * WINDOW


def _dup_heads(x, g):
    lane = lax.broadcasted_iota(jnp.int32, x.shape, 1)
    rolled = pltpu.roll(x, HEAD_DIM, 1)
    keep = (lane < HEAD_DIM) if g == 0 else (lane >= HEAD_DIM)
    return jnp.where(keep, x, rolled)


def _attn_body(*refs, latent, seq_len):
    if latent:
        sink_ref, q_ref, k_ref, v_ref, kc_ref, vc_ref, o_ref = refs
    else:
        sink_ref, q_ref, k_ref, v_ref, o_ref = refs
    tq = q_ref.shape[1]
    i = pl.program_id(1)
    q = q_ref[0]
    lane = lax.broadcasted_iota(jnp.int32, (tq, LANES), 1)
    low = lane < HEAD_DIM
    if latent:
        start = jnp.clip(i * tq - WINDOW, 0, seq_len - KWIN_ATT)
        start = pl.multiple_of(start, WINDOW)
        kl = k_ref[0, pl.ds(start, KWIN_ATT), :]
        vl = v_ref[0, pl.ds(start, KWIN_ATT), :]
        qpos = i * tq + lax.broadcasted_iota(jnp.int32, (tq, KWIN_ATT), 0)
        kpos = start + lax.broadcasted_iota(jnp.int32, (tq, KWIN_ATT), 1)
        bias = jnp.where(jnp.abs(qpos - kpos) <= WINDOW, 0.0, NEG_INF).astype(F32)
        bias = jnp.concatenate([bias] * A_GROUP, axis=0)
        kc = kc_ref[0]
        vc = vc_ref[0]
    else:
        kl = k_ref[0]
        vl = v_ref[0]
    nt = (((1,), (1,)), ((), ()))
    zero = jnp.zeros_like(q[:, :LANES])
    outs = []
    for g in range(A_KV_HEADS):
        parts = []
        for hh in range(A_GROUP):
            h = g * A_GROUP + hh
            blk = q[:, (h // 2) * LANES:(h // 2 + 1) * LANES]
            parts.append(jnp.where(low if h % 2 == 0 else jnp.logical_not(low), blk, zero))
        qs = jnp.concatenate(parts, axis=0)
        sink = jnp.concatenate([jnp.full((tq, 1), sink_ref[g * A_GROUP + hh], F32)
                                for hh in range(A_GROUP)], axis=0)
        kl_g = _dup_heads(kl, g).astype(BF16)
        vl_g = _dup_heads(vl, g).astype(BF16)
        s_loc = lax.dot_general(qs, kl_g, nt, preferred_element_type=F32)
        m = sink
        if latent:
            s_loc = s_loc + bias
            kc_g = _dup_heads(kc, g).astype(BF16)
            vc_g = _dup_heads(vc, g).astype(BF16)
            s_ctx = lax.dot_general(qs, kc_g, nt, preferred_element_type=F32)
            m = jnp.maximum(m, jnp.max(s_ctx, axis=-1, keepdims=True))
        m = jnp.maximum(m, jnp.max(s_loc, axis=-1, keepdims=True))
        p_loc = jnp.exp(s_loc - m)
        den = jnp.exp(sink - m) + jnp.sum(p_loc, axis=-1, keepdims=True)
        acc = jnp.dot(p_loc.astype(BF16), vl_g, preferred_element_type=F32)
        if latent:
            p_ctx = jnp.exp(s_ctx - m)
            den = den + jnp.sum(p_ctx, axis=-1, keepdims=True)
            acc = acc + jnp.dot(p_ctx.astype(BF16), vc_g, preferred_element_type=F32)
        og = acc / den
        for j in range(A_GROUP // 2):
            outs.append(jnp.where(low, og[(2 * j) * tq:(2 * j + 1) * tq], og[(2 * j + 1) * tq:(2 * j + 2) * tq]))
    o_ref[0] = jnp.concatenate(outs, axis=1).astype(o_ref.dtype)


def attention(q, k, v, sink, kc=None, vc=None, tq=TQ_ATT):
    B, L, _ = q.shape
    latent = kc is not None
    tq = min(tq, L)
    seq = lambda wd: pl.BlockSpec((1, L, wd), lambda b, i: (b, 0, 0))
    in_specs = [
        pl.BlockSpec(memory_space=pltpu.SMEM),
        pl.BlockSpec((1, tq, A_WIDTH), lambda b, i: (b, i, 0)),
        seq(LANES), seq(LANES),
    ]
    args = [sink, q, k, v]
    if latent:
        P = kc.shape[1]
        in_specs += [pl.BlockSpec((1, P, LANES), lambda b, i: (b, 0, 0))] * 2
        args += [kc, vc]
    return pl.pallas_call(
        functools.partial(_attn_body, latent=latent, seq_len=L),
        grid=(B, L // tq),
        in_specs=in_specs,
        out_specs=pl.BlockSpec((1, tq, A_WIDTH), lambda b, i: (b, i, 0)),
        out_shape=jax.ShapeDtypeStruct((B, L, A_WIDTH), BF16),
        compiler_params=_cparams("parallel", "arbitrary"),
        name="attention_latent" if latent else "attention_context",
    )(*args)


def _split3(x):
    hi = x.astype(BF16)
    r = x - hi.astype(F32)
    mid = r.astype(BF16)
    lo = (r - mid.astype(F32)).astype(BF16)
    return hi, mid, lo


def _gla_chunk(q, k, v, la, st_ref, d, reverse):
    C = GLA_CHUNK
    W = C_WIDTH
    ti = lax.broadcasted_iota(jnp.int32, (C, C), 0)
    si = lax.broadcasted_iota(jnp.int32, (C, C), 1)
    tri = (si >= ti) if reverse else (si <= ti)
    tri_b = tri.astype(BF16)
    hi, mid, lo = _split3(la)
    b = (jnp.dot(tri_b, hi, preferred_element_type=F32) + jnp.dot(tri_b, mid, preferred_element_type=F32)
         + jnp.dot(tri_b, lo, preferred_element_type=F32))
    b_last = b[0:1] if reverse else b[C - 1:C]
    qg = q * jnp.exp(b)
    kg = (k * jnp.exp(-b)).astype(BF16)
    kd = (k * jnp.exp(b_last - b)).astype(BF16)
    decay = jnp.exp(b_last)
    vb = v.astype(BF16)
    r4 = lax.broadcasted_iota(jnp.int32, (C_HEADS * C, W), 0) // C
    c4 = lax.broadcasted_iota(jnp.int32, (C_HEADS * C, W), 1) // C_DK
    same_head = r4 == c4
    q_bd = jnp.where(same_head, jnp.concatenate([qg] * C_HEADS, axis=0), 0.0).astype(BF16)
    nt = (((1,), (1,)), ((), ()))
    a = lax.dot_general(q_bd, kg, nt, preferred_element_type=F32)
    a = jnp.where(jnp.concatenate([tri] * C_HEADS, axis=0), a, 0.0)
    r = jnp.dot(a.astype(BF16), vb, preferred_element_type=F32)
    r = jnp.where(same_head, r, 0.0)
    o = r[0:C]
    for h in range(1, C_HEADS):
        o = o + r[h * C:(h + 1) * C]
    st = st_ref[d]
    o = o + lax.dot_general(qg.astype(BF16), st.astype(BF16), nt, preferred_element_type=F32)
    ut = lax.dot_general(vb, kd, (((0,), (0,)), ((), ())), preferred_element_type=F32)
    st_ref[d] = st * decay + jnp.where(same_head, ut, 0.0)
    return o


def _gla_body(*refs, has_state, n_chunks):
    if has_state:
        (qf, kf, vf, lf, qb, kb, vb, lb, s0_ref, of_ref, ob_ref, sT_ref, st) = refs
    else:
        (qf, kf, vf, lf, qb, kb, vb, lb, of_ref, ob_ref, sT_ref, st) = refs
    j = pl.program_id(1)

    @pl.when(j == 0)
    def _init():
        if has_state:
            st[...] = s0_ref[0]
        else:
            st[...] = jnp.zeros_like(st)

    C = GLA_CHUNK

    def body(c, carry):
        rf = pl.multiple_of(c * C, C)
        rb = pl.multiple_of((n_chunks - 1 - c) * C, C)
        of_ref[0, pl.ds(rf, C), :] = _gla_chunk(qf[0, pl.ds(rf, C), :], kf[0, pl.ds(rf, C), :],
                                                vf[0, pl.ds(rf, C), :], lf[0, pl.ds(rf, C), :], st, 0, False)
        ob_ref[0, pl.ds(rb, C), :] = _gla_chunk(qb[0, pl.ds(rb, C), :], kb[0, pl.ds(rb, C), :],
                                                vb[0, pl.ds(rb, C), :], lb[0, pl.ds(rb, C), :], st, 1, True)
        return carry
    lax.fori_loop(0, n_chunks, body, 0)

    @pl.when(j == pl.num_programs(1) - 1)
    def _final():
        sT_ref[0] = st[...]


def gla(cq, ck, cv, la, s0=None, rows=512):
    B, L, W = cq.shape
    rows = min(rows, L)
    ng = L // rows
    has_state = s0 is not None
    fwd = lambda: pl.BlockSpec((1, rows, W), lambda b, j: (b, j, 0))
    bwd = lambda: pl.BlockSpec((1, rows, W), lambda b, j: (b, ng - 1 - j, 0))
    in_specs = [fwd(), fwd(), fwd(), pl.BlockSpec((1, rows, W), lambda b, j: (b, j, 0)),
                bwd(), bwd(), bwd(), pl.BlockSpec((1, rows, W), lambda b, j: (b, ng - 1 - j, 1))]
    args = [cq, ck, cv, la, cq, ck, cv, la]
    if has_state:
        in_specs.append(pl.BlockSpec((1, 2, W, W), lambda b, j: (b, 0, 0, 0)))
        args.append(s0)
    return pl.pallas_call(
        functools.partial(_gla_body, has_state=has_state, n_chunks=rows // GLA_CHUNK),
        grid=(B, ng),
        in_specs=in_specs,
        out_specs=[fwd(), bwd(), pl.BlockSpec((1, 2, W, W), lambda b, j: (b, 0, 0, 0))],
        out_shape=[jax.ShapeDtypeStruct((B, L, W), F32), jax.ShapeDtypeStruct((B, L, W), F32),
                   jax.ShapeDtypeStruct((B, 2, W, W), F32)],
        scratch_shapes=[pltpu.VMEM((2, W, W), F32)],
        compiler_params=_cparams("parallel", "arbitrary"),
        name="gla",
    )(*args)


def _ffn_body(x_ref, mod_ref, g_ref, w1_ref, w3_ref, w2_ref, o_ref):
    x = x_ref[...]
    h = _adaln_rows(x, g_ref[...], mod_ref[0, ROW_SHIFT2:ROW_SHIFT2 + 1, :],
                    mod_ref[0, ROW_SCALE2:ROW_SCALE2 + 1, :]).astype(BF16)
    a = jnp.dot(h, w1_ref[...], preferred_element_type=F32)
    b = jnp.dot(h, w3_ref[...], preferred_element_type=F32)
    act = (_silu(a) * b).astype(BF16)
    ff = jnp.dot(act, w2_ref[...], preferred_element_type=F32)
    o_ref[...] = x + mod_ref[0, ROW_GATE2:ROW_GATE2 + 1, :] * ff


def ffn_dense(x, mods, g, w1, w3, w2, seg_len, tm=TM_PROJ):
    T, D = x.shape
    F = w1.shape[1]
    resident = functools.partial(pl.BlockSpec, pipeline_mode=pl.Buffered(1))
    return pl.pallas_call(
        _ffn_body,
        grid=(T // tm,),
        in_specs=[
            pl.BlockSpec((tm, D), lambda i: (i, 0)),
            pl.BlockSpec((1, MOD_ROWS, D), lambda i: (i * tm // seg_len, 0, 0)),
            pl.BlockSpec((1, D), lambda i: (0, 0)),
            resident((D, F), lambda i: (0, 0)),
            resident((D, F), lambda i: (0, 0)),
            resident((F, D), lambda i: (0, 0)),
        ],
        out_specs=pl.BlockSpec((tm, D), lambda i: (i, 0)),
        out_shape=jax.ShapeDtypeStruct((T, D), F32),
        compiler_params=_cparams("parallel"),
        name="ffn_dense",
    )(x, mods, g, w1, w3, w2)


def _router_body(x_ref, mod_ref, g_ref, wr_ref, h_ref, idx_ref, gw_ref):
    h = _adaln_rows(x_ref[...], g_ref[...], mod_ref[0, ROW_SHIFT2:ROW_SHIFT2 + 1, :],
                    mod_ref[0, ROW_SCALE2:ROW_SCALE2 + 1, :])
    h_ref[...] = h
    logits = lax.dot_general(wr_ref[...], h, (((1,), (1,)), ((), ())),
                             precision=lax.Precision.HIGHEST, preferred_element_type=F32)
    eidx = lax.broadcasted_iota(jnp.int32, logits.shape, 0)
    m1 = jnp.max(logits, axis=0, keepdims=True)
    i1 = jnp.min(jnp.where(logits == m1, eidx, N_EXPERTS), axis=0, keepdims=True)
    rest = jnp.where(eidx == i1, -jnp.inf, logits)
    m2 = jnp.max(rest, axis=0, keepdims=True)
    i2 = jnp.min(jnp.where(rest == m2, eidx, N_EXPERTS), axis=0, keepdims=True)
    e2 = jnp.exp(m2 - m1)
    den = 1.0 + e2
    row = lax.broadcasted_iota(jnp.int32, logits.shape, 0)
    idx_ref[...] = jnp.where(row == 0, i1, jnp.where(row == 1, i2, 0))
    gw_ref[...] = jnp.where(row == 0, 1.0 / den, jnp.where(row == 1, e2 / den, 0.0))


def moe_router(x, mods, g, wr_t, seg_len, tm=TM_PROJ):
    T, D = x.shape
    return pl.pallas_call(
        _router_body,
        grid=(T // tm,),
        in_specs=[
            pl.BlockSpec((tm, D), lambda i: (i, 0)),
            pl.BlockSpec((1, MOD_ROWS, D), lambda i: (i * tm // seg_len, 0, 0)),
            pl.BlockSpec((1, D), lambda i: (0, 0)),
            pl.BlockSpec((N_EXPERTS, D), lambda i: (0, 0)),
        ],
        out_specs=[
            pl.BlockSpec((tm, D), lambda i: (i, 0)),
            pl.BlockSpec((N_EXPERTS, tm), lambda i: (0, i)),
            pl.BlockSpec((N_EXPERTS, tm), lambda i: (0, i)),
        ],
        out_shape=[
            jax.ShapeDtypeStruct((T, D), F32),
            jax.ShapeDtypeStruct((N_EXPERTS, T), jnp.int32),
            jax.ShapeDtypeStruct((N_EXPERTS, T), F32),
        ],
        compiler_params=_cparams("parallel"),
        name="moe_router",
    )(x, mods, g, wr_t)


def _row_copy(src_hbm, row, dst, r, sem):
    return pltpu.make_async_copy(src_hbm.at[pl.ds(row, 1)], dst.at[pl.ds(r, 1)], sem)


def _experts_body(te_ref, nv_ref, src_ref, gate_ref, h_hbm, w1_ref, w3_ref, w2_ref, y_ref,
                  hrows, hb, acc, sem, *, tm, nf):
    i = pl.program_id(0)
    f = pl.program_id(1)
    valid = i < nv_ref[0]

    @pl.when(jnp.logical_and(valid, f == 0))
    def _gather():
        def issue(r, carry):
            _row_copy(h_hbm, src_ref[0, 0, r], hrows, r, sem).start()
            return carry
        lax.fori_loop(0, tm, issue, 0)

        def wait(r, carry):
            _row_copy(h_hbm, 0, hrows, r, sem).wait()
            return carry
        lax.fori_loop(0, tm, wait, 0)
        hb[...] = hrows[...].astype(BF16)
        acc[...] = jnp.zeros_like(acc)

    @pl.when(valid)
    def _compute():
        h = hb[...]
        a = jnp.dot(h, w1_ref[0].astype(BF16), preferred_element_type=F32)
        b = jnp.dot(h, w3_ref[0].astype(BF16), preferred_element_type=F32)
        act = (_silu(a) * b).astype(BF16)
        acc[...] += jnp.dot(act, w2_ref[0].astype(BF16), preferred_element_type=F32)

    @pl.when(jnp.logical_and(valid, f == nf - 1))
    def _emit():
        y_ref[...] = acc[...] * gate_ref[...]

    @pl.when(jnp.logical_and(jnp.logical_not(valid), f == nf - 1))
    def _emit_unused():
        y_ref[...] = jnp.zeros_like(y_ref)


def moe_experts(h, tile_expert, n_valid, src_rows, gate_rows, w1, w3, w2, tm=TM_MOE, tf=TF_MOE):
    T, D = h.shape
    F = w1.shape[2]
    n_tiles = src_rows.shape[0]
    nf = F // tf

    def wcol(i, f, te, nv):
        return (te[i], 0, jnp.where(i < nv[0], f, nf - 1))

    def wrow(i, f, te, nv):
        return (te[i], jnp.where(i < nv[0], f, nf - 1), 0)

    grid_spec = pltpu.PrefetchScalarGridSpec(
        num_scalar_prefetch=2,
        grid=(n_tiles, nf),
        in_specs=[
            pl.BlockSpec((1, 1, tm), lambda i, f, te, nv: (i, 0, 0), memory_space=pltpu.SMEM),
            pl.BlockSpec((tm, 1), lambda i, f, te, nv: (i, 0)),
            pl.BlockSpec(memory_space=pl.ANY),
            pl.BlockSpec((1, D, tf), wcol),
            pl.BlockSpec((1, D, tf), wcol),
            pl.BlockSpec((1, tf, D), wrow),
        ],
        out_specs=pl.BlockSpec((tm, D), lambda i, f, te, nv: (i, 0)),
        scratch_shapes=[
            pltpu.VMEM((tm, D), F32),
            pltpu.VMEM((tm, D), BF16),
            pltpu.VMEM((tm, D), F32),
            pltpu.SemaphoreType.DMA(()),
        ],
    )
    return pl.pallas_call(
        functools.partial(_experts_body, tm=tm, nf=nf),
        grid_spec=grid_spec,
        out_shape=jax.ShapeDtypeStruct((n_tiles * tm, D), F32),
        compiler_params=_cparams("arbitrary", "arbitrary"),
        name="moe_experts",
    )(tile_expert, n_valid, src_rows, gate_rows, h, w1, w3, w2)


def _combine_body(pos_ref, x_ref, mod_ref, y_hbm, o_ref, buf0, buf1, sem, *, tm):
    def issue(r, carry):
        _row_copy(y_hbm, pos_ref[0, 0, r], buf0, r, sem).start()
        _row_copy(y_hbm, pos_ref[0, 0, tm + r], buf1, r, sem).start()
        return carry
    lax.fori_loop(0, tm, issue, 0)

    def wait(r, carry):
        _row_copy(y_hbm, 0, buf0, r, sem).wait()
        _row_copy(y_hbm, 0, buf1, r, sem).wait()
        return carry
    lax.fori_loop(0, tm, wait, 0)
    o_ref[...] = x_ref[...] + mod_ref[0, ROW_GATE2:ROW_GATE2 + 1, :] * (buf0[...] + buf1[...])


def moe_combine(x, mods, y, pos_tiles, seg_len, tm=TM_COMB):
    T, D = x.shape
    return pl.pallas_call(
        functools.partial(_combine_body, tm=tm),
        grid=(T // tm,),
        in_specs=[
            pl.BlockSpec((1, 1, 2 * tm), lambda i: (i, 0, 0), memory_space=pltpu.SMEM),
            pl.BlockSpec((tm, D), lambda i: (i, 0)),
            pl.BlockSpec((1, MOD_ROWS, D), lambda i: (i * tm // seg_len, 0, 0)),
            pl.BlockSpec(memory_space=pl.ANY),
        ],
        out_specs=pl.BlockSpec((tm, D), lambda i: (i, 0)),
        out_shape=jax.ShapeDtypeStruct((T, D), F32),
        scratch_shapes=[
            pltpu.VMEM((tm, D), F32),
            pltpu.VMEM((tm, D), F32),
            pltpu.SemaphoreType.DMA(()),
        ],
        compiler_params=_cparams("arbitrary"),
        name="moe_combine",
    )(pos_tiles, x, mods, y)


def moe_dispatch_plan(idx, gw, tm=TM_MOE, tmc=TM_COMB):
    T = idx.shape[1]
    n_slots = TOP_K * T
    n_tiles = n_slots // tm + N_EXPERTS
    n_rows = n_tiles * tm
    e_flat = idx[:TOP_K].reshape(n_slots)
    g_flat = gw[:TOP_K].reshape(n_slots)
    onehot = (e_flat[:, None] == jnp.arange(N_EXPERTS, dtype=jnp.int32)[None, :]).astype(jnp.int32)
    csum = jnp.cumsum(onehot, axis=0)
    counts = csum[-1]
    rank = jnp.sum((csum - onehot) * onehot, axis=1)
    padded = (counts + tm - 1) // tm * tm
    ends = jnp.cumsum(padded)
    offs = ends - padded
    pos = offs[e_flat] + rank
    order = jnp.argsort(e_flat, stable=True).astype(jnp.int32)
    cstart = jnp.cumsum(counts) - counts
    rows = jnp.arange(n_rows, dtype=jnp.int32)
    row_e = jnp.minimum(jnp.searchsorted(ends, rows, side='right'), N_EXPERTS - 1).astype(jnp.int32)
    j = rows - offs[row_e]
    live = j < counts[row_e]
    slot = order[jnp.clip(cstart[row_e] + j, 0, n_slots - 1)]
    src_rows = jnp.where(live, slot % T, 0).astype(jnp.int32).reshape(n_tiles, 1, tm)
    gate_rows = jnp.where(live, g_flat[slot], 0.0).reshape(n_rows, 1)
    tile_start = jnp.arange(n_tiles, dtype=jnp.int32) * tm
    n_valid = (ends[-1] // tm).astype(jnp.int32).reshape(1)
    tile_expert = jnp.minimum(jnp.searchsorted(ends, tile_start, side='right'), N_EXPERTS - 1)
    last_e = tile_expert[jnp.maximum(n_valid[0] - 1, 0)]
    tile_expert = jnp.where(tile_start < ends[-1], tile_expert, last_e).astype(jnp.int32)
    pos2 = pos.reshape(TOP_K, T // tmc, 1, tmc)
    pos_tiles = jnp.concatenate([pos2[0], pos2[1]], axis=-1).astype(jnp.int32)
    return tile_expert, n_valid, src_rows, gate_rows, pos_tiles


def ffn_moe(x, mods, g, wr_t, w1, w3, w2, seg_len, tm=TM_PROJ, tme=TM_MOE, tmc=TM_COMB):
    h, idx, gw = moe_router(x, mods, g, wr_t, seg_len, tm)
    tile_expert, n_valid, src_rows, gate_rows, pos_tiles = moe_dispatch_plan(idx, gw, tme, tmc)
    y = moe_experts(h, tile_expert, n_valid, src_rows, gate_rows, w1, w3, w2, tme)
    return moe_combine(x, mods, y, pos_tiles, seg_len, tmc)


def rms_norm(x, g):
    xf = x.astype(F32)
    y = xf * lax.rsqrt(jnp.mean(xf * xf, axis=-1, keepdims=True) + EPS)
    return (y * g.astype(F32)).astype(x.dtype)


def axial_rope(L):
    rows = L // GRID_W
    r = jnp.repeat(jnp.arange(rows, dtype=F32), GRID_W)
    col = jnp.tile(jnp.arange(GRID_W, dtype=F32), rows)
    n = HEAD_DIM // 4
    freqs = ROPE_BASE ** (-jnp.arange(n, dtype=F32) / n)
    ang = jnp.concatenate([r[:, None] * freqs, col[:, None] * freqs], axis=-1)
    return jnp.cos(ang), jnp.sin(ang)


def rope_part(x, cos, sin):
    x1, x2 = jnp.split(x, 2, axis=-1)
    return jnp.concatenate([x1 * cos - x2 * sin, x2 * cos + x1 * sin], axis=-1)


def apply_axial_rope(x, cos, sin):
    n = HEAD_DIM // 4
    c = cos[None, :, None, :]
    s = sin[None, :, None, :]
    xf = x.astype(F32)
    xr = rope_part(xf[..., :2 * n], c[..., :n], s[..., :n])
    xc = rope_part(xf[..., 2 * n:], c[..., n:], s[..., n:])
    return jnp.concatenate([xr, xc], axis=-1).astype(x.dtype)


def sink_softmax_combine(scores, values, sink):
    B, _, _, Q, _ = scores[0].shape
    sink_col = jnp.broadcast_to(sink.astype(F32).reshape(1, A_KV_HEADS, A_GROUP, 1, 1),
                                (B, A_KV_HEADS, A_GROUP, Q, 1))
    p = jax.nn.softmax(jnp.concatenate([sink_col] + scores, axis=-1), axis=-1)
    outs = []
    start = 1
    for s, v in zip(scores, values):
        n = s.shape[-1]
        outs.append(jnp.einsum('bkgqs,bskd->bqkgd', p[..., start:start + n], v.astype(F32)))
        start += n
    return sum(outs[1:], outs[0])


def context_attention(q, k, v, sink):
    B, L = q.shape[:2]
    nb = L // BLK
    qb = q.reshape(B, nb, BLK, A_KV_HEADS, A_GROUP, HEAD_DIM).swapaxes(0, 1)
    kf = k.astype(F32)

    def block(qi):
        s = jnp.einsum('bqkgd,bskd->bkgqs', qi.astype(F32), kf) * ATT_SCALE
        return sink_softmax_combine([s], [v], sink)

    o = lax.map(block, qb)
    return o.swapaxes(0, 1).reshape(B, L, A_WIDTH)


def latent_attention(q, k, v, kc, vc, sink):
    B, L = q.shape[:2]
    nb = L // BLK
    qb = q.reshape(B, nb, BLK, A_KV_HEADS, A_GROUP, HEAD_DIM).swapaxes(0, 1)
    pad = ((0, 0), (BLK, BLK), (0, 0), (0, 0))
    kp = jnp.pad(k.astype(F32), pad)
    vp = jnp.pad(v, pad)
    kcf = kc.astype(F32)
    offs_q = jnp.arange(BLK)
    offs_k = jnp.arange(3 * BLK) - BLK

    def block(args):
        qi, i = args
        start = i * BLK
        kl = lax.dynamic_slice_in_dim(kp, start, 3 * BLK, axis=1)
        vl = lax.dynamic_slice_in_dim(vp, start, 3 * BLK, axis=1)
        qf = qi.astype(F32)
        s_loc = jnp.einsum('bqkgd,bskd->bkgqs', qf, kl) * ATT_SCALE
        qpos = start + offs_q
        kpos = start + offs_k
        valid = ((kpos[None, :] >= 0) & (kpos[None, :] < L)
                 & (jnp.abs(qpos[:, None] - kpos[None, :]) <= WINDOW))
        s_loc = jnp.where(valid, s_loc, NEG_INF)
        s_ctx = jnp.einsum('bqkgd,bskd->bkgqs', qf, kcf) * ATT_SCALE
        return sink_softmax_combine([s_ctx, s_loc], [vc, vl], sink)

    o = lax.map(block, (qb, jnp.arange(nb)))
    return o.swapaxes(0, 1).reshape(B, L, A_WIDTH)


def short_conv3(x, w, b):
    xp = jnp.pad(x, ((0, 0), (1, 1), (0, 0)))
    return xp[:, :-2] * w[0] + xp[:, 1:-1] * w[1] + xp[:, 2:] * w[2] + b


def hyena_filters(L, lp):
    t = jnp.arange(L, dtype=F32)
    t_norm = t / max(L - 1, 1)
    w = (2.0 * math.pi / L) * t
    f = jnp.linspace(1e-4, HY_BANDS - 1, HY_BANDS, dtype=F32)
    fw = w[:, None] * f[None, :]
    feat = jnp.concatenate([t_norm[:, None], jnp.cos(fw), -jnp.sin(fw)], axis=-1)
    z = jnp.sin(lp['hy_freq1'] * (feat @ lp['hy_w1'] + lp['hy_b1']))
    z = jnp.sin(lp['hy_freq2'] * (z @ lp['hy_w2'] + lp['hy_b2']))
    hf = (z @ lp['hy_w3']).astype(F32).reshape(L, 2, HY_ORDER, HY_CH)
    hf = hf * jnp.exp(-t_norm[:, None, None, None] * jnp.abs(lp['hy_decay'].astype(F32)))
    hf = hf / (jnp.sum(jnp.abs(hf), axis=(0, 1), keepdims=True) + EPS)
    h_fwd, h_bwd = hf[:, 0], hf[:, 1]
    circ = jnp.concatenate([h_fwd, jnp.zeros((1, HY_ORDER, HY_CH), F32), h_bwd[:0:-1]], axis=0)
    return jnp.fft.rfft(circ, axis=0)


def hyena_mixer(u, lp):
    B, L, _ = u.shape
    u = short_conv3(u, lp['hy_conv_w'], lp['hy_conv_b']).astype(F32)
    v, x1, x2 = jnp.split(u, 3, axis=-1)
    filt = hyena_filters(L, lp)
    d = lp['hy_d'].astype(F32)
    z = v
    for o, gate in enumerate((x1, x2)):
        zf = jnp.fft.rfft(z, n=2 * L, axis=1)
        conv = jnp.fft.irfft(zf * filt[None, :, o], n=2 * L, axis=1)[:, :L]
        z = gate * (conv + d[o] * z)
    return z


def gla_chunked(q, k, v, log_a, s0):
    B, L, H, _ = q.shape
    DV = v.shape[-1]
    n = L // GLA_CHUNK

    def blocks(t):
        return t.astype(F32).reshape(B, n, GLA_CHUNK, H, t.shape[-1]).transpose(1, 0, 3, 2, 4)

    qc, kc, vc, la = blocks(q), blocks(k), blocks(v), blocks(log_a)
    b = jnp.cumsum(la, axis=3)
    b_last = b[:, :, :, -1:, :]
    qg = qc * jnp.exp(b)
    kg = kc * jnp.exp(-b)
    kd = kc * jnp.exp(b_last - b)
    causal = jnp.tril(jnp.ones((GLA_CHUNK, GLA_CHUNK), dtype=bool))
    a = jnp.where(causal, jnp.einsum('nbhtd,nbhsd->nbhts', qg, kg), 0.0)
    o_intra = jnp.einsum('nbhts,nbhsv->nbhtv', a, vc)
    u = jnp.einsum('nbhsd,nbhsv->nbhdv', kd, vc)
    decay = jnp.exp(b_last[:, :, :, 0, :])

    def step(s, inp):
        dec, uu = inp
        return dec[..., None] * s + uu, s

    s_final, s_start = lax.scan(step, s0.astype(F32), (decay, u))
    o = o_intra + jnp.einsum('nbhtd,nbhdv->nbhtv', qg, s_start)
    return o.transpose(1, 0, 3, 2, 4).reshape(B, L, H, DV), s_final


def gla_mixer(cq, ck, cv, cg, cr, s_fwd, s_bwd, lp):
    B, L, _ = cq.shape
    q = cq.reshape(B, L, C_HEADS, C_DK) * (C_DK ** -0.5)
    k = ck.reshape(B, L, C_HEADS, C_DK)
    v = cv.reshape(B, L, C_HEADS, C_DV)
    r_f, r_b = jnp.split(cr, 2, axis=-1)
    gw, gb = lp['gla_gate_w'], lp['gla_gate_b']
    la_f = (jax.nn.log_sigmoid((r_f @ gw[0] + gb[0]).astype(F32)) / GLA_TAU).reshape(B, L, C_HEADS, C_DK)
    la_b = (jax.nn.log_sigmoid((r_b @ gw[1] + gb[1]).astype(F32)) / GLA_TAU).reshape(B, L, C_HEADS, C_DK)
    o_f, st_f = gla_chunked(q, k, v, la_f, s_fwd)
    o_b, st_b = gla_chunked(jnp.flip(q, axis=1), jnp.flip(k, axis=1), jnp.flip(v, axis=1),
                            jnp.flip(la_b, axis=1), s_bwd)
    o = rms_norm(o_f + jnp.flip(o_b, axis=1), lp['gla_norm_g']).reshape(B, L, C_WIDTH)
    return o * jax.nn.silu(cg.astype(F32)), jnp.stack([st_f, st_b], axis=1)


def rope_tables(L):
    cos, sin = axial_rope(L)
    n = HEAD_DIM // 4
    cos_h = jnp.concatenate([cos[:, :n], cos[:, :n], cos[:, n:], cos[:, n:]], axis=1)
    sin_h = jnp.concatenate([-sin[:, :n], sin[:, :n], -sin[:, n:], sin[:, n:]], axis=1)
    reps = LANES // HEAD_DIM
    return jnp.tile(cos_h, (1, reps)), jnp.tile(sin_h, (1, reps))


def block_diag_ones(width, block):
    i = jnp.arange(width) // block
    return (i[:, None] == i[None, :]).astype(BF16)


def gla_gate_params(gw, gb):
    w = jnp.zeros((LANES, 2 * C_WIDTH), F32)
    w = w.at[:GLA_RANK, :C_WIDTH].set(gw[0]).at[GLA_RANK:2 * GLA_RANK, C_WIDTH:].set(gw[1])
    return w.astype(BF16), jnp.concatenate([gb[0], gb[1]])[None, :]


def gla_state_to_blockdiag(st):
    B = st.shape[0]
    out = jnp.zeros((B, 2, C_HEADS, C_DV, C_HEADS, C_DK), F32)
    for h in range(C_HEADS):
        out = out.at[:, :, h, :, h, :].set(jnp.swapaxes(st[:, :, h], -1, -2))
    return out.reshape(B, 2, C_WIDTH, C_WIDTH)


def gla_state_from_blockdiag(sT):
    B = sT.shape[0]
    s6 = sT.reshape(B, 2, C_HEADS, C_DV, C_HEADS, C_DK)
    return jnp.stack([jnp.swapaxes(s6[:, :, h, :, h, :], -1, -2) for h in range(C_HEADS)], axis=2)


def mod_table(cvec, w_ada, b_ada):
    m = jax.nn.silu(cvec) @ w_ada + b_ada
    m = m.reshape(cvec.shape[0], 6, D_MODEL)
    return jnp.pad(m, ((0, 0), (0, MOD_ROWS - 6), (0, 0)))


def kernel(x_prompt, x_sample, cache_k, cache_v, state_gla, c, c_ctx, norm1_g, norm2_g, w_ada, b_ada,
           w_in, w_out, q_norm_g, k_norm_g, attn_sink, hy_conv_w, hy_conv_b, hy_w1, hy_b1, hy_freq1,
           hy_w2, hy_b2, hy_freq2, hy_w3, hy_decay, hy_d, gla_gate_w, gla_gate_b, gla_norm_g,
           ffn_w1, ffn_w3, ffn_w2, moe_router, moe_w1, moe_w3, moe_w2):
    D = D_MODEL
    xp = x_prompt.reshape(BATCH * SEQ, D)
    xs = x_sample.reshape(DEC_BATCH * DEC_SEQ, D)
    streams = [
        dict(x=xp, B=BATCH, L=SEQ, seg_len=BATCH * SEQ, cvec=c_ctx[None, :], latent=False),
        dict(x=xs, B=DEC_BATCH, L=DEC_SEQ, seg_len=DEC_SEQ, cvec=c, latent=True),
    ]
    ks_list, vs_list, st_list = [], [], []
    bd_q = block_diag_ones(A_WIDTH, HEAD_DIM)
    for l in range(DEPTH):
        lp = {
            'hy_conv_w': hy_conv_w[l], 'hy_conv_b': hy_conv_b[l],
            'hy_w1': hy_w1[l], 'hy_b1': hy_b1[l], 'hy_freq1': hy_freq1[l], 'hy_w2': hy_w2[l],
            'hy_b2': hy_b2[l], 'hy_freq2': hy_freq2[l], 'hy_w3': hy_w3[l], 'hy_decay': hy_decay[l],
            'hy_d': hy_d[l], 'gla_gate_w': gla_gate_w[l], 'gla_gate_b': gla_gate_b[l],
            'gla_norm_g': gla_norm_g[l],
        }
        w_in_l = jnp.pad(w_in[l], ((0, 0), (0, D_PROJ_PAD - D_PROJ))).astype(BF16)
        w_out_l = w_out[l].astype(BF16)
        g1 = norm1_g[l][None, :]
        g2 = norm2_g[l][None, :]
        qg = jnp.tile(q_norm_g[l], A_HEADS)[None, :]
        kg = jnp.tile(k_norm_g[l], A_KV_HEADS)[None, :]
        gg = jnp.tile(gla_norm_g[l], C_HEADS)[None, :]
        gate_w, gate_b = gla_gate_params(gla_gate_w[l], gla_gate_b[l])
        j = l // 2
        for s in streams:
            B, L = s['B'], s['L']
            mods = mod_table(s['cvec'], w_ada[l], b_ada[l])
            rope = rope_tables(L) if s['latent'] else None
            q, k, v, hy, cq, ck, cv, cg, la = in_proj(s['x'], mods, g1, w_in_l, bd_q, qg, kg, gate_w, gate_b,
                                                      s['seg_len'], rope, L)
            seq = lambda t: t.reshape(B, L, t.shape[-1])
            if s['latent']:
                kc = cache_k[:, l].reshape(DEC_BATCH, PAST_LEN, LANES)
                vc = cache_v[:, l].reshape(DEC_BATCH, PAST_LEN, LANES)
                a_out = attention(seq(q), seq(k), seq(v), attn_sink[l], kc, vc)
                s0 = gla_state_to_blockdiag(state_gla[:, l])
                o_f, o_b, _ = gla(seq(cq), seq(ck), seq(cv), seq(la), s0)
            else:
                a_out = attention(seq(q), seq(k), seq(v), attn_sink[l])
                o_f, o_b, sT = gla(seq(cq), seq(ck), seq(cv), seq(la))
                ks_list.append(k.reshape(B, L, A_KV_HEADS, HEAD_DIM))
                vs_list.append(v.reshape(B, L, A_KV_HEADS, HEAD_DIM))
                st_list.append(gla_state_from_blockdiag(sT))
            h_out = hyena_mixer(seq(hy), lp)
            flat = lambda t: t.reshape(B * L, t.shape[-1])
            x1 = out_proj(s['x'], flat(a_out), flat(h_out), flat(o_f), flat(o_b), cg, mods, w_out_l,
                          bd_q[:C_WIDTH, :C_WIDTH], gg, s['seg_len'])
            if l % 2 == 0:
                s['x'] = ffn_dense(x1, mods, g2, ffn_w1[j].astype(BF16), ffn_w3[j].astype(BF16),
                                   ffn_w2[j].astype(BF16), s['seg_len'])
            else:
                s['x'] = ffn_moe(x1, mods, g2, moe_router[j].T, moe_w1[j], moe_w3[j], moe_w2[j],
                                 s['seg_len'])
    y_prompt = streams[0]['x'].reshape(BATCH, SEQ, D)
    y_sample = streams[1]['x'].reshape(DEC_BATCH, DEC_SEQ, D)
    new_cache_k = jnp.stack(ks_list, axis=1)
    new_cache_v = jnp.stack(vs_list, axis=1)
    new_state_gla = jnp.stack(st_list, axis=1)
    return (y_prompt, y_sample, new_cache_k, new_cache_v, new_state_gla)
```

```python
import math
import functools
import jax
import jax.numpy as jnp
from jax import lax
import numpy as np
from jax.experimental import pallas as pl
from jax.experimental.pallas import tpu as pltpu

D_MODEL = 1024
BATCH = 32
SEQ = 256
DEPTH = 2
DEC_BATCH = 8
DEC_SEQ = 4096
PAST_LEN = 512

GRID_W = 64
HEAD_DIM = 64
D_MIX = D_MODEL
A_HEADS = 8
A_KV_HEADS = 2
A_GROUP = A_HEADS // A_KV_HEADS
A_WIDTH = A_HEADS * HEAD_DIM
WINDOW = 128
BLK = 128
ROPE_BASE = 10000.0
HY_CH = 256
HY_ORDER = 2
HY_BANDS = 16
HY_EMB = 1 + 2 * HY_BANDS
HY_HID = 64
C_HEADS = 4
C_DK = 64
C_DV = 64
C_WIDTH = C_HEADS * C_DV
GLA_RANK = 16
GLA_TAU = 16.0
GLA_CHUNK = 64
D_FF = 2816
N_EXPERTS = 8
TOP_K = 2
D_FF_EXPERT = 3584

PROJ_SIZES = (A_WIDTH, A_KV_HEADS * HEAD_DIM, A_KV_HEADS * HEAD_DIM, 3 * HY_CH,
              C_HEADS * C_DK, C_HEADS * C_DK, C_WIDTH, C_WIDTH, 2 * GLA_RANK)
PROJ_SPLITS = tuple(int(s) for s in np.cumsum(PROJ_SIZES)[:-1])
D_PROJ = int(sum(PROJ_SIZES))

F32 = jnp.float32
BF16 = jnp.bfloat16
ATT_SCALE = HEAD_DIM ** -0.5
NEG_INF = -1e30
EPS = 1e-6

LANES = 128
SUBLANES = 8
D_PROJ_PAD = -(-D_PROJ // LANES) * LANES
VMEM_LIMIT_BYTES = 56 * 1024 * 1024

MOD_ROWS = SUBLANES
ROW_SHIFT1, ROW_SCALE1, ROW_GATE1, ROW_SHIFT2, ROW_SCALE2, ROW_GATE2 = range(6)

TM_PROJ = 512
TM_MOE = 1024
TF_MOE = 512
TM_COMB = 512


def _cparams(*sem):
    return pltpu.CompilerParams(dimension_semantics=sem, vmem_limit_bytes=VMEM_LIMIT_BYTES)


def _adaln_rows(x, g, shift, scale):
    ms = jnp.mean(x * x, axis=-1, keepdims=True)
    return (x * lax.rsqrt(ms + EPS) * g) * (1.0 + scale) + shift


def _silu(a):
    return a * jax.nn.sigmoid(a)


def _group_mean_sq(x, ones_bd):
    sq = x * x
    hi = sq.astype(BF16)
    lo = (sq - hi.astype(F32)).astype(BF16)
    s = jnp.dot(hi, ones_bd, preferred_element_type=F32) + jnp.dot(lo, ones_bd, preferred_element_type=F32)
    return s * (1.0 / HEAD_DIM)


def _rope_rows(x, cos_t, sin_t):
    q4 = HEAD_DIM // 4
    lane = lax.broadcasted_iota(jnp.int32, x.shape, 1)
    partner = jnp.where((lane % (2 * q4)) < q4, pltpu.roll(x, LANES - q4, 1), pltpu.roll(x, q4, 1))
    return x * cos_t + partner * sin_t


def _log_sigmoid(x):
    return jnp.minimum(x, 0.0) - jnp.log(1.0 + jnp.exp(-jnp.abs(x)))


def _inproj_body(*refs, latent):
    if latent:
        (x_ref, mod_ref, g_ref, w_ref, bd_ref, qg_ref, kg_ref, gw_ref, gb_ref, cos_ref, sin_ref,
         q_ref, k_ref, v_ref, hy_ref, cq_ref, ck_ref, cv_ref, cg_ref, la_ref) = refs
    else:
        (x_ref, mod_ref, g_ref, w_ref, bd_ref, qg_ref, kg_ref, gw_ref, gb_ref,
         q_ref, k_ref, v_ref, hy_ref, cq_ref, ck_ref, cv_ref, cg_ref, la_ref) = refs
    h = _adaln_rows(x_ref[...], g_ref[...], mod_ref[0, ROW_SHIFT1:ROW_SHIFT1 + 1, :],
                    mod_ref[0, ROW_SCALE1:ROW_SCALE1 + 1, :])
    acc = jnp.dot(h.astype(BF16), w_ref[...], preferred_element_type=F32)
    o = 0
    q = acc[:, o:o + A_WIDTH]
    o += A_WIDTH
    k = acc[:, o:o + LANES]
    o += LANES
    v_ref[...] = acc[:, o:o + LANES]
    o += LANES
    hy_ref[...] = acc[:, o:o + 3 * HY_CH]
    o += 3 * HY_CH
    cq_ref[...] = acc[:, o:o + C_WIDTH] * (C_DK ** -0.5)
    o += C_WIDTH
    ck_ref[...] = acc[:, o:o + C_WIDTH]
    o += C_WIDTH
    cv_ref[...] = acc[:, o:o + C_WIDTH]
    o += C_WIDTH
    cg_ref[...] = acc[:, o:o + C_WIDTH]
    o += C_WIDTH
    r = acc[:, o:o + LANES]
    la_ref[...] = _log_sigmoid(jnp.dot(r.astype(BF16), gw_ref[...], preferred_element_type=F32)
                               + gb_ref[...]) * (1.0 / GLA_TAU)
    q = q * lax.rsqrt(_group_mean_sq(q, bd_ref[...]) + EPS) * qg_ref[...]
    k = k * lax.rsqrt(_group_mean_sq(k, bd_ref[0:LANES, 0:LANES]) + EPS) * kg_ref[...]
    if latent:
        cos_t = cos_ref[...]
        sin_t = sin_ref[...]
        q = jnp.concatenate([_rope_rows(q[:, j * LANES:(j + 1) * LANES], cos_t, sin_t)
                             for j in range(A_WIDTH // LANES)], axis=1)
        k = _rope_rows(k, cos_t, sin_t)
    q_ref[...] = (q * ATT_SCALE).astype(BF16)
    k_ref[...] = k


def in_proj(x, mods, g, w, bd, qg, kg, gw, gb, seg_len, rope=None, seq_len=None, tm=TM_PROJ):
    T, D = x.shape
    N = w.shape[1]
    latent = rope is not None
    const = lambda shape: pl.BlockSpec(shape, lambda i: (0,) * len(shape))
    in_specs = [
        pl.BlockSpec((tm, D), lambda i: (i, 0)),
        pl.BlockSpec((1, MOD_ROWS, D), lambda i: (i * tm // seg_len, 0, 0)),
        const((1, D)), const((D, N)), const((A_WIDTH, A_WIDTH)), const((1, A_WIDTH)), const((1, LANES)),
        const((LANES, 2 * C_WIDTH)), const((1, 2 * C_WIDTH)),
    ]
    args = [x, mods, g, w, bd, qg, kg, gw, gb]
    if latent:
        tiles_per_seq = seq_len // tm
        in_specs += [pl.BlockSpec((tm, LANES), lambda i: (i % tiles_per_seq, 0))] * 2
        args += list(rope)
    widths = [A_WIDTH, LANES, LANES, 3 * HY_CH, C_WIDTH, C_WIDTH, C_WIDTH, C_WIDTH, 2 * C_WIDTH]
    dtypes = [BF16] + [F32] * 8
    return pl.pallas_call(
        functools.partial(_inproj_body, latent=latent),
        grid=(T // tm,),
        in_specs=in_specs,
        out_specs=[pl.BlockSpec((tm, wd), lambda i: (i, 0)) for wd in widths],
        out_shape=[jax.ShapeDtypeStruct((T, wd), dt) for wd, dt in zip(widths, dtypes)],
        compiler_params=_cparams("parallel"),
        name="in_proj_latent" if latent else "in_proj_context",
    )(*args)


def _outproj_body(x_ref, a_ref, h_ref, of_ref, ob_ref, cg_ref, mod_ref, w_ref, bd_ref, gg_ref, o_ref):
    o = of_ref[...] + ob_ref[...]
    g_out = o * lax.rsqrt(_group_mean_sq(o, bd_ref[...]) + EPS) * gg_ref[...] * _silu(cg_ref[...])
    h0 = A_WIDTH
    g0 = A_WIDTH + HY_CH
    mix = jnp.dot(a_ref[...], w_ref[0:h0, :], preferred_element_type=F32)
    mix += jnp.dot(h_ref[...].astype(BF16), w_ref[h0:g0, :], preferred_element_type=F32)
    mix += jnp.dot(g_out.astype(BF16), w_ref[g0:, :], preferred_element_type=F32)
    o_ref[...] = x_ref[...] + mod_ref[0, ROW_GATE1:ROW_GATE1 + 1, :] * mix


def out_proj(x, a, h, of, ob, cg, mods, w, bd, gg, seg_len, tm=TM_PROJ):
    T, D = x.shape
    row = lambda wd: pl.BlockSpec((tm, wd), lambda i: (i, 0))
    const = lambda shape: pl.BlockSpec(shape, lambda i: (0,) * len(shape))
    return pl.pallas_call(
        _outproj_body,
        grid=(T // tm,),
        in_specs=[
            row(D), row(A_WIDTH), row(HY_CH), row(C_WIDTH), row(C_WIDTH), row(C_WIDTH),
            pl.BlockSpec((1, MOD_ROWS, D), lambda i: (i * tm // seg_len, 0, 0)),
            const((D, D)), const((C_WIDTH, C_WIDTH)), const((1, C_WIDTH)),
        ],
        out_specs=row(D),
        out_shape=jax.ShapeDtypeStruct((T, D), F32),
        compiler_params=_cparams("parallel"),
        name="out_proj",
    )(x, a, h, of, ob, cg, mods, w, bd, gg)


TQ_ATT = 256
KWIN_ATT = TQ_ATT + 2 * WINDOW


def _dup_heads(x, g):
    lane = lax.broadcasted_iota(jnp.int32, x.shape, 1)
    rolled = pltpu.roll(x, HEAD_DIM, 1)
    keep = (lane < HEAD_DIM) if g == 0 else (lane >= HEAD_DIM)
    return jnp.where(keep, x, rolled)


def _attn_body(*refs, latent, seq_len):
    if latent:
        sink_ref, q_ref, k_ref, v_ref, kc_ref, vc_ref, o_ref = refs
    else:
        sink_ref, q_ref, k_ref, v_ref, o_ref = refs
    tq = q_ref.shape[1]
    i = pl.program_id(1)
    q = q_ref[0]
    lane = lax.broadcasted_iota(jnp.int32, (tq, LANES), 1)
    low = lane < HEAD_DIM
    if latent:
        start = jnp.clip(i * tq - WINDOW, 0, seq_len - KWIN_ATT)
        start = pl.multiple_of(start, WINDOW)
        kl = k_ref[0, pl.ds(start, KWIN_ATT), :]
        vl = v_ref[0, pl.ds(start, KWIN_ATT), :]
        qpos = i * tq + lax.broadcasted_iota(jnp.int32, (tq, KWIN_ATT), 0)
        kpos = start + lax.broadcasted_iota(jnp.int32, (tq, KWIN_ATT), 1)
        bias = jnp.where(jnp.abs(qpos - kpos) <= WINDOW, 0.0, NEG_INF).astype(F32)
        bias = jnp.concatenate([bias] * A_GROUP, axis=0)
        kc = kc_ref[0]
        vc = vc_ref[0]
    else:
        kl = k_ref[0]
        vl = v_ref[0]
    nt = (((1,), (1,)), ((), ()))
    zero = jnp.zeros_like(q[:, :LANES])
    outs = []
    for g in range(A_KV_HEADS):
        parts = []
        for hh in range(A_GROUP):
            h = g * A_GROUP + hh
            blk = q[:, (h // 2) * LANES:(h // 2 + 1) * LANES]
            parts.append(jnp.where(low if h % 2 == 0 else jnp.logical_not(low), blk, zero))
        qs = jnp.concatenate(parts, axis=0)
        sink = jnp.concatenate([jnp.full((tq, 1), sink_ref[g * A_GROUP + hh], F32)
                                for hh in range(A_GROUP)], axis=0)
        kl_g = _dup_heads(kl, g).astype(BF16)
        vl_g = _dup_heads(vl, g).astype(BF16)
        s_loc = lax.dot_general(qs, kl_g, nt, preferred_element_type=F32)
        m = sink
        if latent:
            s_loc = s_loc + bias
            kc_g = _dup_heads(kc, g).astype(BF16)
            vc_g = _dup_heads(vc, g).astype(BF16)
            s_ctx = lax.dot_general(qs, kc_g, nt, preferred_element_type=F32)
            m = jnp.maximum(m, jnp.max(s_ctx, axis=-1, keepdims=True))
        m = jnp.maximum(m, jnp.max(s_loc, axis=-1, keepdims=True))
        p_loc = jnp.exp(s_loc - m)
        den = jnp.exp(sink - m) + jnp.sum(p_loc, axis=-1, keepdims=True)
        acc = jnp.dot(p_loc.astype(BF16), vl_g, preferred_element_type=F32)
        if latent:
            p_ctx = jnp.exp(s_ctx - m)
            den = den + jnp.sum(p_ctx, axis=-1, keepdims=True)
            acc = acc + jnp.dot(p_ctx.astype(BF16), vc_g, preferred_element_type=F32)
        og = acc / den
        for j in range(A_GROUP // 2):
            outs.append(jnp.where(low, og[(2 * j) * tq:(2 * j + 1) * tq], og[(2 * j + 1) * tq:(2 * j + 2) * tq]))
    o_ref[0] = jnp.concatenate(outs, axis=1).astype(o_ref.dtype)


def attention(q, k, v, sink, kc=None, vc=None, tq=TQ_ATT):
    B, L, _ = q.shape
    latent = kc is not None
    tq = min(tq, L)
    seq = lambda wd: pl.BlockSpec((1, L, wd), lambda b, i: (b, 0, 0))
    in_specs = [
        pl.BlockSpec(memory_space=pltpu.SMEM),
        pl.BlockSpec((1, tq, A_WIDTH), lambda b, i: (b, i, 0)),
        seq(LANES), seq(LANES),
    ]
    args = [sink, q, k, v]
    if latent:
        P = kc.shape[1]
        in_specs += [pl.BlockSpec((1, P, LANES), lambda b, i: (b, 0, 0))] * 2
        args += [kc, vc]
    return pl.pallas_call(
        functools.partial(_attn_body, latent=latent, seq_len=L),
        grid=(B, L // tq),
        in_specs=in_specs,
        out_specs=pl.BlockSpec((1, tq, A_WIDTH), lambda b, i: (b, i, 0)),
        out_shape=jax.ShapeDtypeStruct((B, L, A_WIDTH), BF16),
        compiler_params=_cparams("parallel", "arbitrary"),
        name="attention_latent" if latent else "attention_context",
    )(*args)


def _split3(x):
    hi = x.astype(BF16)
    r = x - hi.astype(F32)
    mid = r.astype(BF16)
    lo = (r - mid.astype(F32)).astype(BF16)
    return hi, mid, lo


def _gla_chunk(q, k, v, la, st_ref, d, reverse):
    C = GLA_CHUNK
    W = C_WIDTH
    ti = lax.broadcasted_iota(jnp.int32, (C, C), 0)
    si = lax.broadcasted_iota(jnp.int32, (C, C), 1)
    tri = (si >= ti) if reverse else (si <= ti)
    tri_b = tri.astype(BF16)
    hi, mid, lo = _split3(la)
    b = (jnp.dot(tri_b, hi, preferred_element_type=F32) + jnp.dot(tri_b, mid, preferred_element_type=F32)
         + jnp.dot(tri_b, lo, preferred_element_type=F32))
    b_last = b[0:1] if reverse else b[C - 1:C]
    qg = q * jnp.exp(b)
    kg = (k * jnp.exp(-b)).astype(BF16)
    kd = (k * jnp.exp(b_last - b)).astype(BF16)
    decay = jnp.exp(b_last)
    vb = v.astype(BF16)
    r4 = lax.broadcasted_iota(jnp.int32, (C_HEADS * C, W), 0) // C
    c4 = lax.broadcasted_iota(jnp.int32, (C_HEADS * C, W), 1) // C_DK
    same_head = r4 == c4
    q_bd = jnp.where(same_head, jnp.concatenate([qg] * C_HEADS, axis=0), 0.0).astype(BF16)
    nt = (((1,), (1,)), ((), ()))
    a = lax.dot_general(q_bd, kg, nt, preferred_element_type=F32)
    a = jnp.where(jnp.concatenate([tri] * C_HEADS, axis=0), a, 0.0)
    r = jnp.dot(a.astype(BF16), vb, preferred_element_type=F32)
    r = jnp.where(same_head, r, 0.0)
    o = r[0:C]
    for h in range(1, C_HEADS):
        o = o + r[h * C:(h + 1) * C]
    st = st_ref[d]
    o = o + lax.dot_general(qg.astype(BF16), st.astype(BF16), nt, preferred_element_type=F32)
    ut = lax.dot_general(vb, kd, (((0,), (0,)), ((), ())), preferred_element_type=F32)
    st_ref[d] = st * decay + jnp.where(same_head, ut, 0.0)
    return o


def _gla_body(*refs, has_state, n_chunks):
    if has_state:
        (qf, kf, vf, lf, qb, kb, vb, lb, s0_ref, of_ref, ob_ref, sT_ref, st) = refs
    else:
        (qf, kf, vf, lf, qb, kb, vb, lb, of_ref, ob_ref, sT_ref, st) = refs
    j = pl.program_id(1)

    @pl.when(j == 0)
    def _init():
        if has_state:
            st[...] = s0_ref[0]
        else:
            st[...] = jnp.zeros_like(st)

    C = GLA_CHUNK

    def body(c, carry):
        rf = pl.multiple_of(c * C, C)
        rb = pl.multiple_of((n_chunks - 1 - c) * C, C)
        of_ref[0, pl.ds(rf, C), :] = _gla_chunk(qf[0, pl.ds(rf, C), :], kf[0, pl.ds(rf, C), :],
                                                vf[0, pl.ds(rf, C), :], lf[0, pl.ds(rf, C), :], st, 0, False)
        ob_ref[0, pl.ds(rb, C), :] = _gla_chunk(qb[0, pl.ds(rb, C), :], kb[0, pl.ds(rb, C), :],
                                                vb[0, pl.ds(rb, C), :], lb[0, pl.ds(rb, C), :], st, 1, True)
        return carry
    lax.fori_loop(0, n_chunks, body, 0)

    @pl.when(j == pl.num_programs(1) - 1)
    def _final():
        sT_ref[0] = st[...]


def gla(cq, ck, cv, la, s0=None, rows=512):
    B, L, W = cq.shape
    rows = min(rows, L)
    ng = L // rows
    has_state = s0 is not None
    fwd = lambda: pl.BlockSpec((1, rows, W), lambda b, j: (b, j, 0))
    bwd = lambda: pl.BlockSpec((1, rows, W), lambda b, j: (b, ng - 1 - j, 0))
    in_specs = [fwd(), fwd(), fwd(), pl.BlockSpec((1, rows, W), lambda b, j: (b, j, 0)),
                bwd(), bwd(), bwd(), pl.BlockSpec((1, rows, W), lambda b, j: (b, ng - 1 - j, 1))]
    args = [cq, ck, cv, la, cq, ck, cv, la]
    if has_state:
        in_specs.append(pl.BlockSpec((1, 2, W, W), lambda b, j: (b, 0, 0, 0)))
        args.append(s0)
    return pl.pallas_call(
        functools.partial(_gla_body, has_state=has_state, n_chunks=rows // GLA_CHUNK),
        grid=(B, ng),
        in_specs=in_specs,
        out_specs=[fwd(), bwd(), pl.BlockSpec((1, 2, W, W), lambda b, j: (b, 0, 0, 0))],
        out_shape=[jax.ShapeDtypeStruct((B, L, W), F32), jax.ShapeDtypeStruct((B, L, W), F32),
                   jax.ShapeDtypeStruct((B, 2, W, W), F32)],
        scratch_shapes=[pltpu.VMEM((2, W, W), F32)],
        compiler_params=_cparams("parallel", "arbitrary"),
        name="gla",
    )(*args)


TM_DFT = 512
BG_DFT = 4


def _conv3_body(x_ref, w_ref, b_ref, u_ref, ub_ref):
    x = x_ref[0]
    L = x.shape[0]
    row = lax.broadcasted_iota(jnp.int32, (L, 1), 0)
    prev = jnp.where(row == 0, 0.0, pltpu.roll(x, 1, 0))
    nxt = jnp.where(row == L - 1, 0.0, pltpu.roll(x, L - 1, 0))
    u = prev * w_ref[0:1, :] + x * w_ref[1:2, :] + nxt * w_ref[2:3, :] + b_ref[...]
    u_ref[0] = u
    ub_ref[0] = u.astype(BF16)


def hyena_conv3(hy, w, b):
    B, L, C3 = hy.shape
    blk = pl.BlockSpec((1, L, HY_CH), lambda b_, j: (b_, 0, j))
    return pl.pallas_call(
        _conv3_body,
        grid=(B, C3 // HY_CH),
        in_specs=[blk, pl.BlockSpec((SUBLANES, HY_CH), lambda b_, j: (0, j)),
                  pl.BlockSpec((1, HY_CH), lambda b_, j: (0, j))],
        out_specs=[blk, blk],
        out_shape=[jax.ShapeDtypeStruct(hy.shape, F32), jax.ShapeDtypeStruct(hy.shape, BF16)],
        compiler_params=_cparams("parallel", "parallel"),
        name="hyena_conv3",
    )(hy, w, b)


def _freq_weight(i, tm, n):
    k = i * tm + lax.broadcasted_iota(jnp.int32, (tm, 1), 0)
    return jnp.where(k == 0, 1.0 / n, 2.0 / n).astype(F32)


def _filt_body(fr_ref, fi_ref, sg_ref, hs_ref, hd_ref, hr_ref, hi_ref, hny_ref, *, n):
    i = pl.program_id(0)
    wk = _freq_weight(i, fr_ref.shape[0], n)
    fr = fr_ref[...]
    fi = fi_ref[...]
    hr_ref[...] = wk * (jnp.dot(fr, hs_ref[0], preferred_element_type=F32)
                        + jnp.dot(fr, hs_ref[1], preferred_element_type=F32))
    hi_ref[...] = wk * (jnp.dot(fi, hd_ref[0], preferred_element_type=F32)
                        + jnp.dot(fi, hd_ref[1], preferred_element_type=F32))

    @pl.when(i == 0)
    def _nyquist():
        sg = sg_ref[...]
        hny_ref[...] = (jnp.dot(sg, hs_ref[0], preferred_element_type=F32)
                        + jnp.dot(sg, hs_ref[1], preferred_element_type=F32)) * (1.0 / n)


def hyena_filter_spectrum(fr, fi, sg, hs, hd, tm=TM_DFT):
    L = fr.shape[0]
    W = hs.shape[2]
    tm = min(tm, L)
    whole = lambda shape: pl.BlockSpec(shape, lambda i: (0,) * len(shape))
    return pl.pallas_call(
        functools.partial(_filt_body, n=2 * L),
        grid=(L // tm,),
        in_specs=[pl.BlockSpec((tm, L), lambda i: (i, 0)), pl.BlockSpec((tm, L), lambda i: (i, 0)),
                  whole((SUBLANES, L)), whole((2, L, W)), whole((2, L, W))],
        out_specs=[pl.BlockSpec((tm, W), lambda i: (i, 0)), pl.BlockSpec((tm, W), lambda i: (i, 0)),
                   whole((SUBLANES, W))],
        out_shape=[jax.ShapeDtypeStruct((L, W), F32), jax.ShapeDtypeStruct((L, W), F32),
                   jax.ShapeDtypeStruct((SUBLANES, W), F32)],
        compiler_params=_cparams("arbitrary"),
        name="hyena_filter_spectrum",
    )(fr, fi, sg, hs, hd)


def _dft_fwd_body(fr_ref, fi_ref, sg_ref, z_ref, hr_ref, hi_ref, hny_ref, yr_ref, yi_ref, yny_ref):
    nb = z_ref.shape[0]
    hr = hr_ref[...]
    hi = hi_ref[...]

    def body(b, carry):
        zb = z_ref[b]
        xr = jnp.dot(fr_ref[...], zb, preferred_element_type=F32)
        xi = jnp.dot(fi_ref[...], zb, preferred_element_type=F32)
        yr_ref[b] = (xr * hr - xi * hi).astype(BF16)
        yi_ref[b] = (xr * hi + xi * hr).astype(BF16)
        return carry
    lax.fori_loop(0, nb, body, 0)

    @pl.when(pl.program_id(1) == 0)
    def _nyquist():
        def nyq(b, carry):
            yny_ref[b] = jnp.dot(sg_ref[...], z_ref[b], preferred_element_type=F32) * hny_ref[0:1, :]
            return carry
        lax.fori_loop(0, nb, nyq, 0)


def hyena_dft_fwd(fr, fi, sg, zb, zcol, hr, hi, hny, order, tm=TM_DFT, bg=BG_DFT):
    B, L, _ = zb.shape
    C = HY_CH
    tm = min(tm, L)
    bg = B if L < TM_DFT else bg
    ftile = pl.BlockSpec((tm, L), lambda g, i: (i, 0))
    ytile = pl.BlockSpec((bg, tm, C), lambda g, i: (g, i, 0))
    return pl.pallas_call(
        _dft_fwd_body,
        grid=(B // bg, L // tm),
        in_specs=[ftile, ftile, pl.BlockSpec((SUBLANES, L), lambda g, i: (0, 0)),
                  pl.BlockSpec((bg, L, C), lambda g, i: (g, 0, zcol)),
                  pl.BlockSpec((tm, C), lambda g, i: (i, order)), pl.BlockSpec((tm, C), lambda g, i: (i, order)),
                  pl.BlockSpec((SUBLANES, C), lambda g, i: (0, order))],
        out_specs=[ytile, ytile, pl.BlockSpec((bg, SUBLANES, C), lambda g, i: (g, 0, 0))],
        out_shape=[jax.ShapeDtypeStruct((B, L, C), BF16), jax.ShapeDtypeStruct((B, L, C), BF16),
                   jax.ShapeDtypeStruct((B, SUBLANES, C), F32)],
        compiler_params=_cparams("parallel", "arbitrary"),
        name="hyena_dft_fwd",
    )(fr, fi, sg, zb, hr, hi, hny)


def _dft_inv_body(fr_ref, fi_ref, yr_ref, yi_ref, yny_ref, z_ref, gate_ref, d_ref, zo_ref, zob_ref):
    nb = z_ref.shape[0]
    tm = fr_ref.shape[0]
    t = pl.program_id(1) * tm + lax.broadcasted_iota(jnp.int32, (tm, 1), 0)
    sign = jnp.where((t & 1) == 0, 1.0, -1.0).astype(F32)
    d = d_ref[...]

    def body(b, carry):
        conv = (jnp.dot(fr_ref[...], yr_ref[b], preferred_element_type=F32)
                + jnp.dot(fi_ref[...], yi_ref[b], preferred_element_type=F32) + sign * yny_ref[b][0:1, :])
        zn = gate_ref[b] * (conv + d * z_ref[b])
        zo_ref[b] = zn
        zob_ref[b] = zn.astype(BF16)
        return carry
    lax.fori_loop(0, nb, body, 0)


def hyena_dft_inv(fr, fi, yr, yi, yny, z, zcol, gate, gcol, d, tm=TM_DFT, bg=BG_DFT):
    B, L, C = yr.shape
    tm = min(tm, L)
    bg = B if L < TM_DFT else bg
    ftile = pl.BlockSpec((tm, L), lambda g, i: (i, 0))
    whole_y = lambda: pl.BlockSpec((bg, L, C), lambda g, i: (g, 0, 0), pipeline_mode=pl.Buffered(1))
    otile = pl.BlockSpec((bg, tm, C), lambda g, i: (g, i, 0))
    return pl.pallas_call(
        _dft_inv_body,
        grid=(B // bg, L // tm),
        in_specs=[ftile, ftile, whole_y(), whole_y(),
                  pl.BlockSpec((bg, SUBLANES, C), lambda g, i: (g, 0, 0)),
                  pl.BlockSpec((bg, tm, C), lambda g, i: (g, i, zcol)),
                  pl.BlockSpec((bg, tm, C), lambda g, i: (g, i, gcol)),
                  pl.BlockSpec((1, C), lambda g, i: (0, 0))],
        out_specs=[otile, otile],
        out_shape=[jax.ShapeDtypeStruct((B, L, C), F32), jax.ShapeDtypeStruct((B, L, C), BF16)],
        compiler_params=_cparams("parallel", "arbitrary"),
        name="hyena_dft_inv",
    )(fr, fi, yr, yi, yny, z, gate, d)


def dft_matrices(L):
    k = jnp.arange(L, dtype=jnp.int32)
    m = (k[:, None] * k[None, :]) % (2 * L)
    ang = m.astype(F32) * (math.pi / L)
    sg = jnp.zeros((SUBLANES, L), F32).at[0].set(jnp.where(k % 2 == 0, 1.0, -1.0))
    return jnp.cos(ang).astype(BF16), (-jnp.sin(ang)).astype(BF16), sg.astype(BF16)


def hyena_time_filters(L, lp):
    t = jnp.arange(L, dtype=F32)
    t_norm = t / max(L - 1, 1)
    w = (2.0 * math.pi / L) * t
    f = jnp.linspace(1e-4, HY_BANDS - 1, HY_BANDS, dtype=F32)
    fw = w[:, None] * f[None, :]
    feat = jnp.concatenate([t_norm[:, None], jnp.cos(fw), -jnp.sin(fw)], axis=-1)
    z = jnp.sin(lp['hy_freq1'] * (feat @ lp['hy_w1'] + lp['hy_b1']))
    z = jnp.sin(lp['hy_freq2'] * (z @ lp['hy_w2'] + lp['hy_b2']))
    hf = (z @ lp['hy_w3']).astype(F32).reshape(L, 2, HY_ORDER, HY_CH)
    hf = hf * jnp.exp(-t_norm[:, None, None, None] * jnp.abs(lp['hy_decay'].astype(F32)))
    return hf / (jnp.sum(jnp.abs(hf), axis=(0, 1), keepdims=True) + EPS)


def _hi_lo(x):
    hi = x.astype(BF16)
    return jnp.stack([hi, (x - hi.astype(F32)).astype(BF16)])


def hyena(hy, lp, mats):
    B, L, _ = hy.shape
    fr, fi, sg = mats
    hf = hyena_time_filters(L, lp)
    h_fwd = hf[:, 0].reshape(L, HY_ORDER * HY_CH)
    h_bwd = hf[:, 1].at[0].set(0.0).reshape(L, HY_ORDER * HY_CH)
    hr, hi, hny = hyena_filter_spectrum(fr, fi, sg, _hi_lo(h_fwd + h_bwd), _hi_lo(h_fwd - h_bwd))
    w = jnp.pad(lp['hy_conv_w'], ((0, SUBLANES - 3), (0, 0)))
    u, ub = hyena_conv3(hy, w, lp['hy_conv_b'][None, :])
    d = lp['hy_d'].astype(F32)
    z, zb, zcol = u, ub, 0
    for o in range(HY_ORDER):
        yr, yi, yny = hyena_dft_fwd(fr, fi, sg, zb, zcol, hr, hi, hny, o)
        z, zb = hyena_dft_inv(fr, fi, yr, yi, yny, z, zcol, u, 1 + o, d[o][None, :])
        zcol = 0
    return z


def _ffn_body(x_ref, mod_ref, g_ref, w1_ref, w3_ref, w2_ref, o_ref):
    x = x_ref[...]
    h = _adaln_rows(x, g_ref[...], mod_ref[0, ROW_SHIFT2:ROW_SHIFT2 + 1, :],
                    mod_ref[0, ROW_SCALE2:ROW_SCALE2 + 1, :]).astype(BF16)
    a = jnp.dot(h, w1_ref[...], preferred_element_type=F32)
    b = jnp.dot(h, w3_ref[...], preferred_element_type=F32)
    act = (_silu(a) * b).astype(BF16)
    ff = jnp.dot(act, w2_ref[...], preferred_element_type=F32)
    o_ref[...] = x + mod_ref[0, ROW_GATE2:ROW_GATE2 + 1, :] * ff


def ffn_dense(x, mods, g, w1, w3, w2, seg_len, tm=TM_PROJ):
    T, D = x.shape
    F = w1.shape[1]
    resident = functools.partial(pl.BlockSpec, pipeline_mode=pl.Buffered(1))
    return pl.pallas_call(
        _ffn_body,
        grid=(T // tm,),
        in_specs=[
            pl.BlockSpec((tm, D), lambda i: (i, 0)),
            pl.BlockSpec((1, MOD_ROWS, D), lambda i: (i * tm // seg_len, 0, 0)),
            pl.BlockSpec((1, D), lambda i: (0, 0)),
            resident((D, F), lambda i: (0, 0)),
            resident((D, F), lambda i: (0, 0)),
            resident((F, D), lambda i: (0, 0)),
        ],
        out_specs=pl.BlockSpec((tm, D), lambda i: (i, 0)),
        out_shape=jax.ShapeDtypeStruct((T, D), F32),
        compiler_params=_cparams("parallel"),
        name="ffn_dense",
    )(x, mods, g, w1, w3, w2)


def _router_body(x_ref, mod_ref, g_ref, wr_ref, h_ref, idx_ref, gw_ref):
    h = _adaln_rows(x_ref[...], g_ref[...], mod_ref[0, ROW_SHIFT2:ROW_SHIFT2 + 1, :],
                    mod_ref[0, ROW_SCALE2:ROW_SCALE2 + 1, :])
    h_ref[...] = h
    logits = lax.dot_general(wr_ref[...], h, (((1,), (1,)), ((), ())),
                             precision=lax.Precision.HIGHEST, preferred_element_type=F32)
    eidx = lax.broadcasted_iota(jnp.int32, logits.shape, 0)
    m1 = jnp.max(logits, axis=0, keepdims=True)
    i1 = jnp.min(jnp.where(logits == m1, eidx, N_EXPERTS), axis=0, keepdims=True)
    rest = jnp.where(eidx == i1, -jnp.inf, logits)
    m2 = jnp.max(rest, axis=0, keepdims=True)
    i2 = jnp.min(jnp.where(rest == m2, eidx, N_EXPERTS), axis=0, keepdims=True)
    e2 = jnp.exp(m2 - m1)
    den = 1.0 + e2
    row = lax.broadcasted_iota(jnp.int32, logits.shape, 0)
    idx_ref[...] = jnp.where(row == 0, i1, jnp.where(row == 1, i2, 0))
    gw_ref[...] = jnp.where(row == 0, 1.0 / den, jnp.where(row == 1, e2 / den, 0.0))


def moe_router(x, mods, g, wr_t, seg_len, tm=TM_PROJ):
    T, D = x.shape
    return pl.pallas_call(
        _router_body,
        grid=(T // tm,),
        in_specs=[
            pl.BlockSpec((tm, D), lambda i: (i, 0)),
            pl.BlockSpec((1, MOD_ROWS, D), lambda i: (i * tm // seg_len, 0, 0)),
            pl.BlockSpec((1, D), lambda i: (0, 0)),
            pl.BlockSpec((N_EXPERTS, D), lambda i: (0, 0)),
        ],
        out_specs=[
            pl.BlockSpec((tm, D), lambda i: (i, 0)),
            pl.BlockSpec((N_EXPERTS, tm), lambda i: (0, i)),
            pl.BlockSpec((N_EXPERTS, tm), lambda i: (0, i)),
        ],
        out_shape=[
            jax.ShapeDtypeStruct((T, D), F32),
            jax.ShapeDtypeStruct((N_EXPERTS, T), jnp.int32),
            jax.ShapeDtypeStruct((N_EXPERTS, T), F32),
        ],
        compiler_params=_cparams("parallel"),
        name="moe_router",
    )(x, mods, g, wr_t)


def _row_copy(src_hbm, row, dst, r, sem):
    return pltpu.make_async_copy(src_hbm.at[pl.ds(row, 1)], dst.at[pl.ds(r, 1)], sem)


def _experts_body(te_ref, nv_ref, src_ref, gate_ref, h_hbm, w1_ref, w3_ref, w2_ref, y_ref,
                  hrows, hb, acc, sem, *, tm, nf):
    i = pl.program_id(0)
    f = pl.program_id(1)
    valid = i < nv_ref[0]

    @pl.when(jnp.logical_and(valid, f == 0))
    def _gather():
        def issue(r, carry):
            _row_copy(h_hbm, src_ref[0, 0, r], hrows, r, sem).start()
            return carry
        lax.fori_loop(0, tm, issue, 0)

        def wait(r, carry):
            _row_copy(h_hbm, 0, hrows, r, sem).wait()
            return carry
        lax.fori_loop(0, tm, wait, 0)
        hb[...] = hrows[...].astype(BF16)
        acc[...] = jnp.zeros_like(acc)

    @pl.when(valid)
    def _compute():
        h = hb[...]
        a = jnp.dot(h, w1_ref[0].astype(BF16), preferred_element_type=F32)
        b = jnp.dot(h, w3_ref[0].astype(BF16), preferred_element_type=F32)
        act = (_silu(a) * b).astype(BF16)
        acc[...] += jnp.dot(act, w2_ref[0].astype(BF16), preferred_element_type=F32)

    @pl.when(jnp.logical_and(valid, f == nf - 1))
    def _emit():
        y_ref[...] = acc[...] * gate_ref[...]

    @pl.when(jnp.logical_and(jnp.logical_not(valid), f == nf - 1))
    def _emit_unused():
        y_ref[...] = jnp.zeros_like(y_ref)


def moe_experts(h, tile_expert, n_valid, src_rows, gate_rows, w1, w3, w2, tm=TM_MOE, tf=TF_MOE):
    T, D = h.shape
    F = w1.shape[2]
    n_tiles = src_rows.shape[0]
    nf = F // tf

    def wcol(i, f, te, nv):
        return (te[i], 0, jnp.where(i < nv[0], f, nf - 1))

    def wrow(i, f, te, nv):
        return (te[i], jnp.where(i < nv[0], f, nf - 1), 0)

    grid_spec = pltpu.PrefetchScalarGridSpec(
        num_scalar_prefetch=2,
        grid=(n_tiles, nf),
        in_specs=[
            pl.BlockSpec((1, 1, tm), lambda i, f, te, nv: (i, 0, 0), memory_space=pltpu.SMEM),
            pl.BlockSpec((tm, 1), lambda i, f, te, nv: (i, 0)),
            pl.BlockSpec(memory_space=pl.ANY),
            pl.BlockSpec((1, D, tf), wcol),
            pl.BlockSpec((1, D, tf), wcol),
            pl.BlockSpec((1, tf, D), wrow),
        ],
        out_specs=pl.BlockSpec((tm, D), lambda i, f, te, nv: (i, 0)),
        scratch_shapes=[
            pltpu.VMEM((tm, D), F32),
            pltpu.VMEM((tm, D), BF16),
            pltpu.VMEM((tm, D), F32),
            pltpu.SemaphoreType.DMA(()),
        ],
    )
    return pl.pallas_call(
        functools.partial(_experts_body, tm=tm, nf=nf),
        grid_spec=grid_spec,
        out_shape=jax.ShapeDtypeStruct((n_tiles * tm, D), F32),
        compiler_params=_cparams("arbitrary", "arbitrary"),
        name="moe_experts",
    )(tile_expert, n_valid, src_rows, gate_rows, h, w1, w3, w2)


def _combine_body(pos_ref, x_ref, mod_ref, y_hbm, o_ref, buf0, buf1, sem, *, tm):
    def issue(r, carry):
        _row_copy(y_hbm, pos_ref[0, 0, r], buf0, r, sem).start()
        _row_copy(y_hbm, pos_ref[0, 0, tm + r], buf1, r, sem).start()
        return carry
    lax.fori_loop(0, tm, issue, 0)

    def wait(r, carry):
        _row_copy(y_hbm, 0, buf0, r, sem).wait()
        _row_copy(y_hbm, 0, buf1, r, sem).wait()
        return carry
    lax.fori_loop(0, tm, wait, 0)
    o_ref[...] = x_ref[...] + mod_ref[0, ROW_GATE2:ROW_GATE2 + 1, :] * (buf0[...] + buf1[...])


def moe_combine(x, mods, y, pos_tiles, seg_len, tm=TM_COMB):
    T, D = x.shape
    return pl.pallas_call(
        functools.partial(_combine_body, tm=tm),
        grid=(T // tm,),
        in_specs=[
            pl.BlockSpec((1, 1, 2 * tm), lambda i: (i, 0, 0), memory_space=pltpu.SMEM),
            pl.BlockSpec((tm, D), lambda i: (i, 0)),
            pl.BlockSpec((1, MOD_ROWS, D), lambda i: (i * tm // seg_len, 0, 0)),
            pl.BlockSpec(memory_space=pl.ANY),
        ],
        out_specs=pl.BlockSpec((tm, D), lambda i: (i, 0)),
        out_shape=jax.ShapeDtypeStruct((T, D), F32),
        scratch_shapes=[
            pltpu.VMEM((tm, D), F32),
            pltpu.VMEM((tm, D), F32),
            pltpu.SemaphoreType.DMA(()),
        ],
        compiler_params=_cparams("arbitrary"),
        name="moe_combine",
    )(pos_tiles, x, mods, y)


def moe_dispatch_plan(idx, gw, tm=TM_MOE, tmc=TM_COMB):
    T = idx.shape[1]
    n_slots = TOP_K * T
    n_tiles = n_slots // tm + N_EXPERTS
    n_rows = n_tiles * tm
    e_flat = idx[:TOP_K].reshape(n_slots)
    g_flat = gw[:TOP_K].reshape(n_slots)
    onehot = (e_flat[:, None] == jnp.arange(N_EXPERTS, dtype=jnp.int32)[None, :]).astype(jnp.int32)
    csum = jnp.cumsum(onehot, axis=0)
    counts = csum[-1]
    rank = jnp.sum((csum - onehot) * onehot, axis=1)
    padded = (counts + tm - 1) // tm * tm
    ends = jnp.cumsum(padded)
    offs = ends - padded
    pos = offs[e_flat] + rank
    order = jnp.argsort(e_flat, stable=True).astype(jnp.int32)
    cstart = jnp.cumsum(counts) - counts
    rows = jnp.arange(n_rows, dtype=jnp.int32)
    row_e = jnp.minimum(jnp.searchsorted(ends, rows, side='right'), N_EXPERTS - 1).astype(jnp.int32)
    j = rows - offs[row_e]
    live = j < counts[row_e]
    slot = order[jnp.clip(cstart[row_e] + j, 0, n_slots - 1)]
    src_rows = jnp.where(live, slot % T, 0).astype(jnp.int32).reshape(n_tiles, 1, tm)
    gate_rows = jnp.where(live, g_flat[slot], 0.0).reshape(n_rows, 1)
    tile_start = jnp.arange(n_tiles, dtype=jnp.int32) * tm
    n_valid = (ends[-1] // tm).astype(jnp.int32).reshape(1)
    tile_expert = jnp.minimum(jnp.searchsorted(ends, tile_start, side='right'), N_EXPERTS - 1)
    last_e = tile_expert[jnp.maximum(n_valid[0] - 1, 0)]
    tile_expert = jnp.where(tile_start < ends[-1], tile_expert, last_e).astype(jnp.int32)
    pos2 = pos.reshape(TOP_K, T // tmc, 1, tmc)
    pos_tiles = jnp.concatenate([pos2[0], pos2[1]], axis=-1).astype(jnp.int32)
    return tile_expert, n_valid, src_rows, gate_rows, pos_tiles


def ffn_moe(x, mods, g, wr_t, w1, w3, w2, seg_len, tm=TM_PROJ, tme=TM_MOE, tmc=TM_COMB):
    h, idx, gw = moe_router(x, mods, g, wr_t, seg_len, tm)
    tile_expert, n_valid, src_rows, gate_rows, pos_tiles = moe_dispatch_plan(idx, gw, tme, tmc)
    y = moe_experts(h, tile_expert, n_valid, src_rows, gate_rows, w1, w3, w2, tme)
    return moe_combine(x, mods, y, pos_tiles, seg_len, tmc)


def axial_rope(L):
    rows = L // GRID_W
    r = jnp.repeat(jnp.arange(rows, dtype=F32), GRID_W)
    col = jnp.tile(jnp.arange(GRID_W, dtype=F32), rows)
    n = HEAD_DIM // 4
    freqs = ROPE_BASE ** (-jnp.arange(n, dtype=F32) / n)
    ang = jnp.concatenate([r[:, None] * freqs, col[:, None] * freqs], axis=-1)
    return jnp.cos(ang), jnp.sin(ang)


def rope_tables(L):
    cos, sin = axial_rope(L)
    n = HEAD_DIM // 4
    cos_h = jnp.concatenate([cos[:, :n], cos[:, :n], cos[:, n:], cos[:, n:]], axis=1)
    sin_h = jnp.concatenate([-sin[:, :n], sin[:, :n], -sin[:, n:], sin[:, n:]], axis=1)
    reps = LANES // HEAD_DIM
    return jnp.tile(cos_h, (1, reps)), jnp.tile(sin_h, (1, reps))


def block_diag_ones(width, block):
    i = jnp.arange(width) // block
    return (i[:, None] == i[None, :]).astype(BF16)


def gla_gate_params(gw, gb):
    w = jnp.zeros((LANES, 2 * C_WIDTH), F32)
    w = w.at[:GLA_RANK, :C_WIDTH].set(gw[0]).at[GLA_RANK:2 * GLA_RANK, C_WIDTH:].set(gw[1])
    return w.astype(BF16), jnp.concatenate([gb[0], gb[1]])[None, :]


def gla_state_to_blockdiag(st):
    B = st.shape[0]
    out = jnp.zeros((B, 2, C_HEADS, C_DV, C_HEADS, C_DK), F32)
    for h in range(C_HEADS):
        out = out.at[:, :, h, :, h, :].set(jnp.swapaxes(st[:, :, h], -1, -2))
    return out.reshape(B, 2, C_WIDTH, C_WIDTH)


def gla_state_from_blockdiag(sT):
    B = sT.shape[0]
    s6 = sT.reshape(B, 2, C_HEADS, C_DV, C_HEADS, C_DK)
    return jnp.stack([jnp.swapaxes(s6[:, :, h, :, h, :], -1, -2) for h in range(C_HEADS)], axis=2)


def mod_table(cvec, w_ada, b_ada):
    m = jax.nn.silu(cvec) @ w_ada + b_ada
    m = m.reshape(cvec.shape[0], 6, D_MODEL)
    return jnp.pad(m, ((0, 0), (0, MOD_ROWS - 6), (0, 0)))


def kernel(x_prompt, x_sample, cache_k, cache_v, state_gla, c, c_ctx, norm1_g, norm2_g, w_ada, b_ada,
           w_in, w_out, q_norm_g, k_norm_g, attn_sink, hy_conv_w, hy_conv_b, hy_w1, hy_b1, hy_freq1,
           hy_w2, hy_b2, hy_freq2, hy_w3, hy_decay, hy_d, gla_gate_w, gla_gate_b, gla_norm_g,
           ffn_w1, ffn_w3, ffn_w2, moe_router, moe_w1, moe_w3, moe_w2):
    D = D_MODEL
    xp = x_prompt.reshape(BATCH * SEQ, D)
    xs = x_sample.reshape(DEC_BATCH * DEC_SEQ, D)
    streams = [
        dict(x=xp, B=BATCH, L=SEQ, seg_len=BATCH * SEQ, cvec=c_ctx[None, :], latent=False),
        dict(x=xs, B=DEC_BATCH, L=DEC_SEQ, seg_len=DEC_SEQ, cvec=c, latent=True),
    ]
    ks_list, vs_list, st_list = [], [], []
    bd_q = block_diag_ones(A_WIDTH, HEAD_DIM)
    dft = {s['L']: dft_matrices(s['L']) for s in streams}
    for l in range(DEPTH):
        lp = {
            'hy_conv_w': hy_conv_w[l], 'hy_conv_b': hy_conv_b[l],
            'hy_w1': hy_w1[l], 'hy_b1': hy_b1[l], 'hy_freq1': hy_freq1[l], 'hy_w2': hy_w2[l],
            'hy_b2': hy_b2[l], 'hy_freq2': hy_freq2[l], 'hy_w3': hy_w3[l], 'hy_decay': hy_decay[l],
            'hy_d': hy_d[l], 'gla_gate_w': gla_gate_w[l], 'gla_gate_b': gla_gate_b[l],
            'gla_norm_g': gla_norm_g[l],
        }
        w_in_l = jnp.pad(w_in[l], ((0, 0), (0, D_PROJ_PAD - D_PROJ))).astype(BF16)
        w_out_l = w_out[l].astype(BF16)
        g1 = norm1_g[l][None, :]
        g2 = norm2_g[l][None, :]
        qg = jnp.tile(q_norm_g[l], A_HEADS)[None, :]
        kg = jnp.tile(k_norm_g[l], A_KV_HEADS)[None, :]
        gg = jnp.tile(gla_norm_g[l], C_HEADS)[None, :]
        gate_w, gate_b = gla_gate_params(gla_gate_w[l], gla_gate_b[l])
        j = l // 2
        for s in streams:
            B, L = s['B'], s['L']
            mods = mod_table(s['cvec'], w_ada[l], b_ada[l])
            rope = rope_tables(L) if s['latent'] else None
            q, k, v, hy, cq, ck, cv, cg, la = in_proj(s['x'], mods, g1, w_in_l, bd_q, qg, kg, gate_w, gate_b,
                                                      s['seg_len'], rope, L)
            seq = lambda t: t.reshape(B, L, t.shape[-1])
            if s['latent']:
                kc = cache_k[:, l].reshape(DEC_BATCH, PAST_LEN, LANES)
                vc = cache_v[:, l].reshape(DEC_BATCH, PAST_LEN, LANES)
                a_out = attention(seq(q), seq(k), seq(v), attn_sink[l], kc, vc)
                s0 = gla_state_to_blockdiag(state_gla[:, l])
                o_f, o_b, _ = gla(seq(cq), seq(ck), seq(cv), seq(la), s0)
            else:
                a_out = attention(seq(q), seq(k), seq(v), attn_sink[l])
                o_f, o_b, sT = gla(seq(cq), seq(ck), seq(cv), seq(la))
                ks_list.append(k.reshape(B, L, A_KV_HEADS, HEAD_DIM))
                vs_list.append(v.reshape(B, L, A_KV_HEADS, HEAD_DIM))
                st_list.append(gla_state_from_blockdiag(sT))
            h_out = hyena(seq(hy), lp, dft[L])
            flat = lambda t: t.reshape(B * L, t.shape[-1])
            x1 = out_proj(s['x'], flat(a_out), flat(h_out), flat(o_f), flat(o_b), cg, mods, w_out_l,
                          bd_q[:C_WIDTH, :C_WIDTH], gg, s['seg_len'])
            if l % 2 == 0:
                s['x'] = ffn_dense(x1, mods, g2, ffn_w1[j].astype(BF16), ffn_w3[j].astype(BF16),
                                   ffn_w2[j].astype(BF16), s['seg_len'])
            else:
                s['x'] = ffn_moe(x1, mods, g2, moe_router[j].T, moe_w1[j], moe_w3[j], moe_w2[j],
                                 s['seg_len'])
    y_prompt = streams[0]['x'].reshape(BATCH, SEQ, D)
    y_sample = streams[1]['x'].reshape(DEC_BATCH, DEC_SEQ, D)
    new_cache_k = jnp.stack(ks_list, axis=1)
    new_cache_v = jnp.stack(vs_list, axis=1)
    new_state_gla = jnp.stack(st_list, axis=1)
    return (y_prompt, y_sample, new_cache_k, new_cache_v, new_state_gla)
```

```python
import math
import functools
import jax
import jax.numpy as jnp
from jax import lax
import numpy as np
from jax.experimental import pallas as pl
from jax.experimental.pallas import tpu as pltpu

D_MODEL = 1024
BATCH = 32
SEQ = 256
DEPTH = 2
DEC_BATCH = 8
DEC_SEQ = 4096
PAST_LEN = 512

GRID_W = 64
HEAD_DIM = 64
D_MIX = D_MODEL
A_HEADS = 8
A_KV_HEADS = 2
A_GROUP = A_HEADS // A_KV_HEADS
A_WIDTH = A_HEADS * HEAD_DIM
WINDOW = 128
BLK = 128
ROPE_BASE = 10000.0
HY_CH = 256
HY_ORDER = 2
HY_BANDS = 16
HY_EMB = 1 + 2 * HY_BANDS
HY_HID = 64
C_HEADS = 4
C_DK = 64
C_DV = 64
C_WIDTH = C_HEADS * C_DV
GLA_RANK = 16
GLA_TAU = 16.0
GLA_CHUNK = 64
D_FF = 2816
N_EXPERTS = 8
TOP_K = 2
D_FF_EXPERT = 3584

PROJ_SIZES = (A_WIDTH, A_KV_HEADS * HEAD_DIM, A_KV_HEADS * HEAD_DIM, 3 * HY_CH,
              C_HEADS * C_DK, C_HEADS * C_DK, C_WIDTH, C_WIDTH, 2 * GLA_RANK)
PROJ_SPLITS = tuple(int(s) for s in np.cumsum(PROJ_SIZES)[:-1])
D_PROJ = int(sum(PROJ_SIZES))

F32 = jnp.float32
BF16 = jnp.bfloat16
ATT_SCALE = HEAD_DIM ** -0.5
NEG_INF = -1e30
EPS = 1e-6

LANES = 128
SUBLANES = 8
D_PROJ_PAD = -(-D_PROJ // LANES) * LANES
VMEM_LIMIT_BYTES = 56 * 1024 * 1024

MOD_ROWS = SUBLANES
ROW_SHIFT1, ROW_SCALE1, ROW_GATE1, ROW_SHIFT2, ROW_SCALE2, ROW_GATE2 = range(6)

TM_PROJ = 512
TM_MOE = 1024
TF_MOE = 896
TM_COMB = 512


def _cparams(*sem):
    return pltpu.CompilerParams(dimension_semantics=sem, vmem_limit_bytes=VMEM_LIMIT_BYTES)


def _adaln_rows(x, g, shift, scale):
    ms = jnp.mean(x * x, axis=-1, keepdims=True)
    return (x * lax.rsqrt(ms + EPS) * g) * (1.0 + scale) + shift


def _silu(a):
    return a * jax.nn.sigmoid(a)


def _group_mean_sq(x, ones_bd):
    sq = x * x
    hi = sq.astype(BF16)
    lo = (sq - hi.astype(F32)).astype(BF16)
    s = jnp.dot(hi, ones_bd, preferred_element_type=F32) + jnp.dot(lo, ones_bd, preferred_element_type=F32)
    return s * (1.0 / HEAD_DIM)


def _rope_rows(x, cos_t, sin_t):
    q4 = HEAD_DIM // 4
    lane = lax.broadcasted_iota(jnp.int32, x.shape, 1)
    partner = jnp.where((lane % (2 * q4)) < q4, pltpu.roll(x, LANES - q4, 1), pltpu.roll(x, q4, 1))
    return x * cos_t + partner * sin_t


def _log_sigmoid(x):
    return jnp.minimum(x, 0.0) - jnp.log(1.0 + jnp.exp(-jnp.abs(x)))


def _inproj_body(*refs, latent):
    if latent:
        (x_ref, mod_ref, g_ref, w_ref, bd_ref, qg_ref, kg_ref, gw_ref, gb_ref, cos_ref, sin_ref,
         q_ref, k_ref, v_ref, hy_ref, cq_ref, ck_ref, cv_ref, cg_ref, la_ref) = refs
    else:
        (x_ref, mod_ref, g_ref, w_ref, bd_ref, qg_ref, kg_ref, gw_ref, gb_ref,
         q_ref, k_ref, v_ref, hy_ref, cq_ref, ck_ref, cv_ref, cg_ref, la_ref) = refs
    h = _adaln_rows(x_ref[...], g_ref[...], mod_ref[0, ROW_SHIFT1:ROW_SHIFT1 + 1, :],
                    mod_ref[0, ROW_SCALE1:ROW_SCALE1 + 1, :])
    acc = jnp.dot(h.astype(BF16), w_ref[...], preferred_element_type=F32)
    o = 0
    q = acc[:, o:o + A_WIDTH]
    o += A_WIDTH
    k = acc[:, o:o + LANES]
    o += LANES
    v_ref[...] = acc[:, o:o + LANES]
    o += LANES
    hy_ref[...] = acc[:, o:o + 3 * HY_CH]
    o += 3 * HY_CH
    cq_ref[...] = acc[:, o:o + C_WIDTH] * (C_DK ** -0.5)
    o += C_WIDTH
    ck_ref[...] = acc[:, o:o + C_WIDTH]
    o += C_WIDTH
    cv_ref[...] = acc[:, o:o + C_WIDTH]
    o += C_WIDTH
    cg_ref[...] = acc[:, o:o + C_WIDTH]
    o += C_WIDTH
    r = acc[:, o:o + LANES]
    la_ref[...] = _log_sigmoid(jnp.dot(r.astype(BF16), gw_ref[...], preferred_element_type=F32)
                               + gb_ref[...]) * (1.0 / GLA_TAU)
    q = q * lax.rsqrt(_group_mean_sq(q, bd_ref[...]) + EPS) * qg_ref[...]
    k = k * lax.rsqrt(_group_mean_sq(k, bd_ref[0:LANES, 0:LANES]) + EPS) * kg_ref[...]
    if latent:
        cos_t = cos_ref[...]
        sin_t = sin_ref[...]
        q = jnp.concatenate([_rope_rows(q[:, j * LANES:(j + 1) * LANES], cos_t, sin_t)
                             for j in range(A_WIDTH // LANES)], axis=1)
        k = _rope_rows(k, cos_t, sin_t)
    q_ref[...] = (q * ATT_SCALE).astype(BF16)
    k_ref[...] = k


def in_proj(x, mods, g, w, bd, qg, kg, gw, gb, seg_len, rope=None, seq_len=None, tm=TM_PROJ):
    T, D = x.shape
    N = w.shape[1]
    latent = rope is not None
    const = lambda shape: pl.BlockSpec(shape, lambda i: (0,) * len(shape))
    in_specs = [
        pl.BlockSpec((tm, D), lambda i: (i, 0)),
        pl.BlockSpec((1, MOD_ROWS, D), lambda i: (i * tm // seg_len, 0, 0)),
        const((1, D)), const((D, N)), const((A_WIDTH, A_WIDTH)), const((1, A_WIDTH)), const((1, LANES)),
        const((LANES, 2 * C_WIDTH)), const((1, 2 * C_WIDTH)),
    ]
    args = [x, mods, g, w, bd, qg, kg, gw, gb]
    if latent:
        tiles_per_seq = seq_len // tm
        in_specs += [pl.BlockSpec((tm, LANES), lambda i: (i % tiles_per_seq, 0))] * 2
        args += list(rope)
    widths = [A_WIDTH, LANES, LANES, 3 * HY_CH, C_WIDTH, C_WIDTH, C_WIDTH, C_WIDTH, 2 * C_WIDTH]
    dtypes = [BF16] + [F32] * 8
    return pl.pallas_call(
        functools.partial(_inproj_body, latent=latent),
        grid=(T // tm,),
        in_specs=in_specs,
        out_specs=[pl.BlockSpec((tm, wd), lambda i: (i, 0)) for wd in widths],
        out_shape=[jax.ShapeDtypeStruct((T, wd), dt) for wd, dt in zip(widths, dtypes)],
        compiler_params=_cparams("parallel"),
        name="in_proj_latent" if latent else "in_proj_context",
    )(*args)


def _outproj_body(x_ref, a_ref, h_ref, of_ref, ob_ref, cg_ref, mod_ref, w_ref, bd_ref, gg_ref, o_ref):
    o = of_ref[...] + ob_ref[...]
    g_out = o * lax.rsqrt(_group_mean_sq(o, bd_ref[...]) + EPS) * gg_ref[...] * _silu(cg_ref[...])
    h0 = A_WIDTH
    g0 = A_WIDTH + HY_CH
    mix = jnp.dot(a_ref[...], w_ref[0:h0, :], preferred_element_type=F32)
    mix += jnp.dot(h_ref[...].astype(BF16), w_ref[h0:g0, :], preferred_element_type=F32)
    mix += jnp.dot(g_out.astype(BF16), w_ref[g0:, :], preferred_element_type=F32)
    o_ref[...] = x_ref[...] + mod_ref[0, ROW_GATE1:ROW_GATE1 + 1, :] * mix


def out_proj(x, a, h, of, ob, cg, mods, w, bd, gg, seg_len, tm=TM_PROJ):
    T, D = x.shape
    row = lambda wd: pl.BlockSpec((tm, wd), lambda i: (i, 0))
    const = lambda shape: pl.BlockSpec(shape, lambda i: (0,) * len(shape))
    return pl.pallas_call(
        _outproj_body,
        grid=(T // tm,),
        in_specs=[
            row(D), row(A_WIDTH), row(HY_CH), row(C_WIDTH), row(C_WIDTH), row(C_WIDTH),
            pl.BlockSpec((1, MOD_ROWS, D), lambda i: (i * tm // seg_len, 0, 0)),
            const((D, D)), const((C_WIDTH, C_WIDTH)), const((1, C_WIDTH)),
        ],
        out_specs=row(D),
        out_shape=jax.ShapeDtypeStruct((T, D), F32),
        compiler_params=_cparams("parallel"),
        name="out_proj",
    )(x, a, h, of, ob, cg, mods, w, bd, gg)


TQ_ATT = 256
KWIN_ATT = TQ_ATT + 2 * WINDOW


def _dup_heads(x, g):
    lane = lax.broadcasted_iota(jnp.int32, x.shape, 1)
    rolled = pltpu.roll(x, HEAD_DIM, 1)
    keep = (lane < HEAD_DIM) if g == 0 else (lane >= HEAD_DIM)
    return jnp.where(keep, x, rolled)


def _attn_body(*refs, latent, seq_len):
    if latent:
        sink_ref, q_ref, k_ref, v_ref, kc_ref, vc_ref, o_ref = refs
    else:
        sink_ref, q_ref, k_ref, v_ref, o_ref = refs
    tq = q_ref.shape[1]
    i = pl.program_id(1)
    q = q_ref[0]
    lane = lax.broadcasted_iota(jnp.int32, (tq, LANES), 1)
    low = lane < HEAD_DIM
    if latent:
        start = jnp.clip(i * tq - WINDOW, 0, seq_len - KWIN_ATT)
        start = pl.multiple_of(start, WINDOW)
        kl = k_ref[0, pl.ds(start, KWIN_ATT), :]
        vl = v_ref[0, pl.ds(start, KWIN_ATT), :]
        qpos = i * tq + lax.broadcasted_iota(jnp.int32, (tq, KWIN_ATT), 0)
        kpos = start + lax.broadcasted_iota(jnp.int32, (tq, KWIN_ATT), 1)
        bias = jnp.where(jnp.abs(qpos - kpos) <= WINDOW, 0.0, NEG_INF).astype(F32)
        bias = jnp.concatenate([bias] * A_GROUP, axis=0)
        kc = kc_ref[0]
        vc = vc_ref[0]
    else:
        kl = k_ref[0]
        vl = v_ref[0]
    nt = (((1,), (1,)), ((), ()))
    zero = jnp.zeros_like(q[:, :LANES])
    outs = []
    for g in range(A_KV_HEADS):
        parts = []
        for hh in range(A_GROUP):
            h = g * A_GROUP + hh
            blk = q[:, (h // 2) * LANES:(h // 2 + 1) * LANES]
            parts.append(jnp.where(low if h % 2 == 0 else jnp.logical_not(low), blk, zero))
        qs = jnp.concatenate(parts, axis=0)
        sink = jnp.concatenate([jnp.full((tq, 1), sink_ref[g * A_GROUP + hh], F32)
                                for hh in range(A_GROUP)], axis=0)
        kl_g = _dup_heads(kl, g).astype(BF16)
        vl_g = _dup_heads(vl, g).astype(BF16)
        s_loc = lax.dot_general(qs, kl_g, nt, preferred_element_type=F32)
        m = sink
        if latent:
            s_loc = s_loc + bias
            kc_g = _dup_heads(kc, g).astype(BF16)
            vc_g = _dup_heads(vc, g).astype(BF16)
            s_ctx = lax.dot_general(qs, kc_g, nt, preferred_element_type=F32)
            m = jnp.maximum(m, jnp.max(s_ctx, axis=-1, keepdims=True))
        m = jnp.maximum(m, jnp.max(s_loc, axis=-1, keepdims=True))
        p_loc = jnp.exp(s_loc - m)
        den = jnp.exp(sink - m) + jnp.sum(p_loc, axis=-1, keepdims=True)
        acc = jnp.dot(p_loc.astype(BF16), vl_g, preferred_element_type=F32)
        if latent:
            p_ctx = jnp.exp(s_ctx - m)
            den = den + jnp.sum(p_ctx, axis=-1, keepdims=True)
            acc = acc + jnp.dot(p_ctx.astype(BF16), vc_g, preferred_element_type=F32)
        og = acc / den
        for j in range(A_GROUP // 2):
            outs.append(jnp.where(low, og[(2 * j) * tq:(2 * j + 1) * tq], og[(2 * j + 1) * tq:(2 * j + 2) * tq]))
    o_ref[0] = jnp.concatenate(outs, axis=1).astype(o_ref.dtype)


def attention(q, k, v, sink, kc=None, vc=None, tq=TQ_ATT):
    B, L, _ = q.shape
    latent = kc is not None
    tq = min(tq, L)
    seq = lambda wd: pl.BlockSpec((1, L, wd), lambda b, i: (b, 0, 0))
    in_specs = [
        pl.BlockSpec(memory_space=pltpu.SMEM),
        pl.BlockSpec((1, tq, A_WIDTH), lambda b, i: (b, i, 0)),
        seq(LANES), seq(LANES),
    ]
    args = [sink, q, k, v]
    if latent:
        P = kc.shape[1]
        in_specs += [pl.BlockSpec((1, P, LANES), lambda b, i: (b, 0, 0))] * 2
        args += [kc, vc]
    return pl.pallas_call(
        functools.partial(_attn_body, latent=latent, seq_len=L),
        grid=(B, L // tq),
        in_specs=in_specs,
        out_specs=pl.BlockSpec((1, tq, A_WIDTH), lambda b, i: (b, i, 0)),
        out_shape=jax.ShapeDtypeStruct((B, L, A_WIDTH), BF16),
        compiler_params=_cparams("parallel", "arbitrary"),
        name="attention_latent" if latent else "attention_context",
    )(*args)


def _split3(x):
    hi = x.astype(BF16)
    r = x - hi.astype(F32)
    mid = r.astype(BF16)
    lo = (r - mid.astype(F32)).astype(BF16)
    return hi, mid, lo


def _gla_chunk(q, k, v, la, st_ref, d, reverse):
    C = GLA_CHUNK
    W = C_WIDTH
    ti = lax.broadcasted_iota(jnp.int32, (C, C), 0)
    si = lax.broadcasted_iota(jnp.int32, (C, C), 1)
    tri = (si >= ti) if reverse else (si <= ti)
    tri_b = tri.astype(BF16)
    hi, mid, lo = _split3(la)
    b = (jnp.dot(tri_b, hi, preferred_element_type=F32) + jnp.dot(tri_b, mid, preferred_element_type=F32)
         + jnp.dot(tri_b, lo, preferred_element_type=F32))
    b_last = b[0:1] if reverse else b[C - 1:C]
    qg = q * jnp.exp(b)
    kg = (k * jnp.exp(-b)).astype(BF16)
    kd = (k * jnp.exp(b_last - b)).astype(BF16)
    decay = jnp.exp(b_last)
    vb = v.astype(BF16)
    r4 = lax.broadcasted_iota(jnp.int32, (C_HEADS * C, W), 0) // C
    c4 = lax.broadcasted_iota(jnp.int32, (C_HEADS * C, W), 1) // C_DK
    same_head = r4 == c4
    q_bd = jnp.where(same_head, jnp.concatenate([qg] * C_HEADS, axis=0), 0.0).astype(BF16)
    nt = (((1,), (1,)), ((), ()))
    a = lax.dot_general(q_bd, kg, nt, preferred_element_type=F32)
    a = jnp.where(jnp.concatenate([tri] * C_HEADS, axis=0), a, 0.0)
    r = jnp.dot(a.astype(BF16), vb, preferred_element_type=F32)
    r = jnp.where(same_head, r, 0.0)
    o = r[0:C]
    for h in range(1, C_HEADS):
        o = o + r[h * C:(h + 1) * C]
    st = st_ref[d]
    o = o + lax.dot_general(qg.astype(BF16), st.astype(BF16), nt, preferred_element_type=F32)
    ut = lax.dot_general(vb, kd, (((0,), (0,)), ((), ())), preferred_element_type=F32)
    st_ref[d] = st * decay + jnp.where(same_head, ut, 0.0)
    return o


def _gla_body(*refs, has_state, n_chunks):
    if has_state:
        (qf, kf, vf, lf, qb, kb, vb, lb, s0_ref, of_ref, ob_ref, sT_ref, st) = refs
    else:
        (qf, kf, vf, lf, qb, kb, vb, lb, of_ref, ob_ref, sT_ref, st) = refs
    j = pl.program_id(1)

    nb = qf.shape[0]

    @pl.when(j == 0)
    def _init():
        if has_state:
            st[...] = s0_ref[...].reshape(st.shape)
        else:
            st[...] = jnp.zeros_like(st)

    C = GLA_CHUNK

    def body(c, carry):
        rf = pl.multiple_of(c * C, C)
        rb = pl.multiple_of((n_chunks - 1 - c) * C, C)
        for b in range(nb):
            of_ref[b, pl.ds(rf, C), :] = _gla_chunk(qf[b, pl.ds(rf, C), :], kf[b, pl.ds(rf, C), :],
                                                    vf[b, pl.ds(rf, C), :], lf[b, pl.ds(rf, C), :],
                                                    st, 2 * b, False)
            ob_ref[b, pl.ds(rb, C), :] = _gla_chunk(qb[b, pl.ds(rb, C), :], kb[b, pl.ds(rb, C), :],
                                                    vb[b, pl.ds(rb, C), :], lb[b, pl.ds(rb, C), :],
                                                    st, 2 * b + 1, True)
        return carry
    lax.fori_loop(0, n_chunks, body, 0)

    @pl.when(j == pl.num_programs(1) - 1)
    def _final():
        sT_ref[...] = st[...].reshape(sT_ref.shape)


NB_GLA = 2


def gla(cq, ck, cv, la, s0=None, rows=512, nb=NB_GLA):
    B, L, W = cq.shape
    assert B % nb == 0
    rows = min(rows, L)
    ng = L // rows
    has_state = s0 is not None
    fwd = lambda: pl.BlockSpec((nb, rows, W), lambda b, j: (b, j, 0))
    bwd = lambda: pl.BlockSpec((nb, rows, W), lambda b, j: (b, ng - 1 - j, 0))
    state = lambda: pl.BlockSpec((nb, 2, W, W), lambda b, j: (b, 0, 0, 0))
    in_specs = [fwd(), fwd(), fwd(), pl.BlockSpec((nb, rows, W), lambda b, j: (b, j, 0)),
                bwd(), bwd(), bwd(), pl.BlockSpec((nb, rows, W), lambda b, j: (b, ng - 1 - j, 1))]
    args = [cq, ck, cv, la, cq, ck, cv, la]
    if has_state:
        in_specs.append(state())
        args.append(s0)
    return pl.pallas_call(
        functools.partial(_gla_body, has_state=has_state, n_chunks=rows // GLA_CHUNK),
        grid=(B // nb, ng),
        in_specs=in_specs,
        out_specs=[fwd(), bwd(), state()],
        out_shape=[jax.ShapeDtypeStruct((B, L, W), F32), jax.ShapeDtypeStruct((B, L, W), F32),
                   jax.ShapeDtypeStruct((B, 2, W, W), F32)],
        scratch_shapes=[pltpu.VMEM((2 * nb, W, W), F32)],
        compiler_params=_cparams("parallel", "arbitrary"),
        name="gla",
    )(*args)


TM_DFT = 512
BG_DFT = 4


def _conv3_body(x_ref, w_ref, b_ref, u_ref, ub_ref):
    x = x_ref[0]
    L = x.shape[0]
    row = lax.broadcasted_iota(jnp.int32, (L, 1), 0)
    prev = jnp.where(row == 0, 0.0, pltpu.roll(x, 1, 0))
    nxt = jnp.where(row == L - 1, 0.0, pltpu.roll(x, L - 1, 0))
    u = prev * w_ref[0:1, :] + x * w_ref[1:2, :] + nxt * w_ref[2:3, :] + b_ref[...]
    u_ref[0] = u
    ub_ref[0] = u.astype(BF16)


def hyena_conv3(hy, w, b):
    B, L, C3 = hy.shape
    blk = pl.BlockSpec((1, L, HY_CH), lambda b_, j: (b_, 0, j))
    return pl.pallas_call(
        _conv3_body,
        grid=(B, C3 // HY_CH),
        in_specs=[blk, pl.BlockSpec((SUBLANES, HY_CH), lambda b_, j: (0, j)),
                  pl.BlockSpec((1, HY_CH), lambda b_, j: (0, j))],
        out_specs=[blk, blk],
        out_shape=[jax.ShapeDtypeStruct(hy.shape, F32), jax.ShapeDtypeStruct(hy.shape, BF16)],
        compiler_params=_cparams("parallel", "parallel"),
        name="hyena_conv3",
    )(hy, w, b)


def _freq_weight(i, tm, n):
    k = i * tm + lax.broadcasted_iota(jnp.int32, (tm, 1), 0)
    return jnp.where(k == 0, 1.0 / n, 2.0 / n).astype(F32)


def _filt_body(fr_ref, fi_ref, sg_ref, hs_ref, hd_ref, hr_ref, hi_ref, hny_ref, *, n):
    i = pl.program_id(0)
    wk = _freq_weight(i, fr_ref.shape[0], n)
    fr = fr_ref[...]
    fi = fi_ref[...]
    hr_ref[...] = wk * (jnp.dot(fr, hs_ref[0], preferred_element_type=F32)
                        + jnp.dot(fr, hs_ref[1], preferred_element_type=F32))
    hi_ref[...] = wk * (jnp.dot(fi, hd_ref[0], preferred_element_type=F32)
                        + jnp.dot(fi, hd_ref[1], preferred_element_type=F32))

    @pl.when(i == 0)
    def _nyquist():
        sg = sg_ref[...]
        hny_ref[...] = (jnp.dot(sg, hs_ref[0], preferred_element_type=F32)
                        + jnp.dot(sg, hs_ref[1], preferred_element_type=F32)) * (1.0 / n)


def hyena_filter_spectrum(fr, fi, sg, hs, hd, tm=TM_DFT):
    L = fr.shape[0]
    W = hs.shape[2]
    tm = min(tm, L)
    whole = lambda shape: pl.BlockSpec(shape, lambda i: (0,) * len(shape))
    return pl.pallas_call(
        functools.partial(_filt_body, n=2 * L),
        grid=(L // tm,),
        in_specs=[pl.BlockSpec((tm, L), lambda i: (i, 0)), pl.BlockSpec((tm, L), lambda i: (i, 0)),
                  whole((SUBLANES, L)), whole((2, L, W)), whole((2, L, W))],
        out_specs=[pl.BlockSpec((tm, W), lambda i: (i, 0)), pl.BlockSpec((tm, W), lambda i: (i, 0)),
                   whole((SUBLANES, W))],
        out_shape=[jax.ShapeDtypeStruct((L, W), F32), jax.ShapeDtypeStruct((L, W), F32),
                   jax.ShapeDtypeStruct((SUBLANES, W), F32)],
        compiler_params=_cparams("arbitrary"),
        name="hyena_filter_spectrum",
    )(fr, fi, sg, hs, hd)


def _dft_fwd_body(fr_ref, fi_ref, sg_ref, z_ref, hr_ref, hi_ref, hny_ref, yr_ref, yi_ref, yny_ref):
    nb = z_ref.shape[0]
    hr = hr_ref[...]
    hi = hi_ref[...]

    def body(b, carry):
        zb = z_ref[b]
        xr = jnp.dot(fr_ref[...], zb, preferred_element_type=F32)
        xi = jnp.dot(fi_ref[...], zb, preferred_element_type=F32)
        yr_ref[b] = (xr * hr - xi * hi).astype(BF16)
        yi_ref[b] = (xr * hi + xi * hr).astype(BF16)
        return carry
    lax.fori_loop(0, nb, body, 0)

    @pl.when(pl.program_id(1) == 0)
    def _nyquist():
        def nyq(b, carry):
            yny_ref[b] = jnp.dot(sg_ref[...], z_ref[b], preferred_element_type=F32) * hny_ref[0:1, :]
            return carry
        lax.fori_loop(0, nb, nyq, 0)


def hyena_dft_fwd(fr, fi, sg, zb, zcol, hr, hi, hny, order, tm=TM_DFT, bg=BG_DFT):
    B, L, _ = zb.shape
    C = HY_CH
    tm = min(tm, L)
    bg = B if L < TM_DFT else bg
    ftile = pl.BlockSpec((tm, L), lambda g, i: (i, 0))
    ytile = pl.BlockSpec((bg, tm, C), lambda g, i: (g, i, 0))
    return pl.pallas_call(
        _dft_fwd_body,
        grid=(B // bg, L // tm),
        in_specs=[ftile, ftile, pl.BlockSpec((SUBLANES, L), lambda g, i: (0, 0)),
                  pl.BlockSpec((bg, L, C), lambda g, i: (g, 0, zcol)),
                  pl.BlockSpec((tm, C), lambda g, i: (i, order)), pl.BlockSpec((tm, C), lambda g, i: (i, order)),
                  pl.BlockSpec((SUBLANES, C), lambda g, i: (0, order))],
        out_specs=[ytile, ytile, pl.BlockSpec((bg, SUBLANES, C), lambda g, i: (g, 0, 0))],
        out_shape=[jax.ShapeDtypeStruct((B, L, C), BF16), jax.ShapeDtypeStruct((B, L, C), BF16),
                   jax.ShapeDtypeStruct((B, SUBLANES, C), F32)],
        compiler_params=_cparams("parallel", "arbitrary"),
        name="hyena_dft_fwd",
    )(fr, fi, sg, zb, hr, hi, hny)


def _dft_inv_body(fr_ref, fi_ref, yr_ref, yi_ref, yny_ref, z_ref, gate_ref, d_ref, zo_ref, zob_ref):
    nb = z_ref.shape[0]
    tm = fr_ref.shape[0]
    t = pl.program_id(1) * tm + lax.broadcasted_iota(jnp.int32, (tm, 1), 0)
    sign = jnp.where((t & 1) == 0, 1.0, -1.0).astype(F32)
    d = d_ref[...]

    def body(b, carry):
        conv = (jnp.dot(fr_ref[...], yr_ref[b], preferred_element_type=F32)
                + jnp.dot(fi_ref[...], yi_ref[b], preferred_element_type=F32) + sign * yny_ref[b][0:1, :])
        zn = gate_ref[b] * (conv + d * z_ref[b])
        zo_ref[b] = zn
        zob_ref[b] = zn.astype(BF16)
        return carry
    lax.fori_loop(0, nb, body, 0)


def hyena_dft_inv(fr, fi, yr, yi, yny, z, zcol, gate, gcol, d, tm=TM_DFT, bg=BG_DFT):
    B, L, C = yr.shape
    tm = min(tm, L)
    bg = B if L < TM_DFT else bg
    ftile = pl.BlockSpec((tm, L), lambda g, i: (i, 0))
    whole_y = lambda: pl.BlockSpec((bg, L, C), lambda g, i: (g, 0, 0), pipeline_mode=pl.Buffered(1))
    otile = pl.BlockSpec((bg, tm, C), lambda g, i: (g, i, 0))
    return pl.pallas_call(
        _dft_inv_body,
        grid=(B // bg, L // tm),
        in_specs=[ftile, ftile, whole_y(), whole_y(),
                  pl.BlockSpec((bg, SUBLANES, C), lambda g, i: (g, 0, 0)),
                  pl.BlockSpec((bg, tm, C), lambda g, i: (g, i, zcol)),
                  pl.BlockSpec((bg, tm, C), lambda g, i: (g, i, gcol)),
                  pl.BlockSpec((1, C), lambda g, i: (0, 0))],
        out_specs=[otile, otile],
        out_shape=[jax.ShapeDtypeStruct((B, L, C), F32), jax.ShapeDtypeStruct((B, L, C), BF16)],
        compiler_params=_cparams("parallel", "arbitrary"),
        name="hyena_dft_inv",
    )(fr, fi, yr, yi, yny, z, gate, d)


def dft_matrices(L):
    r = 1 << (max(L.bit_length() - 1, 0) // 2)
    t = jnp.arange(L, dtype=jnp.int32)

    def table(k):
        ang = ((k[:, None] * t[None, :]) % (2 * L)).astype(F32) * (math.pi / L)
        return jnp.cos(ang), jnp.sin(ang)
    ca, sa = table(jnp.arange(L // r, dtype=jnp.int32) * r)
    cb, sb = table(jnp.arange(r, dtype=jnp.int32))
    fr = ca[:, None, :] * cb[None, :, :] - sa[:, None, :] * sb[None, :, :]
    fi = -(sa[:, None, :] * cb[None, :, :] + ca[:, None, :] * sb[None, :, :])
    sg = jnp.zeros((SUBLANES, L), F32).at[0].set(jnp.where(t % 2 == 0, 1.0, -1.0))
    return fr.reshape(L, L).astype(BF16), fi.reshape(L, L).astype(BF16), sg.astype(BF16)


def hyena_time_filters(L, lp):
    t = jnp.arange(L, dtype=F32)
    t_norm = t / max(L - 1, 1)
    w = (2.0 * math.pi / L) * t
    f = jnp.linspace(1e-4, HY_BANDS - 1, HY_BANDS, dtype=F32)
    fw = w[:, None] * f[None, :]
    feat = jnp.concatenate([t_norm[:, None], jnp.cos(fw), -jnp.sin(fw)], axis=-1)
    z = jnp.sin(lp['hy_freq1'] * (feat @ lp['hy_w1'] + lp['hy_b1']))
    z = jnp.sin(lp['hy_freq2'] * (z @ lp['hy_w2'] + lp['hy_b2']))
    hf = (z @ lp['hy_w3']).astype(F32).reshape(L, 2, HY_ORDER, HY_CH)
    hf = hf * jnp.exp(-t_norm[:, None, None, None] * jnp.abs(lp['hy_decay'].astype(F32)))
    return hf / (jnp.sum(jnp.abs(hf), axis=(0, 1), keepdims=True) + EPS)


def _hi_lo(x):
    hi = x.astype(BF16)
    return jnp.stack([hi, (x - hi.astype(F32)).astype(BF16)])


def hyena(hy, lp, mats):
    B, L, _ = hy.shape
    fr, fi, sg = mats
    hf = hyena_time_filters(L, lp)
    h_fwd = hf[:, 0].reshape(L, HY_ORDER * HY_CH)
    h_bwd = hf[:, 1].at[0].set(0.0).reshape(L, HY_ORDER * HY_CH)
    hr, hi, hny = hyena_filter_spectrum(fr, fi, sg, _hi_lo(h_fwd + h_bwd), _hi_lo(h_fwd - h_bwd))
    w = jnp.pad(lp['hy_conv_w'], ((0, SUBLANES - 3), (0, 0)))
    u, ub = hyena_conv3(hy, w, lp['hy_conv_b'][None, :])
    d = lp['hy_d'].astype(F32)
    z, zb, zcol = u, ub, 0
    for o in range(HY_ORDER):
        yr, yi, yny = hyena_dft_fwd(fr, fi, sg, zb, zcol, hr, hi, hny, o)
        z, zb = hyena_dft_inv(fr, fi, yr, yi, yny, z, zcol, u, 1 + o, d[o][None, :])
        zcol = 0
    return z


def _ffn_body(x_ref, mod_ref, g_ref, w1_ref, w3_ref, w2_ref, o_ref):
    x = x_ref[...]
    h = _adaln_rows(x, g_ref[...], mod_ref[0, ROW_SHIFT2:ROW_SHIFT2 + 1, :],
                    mod_ref[0, ROW_SCALE2:ROW_SCALE2 + 1, :]).astype(BF16)
    a = jnp.dot(h, w1_ref[...], preferred_element_type=F32)
    b = jnp.dot(h, w3_ref[...], preferred_element_type=F32)
    act = (_silu(a) * b).astype(BF16)
    ff = jnp.dot(act, w2_ref[...], preferred_element_type=F32)
    o_ref[...] = x + mod_ref[0, ROW_GATE2:ROW_GATE2 + 1, :] * ff


def ffn_dense(x, mods, g, w1, w3, w2, seg_len, tm=TM_PROJ):
    T, D = x.shape
    F = w1.shape[1]
    resident = functools.partial(pl.BlockSpec, pipeline_mode=pl.Buffered(1))
    return pl.pallas_call(
        _ffn_body,
        grid=(T // tm,),
        in_specs=[
            pl.BlockSpec((tm, D), lambda i: (i, 0)),
            pl.BlockSpec((1, MOD_ROWS, D), lambda i: (i * tm // seg_len, 0, 0)),
            pl.BlockSpec((1, D), lambda i: (0, 0)),
            resident((D, F), lambda i: (0, 0)),
            resident((D, F), lambda i: (0, 0)),
            resident((F, D), lambda i: (0, 0)),
        ],
        out_specs=pl.BlockSpec((tm, D), lambda i: (i, 0)),
        out_shape=jax.ShapeDtypeStruct((T, D), F32),
        compiler_params=_cparams("parallel"),
        name="ffn_dense",
    )(x, mods, g, w1, w3, w2)


def _store_token_tiles(ref, x):
    n = x.shape[0]
    for s in range(SUBLANES):
        ref[pl.ds(s, n, stride=SUBLANES), :] = x[:, s * LANES:(s + 1) * LANES]


def _load_token_tiles(ref, n):
    return jnp.concatenate([ref[pl.ds(s, n, stride=SUBLANES), :] for s in range(SUBLANES)], axis=1)


def _router_body(xa_ref, xb_ref, mod_ref, g_ref, wr_ref, h_ref, idx_ref, gw_ref, *, na):
    x = jnp.where(pl.program_id(0) < na, xa_ref[...], xb_ref[...])
    h = _adaln_rows(x, g_ref[...], mod_ref[0, ROW_SHIFT2:ROW_SHIFT2 + 1, :],
                    mod_ref[0, ROW_SCALE2:ROW_SCALE2 + 1, :])
    _store_token_tiles(h_ref, h)
    logits = lax.dot_general(wr_ref[...], h, (((1,), (1,)), ((), ())),
                             precision=lax.Precision.HIGHEST, preferred_element_type=F32)
    eidx = lax.broadcasted_iota(jnp.int32, logits.shape, 0)
    m1 = jnp.max(logits, axis=0, keepdims=True)
    i1 = jnp.min(jnp.where(logits == m1, eidx, N_EXPERTS), axis=0, keepdims=True)
    rest = jnp.where(eidx == i1, -jnp.inf, logits)
    m2 = jnp.max(rest, axis=0, keepdims=True)
    i2 = jnp.min(jnp.where(rest == m2, eidx, N_EXPERTS), axis=0, keepdims=True)
    e2 = jnp.exp(m2 - m1)
    den = 1.0 + e2
    row = lax.broadcasted_iota(jnp.int32, logits.shape, 0)
    idx_ref[...] = jnp.where(row == 0, i1, jnp.where(row == 1, i2, 0))
    gw_ref[...] = jnp.where(row == 0, 1.0 / den, jnp.where(row == 1, e2 / den, 0.0))


def _merged_seg(i, na, tm, seg_len_b):
    return jnp.where(i < na, 0, 1 + jnp.maximum(i - na, 0) * tm // seg_len_b)


def moe_router(xa, xb, mods, g, wr_t, seg_len_b, tm=TM_PROJ):
    D = xa.shape[1]
    na = xa.shape[0] // tm
    T = xa.shape[0] + xb.shape[0]
    return pl.pallas_call(
        functools.partial(_router_body, na=na),
        grid=(T // tm,),
        in_specs=[
            pl.BlockSpec((tm, D), lambda i: (jnp.minimum(i, na - 1), 0)),
            pl.BlockSpec((tm, D), lambda i: (jnp.maximum(i - na, 0), 0)),
            pl.BlockSpec((1, MOD_ROWS, D), lambda i: (_merged_seg(i, na, tm, seg_len_b), 0, 0)),
            pl.BlockSpec((1, D), lambda i: (0, 0)),
            pl.BlockSpec((N_EXPERTS, D), lambda i: (0, 0)),
        ],
        out_specs=[
            pl.BlockSpec((tm * SUBLANES, LANES), lambda i: (i, 0)),
            pl.BlockSpec((N_EXPERTS, tm), lambda i: (0, i)),
            pl.BlockSpec((N_EXPERTS, tm), lambda i: (0, i)),
        ],
        out_shape=[
            jax.ShapeDtypeStruct((T * SUBLANES, LANES), F32),
            jax.ShapeDtypeStruct((N_EXPERTS, T), jnp.int32),
            jax.ShapeDtypeStruct((N_EXPERTS, T), F32),
        ],
        compiler_params=_cparams("parallel"),
        name="moe_router",
    )(xa, xb, mods, g, wr_t)


def _experts_body(te_ref, nv_ref, src0_ref, srcn_ref, dst_ref, gate_ref, h_hbm, w1_ref, w3_ref, w2_ref,
                  y_hbm, hrows, hb, acc, ybuf, sem_g, sem_s, *, tm, nf, n_slots):
    i = pl.program_id(0)
    f = pl.program_id(1)
    nv = nv_ref[0]
    valid = i < nv
    slot = lax.rem(i, 2)

    def gather_issue(idx_ref, s):
        def body(r, carry):
            src = pl.multiple_of(idx_ref[0, 0, r] * SUBLANES, SUBLANES)
            dst = pl.multiple_of(r * SUBLANES, SUBLANES)
            pltpu.make_async_copy(h_hbm.at[pl.ds(src, SUBLANES)], hrows.at[s, pl.ds(dst, SUBLANES)],
                                  sem_g.at[s]).start()
            return carry
        lax.fori_loop(0, tm, body, 0, unroll=8)

    def gather_wait(s):
        pltpu.make_async_copy(h_hbm.at[pl.ds(0, tm * SUBLANES)], hrows.at[s], sem_g.at[s]).wait()

    def scatter_wait(s):
        pltpu.make_async_copy(ybuf.at[s], y_hbm.at[pl.ds(0, tm * SUBLANES)], sem_s.at[s]).wait()

    @pl.when(jnp.logical_and(i == 0, f == 0))
    def _first():
        ybuf[0] = jnp.zeros(ybuf.shape[1:], ybuf.dtype)
        spare = lambda p: y_hbm.at[pl.ds((n_slots + p * tm) * SUBLANES, tm * SUBLANES)]
        for p in range(2):
            pltpu.make_async_copy(ybuf.at[0], spare(p), sem_s.at[p]).start()
        for p in range(2):
            pltpu.make_async_copy(ybuf.at[0], spare(p), sem_s.at[p]).wait()
        gather_issue(src0_ref, 0)

    @pl.when(jnp.logical_and(valid, f == 0))
    def _stage():
        gather_wait(slot)
        hb[...] = _load_token_tiles(hrows.at[slot], tm).astype(BF16)
        acc[...] = jnp.zeros_like(acc)

        @pl.when(i + 1 < nv)
        def _prefetch():
            gather_issue(srcn_ref, 1 - slot)

    @pl.when(valid)
    def _compute():
        h = hb[...]
        a = jnp.dot(h, w1_ref[0], preferred_element_type=F32)
        b = jnp.dot(h, w3_ref[0], preferred_element_type=F32)
        act = (_silu(a) * b).astype(BF16)
        acc[...] += jnp.dot(act, w2_ref[0], preferred_element_type=F32)

    @pl.when(jnp.logical_and(valid, f == nf - 1))
    def _emit():
        @pl.when(i >= 2)
        def _reuse():
            scatter_wait(slot)
        _store_token_tiles(ybuf.at[slot], acc[...] * gate_ref[...])

        def body(r, carry):
            src = pl.multiple_of(r * SUBLANES, SUBLANES)
            dst = pl.multiple_of(dst_ref[0, 0, r] * SUBLANES, SUBLANES)
            pltpu.make_async_copy(ybuf.at[slot, pl.ds(src, SUBLANES)], y_hbm.at[pl.ds(dst, SUBLANES)],
                                  sem_s.at[slot]).start()
            return carry
        lax.fori_loop(0, tm, body, 0, unroll=8)

    @pl.when(jnp.logical_and(i == nv, f == 0))
    def _drain():
        scatter_wait(lax.rem(nv + 1, 2))

        @pl.when(nv >= 2)
        def _older():
            scatter_wait(lax.rem(nv, 2))


def moe_experts(h, tile_expert, n_valid, src_rows, dst_rows, gate_rows, w1, w3, w2, tm=TM_MOE, tf=TF_MOE):
    T = h.shape[0] // SUBLANES
    D = w1.shape[1]
    F = w1.shape[2]
    n_tiles = src_rows.shape[0]
    n_slots = TOP_K * T
    nf = F // tf

    def wcol(i, f, te, nv):
        return (te[i], 0, jnp.where(i < nv[0], f, nf - 1))

    def wrow(i, f, te, nv):
        return (te[i], jnp.where(i < nv[0], f, nf - 1), 0)

    smem_tile = lambda fn: pl.BlockSpec((1, 1, tm), fn, memory_space=pltpu.SMEM)
    grid_spec = pltpu.PrefetchScalarGridSpec(
        num_scalar_prefetch=2,
        grid=(n_tiles, nf),
        in_specs=[
            smem_tile(lambda i, f, te, nv: (0, 0, 0)),
            smem_tile(lambda i, f, te, nv: (jnp.minimum(i + 1, n_tiles - 1), 0, 0)),
            smem_tile(lambda i, f, te, nv: (i, 0, 0)),
            pl.BlockSpec((tm, 1), lambda i, f, te, nv: (i, 0)),
            pl.BlockSpec(memory_space=pl.ANY),
            pl.BlockSpec((1, D, tf), wcol),
            pl.BlockSpec((1, D, tf), wcol),
            pl.BlockSpec((1, tf, D), wrow),
        ],
        out_specs=pl.BlockSpec(memory_space=pl.ANY),
        scratch_shapes=[
            pltpu.VMEM((2, tm * SUBLANES, LANES), F32),
            pltpu.VMEM((tm, D), BF16),
            pltpu.VMEM((tm, D), F32),
            pltpu.VMEM((2, tm * SUBLANES, LANES), F32),
            pltpu.SemaphoreType.DMA((2,)),
            pltpu.SemaphoreType.DMA((2,)),
        ],
    )
    return pl.pallas_call(
        functools.partial(_experts_body, tm=tm, nf=nf, n_slots=n_slots),
        grid_spec=grid_spec,
        out_shape=jax.ShapeDtypeStruct(((n_slots + 2 * tm) * SUBLANES, LANES), F32),
        compiler_params=_cparams("arbitrary", "arbitrary"),
        name="moe_experts",
    )(tile_expert, n_valid, src_rows, src_rows, dst_rows, gate_rows, h, w1, w3, w2)


def _combine_body(x_ref, mod_ref, y0_ref, y1_ref, o_ref):
    n = x_ref.shape[0]
    y = _load_token_tiles(y0_ref, n) + _load_token_tiles(y1_ref, n)
    o_ref[...] = x_ref[...] + mod_ref[0, ROW_GATE2:ROW_GATE2 + 1, :] * y


def moe_combine(x, mods, y, row0, n_tok, seg_len, tm=TM_COMB):
    T, D = x.shape
    off = row0 // tm
    return pl.pallas_call(
        _combine_body,
        grid=(T // tm,),
        in_specs=[
            pl.BlockSpec((tm, D), lambda i: (i, 0)),
            pl.BlockSpec((1, MOD_ROWS, D), lambda i: (i * tm // seg_len, 0, 0)),
            pl.BlockSpec((tm * SUBLANES, LANES), lambda i: (off + i, 0)),
            pl.BlockSpec((tm * SUBLANES, LANES), lambda i: (off + n_tok // tm + i, 0)),
        ],
        out_specs=pl.BlockSpec((tm, D), lambda i: (i, 0)),
        out_shape=jax.ShapeDtypeStruct((T, D), F32),
        compiler_params=_cparams("parallel"),
        name="moe_combine",
    )(x, mods, y, y)


def moe_dispatch_plan(idx, gw, tm=TM_MOE):
    T = idx.shape[1]
    n_slots = TOP_K * T
    n_tiles = n_slots // tm + N_EXPERTS
    n_rows = n_tiles * tm
    e_flat = idx[:TOP_K].reshape(n_slots)
    g_flat = gw[:TOP_K].reshape(n_slots)
    counts = jnp.sum((e_flat[:, None] == jnp.arange(N_EXPERTS, dtype=jnp.int32)[None, :]).astype(jnp.int32),
                     axis=0)
    padded = (counts + tm - 1) // tm * tm
    ends = jnp.cumsum(padded)
    offs = ends - padded
    order = jnp.argsort(e_flat, stable=True).astype(jnp.int32)
    cstart = jnp.cumsum(counts) - counts
    rows = jnp.arange(n_rows, dtype=jnp.int32)
    row_e = jnp.minimum(jnp.searchsorted(ends, rows, side='right'), N_EXPERTS - 1).astype(jnp.int32)
    j = rows - offs[row_e]
    live = j < counts[row_e]
    slot = order[jnp.clip(cstart[row_e] + j, 0, n_slots - 1)]
    spare = n_slots + (rows // tm) % 2 * tm + rows % tm
    src_rows = jnp.where(live, slot % T, 0).astype(jnp.int32).reshape(n_tiles, 1, tm)
    dst_rows = jnp.where(live, slot, spare).astype(jnp.int32).reshape(n_tiles, 1, tm)
    gate_rows = jnp.where(live, g_flat[slot], 0.0).reshape(n_rows, 1)
    tile_start = jnp.arange(n_tiles, dtype=jnp.int32) * tm
    n_valid = (ends[-1] // tm).astype(jnp.int32).reshape(1)
    tile_expert = jnp.minimum(jnp.searchsorted(ends, tile_start, side='right'), N_EXPERTS - 1)
    last_e = tile_expert[jnp.maximum(n_valid[0] - 1, 0)]
    tile_expert = jnp.where(tile_start < ends[-1], tile_expert, last_e).astype(jnp.int32)
    return tile_expert, n_valid, src_rows, dst_rows, gate_rows


def ffn_moe(xa, xb, mods_a, mods_b, g, wr_t, w1, w3, w2, seg_len_b, tm=TM_PROJ, tme=TM_MOE, tmc=TM_COMB,
            tf=TF_MOE):
    na, nb = xa.shape[0], xb.shape[0]
    mods = jnp.concatenate([mods_a, mods_b], axis=0)
    h, idx, gw = moe_router(xa, xb, mods, g, wr_t, seg_len_b, tm)
    tile_expert, n_valid, src_rows, dst_rows, gate_rows = moe_dispatch_plan(idx, gw, tme)
    y = moe_experts(h, tile_expert, n_valid, src_rows, dst_rows, gate_rows, w1, w3, w2, tme, tf)
    return (moe_combine(xa, mods_a, y, 0, na + nb, na, tmc),
            moe_combine(xb, mods_b, y, na, na + nb, seg_len_b, tmc))


def axial_rope(L):
    rows = L // GRID_W
    r = jnp.repeat(jnp.arange(rows, dtype=F32), GRID_W)
    col = jnp.tile(jnp.arange(GRID_W, dtype=F32), rows)
    n = HEAD_DIM // 4
    freqs = ROPE_BASE ** (-jnp.arange(n, dtype=F32) / n)
    ang = jnp.concatenate([r[:, None] * freqs, col[:, None] * freqs], axis=-1)
    return jnp.cos(ang), jnp.sin(ang)


def rope_tables(L):
    cos, sin = axial_rope(L)
    n = HEAD_DIM // 4
    cos_h = jnp.concatenate([cos[:, :n], cos[:, :n], cos[:, n:], cos[:, n:]], axis=1)
    sin_h = jnp.concatenate([-sin[:, :n], sin[:, :n], -sin[:, n:], sin[:, n:]], axis=1)
    reps = LANES // HEAD_DIM
    return jnp.tile(cos_h, (1, reps)), jnp.tile(sin_h, (1, reps))


def block_diag_ones(width, block):
    i = jnp.arange(width) // block
    return (i[:, None] == i[None, :]).astype(BF16)


def gla_gate_params(gw, gb):
    w = jnp.zeros((LANES, 2 * C_WIDTH), F32)
    w = w.at[:GLA_RANK, :C_WIDTH].set(gw[0]).at[GLA_RANK:2 * GLA_RANK, C_WIDTH:].set(gw[1])
    return w.astype(BF16), jnp.concatenate([gb[0], gb[1]])[None, :]


def gla_state_to_blockdiag(st):
    B = st.shape[0]
    out = jnp.zeros((B, 2, C_HEADS, C_DV, C_HEADS, C_DK), F32)
    for h in range(C_HEADS):
        out = out.at[:, :, h, :, h, :].set(jnp.swapaxes(st[:, :, h], -1, -2))
    return out.reshape(B, 2, C_WIDTH, C_WIDTH)


def gla_state_from_blockdiag(sT):
    B = sT.shape[0]
    s6 = sT.reshape(B, 2, C_HEADS, C_DV, C_HEADS, C_DK)
    return jnp.stack([jnp.swapaxes(s6[:, :, h, :, h, :], -1, -2) for h in range(C_HEADS)], axis=2)


def mod_table(cvec, w_ada, b_ada):
    m = jax.nn.silu(cvec) @ w_ada + b_ada
    m = m.reshape(cvec.shape[0], 6, D_MODEL)
    return jnp.pad(m, ((0, 0), (0, MOD_ROWS - 6), (0, 0)))


def kernel(x_prompt, x_sample, cache_k, cache_v, state_gla, c, c_ctx, norm1_g, norm2_g, w_ada, b_ada,
           w_in, w_out, q_norm_g, k_norm_g, attn_sink, hy_conv_w, hy_conv_b, hy_w1, hy_b1, hy_freq1,
           hy_w2, hy_b2, hy_freq2, hy_w3, hy_decay, hy_d, gla_gate_w, gla_gate_b, gla_norm_g,
           ffn_w1, ffn_w3, ffn_w2, moe_router, moe_w1, moe_w3, moe_w2):
    D = D_MODEL
    xp = x_prompt.reshape(BATCH * SEQ, D)
    xs = x_sample.reshape(DEC_BATCH * DEC_SEQ, D)
    streams = [
        dict(x=xp, B=BATCH, L=SEQ, seg_len=BATCH * SEQ, cvec=c_ctx[None, :], latent=False),
        dict(x=xs, B=DEC_BATCH, L=DEC_SEQ, seg_len=DEC_SEQ, cvec=c, latent=True),
    ]
    ks_list, vs_list, st_list = [], [], []
    bd_q = block_diag_ones(A_WIDTH, HEAD_DIM)
    dft = {s['L']: dft_matrices(s['L']) for s in streams}
    for l in range(DEPTH):
        lp = {
            'hy_conv_w': hy_conv_w[l], 'hy_conv_b': hy_conv_b[l],
            'hy_w1': hy_w1[l], 'hy_b1': hy_b1[l], 'hy_freq1': hy_freq1[l], 'hy_w2': hy_w2[l],
            'hy_b2': hy_b2[l], 'hy_freq2': hy_freq2[l], 'hy_w3': hy_w3[l], 'hy_decay': hy_decay[l],
            'hy_d': hy_d[l], 'gla_gate_w': gla_gate_w[l], 'gla_gate_b': gla_gate_b[l],
            'gla_norm_g': gla_norm_g[l],
        }
        w_in_l = jnp.pad(w_in[l], ((0, 0), (0, D_PROJ_PAD - D_PROJ))).astype(BF16)
        w_out_l = w_out[l].astype(BF16)
        g1 = norm1_g[l][None, :]
        g2 = norm2_g[l][None, :]
        qg = jnp.tile(q_norm_g[l], A_HEADS)[None, :]
        kg = jnp.tile(k_norm_g[l], A_KV_HEADS)[None, :]
        gg = jnp.tile(gla_norm_g[l], C_HEADS)[None, :]
        gate_w, gate_b = gla_gate_params(gla_gate_w[l], gla_gate_b[l])
        j = l // 2
        for s in streams:
            B, L = s['B'], s['L']
            mods = mod_table(s['cvec'], w_ada[l], b_ada[l])
            rope = rope_tables(L) if s['latent'] else None
            q, k, v, hy, cq, ck, cv, cg, la = in_proj(s['x'], mods, g1, w_in_l, bd_q, qg, kg, gate_w, gate_b,
                                                      s['seg_len'], rope, L)
            seq = lambda t: t.reshape(B, L, t.shape[-1])
            if s['latent']:
                kc = cache_k[:, l].reshape(DEC_BATCH, PAST_LEN, LANES)
                vc = cache_v[:, l].reshape(DEC_BATCH, PAST_LEN, LANES)
                a_out = attention(seq(q), seq(k), seq(v), attn_sink[l], kc, vc)
                s0 = gla_state_to_blockdiag(state_gla[:, l])
                o_f, o_b, _ = gla(seq(cq), seq(ck), seq(cv), seq(la), s0)
            else:
                a_out = attention(seq(q), seq(k), seq(v), attn_sink[l])
                o_f, o_b, sT = gla(seq(cq), seq(ck), seq(cv), seq(la))
                ks_list.append(k.reshape(B, L, A_KV_HEADS, HEAD_DIM))
                vs_list.append(v.reshape(B, L, A_KV_HEADS, HEAD_DIM))
                st_list.append(gla_state_from_blockdiag(sT))
            h_out = hyena(seq(hy), lp, dft[L])
            flat = lambda t: t.reshape(B * L, t.shape[-1])
            x1 = out_proj(s['x'], flat(a_out), flat(h_out), flat(o_f), flat(o_b), cg, mods, w_out_l,
                          bd_q[:C_WIDTH, :C_WIDTH], gg, s['seg_len'])
            if l % 2 == 0:
                s['x'] = ffn_dense(x1, mods, g2, ffn_w1[j].astype(BF16), ffn_w3[j].astype(BF16),
                                   ffn_w2[j].astype(BF16), s['seg_len'])
            else:
                s['x'], s['mods'] = x1, mods
        if l % 2 == 1:
            sa, sb = streams
            sa['x'], sb['x'] = ffn_moe(sa['x'], sb['x'], sa['mods'], sb['mods'], g2, moe_router[j].T,
                                       moe_w1[j].astype(BF16), moe_w3[j].astype(BF16), moe_w2[j].astype(BF16),
                                       sb['seg_len'])
    y_prompt = streams[0]['x'].reshape(BATCH, SEQ, D)
    y_sample = streams[1]['x'].reshape(DEC_BATCH, DEC_SEQ, D)
    new_cache_k = jnp.stack(ks_list, axis=1)
    new_cache_v = jnp.stack(vs_list, axis=1)
    new_state_gla = jnp.stack(st_list, axis=1)
    return (y_prompt, y_sample, new_cache_k, new_cache_v, new_state_gla)
```

```python
import math
import functools
import jax
import jax.numpy as jnp
from jax import lax
import numpy as np
from jax.experimental import pallas as pl
from jax.experimental.pallas import tpu as pltpu

D_MODEL = 1024
BATCH = 32
SEQ = 256
DEPTH = 2
DEC_BATCH = 8
DEC_SEQ = 4096
PAST_LEN = 512

GRID_W = 64
HEAD_DIM = 64
D_MIX = D_MODEL
A_HEADS = 8
A_KV_HEADS = 2
A_GROUP = A_HEADS // A_KV_HEADS
A_WIDTH = A_HEADS * HEAD_DIM
WINDOW = 128
BLK = 128
ROPE_BASE = 10000.0
HY_CH = 256
HY_ORDER = 2
HY_BANDS = 16
HY_EMB = 1 + 2 * HY_BANDS
HY_HID = 64
C_HEADS = 4
C_DK = 64
C_DV = 64
C_WIDTH = C_HEADS * C_DV
GLA_RANK = 16
GLA_TAU = 16.0
GLA_CHUNK = 64
D_FF = 2816
N_EXPERTS = 8
TOP_K = 2
D_FF_EXPERT = 3584

PROJ_SIZES = (A_WIDTH, A_KV_HEADS * HEAD_DIM, A_KV_HEADS * HEAD_DIM, 3 * HY_CH,
              C_HEADS * C_DK, C_HEADS * C_DK, C_WIDTH, C_WIDTH, 2 * GLA_RANK)
PROJ_SPLITS = tuple(int(s) for s in np.cumsum(PROJ_SIZES)[:-1])
D_PROJ = int(sum(PROJ_SIZES))

F32 = jnp.float32
BF16 = jnp.bfloat16
ATT_SCALE = HEAD_DIM ** -0.5
NEG_INF = -1e30
EPS = 1e-6

LANES = 128
SUBLANES = 8
D_PROJ_PAD = -(-D_PROJ // LANES) * LANES
VMEM_LIMIT_BYTES = 56 * 1024 * 1024

MOD_ROWS = SUBLANES
ROW_SHIFT1, ROW_SCALE1, ROW_GATE1, ROW_SHIFT2, ROW_SCALE2, ROW_GATE2 = range(6)

TM_PROJ = 512
TM_MOE = 1024
TF_MOE = 896
TM_COMB = 512


def _cparams(*sem):
    return pltpu.CompilerParams(dimension_semantics=sem, vmem_limit_bytes=VMEM_LIMIT_BYTES)


def _adaln_rows(x, g, shift, scale):
    ms = jnp.mean(x * x, axis=-1, keepdims=True)
    return (x * lax.rsqrt(ms + EPS) * g) * (1.0 + scale) + shift


def _silu(a):
    return a * jax.nn.sigmoid(a)


def _group_mean_sq(x, ones_bd):
    sq = x * x
    hi = sq.astype(BF16)
    lo = (sq - hi.astype(F32)).astype(BF16)
    s = jnp.dot(hi, ones_bd, preferred_element_type=F32) + jnp.dot(lo, ones_bd, preferred_element_type=F32)
    return s * (1.0 / HEAD_DIM)


def _rope_rows(x, cos_t, sin_t):
    q4 = HEAD_DIM // 4
    lane = lax.broadcasted_iota(jnp.int32, x.shape, 1)
    partner = jnp.where((lane % (2 * q4)) < q4, pltpu.roll(x, LANES - q4, 1), pltpu.roll(x, q4, 1))
    return x * cos_t + partner * sin_t


def _log_sigmoid(x):
    return jnp.minimum(x, 0.0) - jnp.log(1.0 + jnp.exp(-jnp.abs(x)))


def _inproj_body(*refs, latent):
    if latent:
        (x_ref, mod_ref, g_ref, w_ref, bd_ref, qg_ref, kg_ref, gw_ref, gb_ref, cos_ref, sin_ref,
         q_ref, k_ref, v_ref, hy_ref, cq_ref, ck_ref, cv_ref, cg_ref, la_ref) = refs
    else:
        (x_ref, mod_ref, g_ref, w_ref, bd_ref, qg_ref, kg_ref, gw_ref, gb_ref,
         q_ref, k_ref, v_ref, hy_ref, cq_ref, ck_ref, cv_ref, cg_ref, la_ref) = refs
    h = _adaln_rows(x_ref[...], g_ref[...], mod_ref[0, ROW_SHIFT1:ROW_SHIFT1 + 1, :],
                    mod_ref[0, ROW_SCALE1:ROW_SCALE1 + 1, :])
    acc = jnp.dot(h.astype(BF16), w_ref[...], preferred_element_type=F32)
    o = 0
    q = acc[:, o:o + A_WIDTH]
    o += A_WIDTH
    k = acc[:, o:o + LANES]
    o += LANES
    v_ref[...] = acc[:, o:o + LANES]
    o += LANES
    hy_ref[...] = acc[:, o:o + 3 * HY_CH]
    o += 3 * HY_CH
    cq_ref[...] = acc[:, o:o + C_WIDTH] * (C_DK ** -0.5)
    o += C_WIDTH
    ck_ref[...] = acc[:, o:o + C_WIDTH]
    o += C_WIDTH
    cv_ref[...] = acc[:, o:o + C_WIDTH]
    o += C_WIDTH
    cg_ref[...] = acc[:, o:o + C_WIDTH]
    o += C_WIDTH
    r = acc[:, o:o + LANES]
    la_ref[...] = _log_sigmoid(jnp.dot(r.astype(BF16), gw_ref[...], preferred_element_type=F32)
                               + gb_ref[...]) * (1.0 / GLA_TAU)
    q = q * lax.rsqrt(_group_mean_sq(q, bd_ref[...]) + EPS) * qg_ref[...]
    k = k * lax.rsqrt(_group_mean_sq(k, bd_ref[0:LANES, 0:LANES]) + EPS) * kg_ref[...]
    if latent:
        cos_t = cos_ref[...]
        sin_t = sin_ref[...]
        q = jnp.concatenate([_rope_rows(q[:, j * LANES:(j + 1) * LANES], cos_t, sin_t)
                             for j in range(A_WIDTH // LANES)], axis=1)
        k = _rope_rows(k, cos_t, sin_t)
    q_ref[...] = (q * ATT_SCALE).astype(BF16)
    k_ref[...] = k


def in_proj(x, mods, g, w, bd, qg, kg, gw, gb, seg_len, rope=None, seq_len=None, tm=TM_PROJ):
    T, D = x.shape
    N = w.shape[1]
    latent = rope is not None
    const = lambda shape: pl.BlockSpec(shape, lambda i: (0,) * len(shape))
    in_specs = [
        pl.BlockSpec((tm, D), lambda i: (i, 0)),
        pl.BlockSpec((1, MOD_ROWS, D), lambda i: (i * tm // seg_len, 0, 0)),
        const((1, D)), const((D, N)), const((A_WIDTH, A_WIDTH)), const((1, A_WIDTH)), const((1, LANES)),
        const((LANES, 2 * C_WIDTH)), const((1, 2 * C_WIDTH)),
    ]
    args = [x, mods, g, w, bd, qg, kg, gw, gb]
    if latent:
        tiles_per_seq = seq_len // tm
        in_specs += [pl.BlockSpec((tm, LANES), lambda i: (i % tiles_per_seq, 0))] * 2
        args += list(rope)
    widths = [A_WIDTH, LANES, LANES, 3 * HY_CH, C_WIDTH, C_WIDTH, C_WIDTH, C_WIDTH, 2 * C_WIDTH]
    dtypes = [BF16] + [F32] * 8
    return pl.pallas_call(
        functools.partial(_inproj_body, latent=latent),
        grid=(T // tm,),
        in_specs=in_specs,
        out_specs=[pl.BlockSpec((tm, wd), lambda i: (i, 0)) for wd in widths],
        out_shape=[jax.ShapeDtypeStruct((T, wd), dt) for wd, dt in zip(widths, dtypes)],
        compiler_params=_cparams("parallel"),
        name="in_proj_latent" if latent else "in_proj_context",
    )(*args)


def _outproj_body(x_ref, a_ref, h_ref, of_ref, ob_ref, cg_ref, mod_ref, w_ref, bd_ref, gg_ref, o_ref):
    o = of_ref[...] + ob_ref[...]
    g_out = o * lax.rsqrt(_group_mean_sq(o, bd_ref[...]) + EPS) * gg_ref[...] * _silu(cg_ref[...])
    h0 = A_WIDTH
    g0 = A_WIDTH + HY_CH
    mix = jnp.dot(a_ref[...], w_ref[0:h0, :], preferred_element_type=F32)
    mix += jnp.dot(h_ref[...].astype(BF16), w_ref[h0:g0, :], preferred_element_type=F32)
    mix += jnp.dot(g_out.astype(BF16), w_ref[g0:, :], preferred_element_type=F32)
    o_ref[...] = x_ref[...] + mod_ref[0, ROW_GATE1:ROW_GATE1 + 1, :] * mix


def out_proj(x, a, h, of, ob, cg, mods, w, bd, gg, seg_len, tm=TM_PROJ):
    T, D = x.shape
    row = lambda wd: pl.BlockSpec((tm, wd), lambda i: (i, 0))
    const = lambda shape: pl.BlockSpec(shape, lambda i: (0,) * len(shape))
    return pl.pallas_call(
        _outproj_body,
        grid=(T // tm,),
        in_specs=[
            row(D), row(A_WIDTH), row(HY_CH), row(C_WIDTH), row(C_WIDTH), row(C_WIDTH),
            pl.BlockSpec((1, MOD_ROWS, D), lambda i: (i * tm // seg_len, 0, 0)),
            const((D, D)), const((C_WIDTH, C_WIDTH)), const((1, C_WIDTH)),
        ],
        out_specs=row(D),
        out_shape=jax.ShapeDtypeStruct((T, D), F32),
        compiler_params=_cparams("parallel"),
        name="out_proj",
    )(x, a, h, of, ob, cg, mods, w, bd, gg)


TQ_ATT = 256
HEADS_PER_STACK = 2
KWIN_ATT = TQ_ATT + 2 * WINDOW


def _dup_heads(x, g):
    lane = lax.broadcasted_iota(jnp.int32, x.shape, 1)
    rolled = pltpu.roll(x, HEAD_DIM, 1)
    keep = (lane < HEAD_DIM) if g == 0 else (lane >= HEAD_DIM)
    return jnp.where(keep, x, rolled)


def _attn_body(*refs, latent, seq_len):
    if latent:
        sink_ref, q_ref, k_ref, v_ref, kc_ref, vc_ref, o_ref = refs
    else:
        sink_ref, q_ref, k_ref, v_ref, o_ref = refs
    tq = q_ref.shape[1]
    i = pl.program_id(1)
    q = q_ref[0]
    lane = lax.broadcasted_iota(jnp.int32, (tq, LANES), 1)
    low = lane < HEAD_DIM
    if latent:
        start = jnp.clip(i * tq - WINDOW, 0, seq_len - KWIN_ATT)
        start = pl.multiple_of(start, WINDOW)
        kl = k_ref[0, pl.ds(start, KWIN_ATT), :]
        vl = v_ref[0, pl.ds(start, KWIN_ATT), :]
        qpos = i * tq + lax.broadcasted_iota(jnp.int32, (tq, KWIN_ATT), 0)
        kpos = start + lax.broadcasted_iota(jnp.int32, (tq, KWIN_ATT), 1)
        bias = jnp.where(jnp.abs(qpos - kpos) <= WINDOW, 0.0, NEG_INF).astype(F32)
        bias = jnp.concatenate([bias] * A_GROUP, axis=0)
        kc = kc_ref[0]
        vc = vc_ref[0]
    else:
        kl = k_ref[0]
        vl = v_ref[0]
    nt = (((1,), (1,)), ((), ()))
    zero = jnp.zeros_like(q[:, :LANES])
    outs = []
    for g in range(A_KV_HEADS):
        kl_g = _dup_heads(kl, g).astype(BF16)
        vl_g = _dup_heads(vl, g).astype(BF16)
        if latent:
            kc_g = _dup_heads(kc, g).astype(BF16)
            vc_g = _dup_heads(vc, g).astype(BF16)
        for j in range(HEADS_PER_STACK // 2 - 1, A_GROUP // 2, HEADS_PER_STACK // 2):
            heads = range(g * A_GROUP + 2 * j + 2 - HEADS_PER_STACK, g * A_GROUP + 2 * j + 2)
            qs = jnp.concatenate([jnp.where(low if h % 2 == 0 else jnp.logical_not(low),
                                            q[:, (h // 2) * LANES:(h // 2 + 1) * LANES], zero) for h in heads], axis=0)
            sink = jnp.concatenate([jnp.full((tq, 1), sink_ref[h], F32) for h in heads], axis=0)
            s_loc = lax.dot_general(qs, kl_g, nt, preferred_element_type=F32)
            m = sink
            if latent:
                s_loc = s_loc + bias[:len(heads) * tq]
                s_ctx = lax.dot_general(qs, kc_g, nt, preferred_element_type=F32)
                m = jnp.maximum(m, jnp.max(s_ctx, axis=-1, keepdims=True))
            m = jnp.maximum(m, jnp.max(s_loc, axis=-1, keepdims=True))
            p_loc = jnp.exp(s_loc - m)
            den = jnp.exp(sink - m) + jnp.sum(p_loc, axis=-1, keepdims=True)
            acc = jnp.dot(p_loc.astype(BF16), vl_g, preferred_element_type=F32)
            if latent:
                p_ctx = jnp.exp(s_ctx - m)
                den = den + jnp.sum(p_ctx, axis=-1, keepdims=True)
                acc = acc + jnp.dot(p_ctx.astype(BF16), vc_g, preferred_element_type=F32)
            og = acc / den
            for jj in range(len(heads) // 2):
                outs.append(jnp.where(low, og[(2 * jj) * tq:(2 * jj + 1) * tq],
                                      og[(2 * jj + 1) * tq:(2 * jj + 2) * tq]))
    o_ref[0] = jnp.concatenate(outs, axis=1).astype(o_ref.dtype)


def attention(q, k, v, sink, kc=None, vc=None, tq=TQ_ATT):
    B, L, _ = q.shape
    latent = kc is not None
    tq = min(tq, L)
    seq = lambda wd: pl.BlockSpec((1, L, wd), lambda b, i: (b, 0, 0))
    in_specs = [
        pl.BlockSpec(memory_space=pltpu.SMEM),
        pl.BlockSpec((1, tq, A_WIDTH), lambda b, i: (b, i, 0)),
        seq(LANES), seq(LANES),
    ]
    args = [sink, q, k, v]
    if latent:
        P = kc.shape[1]
        in_specs += [pl.BlockSpec((1, P, LANES), lambda b, i: (b, 0, 0))] * 2
        args += [kc, vc]
    return pl.pallas_call(
        functools.partial(_attn_body, latent=latent, seq_len=L),
        grid=(B, L // tq),
        in_specs=in_specs,
        out_specs=pl.BlockSpec((1, tq, A_WIDTH), lambda b, i: (b, i, 0)),
        out_shape=jax.ShapeDtypeStruct((B, L, A_WIDTH), BF16),
        compiler_params=_cparams("parallel", "arbitrary"),
        name="attention_latent" if latent else "attention_context",
    )(*args)


def _split3(x):
    hi = x.astype(BF16)
    r = x - hi.astype(F32)
    mid = r.astype(BF16)
    lo = (r - mid.astype(F32)).astype(BF16)
    return hi, mid, lo


def _gla_group(q_ref, k_ref, v_ref, la_ref, o_ref, st_ref, b, d, reverse, n_chunks):
    C = GLA_CHUNK
    W = C_WIDTH
    ti = lax.broadcasted_iota(jnp.int32, (C, C), 0)
    si = lax.broadcasted_iota(jnp.int32, (C, C), 1)
    tri = (si >= ti) if reverse else (si <= ti)
    tri_b = tri.astype(BF16)
    tri4 = jnp.concatenate([tri] * C_HEADS, axis=0)
    r4 = lax.broadcasted_iota(jnp.int32, (C_HEADS * C, W), 0) // C
    c4 = lax.broadcasted_iota(jnp.int32, (C_HEADS * C, W), 1) // C_DK
    same_head = r4 == c4
    nt = (((1,), (1,)), ((), ()))
    tn = (((0,), (0,)), ((), ()))
    chunks = range(n_chunks)
    rows = [pl.ds(c * C, C) for c in chunks]
    vbs = [v_ref[b, r, :].astype(BF16) for r in rows]
    parts = [_split3(la_ref[b, r, :]) for r in rows]
    bsums = [jnp.dot(tri_b, hi, preferred_element_type=F32) + jnp.dot(tri_b, mid, preferred_element_type=F32)
             + jnp.dot(tri_b, lo, preferred_element_type=F32) for hi, mid, lo in parts]
    b_lasts = [s[0:1] if reverse else s[C - 1:C] for s in bsums]
    qgs = [q_ref[b, r, :] * jnp.exp(s) for r, s in zip(rows, bsums)]
    kgs = [(k_ref[b, r, :] * jnp.exp(-s)).astype(BF16) for r, s in zip(rows, bsums)]
    kds = [(k_ref[b, r, :] * jnp.exp(bl - s)).astype(BF16) for r, s, bl in zip(rows, bsums, b_lasts)]
    decays = [jnp.exp(bl) for bl in b_lasts]
    q_bds = [jnp.where(same_head, jnp.concatenate([qg] * C_HEADS, axis=0), 0.0).astype(BF16) for qg in qgs]
    uts = [jnp.where(same_head, lax.dot_general(vb, kd, tn, preferred_element_type=F32), 0.0)
           for vb, kd in zip(vbs, kds)]
    a_s = [jnp.where(tri4, lax.dot_general(qb, kg, nt, preferred_element_type=F32), 0.0).astype(BF16)
           for qb, kg in zip(q_bds, kgs)]
    r_s = [jnp.where(same_head, jnp.dot(a, vb, preferred_element_type=F32), 0.0)
           for a, vb in zip(a_s, vbs)]
    o_intra = [sum([r[h * C:(h + 1) * C] for h in range(1, C_HEADS)], r[0:C]) for r in r_s]
    qgb = [qg.astype(BF16) for qg in qgs]
    st = st_ref[d]
    for c in (reversed(chunks) if reverse else chunks):
        o_ref[b, rows[c], :] = o_intra[c] + lax.dot_general(qgb[c], st.astype(BF16), nt,
                                                            preferred_element_type=F32)
        st = st * decays[c] + uts[c]
    st_ref[d] = st


def _gla_body(*refs, has_state, n_chunks):
    if has_state:
        (qf, kf, vf, lf, qb, kb, vb, lb, s0_ref, of_ref, ob_ref, sT_ref, st) = refs
    else:
        (qf, kf, vf, lf, qb, kb, vb, lb, of_ref, ob_ref, sT_ref, st) = refs
    j = pl.program_id(1)

    nb = qf.shape[0]

    @pl.when(j == 0)
    def _init():
        if has_state:
            st[...] = s0_ref[...].reshape(st.shape)
        else:
            st[...] = jnp.zeros_like(st)

    for b in range(nb):
        _gla_group(qf, kf, vf, lf, of_ref, st, b, 2 * b, False, n_chunks)
        _gla_group(qb, kb, vb, lb, ob_ref, st, b, 2 * b + 1, True, n_chunks)

    @pl.when(j == pl.num_programs(1) - 1)
    def _final():
        sT_ref[...] = st[...].reshape(sT_ref.shape)


NB_GLA = 1


def gla(cq, ck, cv, la, s0=None, rows=512, nb=NB_GLA):
    B, L, W = cq.shape
    assert B % nb == 0
    rows = min(rows, L)
    ng = L // rows
    has_state = s0 is not None
    fwd = lambda: pl.BlockSpec((nb, rows, W), lambda b, j: (b, j, 0))
    bwd = lambda: pl.BlockSpec((nb, rows, W), lambda b, j: (b, ng - 1 - j, 0))
    state = lambda: pl.BlockSpec((nb, 2, W, W), lambda b, j: (b, 0, 0, 0))
    in_specs = [fwd(), fwd(), fwd(), pl.BlockSpec((nb, rows, W), lambda b, j: (b, j, 0)),
                bwd(), bwd(), bwd(), pl.BlockSpec((nb, rows, W), lambda b, j: (b, ng - 1 - j, 1))]
    args = [cq, ck, cv, la, cq, ck, cv, la]
    if has_state:
        in_specs.append(state())
        args.append(s0)
    return pl.pallas_call(
        functools.partial(_gla_body, has_state=has_state, n_chunks=rows // GLA_CHUNK),
        grid=(B // nb, ng),
        in_specs=in_specs,
        out_specs=[fwd(), bwd(), state()],
        out_shape=[jax.ShapeDtypeStruct((B, L, W), F32), jax.ShapeDtypeStruct((B, L, W), F32),
                   jax.ShapeDtypeStruct((B, 2, W, W), F32)],
        scratch_shapes=[pltpu.VMEM((2 * nb, W, W), F32)],
        compiler_params=_cparams("parallel", "arbitrary"),
        name="gla",
    )(*args)


TM_DFT = 512
BG_DFT = 4


def _conv3_body(x_ref, w_ref, b_ref, u_ref, ub_ref):
    x = x_ref[0]
    L = x.shape[0]
    row = lax.broadcasted_iota(jnp.int32, (L, 1), 0)
    prev = jnp.where(row == 0, 0.0, pltpu.roll(x, 1, 0))
    nxt = jnp.where(row == L - 1, 0.0, pltpu.roll(x, L - 1, 0))
    u = prev * w_ref[0:1, :] + x * w_ref[1:2, :] + nxt * w_ref[2:3, :] + b_ref[...]
    u_ref[0] = u
    ub_ref[0] = u.astype(BF16)


def hyena_conv3(hy, w, b):
    B, L, C3 = hy.shape
    blk = pl.BlockSpec((1, L, HY_CH), lambda b_, j: (b_, 0, j))
    return pl.pallas_call(
        _conv3_body,
        grid=(B, C3 // HY_CH),
        in_specs=[blk, pl.BlockSpec((SUBLANES, HY_CH), lambda b_, j: (0, j)),
                  pl.BlockSpec((1, HY_CH), lambda b_, j: (0, j))],
        out_specs=[blk, blk],
        out_shape=[jax.ShapeDtypeStruct(hy.shape, F32), jax.ShapeDtypeStruct(hy.shape, BF16)],
        compiler_params=_cparams("parallel", "parallel"),
        name="hyena_conv3",
    )(hy, w, b)


def _freq_weight(i, tm, n):
    k = i * tm + lax.broadcasted_iota(jnp.int32, (tm, 1), 0)
    return jnp.where(k == 0, 1.0 / n, 2.0 / n).astype(F32)


def _filt_body(fr_ref, fi_ref, sg_ref, hs_ref, hd_ref, hr_ref, hi_ref, hny_ref, *, n):
    i = pl.program_id(0)
    wk = _freq_weight(i, fr_ref.shape[0], n)
    fr = fr_ref[...]
    fi = fi_ref[...]
    hr_ref[...] = wk * (jnp.dot(fr, hs_ref[0], preferred_element_type=F32)
                        + jnp.dot(fr, hs_ref[1], preferred_element_type=F32))
    hi_ref[...] = wk * (jnp.dot(fi, hd_ref[0], preferred_element_type=F32)
                        + jnp.dot(fi, hd_ref[1], preferred_element_type=F32))

    @pl.when(i == 0)
    def _nyquist():
        sg = sg_ref[...]
        hny_ref[...] = (jnp.dot(sg, hs_ref[0], preferred_element_type=F32)
                        + jnp.dot(sg, hs_ref[1], preferred_element_type=F32)) * (1.0 / n)


def hyena_filter_spectrum(fr, fi, sg, hs, hd, tm=TM_DFT):
    L = fr.shape[0]
    W = hs.shape[2]
    tm = min(tm, L)
    whole = lambda shape: pl.BlockSpec(shape, lambda i: (0,) * len(shape))
    return pl.pallas_call(
        functools.partial(_filt_body, n=2 * L),
        grid=(L // tm,),
        in_specs=[pl.BlockSpec((tm, L), lambda i: (i, 0)), pl.BlockSpec((tm, L), lambda i: (i, 0)),
                  whole((SUBLANES, L)), whole((2, L, W)), whole((2, L, W))],
        out_specs=[pl.BlockSpec((tm, W), lambda i: (i, 0)), pl.BlockSpec((tm, W), lambda i: (i, 0)),
                   whole((SUBLANES, W))],
        out_shape=[jax.ShapeDtypeStruct((L, W), F32), jax.ShapeDtypeStruct((L, W), F32),
                   jax.ShapeDtypeStruct((SUBLANES, W), F32)],
        compiler_params=_cparams("arbitrary"),
        name="hyena_filter_spectrum",
    )(fr, fi, sg, hs, hd)


def _dft_fwd_body(fr_ref, fi_ref, sg_ref, z_ref, hr_ref, hi_ref, hny_ref, yr_ref, yi_ref, yny_ref):
    nb = z_ref.shape[0]
    hr = hr_ref[...]
    hi = hi_ref[...]

    def body(b, carry):
        zb = z_ref[b]
        xr = jnp.dot(fr_ref[...], zb, preferred_element_type=F32)
        xi = jnp.dot(fi_ref[...], zb, preferred_element_type=F32)
        yr_ref[b] = (xr * hr - xi * hi).astype(BF16)
        yi_ref[b] = (xr * hi + xi * hr).astype(BF16)
        return carry
    lax.fori_loop(0, nb, body, 0)

    @pl.when(pl.program_id(1) == 0)
    def _nyquist():
        def nyq(b, carry):
            yny_ref[b] = jnp.dot(sg_ref[...], z_ref[b], preferred_element_type=F32) * hny_ref[0:1, :]
            return carry
        lax.fori_loop(0, nb, nyq, 0)


def hyena_dft_fwd(fr, fi, sg, zb, zcol, hr, hi, hny, order, tm=TM_DFT, bg=BG_DFT):
    B, L, _ = zb.shape
    C = HY_CH
    tm = min(tm, L)
    bg = B if L < TM_DFT else bg
    ftile = pl.BlockSpec((tm, L), lambda g, i: (i, 0))
    ytile = pl.BlockSpec((bg, tm, C), lambda g, i: (g, i, 0))
    return pl.pallas_call(
        _dft_fwd_body,
        grid=(B // bg, L // tm),
        in_specs=[ftile, ftile, pl.BlockSpec((SUBLANES, L), lambda g, i: (0, 0)),
                  pl.BlockSpec((bg, L, C), lambda g, i: (g, 0, zcol)),
                  pl.BlockSpec((tm, C), lambda g, i: (i, order)), pl.BlockSpec((tm, C), lambda g, i: (i, order)),
                  pl.BlockSpec((SUBLANES, C), lambda g, i: (0, order))],
        out_specs=[ytile, ytile, pl.BlockSpec((bg, SUBLANES, C), lambda g, i: (g, 0, 0))],
        out_shape=[jax.ShapeDtypeStruct((B, L, C), BF16), jax.ShapeDtypeStruct((B, L, C), BF16),
                   jax.ShapeDtypeStruct((B, SUBLANES, C), F32)],
        compiler_params=_cparams("parallel", "arbitrary"),
        name="hyena_dft_fwd",
    )(fr, fi, sg, zb, hr, hi, hny)


def _dft_inv_body(fr_ref, fi_ref, yr_ref, yi_ref, yny_ref, z_ref, gate_ref, d_ref, zo_ref, zob_ref):
    nb = z_ref.shape[0]
    tm = fr_ref.shape[0]
    t = pl.program_id(1) * tm + lax.broadcasted_iota(jnp.int32, (tm, 1), 0)
    sign = jnp.where((t & 1) == 0, 1.0, -1.0).astype(F32)
    d = d_ref[...]

    def body(b, carry):
        conv = (jnp.dot(fr_ref[...], yr_ref[b], preferred_element_type=F32)
                + jnp.dot(fi_ref[...], yi_ref[b], preferred_element_type=F32) + sign * yny_ref[b][0:1, :])
        zn = gate_ref[b] * (conv + d * z_ref[b])
        zo_ref[b] = zn
        zob_ref[b] = zn.astype(BF16)
        return carry
    lax.fori_loop(0, nb, body, 0)


def hyena_dft_inv(fr, fi, yr, yi, yny, z, zcol, gate, gcol, d, tm=TM_DFT, bg=BG_DFT):
    B, L, C = yr.shape
    tm = min(tm, L)
    bg = B if L < TM_DFT else bg
    ftile = pl.BlockSpec((tm, L), lambda g, i: (i, 0))
    whole_y = lambda: pl.BlockSpec((bg, L, C), lambda g, i: (g, 0, 0), pipeline_mode=pl.Buffered(1))
    otile = pl.BlockSpec((bg, tm, C), lambda g, i: (g, i, 0))
    return pl.pallas_call(
        _dft_inv_body,
        grid=(B // bg, L // tm),
        in_specs=[ftile, ftile, whole_y(), whole_y(),
                  pl.BlockSpec((bg, SUBLANES, C), lambda g, i: (g, 0, 0)),
                  pl.BlockSpec((bg, tm, C), lambda g, i: (g, i, zcol)),
                  pl.BlockSpec((bg, tm, C), lambda g, i: (g, i, gcol)),
                  pl.BlockSpec((1, C), lambda g, i: (0, 0))],
        out_specs=[otile, otile],
        out_shape=[jax.ShapeDtypeStruct((B, L, C), F32), jax.ShapeDtypeStruct((B, L, C), BF16)],
        compiler_params=_cparams("parallel", "arbitrary"),
        name="hyena_dft_inv",
    )(fr, fi, yr, yi, yny, z, gate, d)


def dft_matrices(L):
    r = 1 << (max(L.bit_length() - 1, 0) // 2)
    t = jnp.arange(L, dtype=jnp.int32)

    def table(k):
        ang = ((k[:, None] * t[None, :]) % (2 * L)).astype(F32) * (math.pi / L)
        return jnp.cos(ang), jnp.sin(ang)
    ca, sa = table(jnp.arange(L // r, dtype=jnp.int32) * r)
    cb, sb = table(jnp.arange(r, dtype=jnp.int32))
    fr = ca[:, None, :] * cb[None, :, :] - sa[:, None, :] * sb[None, :, :]
    fi = -(sa[:, None, :] * cb[None, :, :] + ca[:, None, :] * sb[None, :, :])
    sg = jnp.zeros((SUBLANES, L), F32).at[0].set(jnp.where(t % 2 == 0, 1.0, -1.0))
    return fr.reshape(L, L).astype(BF16), fi.reshape(L, L).astype(BF16), sg.astype(BF16)


def hyena_time_filters(L, lp):
    t = jnp.arange(L, dtype=F32)
    t_norm = t / max(L - 1, 1)
    w = (2.0 * math.pi / L) * t
    f = jnp.linspace(1e-4, HY_BANDS - 1, HY_BANDS, dtype=F32)
    fw = w[:, None] * f[None, :]
    feat = jnp.concatenate([t_norm[:, None], jnp.cos(fw), -jnp.sin(fw)], axis=-1)
    z = jnp.sin(lp['hy_freq1'] * (feat @ lp['hy_w1'] + lp['hy_b1']))
    z = jnp.sin(lp['hy_freq2'] * (z @ lp['hy_w2'] + lp['hy_b2']))
    hf = (z @ lp['hy_w3']).astype(F32).reshape(L, 2, HY_ORDER, HY_CH)
    hf = hf * jnp.exp(-t_norm[:, None, None, None] * jnp.abs(lp['hy_decay'].astype(F32)))
    return hf / (jnp.sum(jnp.abs(hf), axis=(0, 1), keepdims=True) + EPS)


def _hi_lo(x):
    hi = x.astype(BF16)
    return jnp.stack([hi, (x - hi.astype(F32)).astype(BF16)])


def hyena(hy, lp, mats):
    B, L, _ = hy.shape
    fr, fi, sg = mats
    hf = hyena_time_filters(L, lp)
    h_fwd = hf[:, 0].reshape(L, HY_ORDER * HY_CH)
    h_bwd = hf[:, 1].at[0].set(0.0).reshape(L, HY_ORDER * HY_CH)
    hr, hi, hny = hyena_filter_spectrum(fr, fi, sg, _hi_lo(h_fwd + h_bwd), _hi_lo(h_fwd - h_bwd))
    w = jnp.pad(lp['hy_conv_w'], ((0, SUBLANES - 3), (0, 0)))
    u, ub = hyena_conv3(hy, w, lp['hy_conv_b'][None, :])
    d = lp['hy_d'].astype(F32)
    z, zb, zcol = u, ub, 0
    for o in range(HY_ORDER):
        yr, yi, yny = hyena_dft_fwd(fr, fi, sg, zb, zcol, hr, hi, hny, o)
        z, zb = hyena_dft_inv(fr, fi, yr, yi, yny, z, zcol, u, 1 + o, d[o][None, :])
        zcol = 0
    return z


def _ffn_body(x_ref, mod_ref, g_ref, w1_ref, w3_ref, w2_ref, o_ref):
    x = x_ref[...]
    h = _adaln_rows(x, g_ref[...], mod_ref[0, ROW_SHIFT2:ROW_SHIFT2 + 1, :],
                    mod_ref[0, ROW_SCALE2:ROW_SCALE2 + 1, :]).astype(BF16)
    a = jnp.dot(h, w1_ref[...], preferred_element_type=F32)
    b = jnp.dot(h, w3_ref[...], preferred_element_type=F32)
    act = (_silu(a) * b).astype(BF16)
    ff = jnp.dot(act, w2_ref[...], preferred_element_type=F32)
    o_ref[...] = x + mod_ref[0, ROW_GATE2:ROW_GATE2 + 1, :] * ff


def ffn_dense(x, mods, g, w1, w3, w2, seg_len, tm=TM_PROJ):
    T, D = x.shape
    F = w1.shape[1]
    resident = functools.partial(pl.BlockSpec, pipeline_mode=pl.Buffered(1))
    return pl.pallas_call(
        _ffn_body,
        grid=(T // tm,),
        in_specs=[
            pl.BlockSpec((tm, D), lambda i: (i, 0)),
            pl.BlockSpec((1, MOD_ROWS, D), lambda i: (i * tm // seg_len, 0, 0)),
            pl.BlockSpec((1, D), lambda i: (0, 0)),
            resident((D, F), lambda i: (0, 0)),
            resident((D, F), lambda i: (0, 0)),
            resident((F, D), lambda i: (0, 0)),
        ],
        out_specs=pl.BlockSpec((tm, D), lambda i: (i, 0)),
        out_shape=jax.ShapeDtypeStruct((T, D), F32),
        compiler_params=_cparams("parallel"),
        name="ffn_dense",
    )(x, mods, g, w1, w3, w2)


def _store_token_tiles(ref, x):
    n = x.shape[0]
    for s in range(SUBLANES):
        ref[pl.ds(s, n, stride=SUBLANES), :] = x[:, s * LANES:(s + 1) * LANES]


def _load_token_tiles(ref, n):
    return jnp.concatenate([ref[pl.ds(s, n, stride=SUBLANES), :] for s in range(SUBLANES)], axis=1)


def _router_body(xa_ref, xb_ref, mod_ref, g_ref, wr_ref, h_ref, idx_ref, gw_ref, *, na):
    x = jnp.where(pl.program_id(0) < na, xa_ref[...], xb_ref[...])
    h = _adaln_rows(x, g_ref[...], mod_ref[0, ROW_SHIFT2:ROW_SHIFT2 + 1, :],
                    mod_ref[0, ROW_SCALE2:ROW_SCALE2 + 1, :])
    _store_token_tiles(h_ref, h)
    logits = lax.dot_general(wr_ref[...], h, (((1,), (1,)), ((), ())),
                             precision=lax.Precision.HIGHEST, preferred_element_type=F32)
    eidx = lax.broadcasted_iota(jnp.int32, logits.shape, 0)
    m1 = jnp.max(logits, axis=0, keepdims=True)
    i1 = jnp.min(jnp.where(logits == m1, eidx, N_EXPERTS), axis=0, keepdims=True)
    rest = jnp.where(eidx == i1, -jnp.inf, logits)
    m2 = jnp.max(rest, axis=0, keepdims=True)
    i2 = jnp.min(jnp.where(rest == m2, eidx, N_EXPERTS), axis=0, keepdims=True)
    e2 = jnp.exp(m2 - m1)
    den = 1.0 + e2
    row = lax.broadcasted_iota(jnp.int32, logits.shape, 0)
    idx_ref[...] = jnp.where(row == 0, i1, jnp.where(row == 1, i2, 0))
    gw_ref[...] = jnp.where(row == 0, 1.0 / den, jnp.where(row == 1, e2 / den, 0.0))


def _merged_seg(i, na, tm, seg_len_b):
    return jnp.where(i < na, 0, 1 + jnp.maximum(i - na, 0) * tm // seg_len_b)


def moe_router(xa, xb, mods, g, wr_t, seg_len_b, tm=TM_PROJ):
    D = xa.shape[1]
    na = xa.shape[0] // tm
    T = xa.shape[0] + xb.shape[0]
    return pl.pallas_call(
        functools.partial(_router_body, na=na),
        grid=(T // tm,),
        in_specs=[
            pl.BlockSpec((tm, D), lambda i: (jnp.minimum(i, na - 1), 0)),
            pl.BlockSpec((tm, D), lambda i: (jnp.maximum(i - na, 0), 0)),
            pl.BlockSpec((1, MOD_ROWS, D), lambda i: (_merged_seg(i, na, tm, seg_len_b), 0, 0)),
            pl.BlockSpec((1, D), lambda i: (0, 0)),
            pl.BlockSpec((N_EXPERTS, D), lambda i: (0, 0)),
        ],
        out_specs=[
            pl.BlockSpec((tm * SUBLANES, LANES), lambda i: (i, 0)),
            pl.BlockSpec((N_EXPERTS, tm), lambda i: (0, i)),
            pl.BlockSpec((N_EXPERTS, tm), lambda i: (0, i)),
        ],
        out_shape=[
            jax.ShapeDtypeStruct((T * SUBLANES, LANES), F32),
            jax.ShapeDtypeStruct((N_EXPERTS, T), jnp.int32),
            jax.ShapeDtypeStruct((N_EXPERTS, T), F32),
        ],
        compiler_params=_cparams("parallel"),
        name="moe_router",
    )(xa, xb, mods, g, wr_t)


def _experts_body(te_ref, nv_ref, src0_ref, srcn_ref, dst_ref, gate_ref, h_hbm, w1_ref, w3_ref, w2_ref,
                  y_hbm, hrows, hb, acc, ybuf, sem_g, sem_s, *, tm, nf, n_slots):
    i = pl.program_id(0)
    f = pl.program_id(1)
    nv = nv_ref[0]
    valid = i < nv
    slot = lax.rem(i, 2)

    def gather_issue(idx_ref, s):
        def body(r, carry):
            src = pl.multiple_of(idx_ref[0, 0, r] * SUBLANES, SUBLANES)
            dst = pl.multiple_of(r * SUBLANES, SUBLANES)
            pltpu.make_async_copy(h_hbm.at[pl.ds(src, SUBLANES)], hrows.at[s, pl.ds(dst, SUBLANES)],
                                  sem_g.at[s]).start()
            return carry
        lax.fori_loop(0, tm, body, 0, unroll=8)

    def gather_wait(s):
        pltpu.make_async_copy(h_hbm.at[pl.ds(0, tm * SUBLANES)], hrows.at[s], sem_g.at[s]).wait()

    def scatter_wait(s):
        pltpu.make_async_copy(ybuf.at[s], y_hbm.at[pl.ds(0, tm * SUBLANES)], sem_s.at[s]).wait()

    @pl.when(jnp.logical_and(i == 0, f == 0))
    def _first():
        ybuf[0] = jnp.zeros(ybuf.shape[1:], ybuf.dtype)
        spare = lambda p: y_hbm.at[pl.ds((n_slots + p * tm) * SUBLANES, tm * SUBLANES)]
        for p in range(2):
            pltpu.make_async_copy(ybuf.at[0], spare(p), sem_s.at[p]).start()
        for p in range(2):
            pltpu.make_async_copy(ybuf.at[0], spare(p), sem_s.at[p]).wait()
        gather_issue(src0_ref, 0)

    @pl.when(jnp.logical_and(valid, f == 0))
    def _stage():
        gather_wait(slot)
        hb[...] = _load_token_tiles(hrows.at[slot], tm).astype(BF16)
        acc[...] = jnp.zeros_like(acc)

        @pl.when(i + 1 < nv)
        def _prefetch():
            gather_issue(srcn_ref, 1 - slot)

    @pl.when(valid)
    def _compute():
        h = hb[...]
        a = jnp.dot(h, w1_ref[0], preferred_element_type=F32)
        b = jnp.dot(h, w3_ref[0], preferred_element_type=F32)
        act = (_silu(a) * b).astype(BF16)
        acc[...] += jnp.dot(act, w2_ref[0], preferred_element_type=F32)

    @pl.when(jnp.logical_and(valid, f == nf - 1))
    def _emit():
        @pl.when(i >= 2)
        def _reuse():
            scatter_wait(slot)
        _store_token_tiles(ybuf.at[slot], acc[...] * gate_ref[...])

        def body(r, carry):
            src = pl.multiple_of(r * SUBLANES, SUBLANES)
            dst = pl.multiple_of(dst_ref[0, 0, r] * SUBLANES, SUBLANES)
            pltpu.make_async_copy(ybuf.at[slot, pl.ds(src, SUBLANES)], y_hbm.at[pl.ds(dst, SUBLANES)],
                                  sem_s.at[slot]).start()
            return carry
        lax.fori_loop(0, tm, body, 0, unroll=8)

    @pl.when(jnp.logical_and(i == nv, f == 0))
    def _drain():
        scatter_wait(lax.rem(nv + 1, 2))

        @pl.when(nv >= 2)
        def _older():
            scatter_wait(lax.rem(nv, 2))


def moe_experts(h, tile_expert, n_valid, src_rows, dst_rows, gate_rows, w1, w3, w2, tm=TM_MOE, tf=TF_MOE):
    T = h.shape[0] // SUBLANES
    D = w1.shape[1]
    F = w1.shape[2]
    n_tiles = src_rows.shape[0]
    n_slots = TOP_K * T
    nf = F // tf

    def wcol(i, f, te, nv):
        return (te[i], 0, jnp.where(i < nv[0], f, nf - 1))

    def wrow(i, f, te, nv):
        return (te[i], jnp.where(i < nv[0], f, nf - 1), 0)

    smem_tile = lambda fn: pl.BlockSpec((1, 1, tm), fn, memory_space=pltpu.SMEM)
    grid_spec = pltpu.PrefetchScalarGridSpec(
        num_scalar_prefetch=2,
        grid=(n_tiles, nf),
        in_specs=[
            smem_tile(lambda i, f, te, nv: (0, 0, 0)),
            smem_tile(lambda i, f, te, nv: (jnp.minimum(i + 1, n_tiles - 1), 0, 0)),
            smem_tile(lambda i, f, te, nv: (i, 0, 0)),
            pl.BlockSpec((tm, 1), lambda i, f, te, nv: (i, 0)),
            pl.BlockSpec(memory_space=pl.ANY),
            pl.BlockSpec((1, D, tf), wcol),
            pl.BlockSpec((1, D, tf), wcol),
            pl.BlockSpec((1, tf, D), wrow),
        ],
        out_specs=pl.BlockSpec(memory_space=pl.ANY),
        scratch_shapes=[
            pltpu.VMEM((2, tm * SUBLANES, LANES), F32),
            pltpu.VMEM((tm, D), BF16),
            pltpu.VMEM((tm, D), F32),
            pltpu.VMEM((2, tm * SUBLANES, LANES), F32),
            pltpu.SemaphoreType.DMA((2,)),
            pltpu.SemaphoreType.DMA((2,)),
        ],
    )
    return pl.pallas_call(
        functools.partial(_experts_body, tm=tm, nf=nf, n_slots=n_slots),
        grid_spec=grid_spec,
        out_shape=jax.ShapeDtypeStruct(((n_slots + 2 * tm) * SUBLANES, LANES), F32),
        compiler_params=_cparams("arbitrary", "arbitrary"),
        name="moe_experts",
    )(tile_expert, n_valid, src_rows, src_rows, dst_rows, gate_rows, h, w1, w3, w2)


def _combine_body(x_ref, mod_ref, y0_ref, y1_ref, o_ref):
    n = x_ref.shape[0]
    y = _load_token_tiles(y0_ref, n) + _load_token_tiles(y1_ref, n)
    o_ref[...] = x_ref[...] + mod_ref[0, ROW_GATE2:ROW_GATE2 + 1, :] * y


def moe_combine(x, mods, y, row0, n_tok, seg_len, tm=TM_COMB):
    T, D = x.shape
    off = row0 // tm
    return pl.pallas_call(
        _combine_body,
        grid=(T // tm,),
        in_specs=[
            pl.BlockSpec((tm, D), lambda i: (i, 0)),
            pl.BlockSpec((1, MOD_ROWS, D), lambda i: (i * tm // seg_len, 0, 0)),
            pl.BlockSpec((tm * SUBLANES, LANES), lambda i: (off + i, 0)),
            pl.BlockSpec((tm * SUBLANES, LANES), lambda i: (off + n_tok // tm + i, 0)),
        ],
        out_specs=pl.BlockSpec((tm, D), lambda i: (i, 0)),
        out_shape=jax.ShapeDtypeStruct((T, D), F32),
        compiler_params=_cparams("parallel"),
        name="moe_combine",
    )(x, mods, y, y)


def moe_dispatch_plan(idx, gw, tm=TM_MOE):
    T = idx.shape[1]
    n_slots = TOP_K * T
    n_tiles = n_slots // tm + N_EXPERTS
    n_rows = n_tiles * tm
    e_flat = idx[:TOP_K].reshape(n_slots)
    g_flat = gw[:TOP_K].reshape(n_slots)
    counts = jnp.sum((e_flat[:, None] == jnp.arange(N_EXPERTS, dtype=jnp.int32)[None, :]).astype(jnp.int32),
                     axis=0)
    padded = (counts + tm - 1) // tm * tm
    ends = jnp.cumsum(padded)
    offs = ends - padded
    order = jnp.argsort(e_flat, stable=True).astype(jnp.int32)
    cstart = jnp.cumsum(counts) - counts
    rows = jnp.arange(n_rows, dtype=jnp.int32)
    row_e = jnp.minimum(jnp.searchsorted(ends, rows, side='right'), N_EXPERTS - 1).astype(jnp.int32)
    j = rows - offs[row_e]
    live = j < counts[row_e]
    slot = order[jnp.clip(cstart[row_e] + j, 0, n_slots - 1)]
    spare = n_slots + (rows // tm) % 2 * tm + rows % tm
    src_rows = jnp.where(live, slot % T, 0).astype(jnp.int32).reshape(n_tiles, 1, tm)
    dst_rows = jnp.where(live, slot, spare).astype(jnp.int32).reshape(n_tiles, 1, tm)
    gate_rows = jnp.where(live, g_flat[slot], 0.0).reshape(n_rows, 1)
    tile_start = jnp.arange(n_tiles, dtype=jnp.int32) * tm
    n_valid = (ends[-1] // tm).astype(jnp.int32).reshape(1)
    tile_expert = jnp.minimum(jnp.searchsorted(ends, tile_start, side='right'), N_EXPERTS - 1)
    last_e = tile_expert[jnp.maximum(n_valid[0] - 1, 0)]
    tile_expert = jnp.where(tile_start < ends[-1], tile_expert, last_e).astype(jnp.int32)
    return tile_expert, n_valid, src_rows, dst_rows, gate_rows


def ffn_moe(xa, xb, mods_a, mods_b, g, wr_t, w1, w3, w2, seg_len_b, tm=TM_PROJ, tme=TM_MOE, tmc=TM_COMB,
            tf=TF_MOE):
    na, nb = xa.shape[0], xb.shape[0]
    mods = jnp.concatenate([mods_a, mods_b], axis=0)
    h, idx, gw = moe_router(xa, xb, mods, g, wr_t, seg_len_b, tm)
    tile_expert, n_valid, src_rows, dst_rows, gate_rows = moe_dispatch_plan(idx, gw, tme)
    y = moe_experts(h, tile_expert, n_valid, src_rows, dst_rows, gate_rows, w1, w3, w2, tme, tf)
    return (moe_combine(xa, mods_a, y, 0, na + nb, na, tmc),
            moe_combine(xb, mods_b, y, na, na + nb, seg_len_b, tmc))


def axial_rope(L):
    rows = L // GRID_W
    r = jnp.repeat(jnp.arange(rows, dtype=F32), GRID_W)
    col = jnp.tile(jnp.arange(GRID_W, dtype=F32), rows)
    n = HEAD_DIM // 4
    freqs = ROPE_BASE ** (-jnp.arange(n, dtype=F32) / n)
    ang = jnp.concatenate([r[:, None] * freqs, col[:, None] * freqs], axis=-1)
    return jnp.cos(ang), jnp.sin(ang)


def rope_tables(L):
    cos, sin = axial_rope(L)
    n = HEAD_DIM // 4
    cos_h = jnp.concatenate([cos[:, :n], cos[:, :n], cos[:, n:], cos[:, n:]], axis=1)
    sin_h = jnp.concatenate([-sin[:, :n], sin[:, :n], -sin[:, n:], sin[:, n:]], axis=1)
    reps = LANES // HEAD_DIM
    return jnp.tile(cos_h, (1, reps)), jnp.tile(sin_h, (1, reps))


def block_diag_ones(width, block):
    i = jnp.arange(width) // block
    return (i[:, None] == i[None, :]).astype(BF16)


def gla_gate_params(gw, gb):
    w = jnp.zeros((LANES, 2 * C_WIDTH), F32)
    w = w.at[:GLA_RANK, :C_WIDTH].set(gw[0]).at[GLA_RANK:2 * GLA_RANK, C_WIDTH:].set(gw[1])
    return w.astype(BF16), jnp.concatenate([gb[0], gb[1]])[None, :]


def gla_state_to_blockdiag(st):
    B = st.shape[0]
    out = jnp.zeros((B, 2, C_HEADS, C_DV, C_HEADS, C_DK), F32)
    for h in range(C_HEADS):
        out = out.at[:, :, h, :, h, :].set(jnp.swapaxes(st[:, :, h], -1, -2))
    return out.reshape(B, 2, C_WIDTH, C_WIDTH)


def gla_state_from_blockdiag(sT):
    B = sT.shape[0]
    s6 = sT.reshape(B, 2, C_HEADS, C_DV, C_HEADS, C_DK)
    return jnp.stack([jnp.swapaxes(s6[:, :, h, :, h, :], -1, -2) for h in range(C_HEADS)], axis=2)


def mod_table(cvec, w_ada, b_ada):
    m = jax.nn.silu(cvec) @ w_ada + b_ada
    m = m.reshape(cvec.shape[0], 6, D_MODEL)
    return jnp.pad(m, ((0, 0), (0, MOD_ROWS - 6), (0, 0)))


def kernel(x_prompt, x_sample, cache_k, cache_v, state_gla, c, c_ctx, norm1_g, norm2_g, w_ada, b_ada,
           w_in, w_out, q_norm_g, k_norm_g, attn_sink, hy_conv_w, hy_conv_b, hy_w1, hy_b1, hy_freq1,
           hy_w2, hy_b2, hy_freq2, hy_w3, hy_decay, hy_d, gla_gate_w, gla_gate_b, gla_norm_g,
           ffn_w1, ffn_w3, ffn_w2, moe_router, moe_w1, moe_w3, moe_w2):
    D = D_MODEL
    xp = x_prompt.reshape(BATCH * SEQ, D)
    xs = x_sample.reshape(DEC_BATCH * DEC_SEQ, D)
    streams = [
        dict(x=xp, B=BATCH, L=SEQ, seg_len=BATCH * SEQ, cvec=c_ctx[None, :], latent=False),
        dict(x=xs, B=DEC_BATCH, L=DEC_SEQ, seg_len=DEC_SEQ, cvec=c, latent=True),
    ]
    ks_list, vs_list, st_list = [], [], []
    bd_q = block_diag_ones(A_WIDTH, HEAD_DIM)
    dft = {s['L']: dft_matrices(s['L']) for s in streams}
    for l in range(DEPTH):
        lp = {
            'hy_conv_w': hy_conv_w[l], 'hy_conv_b': hy_conv_b[l],
            'hy_w1': hy_w1[l], 'hy_b1': hy_b1[l], 'hy_freq1': hy_freq1[l], 'hy_w2': hy_w2[l],
            'hy_b2': hy_b2[l], 'hy_freq2': hy_freq2[l], 'hy_w3': hy_w3[l], 'hy_decay': hy_decay[l],
            'hy_d': hy_d[l], 'gla_gate_w': gla_gate_w[l], 'gla_gate_b': gla_gate_b[l],
            'gla_norm_g': gla_norm_g[l],
        }
        w_in_l = jnp.pad(w_in[l], ((0, 0), (0, D_PROJ_PAD - D_PROJ))).astype(BF16)
        w_out_l = w_out[l].astype(BF16)
        g1 = norm1_g[l][None, :]
        g2 = norm2_g[l][None, :]
        qg = jnp.tile(q_norm_g[l], A_HEADS)[None, :]
        kg = jnp.tile(k_norm_g[l], A_KV_HEADS)[None, :]
        gg = jnp.tile(gla_norm_g[l], C_HEADS)[None, :]
        gate_w, gate_b = gla_gate_params(gla_gate_w[l], gla_gate_b[l])
        j = l // 2
        for s in streams:
            B, L = s['B'], s['L']
            mods = mod_table(s['cvec'], w_ada[l], b_ada[l])
            rope = rope_tables(L) if s['latent'] else None
            q, k, v, hy, cq, ck, cv, cg, la = in_proj(s['x'], mods, g1, w_in_l, bd_q, qg, kg, gate_w, gate_b,
                                                      s['seg_len'], rope, L)
            seq = lambda t: t.reshape(B, L, t.shape[-1])
            if s['latent']:
                kc = cache_k[:, l].reshape(DEC_BATCH, PAST_LEN, LANES)
                vc = cache_v[:, l].reshape(DEC_BATCH, PAST_LEN, LANES)
                a_out = attention(seq(q), seq(k), seq(v), attn_sink[l], kc, vc)
                s0 = gla_state_to_blockdiag(state_gla[:, l])
                o_f, o_b, _ = gla(seq(cq), seq(ck), seq(cv), seq(la), s0)
            else:
                a_out = attention(seq(q), seq(k), seq(v), attn_sink[l])
                o_f, o_b, sT = gla(seq(cq), seq(ck), seq(cv), seq(la))
                ks_list.append(k.reshape(B, L, A_KV_HEADS, HEAD_DIM))
                vs_list.append(v.reshape(B, L, A_KV_HEADS, HEAD_DIM))
                st_list.append(gla_state_from_blockdiag(sT))
            h_out = hyena(seq(hy), lp, dft[L])
            flat = lambda t: t.reshape(B * L, t.shape[-1])
            x1 = out_proj(s['x'], flat(a_out), flat(h_out), flat(o_f), flat(o_b), cg, mods, w_out_l,
                          bd_q[:C_WIDTH, :C_WIDTH], gg, s['seg_len'])
            if l % 2 == 0:
                s['x'] = ffn_dense(x1, mods, g2, ffn_w1[j].astype(BF16), ffn_w3[j].astype(BF16),
                                   ffn_w2[j].astype(BF16), s['seg_len'])
            else:
                s['x'], s['mods'] = x1, mods
        if l % 2 == 1:
            sa, sb = streams
            sa['x'], sb['x'] = ffn_moe(sa['x'], sb['x'], sa['mods'], sb['mods'], g2, moe_router[j].T,
                                       moe_w1[j].astype(BF16), moe_w3[j].astype(BF16), moe_w2[j].astype(BF16),
                                       sb['seg_len'])
    y_prompt = streams[0]['x'].reshape(BATCH, SEQ, D)
    y_sample = streams[1]['x'].reshape(DEC_BATCH, DEC_SEQ, D)
    new_cache_k = jnp.stack(ks_list, axis=1)
    new_cache_v = jnp.stack(vs_list, axis=1)
    new_state_gla = jnp.stack(st_list, axis=1)
    return (y_prompt, y_sample, new_cache_k, new_cache_v, new_state_gla)
```

```python
import math
import functools
import jax
import jax.numpy as jnp
from jax import lax
import numpy as np
from jax.experimental import pallas as pl
from jax.experimental.pallas import tpu as pltpu

D_MODEL = 1024
BATCH = 32
SEQ = 256
DEPTH = 2
DEC_BATCH = 8
DEC_SEQ = 4096
PAST_LEN = 512

GRID_W = 64
HEAD_DIM = 64
D_MIX = D_MODEL
A_HEADS = 8
A_KV_HEADS = 2
A_GROUP = A_HEADS // A_KV_HEADS
A_WIDTH = A_HEADS * HEAD_DIM
WINDOW = 128
BLK = 128
ROPE_BASE = 10000.0
HY_CH = 256
HY_ORDER = 2
HY_BANDS = 16
HY_EMB = 1 + 2 * HY_BANDS
HY_HID = 64
C_HEADS = 4
C_DK = 64
C_DV = 64
C_WIDTH = C_HEADS * C_DV
GLA_RANK = 16
GLA_TAU = 16.0
GLA_CHUNK = 64
D_FF = 2816
N_EXPERTS = 8
TOP_K = 2
D_FF_EXPERT = 3584

PROJ_SIZES = (A_WIDTH, A_KV_HEADS * HEAD_DIM, A_KV_HEADS * HEAD_DIM, 3 * HY_CH,
              C_HEADS * C_DK, C_HEADS * C_DK, C_WIDTH, C_WIDTH, 2 * GLA_RANK)
PROJ_SPLITS = tuple(int(s) for s in np.cumsum(PROJ_SIZES)[:-1])
D_PROJ = int(sum(PROJ_SIZES))

F32 = jnp.float32
BF16 = jnp.bfloat16
ATT_SCALE = HEAD_DIM ** -0.5
NEG_INF = -1e30
EPS = 1e-6

LANES = 128
SUBLANES = 8
D_PROJ_PAD = -(-D_PROJ // LANES) * LANES
VMEM_LIMIT_BYTES = 56 * 1024 * 1024

MOD_ROWS = SUBLANES
ROW_SHIFT1, ROW_SCALE1, ROW_GATE1, ROW_SHIFT2, ROW_SCALE2, ROW_GATE2 = range(6)

TM_PROJ = 512
TM_MOE = 1024
TF_MOE = 896
TM_COMB = 512


def _cparams(*sem):
    return pltpu.CompilerParams(dimension_semantics=sem, vmem_limit_bytes=VMEM_LIMIT_BYTES)


def _adaln_rows(x, g, shift, scale):
    ms = jnp.mean(x * x, axis=-1, keepdims=True)
    return (x * lax.rsqrt(ms + EPS) * g) * (1.0 + scale) + shift


def _silu(a):
    return a * jax.nn.sigmoid(a)


def _group_mean_sq(x, ones_bd):
    sq = x * x
    hi = sq.astype(BF16)
    lo = (sq - hi.astype(F32)).astype(BF16)
    s = jnp.dot(hi, ones_bd, preferred_element_type=F32) + jnp.dot(lo, ones_bd, preferred_element_type=F32)
    return s * (1.0 / HEAD_DIM)


def _rope_rows(x, cos_t, sin_t):
    q4 = HEAD_DIM // 4
    lane = lax.broadcasted_iota(jnp.int32, x.shape, 1)
    partner = jnp.where((lane % (2 * q4)) < q4, pltpu.roll(x, LANES - q4, 1), pltpu.roll(x, q4, 1))
    return x * cos_t + partner * sin_t


def _log_sigmoid(x):
    return jnp.minimum(x, 0.0) - jnp.log(1.0 + jnp.exp(-jnp.abs(x)))


def _inproj_body(*refs, latent):
    if latent:
        (x_ref, mod_ref, g_ref, w_ref, bd_ref, qg_ref, kg_ref, gw_ref, gb_ref, cos_ref, sin_ref,
         q_ref, k_ref, v_ref, hy_ref, cq_ref, ck_ref, cv_ref, cg_ref, la_ref) = refs
    else:
        (x_ref, mod_ref, g_ref, w_ref, bd_ref, qg_ref, kg_ref, gw_ref, gb_ref,
         q_ref, k_ref, v_ref, hy_ref, cq_ref, ck_ref, cv_ref, cg_ref, la_ref) = refs
    h = _adaln_rows(x_ref[...], g_ref[...], mod_ref[0, ROW_SHIFT1:ROW_SHIFT1 + 1, :],
                    mod_ref[0, ROW_SCALE1:ROW_SCALE1 + 1, :])
    acc = jnp.dot(h.astype(BF16), w_ref[...], preferred_element_type=F32)
    o = 0
    q = acc[:, o:o + A_WIDTH]
    o += A_WIDTH
    k = acc[:, o:o + LANES]
    o += LANES
    v_ref[...] = acc[:, o:o + LANES]
    o += LANES
    hy_ref[...] = acc[:, o:o + 3 * HY_CH]
    o += 3 * HY_CH
    cq_ref[...] = acc[:, o:o + C_WIDTH] * (C_DK ** -0.5)
    o += C_WIDTH
    ck_ref[...] = acc[:, o:o + C_WIDTH]
    o += C_WIDTH
    cv_ref[...] = acc[:, o:o + C_WIDTH]
    o += C_WIDTH
    cg_ref[...] = acc[:, o:o + C_WIDTH]
    o += C_WIDTH
    r = acc[:, o:o + LANES]
    la_ref[...] = _log_sigmoid(jnp.dot(r.astype(BF16), gw_ref[...], preferred_element_type=F32)
                               + gb_ref[...]) * (1.0 / GLA_TAU)
    q = q * lax.rsqrt(_group_mean_sq(q, bd_ref[...]) + EPS) * qg_ref[...]
    k = k * lax.rsqrt(_group_mean_sq(k, bd_ref[0:LANES, 0:LANES]) + EPS) * kg_ref[...]
    if latent:
        cos_t = cos_ref[...]
        sin_t = sin_ref[...]
        q = jnp.concatenate([_rope_rows(q[:, j * LANES:(j + 1) * LANES], cos_t, sin_t)
                             for j in range(A_WIDTH // LANES)], axis=1)
        k = _rope_rows(k, cos_t, sin_t)
    q_ref[...] = (q * ATT_SCALE).astype(BF16)
    k_ref[...] = k


def in_proj(x, mods, g, w, bd, qg, kg, gw, gb, seg_len, rope=None, seq_len=None, tm=TM_PROJ):
    T, D = x.shape
    N = w.shape[1]
    latent = rope is not None
    const = lambda shape: pl.BlockSpec(shape, lambda i: (0,) * len(shape))
    in_specs = [
        pl.BlockSpec((tm, D), lambda i: (i, 0)),
        pl.BlockSpec((1, MOD_ROWS, D), lambda i: (i * tm // seg_len, 0, 0)),
        const((1, D)), const((D, N)), const((A_WIDTH, A_WIDTH)), const((1, A_WIDTH)), const((1, LANES)),
        const((LANES, 2 * C_WIDTH)), const((1, 2 * C_WIDTH)),
    ]
    args = [x, mods, g, w, bd, qg, kg, gw, gb]
    if latent:
        tiles_per_seq = seq_len // tm
        in_specs += [pl.BlockSpec((tm, LANES), lambda i: (i % tiles_per_seq, 0))] * 2
        args += list(rope)
    widths = [A_WIDTH, LANES, LANES, 3 * HY_CH, C_WIDTH, C_WIDTH, C_WIDTH, C_WIDTH, 2 * C_WIDTH]
    dtypes = [BF16] + [F32] * 8
    return pl.pallas_call(
        functools.partial(_inproj_body, latent=latent),
        grid=(T // tm,),
        in_specs=in_specs,
        out_specs=[pl.BlockSpec((tm, wd), lambda i: (i, 0)) for wd in widths],
        out_shape=[jax.ShapeDtypeStruct((T, wd), dt) for wd, dt in zip(widths, dtypes)],
        compiler_params=_cparams("parallel"),
        name="in_proj_latent" if latent else "in_proj_context",
    )(*args)


def _outproj_body(x_ref, a_ref, h_ref, of_ref, ob_ref, cg_ref, mod_ref, w_ref, bd_ref, gg_ref, o_ref):
    o = of_ref[...] + ob_ref[...]
    g_out = o * lax.rsqrt(_group_mean_sq(o, bd_ref[...]) + EPS) * gg_ref[...] * _silu(cg_ref[...])
    h0 = A_WIDTH
    g0 = A_WIDTH + HY_CH
    mix = jnp.dot(a_ref[...], w_ref[0:h0, :], preferred_element_type=F32)
    mix += jnp.dot(h_ref[...].astype(BF16), w_ref[h0:g0, :], preferred_element_type=F32)
    mix += jnp.dot(g_out.astype(BF16), w_ref[g0:, :], preferred_element_type=F32)
    o_ref[...] = x_ref[...] + mod_ref[0, ROW_GATE1:ROW_GATE1 + 1, :] * mix


def out_proj(x, a, h, of, ob, cg, mods, w, bd, gg, seg_len, tm=TM_PROJ):
    T, D = x.shape
    row = lambda wd: pl.BlockSpec((tm, wd), lambda i: (i, 0))
    const = lambda shape: pl.BlockSpec(shape, lambda i: (0,) * len(shape))
    return pl.pallas_call(
        _outproj_body,
        grid=(T // tm,),
        in_specs=[
            row(D), row(A_WIDTH), row(HY_CH), row(C_WIDTH), row(C_WIDTH), row(C_WIDTH),
            pl.BlockSpec((1, MOD_ROWS, D), lambda i: (i * tm // seg_len, 0, 0)),
            const((D, D)), const((C_WIDTH, C_WIDTH)), const((1, C_WIDTH)),
        ],
        out_specs=row(D),
        out_shape=jax.ShapeDtypeStruct((T, D), F32),
        compiler_params=_cparams("parallel"),
        name="out_proj",
    )(x, a, h, of, ob, cg, mods, w, bd, gg)


TQ_ATT = 256
HEADS_PER_STACK = 2
KWIN_ATT = TQ_ATT + 2 * WINDOW


def _dup_heads(x, g):
    lane = lax.broadcasted_iota(jnp.int32, x.shape, 1)
    rolled = pltpu.roll(x, HEAD_DIM, 1)
    keep = (lane < HEAD_DIM) if g == 0 else (lane >= HEAD_DIM)
    return jnp.where(keep, x, rolled)


def _attn_body(*refs, latent, seq_len):
    if latent:
        sink_ref, q_ref, k_ref, v_ref, kc_ref, vc_ref, o_ref = refs
    else:
        sink_ref, q_ref, k_ref, v_ref, o_ref = refs
    tq = q_ref.shape[1]
    i = pl.program_id(1)
    q = q_ref[0]
    lane = lax.broadcasted_iota(jnp.int32, (tq, LANES), 1)
    low = lane < HEAD_DIM
    if latent:
        start = jnp.clip(i * tq - WINDOW, 0, seq_len - KWIN_ATT)
        start = pl.multiple_of(start, WINDOW)
        kl = k_ref[0, pl.ds(start, KWIN_ATT), :]
        vl = v_ref[0, pl.ds(start, KWIN_ATT), :]
        qpos = i * tq + lax.broadcasted_iota(jnp.int32, (tq, KWIN_ATT), 0)
        kpos = start + lax.broadcasted_iota(jnp.int32, (tq, KWIN_ATT), 1)
        bias = jnp.where(jnp.abs(qpos - kpos) <= WINDOW, 0.0, NEG_INF).astype(F32)
        bias = jnp.concatenate([bias] * A_GROUP, axis=0)
        kc = kc_ref[0]
        vc = vc_ref[0]
    else:
        kl = k_ref[0]
        vl = v_ref[0]
    nt = (((1,), (1,)), ((), ()))
    zero = jnp.zeros_like(q[:, :LANES])
    outs = []
    for g in range(A_KV_HEADS):
        kl_g = _dup_heads(kl, g).astype(BF16)
        vl_g = _dup_heads(vl, g).astype(BF16)
        if latent:
            kc_g = _dup_heads(kc, g).astype(BF16)
            vc_g = _dup_heads(vc, g).astype(BF16)
        for j in range(HEADS_PER_STACK // 2 - 1, A_GROUP // 2, HEADS_PER_STACK // 2):
            heads = range(g * A_GROUP + 2 * j + 2 - HEADS_PER_STACK, g * A_GROUP + 2 * j + 2)
            qs = jnp.concatenate([jnp.where(low if h % 2 == 0 else jnp.logical_not(low),
                                            q[:, (h // 2) * LANES:(h // 2 + 1) * LANES], zero) for h in heads], axis=0)
            sink = jnp.concatenate([jnp.full((tq, 1), sink_ref[h], F32) for h in heads], axis=0)
            s_loc = lax.dot_general(qs, kl_g, nt, preferred_element_type=F32)
            m = sink
            if latent:
                s_loc = s_loc + bias[:len(heads) * tq]
                s_ctx = lax.dot_general(qs, kc_g, nt, preferred_element_type=F32)
                m = jnp.maximum(m, jnp.max(s_ctx, axis=-1, keepdims=True))
            m = jnp.maximum(m, jnp.max(s_loc, axis=-1, keepdims=True))
            p_loc = jnp.exp(s_loc - m)
            den = jnp.exp(sink - m) + jnp.sum(p_loc, axis=-1, keepdims=True)
            acc = jnp.dot(p_loc.astype(BF16), vl_g, preferred_element_type=F32)
            if latent:
                p_ctx = jnp.exp(s_ctx - m)
                den = den + jnp.sum(p_ctx, axis=-1, keepdims=True)
                acc = acc + jnp.dot(p_ctx.astype(BF16), vc_g, preferred_element_type=F32)
            og = acc / den
            for jj in range(len(heads) // 2):
                outs.append(jnp.where(low, og[(2 * jj) * tq:(2 * jj + 1) * tq],
                                      og[(2 * jj + 1) * tq:(2 * jj + 2) * tq]))
    o_ref[0] = jnp.concatenate(outs, axis=1).astype(o_ref.dtype)


def attention(q, k, v, sink, kc=None, vc=None, tq=TQ_ATT):
    B, L, _ = q.shape
    latent = kc is not None
    tq = min(tq, L)
    seq = lambda wd: pl.BlockSpec((1, L, wd), lambda b, i: (b, 0, 0))
    in_specs = [
        pl.BlockSpec(memory_space=pltpu.SMEM),
        pl.BlockSpec((1, tq, A_WIDTH), lambda b, i: (b, i, 0)),
        seq(LANES), seq(LANES),
    ]
    args = [sink, q, k, v]
    if latent:
        P = kc.shape[1]
        in_specs += [pl.BlockSpec((1, P, LANES), lambda b, i: (b, 0, 0))] * 2
        args += [kc, vc]
    return pl.pallas_call(
        functools.partial(_attn_body, latent=latent, seq_len=L),
        grid=(B, L // tq),
        in_specs=in_specs,
        out_specs=pl.BlockSpec((1, tq, A_WIDTH), lambda b, i: (b, i, 0)),
        out_shape=jax.ShapeDtypeStruct((B, L, A_WIDTH), BF16),
        compiler_params=_cparams("parallel", "arbitrary"),
        name="attention_latent" if latent else "attention_context",
    )(*args)


def _split3(x):
    hi = x.astype(BF16)
    r = x - hi.astype(F32)
    mid = r.astype(BF16)
    lo = (r - mid.astype(F32)).astype(BF16)
    return hi, mid, lo


def _gla_group(q_ref, k_ref, v_ref, la_ref, o_ref, st_ref, b, d, reverse, n_chunks):
    C = GLA_CHUNK
    W = C_WIDTH
    ti = lax.broadcasted_iota(jnp.int32, (C, C), 0)
    si = lax.broadcasted_iota(jnp.int32, (C, C), 1)
    tri = (si >= ti) if reverse else (si <= ti)
    tri_b = tri.astype(BF16)
    tri4 = jnp.concatenate([tri] * C_HEADS, axis=0)
    r4 = lax.broadcasted_iota(jnp.int32, (C_HEADS * C, W), 0) // C
    c4 = lax.broadcasted_iota(jnp.int32, (C_HEADS * C, W), 1) // C_DK
    same_head = r4 == c4
    nt = (((1,), (1,)), ((), ()))
    tn = (((0,), (0,)), ((), ()))
    chunks = range(n_chunks)
    rows = [pl.ds(c * C, C) for c in chunks]
    vbs = [v_ref[b, r, :].astype(BF16) for r in rows]
    parts = [_split3(la_ref[b, r, :]) for r in rows]
    bsums = [jnp.dot(tri_b, hi, preferred_element_type=F32) + jnp.dot(tri_b, mid, preferred_element_type=F32)
             + jnp.dot(tri_b, lo, preferred_element_type=F32) for hi, mid, lo in parts]
    b_lasts = [s[0:1] if reverse else s[C - 1:C] for s in bsums]
    qgs = [q_ref[b, r, :] * jnp.exp(s) for r, s in zip(rows, bsums)]
    kgs = [(k_ref[b, r, :] * jnp.exp(-s)).astype(BF16) for r, s in zip(rows, bsums)]
    kds = [(k_ref[b, r, :] * jnp.exp(bl - s)).astype(BF16) for r, s, bl in zip(rows, bsums, b_lasts)]
    decays = [jnp.exp(bl) for bl in b_lasts]
    q_bds = [jnp.where(same_head, jnp.concatenate([qg] * C_HEADS, axis=0), 0.0).astype(BF16) for qg in qgs]
    uts = [jnp.where(same_head, lax.dot_general(vb, kd, tn, preferred_element_type=F32), 0.0)
           for vb, kd in zip(vbs, kds)]
    a_s = [jnp.where(tri4, lax.dot_general(qb, kg, nt, preferred_element_type=F32), 0.0).astype(BF16)
           for qb, kg in zip(q_bds, kgs)]
    r_s = [jnp.where(same_head, jnp.dot(a, vb, preferred_element_type=F32), 0.0)
           for a, vb in zip(a_s, vbs)]
    o_intra = [sum([r[h * C:(h + 1) * C] for h in range(1, C_HEADS)], r[0:C]) for r in r_s]
    qgb = [qg.astype(BF16) for qg in qgs]
    st = st_ref[d]
    for c in (reversed(chunks) if reverse else chunks):
        o_ref[b, rows[c], :] = o_intra[c] + lax.dot_general(qgb[c], st.astype(BF16), nt,
                                                            preferred_element_type=F32)
        st = st * decays[c] + uts[c]
    st_ref[d] = st


def _gla_body(*refs, has_state, n_chunks):
    if has_state:
        (qf, kf, vf, lf, qb, kb, vb, lb, s0_ref, of_ref, ob_ref, sT_ref, st) = refs
    else:
        (qf, kf, vf, lf, qb, kb, vb, lb, of_ref, ob_ref, sT_ref, st) = refs
    j = pl.program_id(1)

    nb = qf.shape[0]

    @pl.when(j == 0)
    def _init():
        if has_state:
            st[...] = s0_ref[...].reshape(st.shape)
        else:
            st[...] = jnp.zeros_like(st)

    for b in range(nb):
        _gla_group(qf, kf, vf, lf, of_ref, st, b, 2 * b, False, n_chunks)
        _gla_group(qb, kb, vb, lb, ob_ref, st, b, 2 * b + 1, True, n_chunks)

    @pl.when(j == pl.num_programs(1) - 1)
    def _final():
        sT_ref[...] = st[...].reshape(sT_ref.shape)


NB_GLA = 1


def gla(cq, ck, cv, la, s0=None, rows=512, nb=NB_GLA):
    B, L, W = cq.shape
    assert B % nb == 0
    rows = min(rows, L)
    ng = L // rows
    has_state = s0 is not None
    fwd = lambda: pl.BlockSpec((nb, rows, W), lambda b, j: (b, j, 0))
    bwd = lambda: pl.BlockSpec((nb, rows, W), lambda b, j: (b, ng - 1 - j, 0))
    state = lambda: pl.BlockSpec((nb, 2, W, W), lambda b, j: (b, 0, 0, 0))
    in_specs = [fwd(), fwd(), fwd(), pl.BlockSpec((nb, rows, W), lambda b, j: (b, j, 0)),
                bwd(), bwd(), bwd(), pl.BlockSpec((nb, rows, W), lambda b, j: (b, ng - 1 - j, 1))]
    args = [cq, ck, cv, la, cq, ck, cv, la]
    if has_state:
        in_specs.append(state())
        args.append(s0)
    return pl.pallas_call(
        functools.partial(_gla_body, has_state=has_state, n_chunks=rows // GLA_CHUNK),
        grid=(B // nb, ng),
        in_specs=in_specs,
        out_specs=[fwd(), bwd(), state()],
        out_shape=[jax.ShapeDtypeStruct((B, L, W), F32), jax.ShapeDtypeStruct((B, L, W), F32),
                   jax.ShapeDtypeStruct((B, 2, W, W), F32)],
        scratch_shapes=[pltpu.VMEM((2 * nb, W, W), F32)],
        compiler_params=_cparams("parallel", "arbitrary"),
        name="gla",
    )(*args)


TM_DFT = 512
BG_DFT = 4


def _conv3_body(x_ref, w_ref, b_ref, u_ref, ub_ref):
    x = x_ref[0]
    L = x.shape[0]
    row = lax.broadcasted_iota(jnp.int32, (L, 1), 0)
    prev = jnp.where(row == 0, 0.0, pltpu.roll(x, 1, 0))
    nxt = jnp.where(row == L - 1, 0.0, pltpu.roll(x, L - 1, 0))
    u = prev * w_ref[0:1, :] + x * w_ref[1:2, :] + nxt * w_ref[2:3, :] + b_ref[...]
    u_ref[0] = u
    ub_ref[0] = u.astype(BF16)


def hyena_conv3(hy, w, b):
    B, L, C3 = hy.shape
    blk = pl.BlockSpec((1, L, HY_CH), lambda b_, j: (b_, 0, j))
    return pl.pallas_call(
        _conv3_body,
        grid=(B, C3 // HY_CH),
        in_specs=[blk, pl.BlockSpec((SUBLANES, HY_CH), lambda b_, j: (0, j)),
                  pl.BlockSpec((1, HY_CH), lambda b_, j: (0, j))],
        out_specs=[blk, blk],
        out_shape=[jax.ShapeDtypeStruct(hy.shape, F32), jax.ShapeDtypeStruct(hy.shape, BF16)],
        compiler_params=_cparams("parallel", "parallel"),
        name="hyena_conv3",
    )(hy, w, b)


def _freq_weight(i, tm, n):
    k = i * tm + lax.broadcasted_iota(jnp.int32, (tm, 1), 0)
    return jnp.where(k == 0, 1.0 / n, 2.0 / n).astype(F32)


def _filt_body(fr_ref, fi_ref, sg_ref, hs_ref, hd_ref, hr_ref, hi_ref, hny_ref, *, n):
    i = pl.program_id(0)
    wk = _freq_weight(i, fr_ref.shape[0], n)
    fr = fr_ref[...]
    fi = fi_ref[...]
    hr_ref[...] = wk * (jnp.dot(fr, hs_ref[0], preferred_element_type=F32)
                        + jnp.dot(fr, hs_ref[1], preferred_element_type=F32))
    hi_ref[...] = wk * (jnp.dot(fi, hd_ref[0], preferred_element_type=F32)
                        + jnp.dot(fi, hd_ref[1], preferred_element_type=F32))

    @pl.when(i == 0)
    def _nyquist():
        sg = sg_ref[...]
        hny_ref[...] = (jnp.dot(sg, hs_ref[0], preferred_element_type=F32)
                        + jnp.dot(sg, hs_ref[1], preferred_element_type=F32)) * (1.0 / n)


def hyena_filter_spectrum(fr, fi, sg, hs, hd, tm=TM_DFT):
    L = fr.shape[0]
    W = hs.shape[2]
    tm = min(tm, L)
    whole = lambda shape: pl.BlockSpec(shape, lambda i: (0,) * len(shape))
    return pl.pallas_call(
        functools.partial(_filt_body, n=2 * L),
        grid=(L // tm,),
        in_specs=[pl.BlockSpec((tm, L), lambda i: (i, 0)), pl.BlockSpec((tm, L), lambda i: (i, 0)),
                  whole((SUBLANES, L)), whole((2, L, W)), whole((2, L, W))],
        out_specs=[pl.BlockSpec((tm, W), lambda i: (i, 0)), pl.BlockSpec((tm, W), lambda i: (i, 0)),
                   whole((SUBLANES, W))],
        out_shape=[jax.ShapeDtypeStruct((L, W), F32), jax.ShapeDtypeStruct((L, W), F32),
                   jax.ShapeDtypeStruct((SUBLANES, W), F32)],
        compiler_params=_cparams("arbitrary"),
        name="hyena_filter_spectrum",
    )(fr, fi, sg, hs, hd)


def _dft_fwd_body(fr_ref, fi_ref, sg_ref, z_ref, hr_ref, hi_ref, hny_ref, yr_ref, yi_ref, yny_ref):
    nb = z_ref.shape[0]
    hr = hr_ref[...]
    hi = hi_ref[...]

    def body(b, carry):
        zb = z_ref[b]
        xr = jnp.dot(fr_ref[...], zb, preferred_element_type=F32)
        xi = jnp.dot(fi_ref[...], zb, preferred_element_type=F32)
        yr_ref[b] = (xr * hr - xi * hi).astype(BF16)
        yi_ref[b] = (xr * hi + xi * hr).astype(BF16)
        return carry
    lax.fori_loop(0, nb, body, 0)

    @pl.when(pl.program_id(1) == 0)
    def _nyquist():
        def nyq(b, carry):
            yny_ref[b] = jnp.dot(sg_ref[...], z_ref[b], preferred_element_type=F32) * hny_ref[0:1, :]
            return carry
        lax.fori_loop(0, nb, nyq, 0)


def hyena_dft_fwd(fr, fi, sg, zb, zcol, hr, hi, hny, order, tm=TM_DFT, bg=BG_DFT):
    B, L, _ = zb.shape
    C = HY_CH
    tm = min(tm, L)
    bg = B if L < TM_DFT else bg
    ftile = pl.BlockSpec((tm, L), lambda g, i: (i, 0))
    ytile = pl.BlockSpec((bg, tm, C), lambda g, i: (g, i, 0))
    return pl.pallas_call(
        _dft_fwd_body,
        grid=(B // bg, L // tm),
        in_specs=[ftile, ftile, pl.BlockSpec((SUBLANES, L), lambda g, i: (0, 0)),
                  pl.BlockSpec((bg, L, C), lambda g, i: (g, 0, zcol)),
                  pl.BlockSpec((tm, C), lambda g, i: (i, order)), pl.BlockSpec((tm, C), lambda g, i: (i, order)),
                  pl.BlockSpec((SUBLANES, C), lambda g, i: (0, order))],
        out_specs=[ytile, ytile, pl.BlockSpec((bg, SUBLANES, C), lambda g, i: (g, 0, 0))],
        out_shape=[jax.ShapeDtypeStruct((B, L, C), BF16), jax.ShapeDtypeStruct((B, L, C), BF16),
                   jax.ShapeDtypeStruct((B, SUBLANES, C), F32)],
        compiler_params=_cparams("parallel", "arbitrary"),
        name="hyena_dft_fwd",
    )(fr, fi, sg, zb, hr, hi, hny)


def _dft_inv_body(fr_ref, fi_ref, yr_ref, yi_ref, yny_ref, z_ref, gate_ref, d_ref, zo_ref, zob_ref):
    nb = z_ref.shape[0]
    tm = fr_ref.shape[0]
    t = pl.program_id(1) * tm + lax.broadcasted_iota(jnp.int32, (tm, 1), 0)
    sign = jnp.where((t & 1) == 0, 1.0, -1.0).astype(F32)
    d = d_ref[...]

    def body(b, carry):
        conv = (jnp.dot(fr_ref[...], yr_ref[b], preferred_element_type=F32)
                + jnp.dot(fi_ref[...], yi_ref[b], preferred_element_type=F32) + sign * yny_ref[b][0:1, :])
        zn = gate_ref[b] * (conv + d * z_ref[b])
        zo_ref[b] = zn
        zob_ref[b] = zn.astype(BF16)
        return carry
    lax.fori_loop(0, nb, body, 0)


def hyena_dft_inv(fr, fi, yr, yi, yny, z, zcol, gate, gcol, d, tm=TM_DFT, bg=BG_DFT):
    B, L, C = yr.shape
    tm = min(tm, L)
    bg = B if L < TM_DFT else bg
    ftile = pl.BlockSpec((tm, L), lambda g, i: (i, 0))
    whole_y = lambda: pl.BlockSpec((bg, L, C), lambda g, i: (g, 0, 0), pipeline_mode=pl.Buffered(1))
    otile = pl.BlockSpec((bg, tm, C), lambda g, i: (g, i, 0))
    return pl.pallas_call(
        _dft_inv_body,
        grid=(B // bg, L // tm),
        in_specs=[ftile, ftile, whole_y(), whole_y(),
                  pl.BlockSpec((bg, SUBLANES, C), lambda g, i: (g, 0, 0)),
                  pl.BlockSpec((bg, tm, C), lambda g, i: (g, i, zcol)),
                  pl.BlockSpec((bg, tm, C), lambda g, i: (g, i, gcol)),
                  pl.BlockSpec((1, C), lambda g, i: (0, 0))],
        out_specs=[otile, otile],
        out_shape=[jax.ShapeDtypeStruct((B, L, C), F32), jax.ShapeDtypeStruct((B, L, C), BF16)],
        compiler_params=_cparams("parallel", "arbitrary"),
        name="hyena_dft_inv",
    )(fr, fi, yr, yi, yny, z, gate, d)


def dft_matrices(L):
    r = 1 << (max(L.bit_length() - 1, 0) // 2)
    t = jnp.arange(L, dtype=jnp.int32)

    def table(k):
        ang = ((k[:, None] * t[None, :]) % (2 * L)).astype(F32) * (math.pi / L)
        return jnp.cos(ang), jnp.sin(ang)
    ca, sa = table(jnp.arange(L // r, dtype=jnp.int32) * r)
    cb, sb = table(jnp.arange(r, dtype=jnp.int32))
    fr = ca[:, None, :] * cb[None, :, :] - sa[:, None, :] * sb[None, :, :]
    fi = -(sa[:, None, :] * cb[None, :, :] + ca[:, None, :] * sb[None, :, :])
    sg = jnp.zeros((SUBLANES, L), F32).at[0].set(jnp.where(t % 2 == 0, 1.0, -1.0))
    return fr.reshape(L, L).astype(BF16), fi.reshape(L, L).astype(BF16), sg.astype(BF16)


def hyena_time_filters(L, lp):
    t = jnp.arange(L, dtype=F32)
    t_norm = t / max(L - 1, 1)
    w = (2.0 * math.pi / L) * t
    f = jnp.linspace(1e-4, HY_BANDS - 1, HY_BANDS, dtype=F32)
    fw = w[:, None] * f[None, :]
    feat = jnp.concatenate([t_norm[:, None], jnp.cos(fw), -jnp.sin(fw)], axis=-1)
    z = jnp.sin(lp['hy_freq1'] * (feat @ lp['hy_w1'] + lp['hy_b1']))
    z = jnp.sin(lp['hy_freq2'] * (z @ lp['hy_w2'] + lp['hy_b2']))
    hf = (z @ lp['hy_w3']).astype(F32).reshape(L, 2, HY_ORDER, HY_CH)
    hf = hf * jnp.exp(-t_norm[:, None, None, None] * jnp.abs(lp['hy_decay'].astype(F32)))
    return hf / (jnp.sum(jnp.abs(hf), axis=(0, 1), keepdims=True) + EPS)


def _hi_lo(x):
    hi = x.astype(BF16)
    return jnp.stack([hi, (x - hi.astype(F32)).astype(BF16)])


def hyena(hy, lp, mats):
    B, L, _ = hy.shape
    fr, fi, sg = mats
    hf = hyena_time_filters(L, lp)
    h_fwd = hf[:, 0].reshape(L, HY_ORDER * HY_CH)
    h_bwd = hf[:, 1].at[0].set(0.0).reshape(L, HY_ORDER * HY_CH)
    hr, hi, hny = hyena_filter_spectrum(fr, fi, sg, _hi_lo(h_fwd + h_bwd), _hi_lo(h_fwd - h_bwd))
    w = jnp.pad(lp['hy_conv_w'], ((0, SUBLANES - 3), (0, 0)))
    u, ub = hyena_conv3(hy, w, lp['hy_conv_b'][None, :])
    d = lp['hy_d'].astype(F32)
    z, zb, zcol = u, ub, 0
    for o in range(HY_ORDER):
        yr, yi, yny = hyena_dft_fwd(fr, fi, sg, zb, zcol, hr, hi, hny, o)
        z, zb = hyena_dft_inv(fr, fi, yr, yi, yny, z, zcol, u, 1 + o, d[o][None, :])
        zcol = 0
    return z


BG_HALF = 2


def _shift_rows(x, down):
    n = x.shape[0]
    row = lax.broadcasted_iota(jnp.int32, (n, 1), 0)
    if down:
        return jnp.where(row == 0, 0.0, pltpu.roll(x, 1, 0))
    return jnp.where(row == n - 1, 0.0, pltpu.roll(x, n - 1, 0))


def _conv3_planes_body(e_ref, o_ref, w_ref, b_ref, u_ref, ub_ref):
    e = e_ref[0]
    o = o_ref[0]
    w0, w1, w2 = w_ref[0:1, :], w_ref[1:2, :], w_ref[2:3, :]
    ue = _shift_rows(o, True) * w0 + e * w1 + o * w2 + b_ref[...]
    uo = e * w0 + o * w1 + _shift_rows(e, False) * w2 + b_ref[...]
    u_ref[0, 0] = ue
    u_ref[0, 1] = uo
    ub_ref[0, 0] = ue.astype(BF16)
    ub_ref[0, 1] = uo.astype(BF16)


def hyena_conv3_planes(hy, w, b):
    B, L, C3 = hy.shape
    H = L // 2
    hy4 = hy.reshape(B, H, 2 * C3)
    plane = lambda p: pl.BlockSpec((1, H, HY_CH), lambda b_, j: (b_, 0, p * (C3 // HY_CH) + j))
    out = pl.BlockSpec((1, 2, H, HY_CH), lambda b_, j: (b_, 0, 0, j))
    return pl.pallas_call(
        _conv3_planes_body,
        grid=(B, C3 // HY_CH),
        in_specs=[plane(0), plane(1), pl.BlockSpec((SUBLANES, HY_CH), lambda b_, j: (0, j)),
                  pl.BlockSpec((1, HY_CH), lambda b_, j: (0, j))],
        out_specs=[out, out],
        out_shape=[jax.ShapeDtypeStruct((B, 2, H, C3), F32), jax.ShapeDtypeStruct((B, 2, H, C3), BF16)],
        compiler_params=_cparams("parallel", "parallel"),
        name="hyena_conv3",
    )(hy4, hy4, w, b)


def _half_filt_body(fe_ref, fo_ref, sg_ref, hse_ref, hso_ref, hde_ref, hdo_ref,
                    lr_ref, li_ref, hr_ref, hi_ref, sp_ref, *, n):
    i = pl.program_id(0)
    tm = lr_ref.shape[0]
    wk = _freq_weight(i, tm, n)

    def two_pass(m, parts_ref):
        return (jnp.dot(m, parts_ref[0], preferred_element_type=F32)
                + jnp.dot(m, parts_ref[1], preferred_element_type=F32))
    a = two_pass(fe_ref[0], hse_ref)
    b = two_pass(fo_ref[0], hso_ref)
    c = two_pass(fe_ref[0], hde_ref)
    d = two_pass(fo_ref[0], hdo_ref)
    lr_ref[...] = wk * (a[:tm] + b[:tm])
    hr_ref[...] = wk * (a[:tm] - b[:tm])
    li_ref[...] = wk * (c[tm:] + d[tm:])
    hi_ref[...] = wk * (d[tm:] - c[tm:])

    @pl.when(i == 0)
    def _middle_bin():
        sr = two_pass(sg_ref[...], hse_ref)[0:1, :]
        si = -two_pass(sg_ref[...], hdo_ref)[0:1, :]
        row = lax.broadcasted_iota(jnp.int32, sp_ref.shape, 0)
        sp_ref[...] = jnp.where(row == 0, sr, jnp.where(row == 1, si, 0.0)) * (2.0 / n)


def hyena_half_filter_spectrum(fe, fo, sg, hs, hd, tm):
    H = fe.shape[2]
    W = hs.shape[2]
    planes = lambda h: (h.reshape(2, H, 2, W)[:, :, 0], h.reshape(2, H, 2, W)[:, :, 1])
    hse, hso = planes(hs)
    hde, hdo = planes(hd)
    whole = lambda shape: pl.BlockSpec(shape, lambda i: (0,) * len(shape))
    ftile = pl.BlockSpec((1, 2 * tm, H), lambda i: (i, 0, 0))
    otile = pl.BlockSpec((tm, W), lambda i: (i, 0))
    out = jax.ShapeDtypeStruct((H, W), F32)
    return pl.pallas_call(
        functools.partial(_half_filt_body, n=4 * H),
        grid=(H // tm,),
        in_specs=[ftile, ftile, whole((SUBLANES, H))] + [whole((2, H, W))] * 4,
        out_specs=[otile, otile, otile, otile, whole((SUBLANES, W))],
        out_shape=[out, out, out, out, jax.ShapeDtypeStruct((SUBLANES, W), F32)],
        compiler_params=_cparams("arbitrary"),
        name="hyena_filter_spectrum",
    )(fe, fo, sg, hse, hso, hde, hdo)


def _cmul(xr, xi, hr, hi):
    return xr * hr - xi * hi, xr * hi + xi * hr


def _half_fwd_body(fe_ref, fo_ref, sg_ref, ze_ref, zo_ref, lr_ref, li_ref, hr_ref, hi_ref, sp_ref,
                   per_ref, pei_ref, por_ref, poi_ref, ysp_ref):
    nb = ze_ref.shape[0]
    tm = lr_ref.shape[0]

    def body(b, carry):
        a = jnp.dot(fe_ref[0], ze_ref[b], preferred_element_type=F32)
        o = jnp.dot(fo_ref[0], zo_ref[b], preferred_element_type=F32)
        ylr, yli = _cmul(a[:tm] + o[:tm], a[tm:] + o[tm:], lr_ref[...], li_ref[...])
        yhr, yhi = _cmul(a[:tm] - o[:tm], o[tm:] - a[tm:], hr_ref[...], hi_ref[...])
        per_ref[b] = (ylr + yhr).astype(BF16)
        pei_ref[b] = (yli - yhi).astype(BF16)
        por_ref[b] = (ylr - yhr).astype(BF16)
        poi_ref[b] = (yli + yhi).astype(BF16)
        return carry
    lax.fori_loop(0, nb, body, 0)

    @pl.when(pl.program_id(1) == 0)
    def _middle_bin():
        def mid(b, carry):
            xr = jnp.dot(sg_ref[...], ze_ref[b], preferred_element_type=F32)[0:1, :]
            xi = -jnp.dot(sg_ref[...], zo_ref[b], preferred_element_type=F32)[0:1, :]
            yr, yi = _cmul(xr, xi, sp_ref[0:1, :], sp_ref[1:2, :])
            row = lax.broadcasted_iota(jnp.int32, ysp_ref.shape[1:], 0)
            ysp_ref[b] = jnp.where(row == 0, yr, jnp.where(row == 1, yi, 0.0))
            return carry
        lax.fori_loop(0, nb, mid, 0)


def hyena_half_fwd(fe, fo, sg, zb, zcol, spec, order, tm, bg):
    B, _, H, _ = zb.shape
    C = HY_CH
    lr, li, hr, hi, sp = spec
    ftile = pl.BlockSpec((1, 2 * tm, H), lambda g, i: (i, 0, 0))
    plane = lambda p: pl.BlockSpec((bg, None, H, C), lambda g, i: (g, p, 0, zcol))
    stile = pl.BlockSpec((tm, C), lambda g, i: (i, order))
    ptile = pl.BlockSpec((bg, tm, C), lambda g, i: (g, i, 0))
    pshape = jax.ShapeDtypeStruct((B, H, C), BF16)
    return pl.pallas_call(
        _half_fwd_body,
        grid=(B // bg, H // tm),
        in_specs=[ftile, ftile, pl.BlockSpec((SUBLANES, H), lambda g, i: (0, 0)), plane(0), plane(1),
                  stile, stile, stile, stile, pl.BlockSpec((SUBLANES, C), lambda g, i: (0, order))],
        out_specs=[ptile, ptile, ptile, ptile, pl.BlockSpec((bg, SUBLANES, C), lambda g, i: (g, 0, 0))],
        out_shape=[pshape, pshape, pshape, pshape, jax.ShapeDtypeStruct((B, SUBLANES, C), F32)],
        compiler_params=_cparams("parallel", "arbitrary"),
        name="hyena_dft_fwd",
    )(fe, fo, sg, zb, zb, lr, li, hr, hi, sp)


def _half_inv_body(fer_ref, fei_ref, fotr_ref, foti_ref, per_ref, pei_ref, por_ref, poi_ref, ysp_ref,
                   ze_ref, zo_ref, ge_ref, go_ref, d_ref, z_ref, zb_ref):
    nb = ze_ref.shape[0]
    tm = fer_ref.shape[0]
    u = pl.program_id(1) * tm + lax.broadcasted_iota(jnp.int32, (tm, 1), 0)
    sign = jnp.where((u & 1) == 0, 1.0, -1.0).astype(F32)
    d = d_ref[...]

    def body(b, carry):
        ce = (jnp.dot(fer_ref[...], per_ref[b], preferred_element_type=F32)
              + jnp.dot(fei_ref[...], pei_ref[b], preferred_element_type=F32) + sign * ysp_ref[b][0:1, :])
        co = (jnp.dot(fotr_ref[...], por_ref[b], preferred_element_type=F32)
              + jnp.dot(foti_ref[...], poi_ref[b], preferred_element_type=F32) - sign * ysp_ref[b][1:2, :])
        zne = ge_ref[b] * (ce + d * ze_ref[b])
        zno = go_ref[b] * (co + d * zo_ref[b])
        z_ref[b, 0] = zne
        z_ref[b, 1] = zno
        zb_ref[b, 0] = zne.astype(BF16)
        zb_ref[b, 1] = zno.astype(BF16)
        return carry
    lax.fori_loop(0, nb, body, 0)


def hyena_half_inv(mats, ps, ysp, z, zcol, gate, gcol, d, tm, bg):
    fer, fei, fotr, foti = mats
    per = ps[0]
    B, H, C = per.shape
    ftile = pl.BlockSpec((tm, H), lambda g, i: (i, 0))
    whole_p = lambda: pl.BlockSpec((bg, H, C), lambda g, i: (g, 0, 0), pipeline_mode=pl.Buffered(1))
    tile = lambda p, col: pl.BlockSpec((bg, None, tm, C), lambda g, i: (g, p, i, col))
    otile = pl.BlockSpec((bg, 2, tm, C), lambda g, i: (g, 0, i, 0))
    return pl.pallas_call(
        _half_inv_body,
        grid=(B // bg, H // tm),
        in_specs=[ftile, ftile, ftile, ftile, whole_p(), whole_p(), whole_p(), whole_p(),
                  pl.BlockSpec((bg, SUBLANES, C), lambda g, i: (g, 0, 0)),
                  tile(0, zcol), tile(1, zcol), tile(0, gcol), tile(1, gcol),
                  pl.BlockSpec((1, C), lambda g, i: (0, 0))],
        out_specs=[otile, otile],
        out_shape=[jax.ShapeDtypeStruct((B, 2, H, C), F32), jax.ShapeDtypeStruct((B, 2, H, C), BF16)],
        compiler_params=_cparams("parallel", "arbitrary"),
        name="hyena_dft_inv",
    )(fer, fei, fotr, foti, *ps, ysp, z, z, gate, gate, d)


def dft_half_matrices(L, tm):
    H = L // 2
    r = 1 << (max(H.bit_length() - 1, 0) // 2)
    t = jnp.arange(L, dtype=jnp.int32)

    def table(k):
        ang = ((k[:, None] * t[None, :]) % (2 * L)).astype(F32) * (math.pi / L)
        return jnp.cos(ang), jnp.sin(ang)
    ca, sa = table(jnp.arange(H // r, dtype=jnp.int32) * r)
    cb, sb = table(jnp.arange(r, dtype=jnp.int32))
    fr = (ca[:, None, :] * cb[None, :, :] - sa[:, None, :] * sb[None, :, :]).reshape(H, H, 2)
    fi = (-(sa[:, None, :] * cb[None, :, :] + ca[:, None, :] * sb[None, :, :])).reshape(H, H, 2)
    fre, fro = fr[:, :, 0].astype(BF16), fr[:, :, 1].astype(BF16)
    fie, fio = fi[:, :, 0].astype(BF16), fi[:, :, 1].astype(BF16)
    stack = lambda a, b: jnp.concatenate([a.reshape(H // tm, tm, H), b.reshape(H // tm, tm, H)], axis=1)
    sg = jnp.zeros((SUBLANES, H), F32).at[0].set(jnp.where(t[:H] % 2 == 0, 1.0, -1.0)).astype(BF16)
    return dict(fe=stack(fre, fie), fo=stack(fro, fio), sg=sg, inv=(fre, fie, fro.T, fio.T), tm=tm)


def hyena_half(hy, lp, mats):
    B, L, _ = hy.shape
    tm = mats['tm']
    bg = B if L < TM_DFT else BG_HALF
    hf = hyena_time_filters(L, lp)
    h_fwd = hf[:, 0].reshape(L, HY_ORDER * HY_CH)
    h_bwd = hf[:, 1].at[0].set(0.0).reshape(L, HY_ORDER * HY_CH)
    spec = hyena_half_filter_spectrum(mats['fe'], mats['fo'], mats['sg'], _hi_lo(h_fwd + h_bwd),
                                      _hi_lo(h_fwd - h_bwd), tm)
    w = jnp.pad(lp['hy_conv_w'], ((0, SUBLANES - 3), (0, 0)))
    u, ub = hyena_conv3_planes(hy, w, lp['hy_conv_b'][None, :])
    d = lp['hy_d'].astype(F32)
    z, zb, zcol = u, ub, 0
    for o in range(HY_ORDER):
        *ps, ysp = hyena_half_fwd(mats['fe'], mats['fo'], mats['sg'], zb, zcol, spec, o, tm, bg)
        z, zb = hyena_half_inv(mats['inv'], ps, ysp, z, zcol, u, 1 + o, d[o][None, :], tm, bg)
        zcol = 0
    return jnp.swapaxes(z, 1, 2).reshape(B, L, HY_CH)


def _ffn_body(x_ref, mod_ref, g_ref, w1_ref, w3_ref, w2_ref, o_ref):
    x = x_ref[...]
    h = _adaln_rows(x, g_ref[...], mod_ref[0, ROW_SHIFT2:ROW_SHIFT2 + 1, :],
                    mod_ref[0, ROW_SCALE2:ROW_SCALE2 + 1, :]).astype(BF16)
    a = jnp.dot(h, w1_ref[...], preferred_element_type=F32)
    b = jnp.dot(h, w3_ref[...], preferred_element_type=F32)
    act = (_silu(a) * b).astype(BF16)
    ff = jnp.dot(act, w2_ref[...], preferred_element_type=F32)
    o_ref[...] = x + mod_ref[0, ROW_GATE2:ROW_GATE2 + 1, :] * ff


def ffn_dense(x, mods, g, w1, w3, w2, seg_len, tm=TM_PROJ):
    T, D = x.shape
    F = w1.shape[1]
    resident = functools.partial(pl.BlockSpec, pipeline_mode=pl.Buffered(1))
    return pl.pallas_call(
        _ffn_body,
        grid=(T // tm,),
        in_specs=[
            pl.BlockSpec((tm, D), lambda i: (i, 0)),
            pl.BlockSpec((1, MOD_ROWS, D), lambda i: (i * tm // seg_len, 0, 0)),
            pl.BlockSpec((1, D), lambda i: (0, 0)),
            resident((D, F), lambda i: (0, 0)),
            resident((D, F), lambda i: (0, 0)),
            resident((F, D), lambda i: (0, 0)),
        ],
        out_specs=pl.BlockSpec((tm, D), lambda i: (i, 0)),
        out_shape=jax.ShapeDtypeStruct((T, D), F32),
        compiler_params=_cparams("parallel"),
        name="ffn_dense",
    )(x, mods, g, w1, w3, w2)


def _store_token_tiles(ref, x):
    n = x.shape[0]
    for s in range(SUBLANES):
        ref[pl.ds(s, n, stride=SUBLANES), :] = x[:, s * LANES:(s + 1) * LANES]


def _load_token_tiles(ref, n):
    return jnp.concatenate([ref[pl.ds(s, n, stride=SUBLANES), :] for s in range(SUBLANES)], axis=1)


def _router_body(xa_ref, xb_ref, mod_ref, g_ref, wr_ref, h_ref, idx_ref, gw_ref, *, na):
    x = jnp.where(pl.program_id(0) < na, xa_ref[...], xb_ref[...])
    h = _adaln_rows(x, g_ref[...], mod_ref[0, ROW_SHIFT2:ROW_SHIFT2 + 1, :],
                    mod_ref[0, ROW_SCALE2:ROW_SCALE2 + 1, :])
    _store_token_tiles(h_ref, h)
    logits = lax.dot_general(wr_ref[...], h, (((1,), (1,)), ((), ())),
                             precision=lax.Precision.HIGHEST, preferred_element_type=F32)
    eidx = lax.broadcasted_iota(jnp.int32, logits.shape, 0)
    m1 = jnp.max(logits, axis=0, keepdims=True)
    i1 = jnp.min(jnp.where(logits == m1, eidx, N_EXPERTS), axis=0, keepdims=True)
    rest = jnp.where(eidx == i1, -jnp.inf, logits)
    m2 = jnp.max(rest, axis=0, keepdims=True)
    i2 = jnp.min(jnp.where(rest == m2, eidx, N_EXPERTS), axis=0, keepdims=True)
    e2 = jnp.exp(m2 - m1)
    den = 1.0 + e2
    row = lax.broadcasted_iota(jnp.int32, logits.shape, 0)
    idx_ref[...] = jnp.where(row == 0, i1, jnp.where(row == 1, i2, 0))
    gw_ref[...] = jnp.where(row == 0, 1.0 / den, jnp.where(row == 1, e2 / den, 0.0))


def _merged_seg(i, na, tm, seg_len_b):
    return jnp.where(i < na, 0, 1 + jnp.maximum(i - na, 0) * tm // seg_len_b)


def moe_router(xa, xb, mods, g, wr_t, seg_len_b, tm=TM_PROJ):
    D = xa.shape[1]
    na = xa.shape[0] // tm
    T = xa.shape[0] + xb.shape[0]
    return pl.pallas_call(
        functools.partial(_router_body, na=na),
        grid=(T // tm,),
        in_specs=[
            pl.BlockSpec((tm, D), lambda i: (jnp.minimum(i, na - 1), 0)),
            pl.BlockSpec((tm, D), lambda i: (jnp.maximum(i - na, 0), 0)),
            pl.BlockSpec((1, MOD_ROWS, D), lambda i: (_merged_seg(i, na, tm, seg_len_b), 0, 0)),
            pl.BlockSpec((1, D), lambda i: (0, 0)),
            pl.BlockSpec((N_EXPERTS, D), lambda i: (0, 0)),
        ],
        out_specs=[
            pl.BlockSpec((tm * SUBLANES, LANES), lambda i: (i, 0)),
            pl.BlockSpec((N_EXPERTS, tm), lambda i: (0, i)),
            pl.BlockSpec((N_EXPERTS, tm), lambda i: (0, i)),
        ],
        out_shape=[
            jax.ShapeDtypeStruct((T * SUBLANES, LANES), F32),
            jax.ShapeDtypeStruct((N_EXPERTS, T), jnp.int32),
            jax.ShapeDtypeStruct((N_EXPERTS, T), F32),
        ],
        compiler_params=_cparams("parallel"),
        name="moe_router",
    )(xa, xb, mods, g, wr_t)


def _experts_body(te_ref, nv_ref, src0_ref, srcn_ref, dst_ref, gate_ref, h_hbm, w1_ref, w3_ref, w2_ref,
                  y_hbm, hrows, hb, acc, ybuf, sem_g, sem_s, *, tm, nf, n_slots):
    i = pl.program_id(0)
    f = pl.program_id(1)
    nv = nv_ref[0]
    valid = i < nv
    slot = lax.rem(i, 2)

    def gather_issue(idx_ref, s):
        def body(r, carry):
            src = pl.multiple_of(idx_ref[0, 0, r] * SUBLANES, SUBLANES)
            dst = pl.multiple_of(r * SUBLANES, SUBLANES)
            pltpu.make_async_copy(h_hbm.at[pl.ds(src, SUBLANES)], hrows.at[s, pl.ds(dst, SUBLANES)],
                                  sem_g.at[s]).start()
            return carry
        lax.fori_loop(0, tm, body, 0, unroll=8)

    def gather_wait(s):
        pltpu.make_async_copy(h_hbm.at[pl.ds(0, tm * SUBLANES)], hrows.at[s], sem_g.at[s]).wait()

    def scatter_wait(s):
        pltpu.make_async_copy(ybuf.at[s], y_hbm.at[pl.ds(0, tm * SUBLANES)], sem_s.at[s]).wait()

    @pl.when(jnp.logical_and(i == 0, f == 0))
    def _first():
        ybuf[0] = jnp.zeros(ybuf.shape[1:], ybuf.dtype)
        spare = lambda p: y_hbm.at[pl.ds((n_slots + p * tm) * SUBLANES, tm * SUBLANES)]
        for p in range(2):
            pltpu.make_async_copy(ybuf.at[0], spare(p), sem_s.at[p]).start()
        for p in range(2):
            pltpu.make_async_copy(ybuf.at[0], spare(p), sem_s.at[p]).wait()
        gather_issue(src0_ref, 0)

    @pl.when(jnp.logical_and(valid, f == 0))
    def _stage():
        gather_wait(slot)
        hb[...] = _load_token_tiles(hrows.at[slot], tm).astype(BF16)
        acc[...] = jnp.zeros_like(acc)

        @pl.when(i + 1 < nv)
        def _prefetch():
            gather_issue(srcn_ref, 1 - slot)

    @pl.when(valid)
    def _compute():
        h = hb[...]
        a = jnp.dot(h, w1_ref[0], preferred_element_type=F32)
        b = jnp.dot(h, w3_ref[0], preferred_element_type=F32)
        act = (_silu(a) * b).astype(BF16)
        acc[...] += jnp.dot(act, w2_ref[0], preferred_element_type=F32)

    @pl.when(jnp.logical_and(valid, f == nf - 1))
    def _emit():
        @pl.when(i >= 2)
        def _reuse():
            scatter_wait(slot)
        _store_token_tiles(ybuf.at[slot], acc[...] * gate_ref[...])

        def body(r, carry):
            src = pl.multiple_of(r * SUBLANES, SUBLANES)
            dst = pl.multiple_of(dst_ref[0, 0, r] * SUBLANES, SUBLANES)
            pltpu.make_async_copy(ybuf.at[slot, pl.ds(src, SUBLANES)], y_hbm.at[pl.ds(dst, SUBLANES)],
                                  sem_s.at[slot]).start()
            return carry
        lax.fori_loop(0, tm, body, 0, unroll=8)

    @pl.when(jnp.logical_and(i == nv, f == 0))
    def _drain():
        scatter_wait(lax.rem(nv + 1, 2))

        @pl.when(nv >= 2)
        def _older():
            scatter_wait(lax.rem(nv, 2))


def moe_experts(h, tile_expert, n_valid, src_rows, dst_rows, gate_rows, w1, w3, w2, tm=TM_MOE, tf=TF_MOE):
    T = h.shape[0] // SUBLANES
    D = w1.shape[1]
    F = w1.shape[2]
    n_tiles = src_rows.shape[0]
    n_slots = TOP_K * T
    nf = F // tf

    def wcol(i, f, te, nv):
        return (te[i], 0, jnp.where(i < nv[0], f, nf - 1))

    def wrow(i, f, te, nv):
        return (te[i], jnp.where(i < nv[0], f, nf - 1), 0)

    smem_tile = lambda fn: pl.BlockSpec((1, 1, tm), fn, memory_space=pltpu.SMEM)
    grid_spec = pltpu.PrefetchScalarGridSpec(
        num_scalar_prefetch=2,
        grid=(n_tiles, nf),
        in_specs=[
            smem_tile(lambda i, f, te, nv: (0, 0, 0)),
            smem_tile(lambda i, f, te, nv: (jnp.minimum(i + 1, n_tiles - 1), 0, 0)),
            smem_tile(lambda i, f, te, nv: (i, 0, 0)),
            pl.BlockSpec((tm, 1), lambda i, f, te, nv: (i, 0)),
            pl.BlockSpec(memory_space=pl.ANY),
            pl.BlockSpec((1, D, tf), wcol),
            pl.BlockSpec((1, D, tf), wcol),
            pl.BlockSpec((1, tf, D), wrow),
        ],
        out_specs=pl.BlockSpec(memory_space=pl.ANY),
        scratch_shapes=[
            pltpu.VMEM((2, tm * SUBLANES, LANES), F32),
            pltpu.VMEM((tm, D), BF16),
            pltpu.VMEM((tm, D), F32),
            pltpu.VMEM((2, tm * SUBLANES, LANES), F32),
            pltpu.SemaphoreType.DMA((2,)),
            pltpu.SemaphoreType.DMA((2,)),
        ],
    )
    return pl.pallas_call(
        functools.partial(_experts_body, tm=tm, nf=nf, n_slots=n_slots),
        grid_spec=grid_spec,
        out_shape=jax.ShapeDtypeStruct(((n_slots + 2 * tm) * SUBLANES, LANES), F32),
        compiler_params=_cparams("arbitrary", "arbitrary"),
        name="moe_experts",
    )(tile_expert, n_valid, src_rows, src_rows, dst_rows, gate_rows, h, w1, w3, w2)


def _combine_body(x_ref, mod_ref, y0_ref, y1_ref, o_ref):
    n = x_ref.shape[0]
    y = _load_token_tiles(y0_ref, n) + _load_token_tiles(y1_ref, n)
    o_ref[...] = x_ref[...] + mod_ref[0, ROW_GATE2:ROW_GATE2 + 1, :] * y


def moe_combine(x, mods, y, row0, n_tok, seg_len, tm=TM_COMB):
    T, D = x.shape
    off = row0 // tm
    return pl.pallas_call(
        _combine_body,
        grid=(T // tm,),
        in_specs=[
            pl.BlockSpec((tm, D), lambda i: (i, 0)),
            pl.BlockSpec((1, MOD_ROWS, D), lambda i: (i * tm // seg_len, 0, 0)),
            pl.BlockSpec((tm * SUBLANES, LANES), lambda i: (off + i, 0)),
            pl.BlockSpec((tm * SUBLANES, LANES), lambda i: (off + n_tok // tm + i, 0)),
        ],
        out_specs=pl.BlockSpec((tm, D), lambda i: (i, 0)),
        out_shape=jax.ShapeDtypeStruct((T, D), F32),
        compiler_params=_cparams("parallel"),
        name="moe_combine",
    )(x, mods, y, y)


def moe_dispatch_plan(idx, gw, tm=TM_MOE):
    T = idx.shape[1]
    n_slots = TOP_K * T
    n_tiles = n_slots // tm + N_EXPERTS
    n_rows = n_tiles * tm
    e_flat = idx[:TOP_K].reshape(n_slots)
    g_flat = gw[:TOP_K].reshape(n_slots)
    counts = jnp.sum((e_flat[:, None] == jnp.arange(N_EXPERTS, dtype=jnp.int32)[None, :]).astype(jnp.int32),
                     axis=0)
    padded = (counts + tm - 1) // tm * tm
    ends = jnp.cumsum(padded)
    offs = ends - padded
    order = jnp.argsort(e_flat, stable=True).astype(jnp.int32)
    cstart = jnp.cumsum(counts) - counts
    rows = jnp.arange(n_rows, dtype=jnp.int32)
    row_e = jnp.minimum(jnp.searchsorted(ends, rows, side='right'), N_EXPERTS - 1).astype(jnp.int32)
    j = rows - offs[row_e]
    live = j < counts[row_e]
    slot = order[jnp.clip(cstart[row_e] + j, 0, n_slots - 1)]
    spare = n_slots + (rows // tm) % 2 * tm + rows % tm
    src_rows = jnp.where(live, slot % T, 0).astype(jnp.int32).reshape(n_tiles, 1, tm)
    dst_rows = jnp.where(live, slot, spare).astype(jnp.int32).reshape(n_tiles, 1, tm)
    gate_rows = jnp.where(live, g_flat[slot], 0.0).reshape(n_rows, 1)
    tile_start = jnp.arange(n_tiles, dtype=jnp.int32) * tm
    n_valid = (ends[-1] // tm).astype(jnp.int32).reshape(1)
    tile_expert = jnp.minimum(jnp.searchsorted(ends, tile_start, side='right'), N_EXPERTS - 1)
    last_e = tile_expert[jnp.maximum(n_valid[0] - 1, 0)]
    tile_expert = jnp.where(tile_start < ends[-1], tile_expert, last_e).astype(jnp.int32)
    return tile_expert, n_valid, src_rows, dst_rows, gate_rows


def ffn_moe(xa, xb, mods_a, mods_b, g, wr_t, w1, w3, w2, seg_len_b, tm=TM_PROJ, tme=TM_MOE, tmc=TM_COMB,
            tf=TF_MOE):
    na, nb = xa.shape[0], xb.shape[0]
    mods = jnp.concatenate([mods_a, mods_b], axis=0)
    h, idx, gw = moe_router(xa, xb, mods, g, wr_t, seg_len_b, tm)
    tile_expert, n_valid, src_rows, dst_rows, gate_rows = moe_dispatch_plan(idx, gw, tme)
    y = moe_experts(h, tile_expert, n_valid, src_rows, dst_rows, gate_rows, w1, w3, w2, tme, tf)
    return (moe_combine(xa, mods_a, y, 0, na + nb, na, tmc),
            moe_combine(xb, mods_b, y, na, na + nb, seg_len_b, tmc))


def axial_rope(L):
    rows = L // GRID_W
    r = jnp.repeat(jnp.arange(rows, dtype=F32), GRID_W)
    col = jnp.tile(jnp.arange(GRID_W, dtype=F32), rows)
    n = HEAD_DIM // 4
    freqs = ROPE_BASE ** (-jnp.arange(n, dtype=F32) / n)
    ang = jnp.concatenate([r[:, None] * freqs, col[:, None] * freqs], axis=-1)
    return jnp.cos(ang), jnp.sin(ang)


def rope_tables(L):
    cos, sin = axial_rope(L)
    n = HEAD_DIM // 4
    cos_h = jnp.concatenate([cos[:, :n], cos[:, :n], cos[:, n:], cos[:, n:]], axis=1)
    sin_h = jnp.concatenate([-sin[:, :n], sin[:, :n], -sin[:, n:], sin[:, n:]], axis=1)
    reps = LANES // HEAD_DIM
    return jnp.tile(cos_h, (1, reps)), jnp.tile(sin_h, (1, reps))


def block_diag_ones(width, block):
    i = jnp.arange(width) // block
    return (i[:, None] == i[None, :]).astype(BF16)


def gla_gate_params(gw, gb):
    w = jnp.zeros((LANES, 2 * C_WIDTH), F32)
    w = w.at[:GLA_RANK, :C_WIDTH].set(gw[0]).at[GLA_RANK:2 * GLA_RANK, C_WIDTH:].set(gw[1])
    return w.astype(BF16), jnp.concatenate([gb[0], gb[1]])[None, :]


def gla_state_to_blockdiag(st):
    B = st.shape[0]
    out = jnp.zeros((B, 2, C_HEADS, C_DV, C_HEADS, C_DK), F32)
    for h in range(C_HEADS):
        out = out.at[:, :, h, :, h, :].set(jnp.swapaxes(st[:, :, h], -1, -2))
    return out.reshape(B, 2, C_WIDTH, C_WIDTH)


def gla_state_from_blockdiag(sT):
    B = sT.shape[0]
    s6 = sT.reshape(B, 2, C_HEADS, C_DV, C_HEADS, C_DK)
    return jnp.stack([jnp.swapaxes(s6[:, :, h, :, h, :], -1, -2) for h in range(C_HEADS)], axis=2)


def mod_table(cvec, w_ada, b_ada):
    m = jax.nn.silu(cvec) @ w_ada + b_ada
    m = m.reshape(cvec.shape[0], 6, D_MODEL)
    return jnp.pad(m, ((0, 0), (0, MOD_ROWS - 6), (0, 0)))


def kernel(x_prompt, x_sample, cache_k, cache_v, state_gla, c, c_ctx, norm1_g, norm2_g, w_ada, b_ada,
           w_in, w_out, q_norm_g, k_norm_g, attn_sink, hy_conv_w, hy_conv_b, hy_w1, hy_b1, hy_freq1,
           hy_w2, hy_b2, hy_freq2, hy_w3, hy_decay, hy_d, gla_gate_w, gla_gate_b, gla_norm_g,
           ffn_w1, ffn_w3, ffn_w2, moe_router, moe_w1, moe_w3, moe_w2):
    D = D_MODEL
    xp = x_prompt.reshape(BATCH * SEQ, D)
    xs = x_sample.reshape(DEC_BATCH * DEC_SEQ, D)
    streams = [
        dict(x=xp, B=BATCH, L=SEQ, seg_len=BATCH * SEQ, cvec=c_ctx[None, :], latent=False),
        dict(x=xs, B=DEC_BATCH, L=DEC_SEQ, seg_len=DEC_SEQ, cvec=c, latent=True),
    ]
    ks_list, vs_list, st_list = [], [], []
    bd_q = block_diag_ones(A_WIDTH, HEAD_DIM)
    dft = {s['L']: dft_half_matrices(s['L'], min(TM_DFT, s['L'] // 2)) for s in streams}
    for l in range(DEPTH):
        lp = {
            'hy_conv_w': hy_conv_w[l], 'hy_conv_b': hy_conv_b[l],
            'hy_w1': hy_w1[l], 'hy_b1': hy_b1[l], 'hy_freq1': hy_freq1[l], 'hy_w2': hy_w2[l],
            'hy_b2': hy_b2[l], 'hy_freq2': hy_freq2[l], 'hy_w3': hy_w3[l], 'hy_decay': hy_decay[l],
            'hy_d': hy_d[l], 'gla_gate_w': gla_gate_w[l], 'gla_gate_b': gla_gate_b[l],
            'gla_norm_g': gla_norm_g[l],
        }
        w_in_l = jnp.pad(w_in[l], ((0, 0), (0, D_PROJ_PAD - D_PROJ))).astype(BF16)
        w_out_l = w_out[l].astype(BF16)
        g1 = norm1_g[l][None, :]
        g2 = norm2_g[l][None, :]
        qg = jnp.tile(q_norm_g[l], A_HEADS)[None, :]
        kg = jnp.tile(k_norm_g[l], A_KV_HEADS)[None, :]
        gg = jnp.tile(gla_norm_g[l], C_HEADS)[None, :]
        gate_w, gate_b = gla_gate_params(gla_gate_w[l], gla_gate_b[l])
        j = l // 2
        for s in streams:
            B, L = s['B'], s['L']
            mods = mod_table(s['cvec'], w_ada[l], b_ada[l])
            rope = rope_tables(L) if s['latent'] else None
            q, k, v, hy, cq, ck, cv, cg, la = in_proj(s['x'], mods, g1, w_in_l, bd_q, qg, kg, gate_w, gate_b,
                                                      s['seg_len'], rope, L)
            seq = lambda t: t.reshape(B, L, t.shape[-1])
            if s['latent']:
                kc = cache_k[:, l].reshape(DEC_BATCH, PAST_LEN, LANES)
                vc = cache_v[:, l].reshape(DEC_BATCH, PAST_LEN, LANES)
                a_out = attention(seq(q), seq(k), seq(v), attn_sink[l], kc, vc)
                s0 = gla_state_to_blockdiag(state_gla[:, l])
                o_f, o_b, _ = gla(seq(cq), seq(ck), seq(cv), seq(la), s0)
            else:
                a_out = attention(seq(q), seq(k), seq(v), attn_sink[l])
                o_f, o_b, sT = gla(seq(cq), seq(ck), seq(cv), seq(la))
                ks_list.append(k.reshape(B, L, A_KV_HEADS, HEAD_DIM))
                vs_list.append(v.reshape(B, L, A_KV_HEADS, HEAD_DIM))
                st_list.append(gla_state_from_blockdiag(sT))
            h_out = hyena_half(seq(hy), lp, dft[L])
            flat = lambda t: t.reshape(B * L, t.shape[-1])
            x1 = out_proj(s['x'], flat(a_out), flat(h_out), flat(o_f), flat(o_b), cg, mods, w_out_l,
                          bd_q[:C_WIDTH, :C_WIDTH], gg, s['seg_len'])
            if l % 2 == 0:
                s['x'] = ffn_dense(x1, mods, g2, ffn_w1[j].astype(BF16), ffn_w3[j].astype(BF16),
                                   ffn_w2[j].astype(BF16), s['seg_len'])
            else:
                s['x'], s['mods'] = x1, mods
        if l % 2 == 1:
            sa, sb = streams
            sa['x'], sb['x'] = ffn_moe(sa['x'], sb['x'], sa['mods'], sb['mods'], g2, moe_router[j].T,
                                       moe_w1[j].astype(BF16), moe_w3[j].astype(BF16), moe_w2[j].astype(BF16),
                                       sb['seg_len'])
    y_prompt = streams[0]['x'].reshape(BATCH, SEQ, D)
    y_sample = streams[1]['x'].reshape(DEC_BATCH, DEC_SEQ, D)
    new_cache_k = jnp.stack(ks_list, axis=1)
    new_cache_v = jnp.stack(vs_list, axis=1)
    new_state_gla = jnp.stack(st_list, axis=1)
    return (y_prompt, y_sample, new_cache_k, new_cache_v, new_state_gla)
```

```python
import math
import functools
import jax
import jax.numpy as jnp
from jax import lax
import numpy as np
from jax.experimental import pallas as pl
from jax.experimental.pallas import tpu as pltpu

D_MODEL = 1024
BATCH = 32
SEQ = 256
DEPTH = 2
DEC_BATCH = 8
DEC_SEQ = 4096
PAST_LEN = 512

GRID_W = 64
HEAD_DIM = 64
D_MIX = D_MODEL
A_HEADS = 8
A_KV_HEADS = 2
A_GROUP = A_HEADS // A_KV_HEADS
A_WIDTH = A_HEADS * HEAD_DIM
WINDOW = 128
BLK = 128
ROPE_BASE = 10000.0
HY_CH = 256
HY_ORDER = 2
HY_BANDS = 16
HY_EMB = 1 + 2 * HY_BANDS
HY_HID = 64
C_HEADS = 4
C_DK = 64
C_DV = 64
C_WIDTH = C_HEADS * C_DV
GLA_RANK = 16
GLA_TAU = 16.0
GLA_CHUNK = 64
D_FF = 2816
N_EXPERTS = 8
TOP_K = 2
D_FF_EXPERT = 3584

PROJ_SIZES = (A_WIDTH, A_KV_HEADS * HEAD_DIM, A_KV_HEADS * HEAD_DIM, 3 * HY_CH,
              C_HEADS * C_DK, C_HEADS * C_DK, C_WIDTH, C_WIDTH, 2 * GLA_RANK)
PROJ_SPLITS = tuple(int(s) for s in np.cumsum(PROJ_SIZES)[:-1])
D_PROJ = int(sum(PROJ_SIZES))

F32 = jnp.float32
BF16 = jnp.bfloat16
ATT_SCALE = HEAD_DIM ** -0.5
NEG_INF = -1e30
EPS = 1e-6

LANES = 128
SUBLANES = 8
D_PROJ_PAD = -(-D_PROJ // LANES) * LANES
VMEM_LIMIT_BYTES = 56 * 1024 * 1024

MOD_ROWS = SUBLANES
ROW_SHIFT1, ROW_SCALE1, ROW_GATE1, ROW_SHIFT2, ROW_SCALE2, ROW_GATE2 = range(6)

TM_PROJ = 512
TM_MOE = 1024
TF_MOE = 896
TM_COMB = 512


def _cparams(*sem):
    return pltpu.CompilerParams(dimension_semantics=sem, vmem_limit_bytes=VMEM_LIMIT_BYTES)


def _adaln_rows(x, g, shift, scale):
    ms = jnp.mean(x * x, axis=-1, keepdims=True)
    return (x * lax.rsqrt(ms + EPS) * g) * (1.0 + scale) + shift


def _silu(a):
    return a * jax.nn.sigmoid(a)


def _group_mean_sq(x, ones_bd):
    sq = x * x
    hi = sq.astype(BF16)
    lo = (sq - hi.astype(F32)).astype(BF16)
    s = jnp.dot(hi, ones_bd, preferred_element_type=F32) + jnp.dot(lo, ones_bd, preferred_element_type=F32)
    return s * (1.0 / HEAD_DIM)


def _rope_rows(x, cos_t, sin_t):
    q4 = HEAD_DIM // 4
    lane = lax.broadcasted_iota(jnp.int32, x.shape, 1)
    partner = jnp.where((lane % (2 * q4)) < q4, pltpu.roll(x, LANES - q4, 1), pltpu.roll(x, q4, 1))
    return x * cos_t + partner * sin_t


def _log_sigmoid(x):
    return jnp.minimum(x, 0.0) - jnp.log(1.0 + jnp.exp(-jnp.abs(x)))


def _inproj_body(*refs, latent):
    if latent:
        (x_ref, mod_ref, g_ref, w_ref, bd_ref, qg_ref, kg_ref, gw_ref, gb_ref, cos_ref, sin_ref,
         q_ref, k_ref, v_ref, hy_ref, cq_ref, ck_ref, cv_ref, cg_ref, la_ref) = refs
    else:
        (x_ref, mod_ref, g_ref, w_ref, bd_ref, qg_ref, kg_ref, gw_ref, gb_ref,
         q_ref, k_ref, v_ref, hy_ref, cq_ref, ck_ref, cv_ref, cg_ref, la_ref) = refs
    h = _adaln_rows(x_ref[...], g_ref[...], mod_ref[0, ROW_SHIFT1:ROW_SHIFT1 + 1, :],
                    mod_ref[0, ROW_SCALE1:ROW_SCALE1 + 1, :])
    acc = jnp.dot(h.astype(BF16), w_ref[...], preferred_element_type=F32)
    o = 0
    q = acc[:, o:o + A_WIDTH]
    o += A_WIDTH
    k = acc[:, o:o + LANES]
    o += LANES
    v_ref[...] = acc[:, o:o + LANES]
    o += LANES
    hy_ref[...] = acc[:, o:o + 3 * HY_CH]
    o += 3 * HY_CH
    cq_ref[...] = acc[:, o:o + C_WIDTH] * (C_DK ** -0.5)
    o += C_WIDTH
    ck_ref[...] = acc[:, o:o + C_WIDTH]
    o += C_WIDTH
    cv_ref[...] = acc[:, o:o + C_WIDTH]
    o += C_WIDTH
    cg_ref[...] = acc[:, o:o + C_WIDTH]
    o += C_WIDTH
    r = acc[:, o:o + LANES]
    la_ref[...] = _log_sigmoid(jnp.dot(r.astype(BF16), gw_ref[...], preferred_element_type=F32)
                               + gb_ref[...]) * (1.0 / GLA_TAU)
    q = q * lax.rsqrt(_group_mean_sq(q, bd_ref[...]) + EPS) * qg_ref[...]
    k = k * lax.rsqrt(_group_mean_sq(k, bd_ref[0:LANES, 0:LANES]) + EPS) * kg_ref[...]
    if latent:
        cos_t = cos_ref[...]
        sin_t = sin_ref[...]
        q = jnp.concatenate([_rope_rows(q[:, j * LANES:(j + 1) * LANES], cos_t, sin_t)
                             for j in range(A_WIDTH // LANES)], axis=1)
        k = _rope_rows(k, cos_t, sin_t)
    q_ref[...] = (q * ATT_SCALE).astype(BF16)
    k_ref[...] = k


def in_proj(x, mods, g, w, bd, qg, kg, gw, gb, seg_len, rope=None, seq_len=None, tm=TM_PROJ):
    T, D = x.shape
    N = w.shape[1]
    latent = rope is not None
    const = lambda shape: pl.BlockSpec(shape, lambda i: (0,) * len(shape))
    in_specs = [
        pl.BlockSpec((tm, D), lambda i: (i, 0)),
        pl.BlockSpec((1, MOD_ROWS, D), lambda i: (i * tm // seg_len, 0, 0)),
        const((1, D)), const((D, N)), const((A_WIDTH, A_WIDTH)), const((1, A_WIDTH)), const((1, LANES)),
        const((LANES, 2 * C_WIDTH)), const((1, 2 * C_WIDTH)),
    ]
    args = [x, mods, g, w, bd, qg, kg, gw, gb]
    if latent:
        tiles_per_seq = seq_len // tm
        in_specs += [pl.BlockSpec((tm, LANES), lambda i: (i % tiles_per_seq, 0))] * 2
        args += list(rope)
    widths = [A_WIDTH, LANES, LANES, 3 * HY_CH, C_WIDTH, C_WIDTH, C_WIDTH, C_WIDTH, 2 * C_WIDTH]
    dtypes = [BF16] + [F32] * 8
    return pl.pallas_call(
        functools.partial(_inproj_body, latent=latent),
        grid=(T // tm,),
        in_specs=in_specs,
        out_specs=[pl.BlockSpec((tm, wd), lambda i: (i, 0)) for wd in widths],
        out_shape=[jax.ShapeDtypeStruct((T, wd), dt) for wd, dt in zip(widths, dtypes)],
        compiler_params=_cparams("parallel"),
        name="in_proj_latent" if latent else "in_proj_context",
    )(*args)


def _outproj_body(x_ref, a_ref, h0_ref, h1_ref, of_ref, ob_ref, cg_ref, mod_ref, w_ref, bd_ref, gg_ref, o_ref):
    o = of_ref[...] + ob_ref[...]
    g_out = o * lax.rsqrt(_group_mean_sq(o, bd_ref[...]) + EPS) * gg_ref[...] * _silu(cg_ref[...])
    h0 = A_WIDTH
    g0 = A_WIDTH + HY_CH
    mix = jnp.dot(a_ref[...], w_ref[0:h0, :], preferred_element_type=F32)
    h = jnp.concatenate([h0_ref[...], h1_ref[...]], axis=1)
    mix += jnp.dot(h.astype(BF16), w_ref[h0:g0, :], preferred_element_type=F32)
    mix += jnp.dot(g_out.astype(BF16), w_ref[g0:, :], preferred_element_type=F32)
    o_ref[...] = x_ref[...] + mod_ref[0, ROW_GATE1:ROW_GATE1 + 1, :] * mix


def out_proj(x, a, h, of, ob, cg, mods, w, bd, gg, seg_len, tm=TM_PROJ):
    T, D = x.shape
    row = lambda wd: pl.BlockSpec((tm, wd), lambda i: (i, 0))
    const = lambda shape: pl.BlockSpec(shape, lambda i: (0,) * len(shape))
    return pl.pallas_call(
        _outproj_body,
        grid=(T // tm,),
        in_specs=[
            row(D), row(A_WIDTH), row(LANES), row(LANES), row(C_WIDTH), row(C_WIDTH), row(C_WIDTH),
            pl.BlockSpec((1, MOD_ROWS, D), lambda i: (i * tm // seg_len, 0, 0)),
            const((D, D)), const((C_WIDTH, C_WIDTH)), const((1, C_WIDTH)),
        ],
        out_specs=row(D),
        out_shape=jax.ShapeDtypeStruct((T, D), F32),
        compiler_params=_cparams("parallel"),
        name="out_proj",
    )(x, a, h[0], h[1], of, ob, cg, mods, w, bd, gg)


TQ_ATT = 256
HEADS_PER_STACK = 2
KWIN_ATT = TQ_ATT + 2 * WINDOW


def _dup_heads(x, g):
    lane = lax.broadcasted_iota(jnp.int32, x.shape, 1)
    rolled = pltpu.roll(x, HEAD_DIM, 1)
    keep = (lane < HEAD_DIM) if g == 0 else (lane >= HEAD_DIM)
    return jnp.where(keep, x, rolled)


def _attn_body(*refs, latent, seq_len):
    if latent:
        sink_ref, q_ref, k_ref, v_ref, kc_ref, vc_ref, o_ref = refs
    else:
        sink_ref, q_ref, k_ref, v_ref, o_ref = refs
    tq = q_ref.shape[1]
    i = pl.program_id(1)
    q = q_ref[0]
    lane = lax.broadcasted_iota(jnp.int32, (tq, LANES), 1)
    low = lane < HEAD_DIM
    if latent:
        start = jnp.clip(i * tq - WINDOW, 0, seq_len - KWIN_ATT)
        start = pl.multiple_of(start, WINDOW)
        kl = k_ref[0, pl.ds(start, KWIN_ATT), :]
        vl = v_ref[0, pl.ds(start, KWIN_ATT), :]
        qpos = i * tq + lax.broadcasted_iota(jnp.int32, (tq, KWIN_ATT), 0)
        kpos = start + lax.broadcasted_iota(jnp.int32, (tq, KWIN_ATT), 1)
        bias = jnp.where(jnp.abs(qpos - kpos) <= WINDOW, 0.0, NEG_INF).astype(F32)
        bias = jnp.concatenate([bias] * A_GROUP, axis=0)
        kc = kc_ref[0]
        vc = vc_ref[0]
    else:
        kl = k_ref[0]
        vl = v_ref[0]
    nt = (((1,), (1,)), ((), ()))
    zero = jnp.zeros_like(q[:, :LANES])
    outs = []
    for g in range(A_KV_HEADS):
        kl_g = _dup_heads(kl, g).astype(BF16)
        vl_g = _dup_heads(vl, g).astype(BF16)
        if latent:
            kc_g = _dup_heads(kc, g).astype(BF16)
            vc_g = _dup_heads(vc, g).astype(BF16)
        for j in range(HEADS_PER_STACK // 2 - 1, A_GROUP // 2, HEADS_PER_STACK // 2):
            heads = range(g * A_GROUP + 2 * j + 2 - HEADS_PER_STACK, g * A_GROUP + 2 * j + 2)
            qs = jnp.concatenate([jnp.where(low if h % 2 == 0 else jnp.logical_not(low),
                                            q[:, (h // 2) * LANES:(h // 2 + 1) * LANES], zero) for h in heads], axis=0)
            sink = jnp.concatenate([jnp.full((tq, 1), sink_ref[h], F32) for h in heads], axis=0)
            s_loc = lax.dot_general(qs, kl_g, nt, preferred_element_type=F32)
            m = sink
            if latent:
                s_loc = s_loc + bias[:len(heads) * tq]
                s_ctx = lax.dot_general(qs, kc_g, nt, preferred_element_type=F32)
                m = jnp.maximum(m, jnp.max(s_ctx, axis=-1, keepdims=True))
            m = jnp.maximum(m, jnp.max(s_loc, axis=-1, keepdims=True))
            p_loc = jnp.exp(s_loc - m)
            den = jnp.exp(sink - m) + jnp.sum(p_loc, axis=-1, keepdims=True)
            acc = jnp.dot(p_loc.astype(BF16), vl_g, preferred_element_type=F32)
            if latent:
                p_ctx = jnp.exp(s_ctx - m)
                den = den + jnp.sum(p_ctx, axis=-1, keepdims=True)
                acc = acc + jnp.dot(p_ctx.astype(BF16), vc_g, preferred_element_type=F32)
            og = acc / den
            for jj in range(len(heads) // 2):
                outs.append(jnp.where(low, og[(2 * jj) * tq:(2 * jj + 1) * tq],
                                      og[(2 * jj + 1) * tq:(2 * jj + 2) * tq]))
    o_ref[0] = jnp.concatenate(outs, axis=1).astype(o_ref.dtype)


def attention(q, k, v, sink, kc=None, vc=None, tq=TQ_ATT):
    B, L, _ = q.shape
    latent = kc is not None
    tq = min(tq, L)
    seq = lambda wd: pl.BlockSpec((1, L, wd), lambda b, i: (b, 0, 0))
    in_specs = [
        pl.BlockSpec(memory_space=pltpu.SMEM),
        pl.BlockSpec((1, tq, A_WIDTH), lambda b, i: (b, i, 0)),
        seq(LANES), seq(LANES),
    ]
    args = [sink, q, k, v]
    if latent:
        P = kc.shape[1]
        in_specs += [pl.BlockSpec((1, P, LANES), lambda b, i: (b, 0, 0))] * 2
        args += [kc, vc]
    return pl.pallas_call(
        functools.partial(_attn_body, latent=latent, seq_len=L),
        grid=(B, L // tq),
        in_specs=in_specs,
        out_specs=pl.BlockSpec((1, tq, A_WIDTH), lambda b, i: (b, i, 0)),
        out_shape=jax.ShapeDtypeStruct((B, L, A_WIDTH), BF16),
        compiler_params=_cparams("parallel", "arbitrary"),
        name="attention_latent" if latent else "attention_context",
    )(*args)


def _split3(x):
    hi = x.astype(BF16)
    r = x - hi.astype(F32)
    mid = r.astype(BF16)
    lo = (r - mid.astype(F32)).astype(BF16)
    return hi, mid, lo


def _gla_group(q_ref, k_ref, v_ref, la_ref, o_ref, st_ref, b, d, reverse, n_chunks):
    C = GLA_CHUNK
    W = C_WIDTH
    ti = lax.broadcasted_iota(jnp.int32, (C, C), 0)
    si = lax.broadcasted_iota(jnp.int32, (C, C), 1)
    tri = (si >= ti) if reverse else (si <= ti)
    tri_b = tri.astype(BF16)
    tri4 = jnp.concatenate([tri] * C_HEADS, axis=0)
    r4 = lax.broadcasted_iota(jnp.int32, (C_HEADS * C, W), 0) // C
    c4 = lax.broadcasted_iota(jnp.int32, (C_HEADS * C, W), 1) // C_DK
    same_head = r4 == c4
    nt = (((1,), (1,)), ((), ()))
    tn = (((0,), (0,)), ((), ()))
    chunks = range(n_chunks)
    rows = [pl.ds(c * C, C) for c in chunks]
    vbs = [v_ref[b, r, :].astype(BF16) for r in rows]
    parts = [_split3(la_ref[b, r, :]) for r in rows]
    bsums = [jnp.dot(tri_b, hi, preferred_element_type=F32) + jnp.dot(tri_b, mid, preferred_element_type=F32)
             + jnp.dot(tri_b, lo, preferred_element_type=F32) for hi, mid, lo in parts]
    b_lasts = [s[0:1] if reverse else s[C - 1:C] for s in bsums]
    qgs = [q_ref[b, r, :] * jnp.exp(s) for r, s in zip(rows, bsums)]
    kgs = [(k_ref[b, r, :] * jnp.exp(-s)).astype(BF16) for r, s in zip(rows, bsums)]
    kds = [(k_ref[b, r, :] * jnp.exp(bl - s)).astype(BF16) for r, s, bl in zip(rows, bsums, b_lasts)]
    decays = [jnp.exp(bl) for bl in b_lasts]
    q_bds = [jnp.where(same_head, jnp.concatenate([qg] * C_HEADS, axis=0), 0.0).astype(BF16) for qg in qgs]
    uts = [jnp.where(same_head, lax.dot_general(vb, kd, tn, preferred_element_type=F32), 0.0)
           for vb, kd in zip(vbs, kds)]
    a_s = [jnp.where(tri4, lax.dot_general(qb, kg, nt, preferred_element_type=F32), 0.0).astype(BF16)
           for qb, kg in zip(q_bds, kgs)]
    r_s = [jnp.where(same_head, jnp.dot(a, vb, preferred_element_type=F32), 0.0)
           for a, vb in zip(a_s, vbs)]
    o_intra = [sum([r[h * C:(h + 1) * C] for h in range(1, C_HEADS)], r[0:C]) for r in r_s]
    qgb = [qg.astype(BF16) for qg in qgs]
    st = st_ref[d]
    for c in (reversed(chunks) if reverse else chunks):
        o_ref[b, rows[c], :] = o_intra[c] + lax.dot_general(qgb[c], st.astype(BF16), nt,
                                                            preferred_element_type=F32)
        st = st * decays[c] + uts[c]
    st_ref[d] = st


def _gla_body(*refs, has_state, n_chunks):
    if has_state:
        (qf, kf, vf, lf, qb, kb, vb, lb, s0_ref, of_ref, ob_ref, sT_ref, st) = refs
    else:
        (qf, kf, vf, lf, qb, kb, vb, lb, of_ref, ob_ref, sT_ref, st) = refs
    j = pl.program_id(1)

    nb = qf.shape[0]

    @pl.when(j == 0)
    def _init():
        if has_state:
            st[...] = s0_ref[...].reshape(st.shape)
        else:
            st[...] = jnp.zeros_like(st)

    for b in range(nb):
        _gla_group(qf, kf, vf, lf, of_ref, st, b, 2 * b, False, n_chunks)
        _gla_group(qb, kb, vb, lb, ob_ref, st, b, 2 * b + 1, True, n_chunks)

    @pl.when(j == pl.num_programs(1) - 1)
    def _final():
        sT_ref[...] = st[...].reshape(sT_ref.shape)


NB_GLA = 1


def gla(cq, ck, cv, la, s0=None, rows=512, nb=NB_GLA):
    B, L, W = cq.shape
    assert B % nb == 0
    rows = min(rows, L)
    ng = L // rows
    has_state = s0 is not None
    fwd = lambda: pl.BlockSpec((nb, rows, W), lambda b, j: (b, j, 0))
    bwd = lambda: pl.BlockSpec((nb, rows, W), lambda b, j: (b, ng - 1 - j, 0))
    state = lambda: pl.BlockSpec((nb, 2, W, W), lambda b, j: (b, 0, 0, 0))
    in_specs = [fwd(), fwd(), fwd(), pl.BlockSpec((nb, rows, W), lambda b, j: (b, j, 0)),
                bwd(), bwd(), bwd(), pl.BlockSpec((nb, rows, W), lambda b, j: (b, ng - 1 - j, 1))]
    args = [cq, ck, cv, la, cq, ck, cv, la]
    if has_state:
        in_specs.append(state())
        args.append(s0)
    return pl.pallas_call(
        functools.partial(_gla_body, has_state=has_state, n_chunks=rows // GLA_CHUNK),
        grid=(B // nb, ng),
        in_specs=in_specs,
        out_specs=[fwd(), bwd(), state()],
        out_shape=[jax.ShapeDtypeStruct((B, L, W), F32), jax.ShapeDtypeStruct((B, L, W), F32),
                   jax.ShapeDtypeStruct((B, 2, W, W), F32)],
        scratch_shapes=[pltpu.VMEM((2 * nb, W, W), F32)],
        compiler_params=_cparams("parallel", "arbitrary"),
        name="gla",
    )(*args)


TM_DFT = 512
BG_DFT = 4


def _conv3_body(x_ref, w_ref, b_ref, u_ref, ub_ref):
    x = x_ref[0]
    L = x.shape[0]
    row = lax.broadcasted_iota(jnp.int32, (L, 1), 0)
    prev = jnp.where(row == 0, 0.0, pltpu.roll(x, 1, 0))
    nxt = jnp.where(row == L - 1, 0.0, pltpu.roll(x, L - 1, 0))
    u = prev * w_ref[0:1, :] + x * w_ref[1:2, :] + nxt * w_ref[2:3, :] + b_ref[...]
    u_ref[0] = u
    ub_ref[0] = u.astype(BF16)


def hyena_conv3(hy, w, b):
    B, L, C3 = hy.shape
    blk = pl.BlockSpec((1, L, HY_CH), lambda b_, j: (b_, 0, j))
    return pl.pallas_call(
        _conv3_body,
        grid=(B, C3 // HY_CH),
        in_specs=[blk, pl.BlockSpec((SUBLANES, HY_CH), lambda b_, j: (0, j)),
                  pl.BlockSpec((1, HY_CH), lambda b_, j: (0, j))],
        out_specs=[blk, blk],
        out_shape=[jax.ShapeDtypeStruct(hy.shape, F32), jax.ShapeDtypeStruct(hy.shape, BF16)],
        compiler_params=_cparams("parallel", "parallel"),
        name="hyena_conv3",
    )(hy, w, b)


def _freq_weight(i, tm, n):
    k = i * tm + lax.broadcasted_iota(jnp.int32, (tm, 1), 0)
    return jnp.where(k == 0, 1.0 / n, 2.0 / n).astype(F32)


def _filt_body(fr_ref, fi_ref, sg_ref, hs_ref, hd_ref, hr_ref, hi_ref, hny_ref, *, n):
    i = pl.program_id(0)
    wk = _freq_weight(i, fr_ref.shape[0], n)
    fr = fr_ref[...]
    fi = fi_ref[...]
    hr_ref[...] = wk * (jnp.dot(fr, hs_ref[0], preferred_element_type=F32)
                        + jnp.dot(fr, hs_ref[1], preferred_element_type=F32))
    hi_ref[...] = wk * (jnp.dot(fi, hd_ref[0], preferred_element_type=F32)
                        + jnp.dot(fi, hd_ref[1], preferred_element_type=F32))

    @pl.when(i == 0)
    def _nyquist():
        sg = sg_ref[...]
        hny_ref[...] = (jnp.dot(sg, hs_ref[0], preferred_element_type=F32)
                        + jnp.dot(sg, hs_ref[1], preferred_element_type=F32)) * (1.0 / n)


def hyena_filter_spectrum(fr, fi, sg, hs, hd, tm=TM_DFT):
    L = fr.shape[0]
    W = hs.shape[2]
    tm = min(tm, L)
    whole = lambda shape: pl.BlockSpec(shape, lambda i: (0,) * len(shape))
    return pl.pallas_call(
        functools.partial(_filt_body, n=2 * L),
        grid=(L // tm,),
        in_specs=[pl.BlockSpec((tm, L), lambda i: (i, 0)), pl.BlockSpec((tm, L), lambda i: (i, 0)),
                  whole((SUBLANES, L)), whole((2, L, W)), whole((2, L, W))],
        out_specs=[pl.BlockSpec((tm, W), lambda i: (i, 0)), pl.BlockSpec((tm, W), lambda i: (i, 0)),
                   whole((SUBLANES, W))],
        out_shape=[jax.ShapeDtypeStruct((L, W), F32), jax.ShapeDtypeStruct((L, W), F32),
                   jax.ShapeDtypeStruct((SUBLANES, W), F32)],
        compiler_params=_cparams("arbitrary"),
        name="hyena_filter_spectrum",
    )(fr, fi, sg, hs, hd)


def _dft_fwd_body(fr_ref, fi_ref, sg_ref, z_ref, hr_ref, hi_ref, hny_ref, yr_ref, yi_ref, yny_ref):
    nb = z_ref.shape[0]
    hr = hr_ref[...]
    hi = hi_ref[...]

    def body(b, carry):
        zb = z_ref[b]
        xr = jnp.dot(fr_ref[...], zb, preferred_element_type=F32)
        xi = jnp.dot(fi_ref[...], zb, preferred_element_type=F32)
        yr_ref[b] = (xr * hr - xi * hi).astype(BF16)
        yi_ref[b] = (xr * hi + xi * hr).astype(BF16)
        return carry
    lax.fori_loop(0, nb, body, 0)

    @pl.when(pl.program_id(1) == 0)
    def _nyquist():
        def nyq(b, carry):
            yny_ref[b] = jnp.dot(sg_ref[...], z_ref[b], preferred_element_type=F32) * hny_ref[0:1, :]
            return carry
        lax.fori_loop(0, nb, nyq, 0)


def hyena_dft_fwd(fr, fi, sg, zb, zcol, hr, hi, hny, order, tm=TM_DFT, bg=BG_DFT):
    B, L, _ = zb.shape
    C = HY_CH
    tm = min(tm, L)
    bg = B if L < TM_DFT else bg
    ftile = pl.BlockSpec((tm, L), lambda g, i: (i, 0))
    ytile = pl.BlockSpec((bg, tm, C), lambda g, i: (g, i, 0))
    return pl.pallas_call(
        _dft_fwd_body,
        grid=(B // bg, L // tm),
        in_specs=[ftile, ftile, pl.BlockSpec((SUBLANES, L), lambda g, i: (0, 0)),
                  pl.BlockSpec((bg, L, C), lambda g, i: (g, 0, zcol)),
                  pl.BlockSpec((tm, C), lambda g, i: (i, order)), pl.BlockSpec((tm, C), lambda g, i: (i, order)),
                  pl.BlockSpec((SUBLANES, C), lambda g, i: (0, order))],
        out_specs=[ytile, ytile, pl.BlockSpec((bg, SUBLANES, C), lambda g, i: (g, 0, 0))],
        out_shape=[jax.ShapeDtypeStruct((B, L, C), BF16), jax.ShapeDtypeStruct((B, L, C), BF16),
                   jax.ShapeDtypeStruct((B, SUBLANES, C), F32)],
        compiler_params=_cparams("parallel", "arbitrary"),
        name="hyena_dft_fwd",
    )(fr, fi, sg, zb, hr, hi, hny)


def _dft_inv_body(fr_ref, fi_ref, yr_ref, yi_ref, yny_ref, z_ref, gate_ref, d_ref, zo_ref, zob_ref):
    nb = z_ref.shape[0]
    tm = fr_ref.shape[0]
    t = pl.program_id(1) * tm + lax.broadcasted_iota(jnp.int32, (tm, 1), 0)
    sign = jnp.where((t & 1) == 0, 1.0, -1.0).astype(F32)
    d = d_ref[...]

    def body(b, carry):
        conv = (jnp.dot(fr_ref[...], yr_ref[b], preferred_element_type=F32)
                + jnp.dot(fi_ref[...], yi_ref[b], preferred_element_type=F32) + sign * yny_ref[b][0:1, :])
        zn = gate_ref[b] * (conv + d * z_ref[b])
        zo_ref[b] = zn
        zob_ref[b] = zn.astype(BF16)
        return carry
    lax.fori_loop(0, nb, body, 0)


def hyena_dft_inv(fr, fi, yr, yi, yny, z, zcol, gate, gcol, d, tm=TM_DFT, bg=BG_DFT):
    B, L, C = yr.shape
    tm = min(tm, L)
    bg = B if L < TM_DFT else bg
    ftile = pl.BlockSpec((tm, L), lambda g, i: (i, 0))
    whole_y = lambda: pl.BlockSpec((bg, L, C), lambda g, i: (g, 0, 0), pipeline_mode=pl.Buffered(1))
    otile = pl.BlockSpec((bg, tm, C), lambda g, i: (g, i, 0))
    return pl.pallas_call(
        _dft_inv_body,
        grid=(B // bg, L // tm),
        in_specs=[ftile, ftile, whole_y(), whole_y(),
                  pl.BlockSpec((bg, SUBLANES, C), lambda g, i: (g, 0, 0)),
                  pl.BlockSpec((bg, tm, C), lambda g, i: (g, i, zcol)),
                  pl.BlockSpec((bg, tm, C), lambda g, i: (g, i, gcol)),
                  pl.BlockSpec((1, C), lambda g, i: (0, 0))],
        out_specs=[otile, otile],
        out_shape=[jax.ShapeDtypeStruct((B, L, C), F32), jax.ShapeDtypeStruct((B, L, C), BF16)],
        compiler_params=_cparams("parallel", "arbitrary"),
        name="hyena_dft_inv",
    )(fr, fi, yr, yi, yny, z, gate, d)


def dft_matrices(L):
    r = 1 << (max(L.bit_length() - 1, 0) // 2)
    t = jnp.arange(L, dtype=jnp.int32)

    def table(k):
        ang = ((k[:, None] * t[None, :]) % (2 * L)).astype(F32) * (math.pi / L)
        return jnp.cos(ang), jnp.sin(ang)
    ca, sa = table(jnp.arange(L // r, dtype=jnp.int32) * r)
    cb, sb = table(jnp.arange(r, dtype=jnp.int32))
    fr = ca[:, None, :] * cb[None, :, :] - sa[:, None, :] * sb[None, :, :]
    fi = -(sa[:, None, :] * cb[None, :, :] + ca[:, None, :] * sb[None, :, :])
    sg = jnp.zeros((SUBLANES, L), F32).at[0].set(jnp.where(t % 2 == 0, 1.0, -1.0))
    return fr.reshape(L, L).astype(BF16), fi.reshape(L, L).astype(BF16), sg.astype(BF16)


def hyena_time_filters(L, lp):
    t = jnp.arange(L, dtype=F32)
    t_norm = t / max(L - 1, 1)
    w = (2.0 * math.pi / L) * t
    f = jnp.linspace(1e-4, HY_BANDS - 1, HY_BANDS, dtype=F32)
    fw = w[:, None] * f[None, :]
    feat = jnp.concatenate([t_norm[:, None], jnp.cos(fw), -jnp.sin(fw)], axis=-1)
    z = jnp.sin(lp['hy_freq1'] * (feat @ lp['hy_w1'] + lp['hy_b1']))
    z = jnp.sin(lp['hy_freq2'] * (z @ lp['hy_w2'] + lp['hy_b2']))
    hf = (z @ lp['hy_w3']).astype(F32).reshape(L, 2, HY_ORDER, HY_CH)
    hf = hf * jnp.exp(-t_norm[:, None, None, None] * jnp.abs(lp['hy_decay'].astype(F32)))
    return hf / (jnp.sum(jnp.abs(hf), axis=(0, 1), keepdims=True) + EPS)


def _hi_lo(x):
    hi = x.astype(BF16)
    return jnp.stack([hi, (x - hi.astype(F32)).astype(BF16)])


def hyena(hy, lp, mats):
    B, L, _ = hy.shape
    fr, fi, sg = mats
    hf = hyena_time_filters(L, lp)
    h_fwd = hf[:, 0].reshape(L, HY_ORDER * HY_CH)
    h_bwd = hf[:, 1].at[0].set(0.0).reshape(L, HY_ORDER * HY_CH)
    hr, hi, hny = hyena_filter_spectrum(fr, fi, sg, _hi_lo(h_fwd + h_bwd), _hi_lo(h_fwd - h_bwd))
    w = jnp.pad(lp['hy_conv_w'], ((0, SUBLANES - 3), (0, 0)))
    u, ub = hyena_conv3(hy, w, lp['hy_conv_b'][None, :])
    d = lp['hy_d'].astype(F32)
    z, zb, zcol = u, ub, 0
    for o in range(HY_ORDER):
        yr, yi, yny = hyena_dft_fwd(fr, fi, sg, zb, zcol, hr, hi, hny, o)
        z, zb = hyena_dft_inv(fr, fi, yr, yi, yny, z, zcol, u, 1 + o, d[o][None, :])
        zcol = 0
    return z


BG_HALF = 2


def _shift_rows(x, down):
    n = x.shape[0]
    row = lax.broadcasted_iota(jnp.int32, (n, 1), 0)
    if down:
        return jnp.where(row == 0, 0.0, pltpu.roll(x, 1, 0))
    return jnp.where(row == n - 1, 0.0, pltpu.roll(x, n - 1, 0))


def _conv3_planes_body(x_ref, w_ref, b_ref, u_ref, ub_ref):
    H = x_ref.shape[0] // 2
    e = x_ref[pl.ds(0, H, stride=2), :]
    o = x_ref[pl.ds(1, H, stride=2), :]
    w0, w1, w2 = w_ref[0:1, :], w_ref[1:2, :], w_ref[2:3, :]
    ue = _shift_rows(o, True) * w0 + e * w1 + o * w2 + b_ref[...]
    uo = e * w0 + o * w1 + _shift_rows(e, False) * w2 + b_ref[...]
    u_ref[0, 0] = ue
    u_ref[0, 1] = uo
    ub_ref[0, 0] = ue.astype(BF16)
    ub_ref[0, 1] = uo.astype(BF16)


def hyena_conv3_planes(hy, w, b):
    B, L, C3 = hy.shape
    H = L // 2
    out = pl.BlockSpec((1, 2, H, LANES), lambda b_, j: (b_, 0, 0, j))
    return pl.pallas_call(
        _conv3_planes_body,
        grid=(B, C3 // LANES),
        in_specs=[pl.BlockSpec((None, L, LANES), lambda b_, j: (b_, 0, j)),
                  pl.BlockSpec((SUBLANES, LANES), lambda b_, j: (0, j)),
                  pl.BlockSpec((1, LANES), lambda b_, j: (0, j))],
        out_specs=[out, out],
        out_shape=[jax.ShapeDtypeStruct((B, 2, H, C3), F32), jax.ShapeDtypeStruct((B, 2, H, C3), BF16)],
        compiler_params=_cparams("parallel", "parallel"),
        name="hyena_conv3",
    )(hy, w, b)


def _half_filt_body(fe_ref, fo_ref, sg_ref, hse_ref, hso_ref, hde_ref, hdo_ref,
                    lr_ref, li_ref, hr_ref, hi_ref, sp_ref, *, n):
    i = pl.program_id(0)
    tm = lr_ref.shape[0]
    wk = _freq_weight(i, tm, n)

    def two_pass(m, parts_ref):
        return (jnp.dot(m, parts_ref[0], preferred_element_type=F32)
                + jnp.dot(m, parts_ref[1], preferred_element_type=F32))
    a = two_pass(fe_ref[0], hse_ref)
    b = two_pass(fo_ref[0], hso_ref)
    c = two_pass(fe_ref[0], hde_ref)
    d = two_pass(fo_ref[0], hdo_ref)
    lr_ref[...] = wk * (a[:tm] + b[:tm])
    hr_ref[...] = wk * (a[:tm] - b[:tm])
    li_ref[...] = wk * (c[tm:] + d[tm:])
    hi_ref[...] = wk * (d[tm:] - c[tm:])

    @pl.when(i == 0)
    def _middle_bin():
        sr = two_pass(sg_ref[...], hse_ref)[0:1, :]
        si = -two_pass(sg_ref[...], hdo_ref)[0:1, :]
        row = lax.broadcasted_iota(jnp.int32, sp_ref.shape, 0)
        sp_ref[...] = jnp.where(row == 0, sr, jnp.where(row == 1, si, 0.0)) * (2.0 / n)


def hyena_half_filter_spectrum(fe, fo, sg, hs, hd, tm):
    H = fe.shape[2]
    W = hs.shape[2]
    planes = lambda h: (h.reshape(2, H, 2, W)[:, :, 0], h.reshape(2, H, 2, W)[:, :, 1])
    hse, hso = planes(hs)
    hde, hdo = planes(hd)
    whole = lambda shape: pl.BlockSpec(shape, lambda i: (0,) * len(shape))
    ftile = pl.BlockSpec((1, 2 * tm, H), lambda i: (i, 0, 0))
    otile = pl.BlockSpec((tm, W), lambda i: (i, 0))
    out = jax.ShapeDtypeStruct((H, W), F32)
    return pl.pallas_call(
        functools.partial(_half_filt_body, n=4 * H),
        grid=(H // tm,),
        in_specs=[ftile, ftile, whole((SUBLANES, H))] + [whole((2, H, W))] * 4,
        out_specs=[otile, otile, otile, otile, whole((SUBLANES, W))],
        out_shape=[out, out, out, out, jax.ShapeDtypeStruct((SUBLANES, W), F32)],
        compiler_params=_cparams("arbitrary"),
        name="hyena_filter_spectrum",
    )(fe, fo, sg, hse, hso, hde, hdo)


def _cmul(xr, xi, hr, hi):
    return xr * hr - xi * hi, xr * hi + xi * hr


def _half_fwd_body(fe_ref, fo_ref, sg_ref, ze_ref, zo_ref, lr_ref, li_ref, hr_ref, hi_ref, sp_ref,
                   per_ref, pei_ref, por_ref, poi_ref, ysp_ref):
    nb = ze_ref.shape[0]
    tm = lr_ref.shape[0]

    def body(b, carry):
        a = jnp.dot(fe_ref[0], ze_ref[b], preferred_element_type=F32)
        o = jnp.dot(fo_ref[0], zo_ref[b], preferred_element_type=F32)
        ylr, yli = _cmul(a[:tm] + o[:tm], a[tm:] + o[tm:], lr_ref[...], li_ref[...])
        yhr, yhi = _cmul(a[:tm] - o[:tm], o[tm:] - a[tm:], hr_ref[...], hi_ref[...])
        per_ref[b] = (ylr + yhr).astype(BF16)
        pei_ref[b] = (yli - yhi).astype(BF16)
        por_ref[b] = (ylr - yhr).astype(BF16)
        poi_ref[b] = (yli + yhi).astype(BF16)
        return carry
    lax.fori_loop(0, nb, body, 0)

    @pl.when(pl.program_id(1) == 0)
    def _middle_bin():
        def mid(b, carry):
            xr = jnp.dot(sg_ref[...], ze_ref[b], preferred_element_type=F32)[0:1, :]
            xi = -jnp.dot(sg_ref[...], zo_ref[b], preferred_element_type=F32)[0:1, :]
            yr, yi = _cmul(xr, xi, sp_ref[0:1, :], sp_ref[1:2, :])
            row = lax.broadcasted_iota(jnp.int32, ysp_ref.shape[1:], 0)
            ysp_ref[b] = jnp.where(row == 0, yr, jnp.where(row == 1, yi, 0.0))
            return carry
        lax.fori_loop(0, nb, mid, 0)


def hyena_half_fwd(fe, fo, sg, zb, zcol, spec, order, tm, bg):
    B, _, H, _ = zb.shape
    C = HY_CH
    lr, li, hr, hi, sp = spec
    ftile = pl.BlockSpec((1, 2 * tm, H), lambda g, i: (i, 0, 0))
    plane = lambda p: pl.BlockSpec((bg, None, H, C), lambda g, i: (g, p, 0, zcol))
    stile = pl.BlockSpec((tm, C), lambda g, i: (i, order))
    ptile = pl.BlockSpec((bg, tm, C), lambda g, i: (g, i, 0))
    pshape = jax.ShapeDtypeStruct((B, H, C), BF16)
    return pl.pallas_call(
        _half_fwd_body,
        grid=(B // bg, H // tm),
        in_specs=[ftile, ftile, pl.BlockSpec((SUBLANES, H), lambda g, i: (0, 0)), plane(0), plane(1),
                  stile, stile, stile, stile, pl.BlockSpec((SUBLANES, C), lambda g, i: (0, order))],
        out_specs=[ptile, ptile, ptile, ptile, pl.BlockSpec((bg, SUBLANES, C), lambda g, i: (g, 0, 0))],
        out_shape=[pshape, pshape, pshape, pshape, jax.ShapeDtypeStruct((B, SUBLANES, C), F32)],
        compiler_params=_cparams("parallel", "arbitrary"),
        name="hyena_dft_fwd",
    )(fe, fo, sg, zb, zb, lr, li, hr, hi, sp)


def _half_inv_body(fer_ref, fei_ref, fotr_ref, foti_ref, per_ref, pei_ref, por_ref, poi_ref, ysp_ref,
                   ze_ref, zo_ref, ge_ref, go_ref, d_ref, *out_refs, natural):
    nb = ze_ref.shape[0]
    tm = fer_ref.shape[0]
    u = pl.program_id(1) * tm + lax.broadcasted_iota(jnp.int32, (tm, 1), 0)
    sign = jnp.where((u & 1) == 0, 1.0, -1.0).astype(F32)
    d = d_ref[...]

    def body(b, carry):
        ce = (jnp.dot(fer_ref[...], per_ref[b], preferred_element_type=F32)
              + jnp.dot(fei_ref[...], pei_ref[b], preferred_element_type=F32) + sign * ysp_ref[b][0:1, :])
        co = (jnp.dot(fotr_ref[...], por_ref[b], preferred_element_type=F32)
              + jnp.dot(foti_ref[...], poi_ref[b], preferred_element_type=F32) - sign * ysp_ref[b][1:2, :])
        zne = ge_ref[b] * (ce + d * ze_ref[b])
        zno = go_ref[b] * (co + d * zo_ref[b])
        if natural:
            bi = jnp.asarray(b, jnp.int32)
            for c, o_ref in enumerate(out_refs):
                o_ref.at[bi][pl.ds(0, tm, stride=2), :] = zne[:, c * LANES:(c + 1) * LANES]
                o_ref.at[bi][pl.ds(1, tm, stride=2), :] = zno[:, c * LANES:(c + 1) * LANES]
        else:
            z_ref, zb_ref = out_refs
            z_ref[b, 0] = zne
            z_ref[b, 1] = zno
            zb_ref[b, 0] = zne.astype(BF16)
            zb_ref[b, 1] = zno.astype(BF16)
        return carry
    lax.fori_loop(0, nb, body, 0)


def hyena_half_inv(mats, ps, ysp, z, zcol, gate, gcol, d, tm, bg, natural=False):
    fer, fei, fotr, foti = mats
    per = ps[0]
    B, H, C = per.shape
    ftile = pl.BlockSpec((tm, H), lambda g, i: (i, 0))
    whole_p = lambda: pl.BlockSpec((bg, H, C), lambda g, i: (g, 0, 0), pipeline_mode=pl.Buffered(1))
    tile = lambda p, col: pl.BlockSpec((bg, None, tm, C), lambda g, i: (g, p, i, col))
    if natural:
        out_specs = [pl.BlockSpec((bg, 2 * tm, LANES), lambda g, i: (g, i, 0))] * (C // LANES)
        out_shape = [jax.ShapeDtypeStruct((B, 2 * H, LANES), F32)] * (C // LANES)
    else:
        out_specs = [pl.BlockSpec((bg, 2, tm, C), lambda g, i: (g, 0, i, 0))] * 2
        out_shape = [jax.ShapeDtypeStruct((B, 2, H, C), F32), jax.ShapeDtypeStruct((B, 2, H, C), BF16)]
    return pl.pallas_call(
        functools.partial(_half_inv_body, natural=natural),
        grid=(B // bg, H // tm),
        in_specs=[ftile, ftile, ftile, ftile, whole_p(), whole_p(), whole_p(), whole_p(),
                  pl.BlockSpec((bg, SUBLANES, C), lambda g, i: (g, 0, 0)),
                  tile(0, zcol), tile(1, zcol), tile(0, gcol), tile(1, gcol),
                  pl.BlockSpec((1, C), lambda g, i: (0, 0))],
        out_specs=out_specs,
        out_shape=out_shape,
        compiler_params=_cparams("parallel", "arbitrary"),
        name="hyena_dft_inv",
    )(fer, fei, fotr, foti, *ps, ysp, z, z, gate, gate, d)


def dft_half_matrices(L, tm):
    H = L // 2
    r = 1 << (max(H.bit_length() - 1, 0) // 2)
    t = jnp.arange(L, dtype=jnp.int32)

    def table(k):
        ang = ((k[:, None] * t[None, :]) % (2 * L)).astype(F32) * (math.pi / L)
        return jnp.cos(ang), jnp.sin(ang)
    ca, sa = table(jnp.arange(H // r, dtype=jnp.int32) * r)
    cb, sb = table(jnp.arange(r, dtype=jnp.int32))
    fr = (ca[:, None, :] * cb[None, :, :] - sa[:, None, :] * sb[None, :, :]).reshape(H, H, 2)
    fi = (-(sa[:, None, :] * cb[None, :, :] + ca[:, None, :] * sb[None, :, :])).reshape(H, H, 2)
    fre, fro = fr[:, :, 0].astype(BF16), fr[:, :, 1].astype(BF16)
    fie, fio = fi[:, :, 0].astype(BF16), fi[:, :, 1].astype(BF16)
    stack = lambda a, b: jnp.concatenate([a.reshape(H // tm, tm, H), b.reshape(H // tm, tm, H)], axis=1)
    sg = jnp.zeros((SUBLANES, H), F32).at[0].set(jnp.where(t[:H] % 2 == 0, 1.0, -1.0)).astype(BF16)
    return dict(fe=stack(fre, fie), fo=stack(fro, fio), sg=sg, inv=(fre, fie, fro.T, fio.T), tm=tm)


def hyena_half(hy, lp, mats):
    B, L, _ = hy.shape
    tm = mats['tm']
    bg = B if L < TM_DFT else BG_HALF
    hf = hyena_time_filters(L, lp)
    h_fwd = hf[:, 0].reshape(L, HY_ORDER * HY_CH)
    h_bwd = hf[:, 1].at[0].set(0.0).reshape(L, HY_ORDER * HY_CH)
    spec = hyena_half_filter_spectrum(mats['fe'], mats['fo'], mats['sg'], _hi_lo(h_fwd + h_bwd),
                                      _hi_lo(h_fwd - h_bwd), tm)
    w = jnp.pad(lp['hy_conv_w'], ((0, SUBLANES - 3), (0, 0)))
    u, ub = hyena_conv3_planes(hy, w, lp['hy_conv_b'][None, :])
    d = lp['hy_d'].astype(F32)
    z, zb, zcol = u, ub, 0
    for o in range(HY_ORDER):
        *ps, ysp = hyena_half_fwd(mats['fe'], mats['fo'], mats['sg'], zb, zcol, spec, o, tm, bg)
        last = o == HY_ORDER - 1
        out = hyena_half_inv(mats['inv'], ps, ysp, z, zcol, u, 1 + o, d[o][None, :], tm, bg, natural=last)
        if last:
            return out
        z, zb = out
        zcol = 0


def _ffn_body(x_ref, mod_ref, g_ref, w1_ref, w3_ref, w2_ref, o_ref):
    x = x_ref[...]
    h = _adaln_rows(x, g_ref[...], mod_ref[0, ROW_SHIFT2:ROW_SHIFT2 + 1, :],
                    mod_ref[0, ROW_SCALE2:ROW_SCALE2 + 1, :]).astype(BF16)
    a = jnp.dot(h, w1_ref[...], preferred_element_type=F32)
    b = jnp.dot(h, w3_ref[...], preferred_element_type=F32)
    act = (_silu(a) * b).astype(BF16)
    ff = jnp.dot(act, w2_ref[...], preferred_element_type=F32)
    o_ref[...] = x + mod_ref[0, ROW_GATE2:ROW_GATE2 + 1, :] * ff


def ffn_dense(x, mods, g, w1, w3, w2, seg_len, tm=TM_PROJ):
    T, D = x.shape
    F = w1.shape[1]
    resident = functools.partial(pl.BlockSpec, pipeline_mode=pl.Buffered(1))
    return pl.pallas_call(
        _ffn_body,
        grid=(T // tm,),
        in_specs=[
            pl.BlockSpec((tm, D), lambda i: (i, 0)),
            pl.BlockSpec((1, MOD_ROWS, D), lambda i: (i * tm // seg_len, 0, 0)),
            pl.BlockSpec((1, D), lambda i: (0, 0)),
            resident((D, F), lambda i: (0, 0)),
            resident((D, F), lambda i: (0, 0)),
            resident((F, D), lambda i: (0, 0)),
        ],
        out_specs=pl.BlockSpec((tm, D), lambda i: (i, 0)),
        out_shape=jax.ShapeDtypeStruct((T, D), F32),
        compiler_params=_cparams("parallel"),
        name="ffn_dense",
    )(x, mods, g, w1, w3, w2)


def _store_token_tiles(ref, x):
    n = x.shape[0]
    for s in range(SUBLANES):
        ref[pl.ds(s, n, stride=SUBLANES), :] = x[:, s * LANES:(s + 1) * LANES]


def _load_token_tiles(ref, n):
    return jnp.concatenate([ref[pl.ds(s, n, stride=SUBLANES), :] for s in range(SUBLANES)], axis=1)


def _router_body(xa_ref, xb_ref, mod_ref, g_ref, wr_ref, h_ref, idx_ref, gw_ref, *, na):
    x = jnp.where(pl.program_id(0) < na, xa_ref[...], xb_ref[...])
    h = _adaln_rows(x, g_ref[...], mod_ref[0, ROW_SHIFT2:ROW_SHIFT2 + 1, :],
                    mod_ref[0, ROW_SCALE2:ROW_SCALE2 + 1, :])
    _store_token_tiles(h_ref, h)
    logits = lax.dot_general(wr_ref[...], h, (((1,), (1,)), ((), ())),
                             precision=lax.Precision.HIGHEST, preferred_element_type=F32)
    eidx = lax.broadcasted_iota(jnp.int32, logits.shape, 0)
    m1 = jnp.max(logits, axis=0, keepdims=True)
    i1 = jnp.min(jnp.where(logits == m1, eidx, N_EXPERTS), axis=0, keepdims=True)
    rest = jnp.where(eidx == i1, -jnp.inf, logits)
    m2 = jnp.max(rest, axis=0, keepdims=True)
    i2 = jnp.min(jnp.where(rest == m2, eidx, N_EXPERTS), axis=0, keepdims=True)
    e2 = jnp.exp(m2 - m1)
    den = 1.0 + e2
    row = lax.broadcasted_iota(jnp.int32, logits.shape, 0)
    idx_ref[...] = jnp.where(row == 0, i1, jnp.where(row == 1, i2, 0))
    gw_ref[...] = jnp.where(row == 0, 1.0 / den, jnp.where(row == 1, e2 / den, 0.0))


def _merged_seg(i, na, tm, seg_len_b):
    return jnp.where(i < na, 0, 1 + jnp.maximum(i - na, 0) * tm // seg_len_b)


def moe_router(xa, xb, mods, g, wr_t, seg_len_b, tm=TM_PROJ):
    D = xa.shape[1]
    na = xa.shape[0] // tm
    T = xa.shape[0] + xb.shape[0]
    return pl.pallas_call(
        functools.partial(_router_body, na=na),
        grid=(T // tm,),
        in_specs=[
            pl.BlockSpec((tm, D), lambda i: (jnp.minimum(i, na - 1), 0)),
            pl.BlockSpec((tm, D), lambda i: (jnp.maximum(i - na, 0), 0)),
            pl.BlockSpec((1, MOD_ROWS, D), lambda i: (_merged_seg(i, na, tm, seg_len_b), 0, 0)),
            pl.BlockSpec((1, D), lambda i: (0, 0)),
            pl.BlockSpec((N_EXPERTS, D), lambda i: (0, 0)),
        ],
        out_specs=[
            pl.BlockSpec((tm * SUBLANES, LANES), lambda i: (i, 0)),
            pl.BlockSpec((N_EXPERTS, tm), lambda i: (0, i)),
            pl.BlockSpec((N_EXPERTS, tm), lambda i: (0, i)),
        ],
        out_shape=[
            jax.ShapeDtypeStruct((T * SUBLANES, LANES), F32),
            jax.ShapeDtypeStruct((N_EXPERTS, T), jnp.int32),
            jax.ShapeDtypeStruct((N_EXPERTS, T), F32),
        ],
        compiler_params=_cparams("parallel"),
        name="moe_router",
    )(xa, xb, mods, g, wr_t)


def _experts_body(te_ref, nv_ref, src0_ref, srcn_ref, dst_ref, gate_ref, h_hbm, w1_ref, w3_ref, w2_ref,
                  y_hbm, hrows, hb, acc, ybuf, sem_g, sem_s, *, tm, nf, n_slots):
    i = pl.program_id(0)
    f = pl.program_id(1)
    nv = nv_ref[0]
    valid = i < nv
    slot = lax.rem(i, 2)

    def gather_issue(idx_ref, s):
        def body(r, carry):
            src = pl.multiple_of(idx_ref[0, 0, r] * SUBLANES, SUBLANES)
            dst = pl.multiple_of(r * SUBLANES, SUBLANES)
            pltpu.make_async_copy(h_hbm.at[pl.ds(src, SUBLANES)], hrows.at[s, pl.ds(dst, SUBLANES)],
                                  sem_g.at[s]).start()
            return carry
        lax.fori_loop(0, tm, body, 0, unroll=8)

    def gather_wait(s):
        pltpu.make_async_copy(h_hbm.at[pl.ds(0, tm * SUBLANES)], hrows.at[s], sem_g.at[s]).wait()

    def scatter_wait(s):
        pltpu.make_async_copy(ybuf.at[s], y_hbm.at[pl.ds(0, tm * SUBLANES)], sem_s.at[s]).wait()

    @pl.when(jnp.logical_and(i == 0, f == 0))
    def _first():
        ybuf[0] = jnp.zeros(ybuf.shape[1:], ybuf.dtype)
        spare = lambda p: y_hbm.at[pl.ds((n_slots + p * tm) * SUBLANES, tm * SUBLANES)]
        for p in range(2):
            pltpu.make_async_copy(ybuf.at[0], spare(p), sem_s.at[p]).start()
        for p in range(2):
            pltpu.make_async_copy(ybuf.at[0], spare(p), sem_s.at[p]).wait()
        gather_issue(src0_ref, 0)

    @pl.when(jnp.logical_and(valid, f == 0))
    def _stage():
        gather_wait(slot)
        hb[...] = _load_token_tiles(hrows.at[slot], tm).astype(BF16)
        acc[...] = jnp.zeros_like(acc)

        @pl.when(i + 1 < nv)
        def _prefetch():
            gather_issue(srcn_ref, 1 - slot)

    @pl.when(valid)
    def _compute():
        h = hb[...]
        a = jnp.dot(h, w1_ref[0], preferred_element_type=F32)
        b = jnp.dot(h, w3_ref[0], preferred_element_type=F32)
        act = (_silu(a) * b).astype(BF16)
        acc[...] += jnp.dot(act, w2_ref[0], preferred_element_type=F32)

    @pl.when(jnp.logical_and(valid, f == nf - 1))
    def _emit():
        @pl.when(i >= 2)
        def _reuse():
            scatter_wait(slot)
        _store_token_tiles(ybuf.at[slot], acc[...] * gate_ref[...])

        def body(r, carry):
            src = pl.multiple_of(r * SUBLANES, SUBLANES)
            dst = pl.multiple_of(dst_ref[0, 0, r] * SUBLANES, SUBLANES)
            pltpu.make_async_copy(ybuf.at[slot, pl.ds(src, SUBLANES)], y_hbm.at[pl.ds(dst, SUBLANES)],
                                  sem_s.at[slot]).start()
            return carry
        lax.fori_loop(0, tm, body, 0, unroll=8)

    @pl.when(jnp.logical_and(i == nv, f == 0))
    def _drain():
        scatter_wait(lax.rem(nv + 1, 2))

        @pl.when(nv >= 2)
        def _older():
            scatter_wait(lax.rem(nv, 2))


def moe_experts(h, tile_expert, n_valid, src_rows, dst_rows, gate_rows, w1, w3, w2, tm=TM_MOE, tf=TF_MOE):
    T = h.shape[0] // SUBLANES
    D = w1.shape[1]
    F = w1.shape[2]
    n_tiles = src_rows.shape[0]
    n_slots = TOP_K * T
    nf = F // tf

    def wcol(i, f, te, nv):
        return (te[i], 0, jnp.where(i < nv[0], f, nf - 1))

    def wrow(i, f, te, nv):
        return (te[i], jnp.where(i < nv[0], f, nf - 1), 0)

    smem_tile = lambda fn: pl.BlockSpec((1, 1, tm), fn, memory_space=pltpu.SMEM)
    grid_spec = pltpu.PrefetchScalarGridSpec(
        num_scalar_prefetch=2,
        grid=(n_tiles, nf),
        in_specs=[
            smem_tile(lambda i, f, te, nv: (0, 0, 0)),
            smem_tile(lambda i, f, te, nv: (jnp.minimum(i + 1, n_tiles - 1), 0, 0)),
            smem_tile(lambda i, f, te, nv: (i, 0, 0)),
            pl.BlockSpec((tm, 1), lambda i, f, te, nv: (i, 0)),
            pl.BlockSpec(memory_space=pl.ANY),
            pl.BlockSpec((1, D, tf), wcol),
            pl.BlockSpec((1, D, tf), wcol),
            pl.BlockSpec((1, tf, D), wrow),
        ],
        out_specs=pl.BlockSpec(memory_space=pl.ANY),
        scratch_shapes=[
            pltpu.VMEM((2, tm * SUBLANES, LANES), F32),
            pltpu.VMEM((tm, D), BF16),
            pltpu.VMEM((tm, D), F32),
            pltpu.VMEM((2, tm * SUBLANES, LANES), F32),
            pltpu.SemaphoreType.DMA((2,)),
            pltpu.SemaphoreType.DMA((2,)),
        ],
    )
    return pl.pallas_call(
        functools.partial(_experts_body, tm=tm, nf=nf, n_slots=n_slots),
        grid_spec=grid_spec,
        out_shape=jax.ShapeDtypeStruct(((n_slots + 2 * tm) * SUBLANES, LANES), F32),
        compiler_params=_cparams("arbitrary", "arbitrary"),
        name="moe_experts",
    )(tile_expert, n_valid, src_rows, src_rows, dst_rows, gate_rows, h, w1, w3, w2)


def _combine_body(x_ref, mod_ref, y0_ref, y1_ref, o_ref):
    n = x_ref.shape[0]
    y = _load_token_tiles(y0_ref, n) + _load_token_tiles(y1_ref, n)
    o_ref[...] = x_ref[...] + mod_ref[0, ROW_GATE2:ROW_GATE2 + 1, :] * y


def moe_combine(x, mods, y, row0, n_tok, seg_len, tm=TM_COMB):
    T, D = x.shape
    off = row0 // tm
    return pl.pallas_call(
        _combine_body,
        grid=(T // tm,),
        in_specs=[
            pl.BlockSpec((tm, D), lambda i: (i, 0)),
            pl.BlockSpec((1, MOD_ROWS, D), lambda i: (i * tm // seg_len, 0, 0)),
            pl.BlockSpec((tm * SUBLANES, LANES), lambda i: (off + i, 0)),
            pl.BlockSpec((tm * SUBLANES, LANES), lambda i: (off + n_tok // tm + i, 0)),
        ],
        out_specs=pl.BlockSpec((tm, D), lambda i: (i, 0)),
        out_shape=jax.ShapeDtypeStruct((T, D), F32),
        compiler_params=_cparams("parallel"),
        name="moe_combine",
    )(x, mods, y, y)


def moe_dispatch_plan(idx, gw, tm=TM_MOE):
    T = idx.shape[1]
    n_slots = TOP_K * T
    n_tiles = n_slots // tm + N_EXPERTS
    n_rows = n_tiles * tm
    e_flat = idx[:TOP_K].reshape(n_slots)
    g_flat = gw[:TOP_K].reshape(n_slots)
    counts = jnp.sum((e_flat[:, None] == jnp.arange(N_EXPERTS, dtype=jnp.int32)[None, :]).astype(jnp.int32),
                     axis=0)
    padded = (counts + tm - 1) // tm * tm
    ends = jnp.cumsum(padded)
    offs = ends - padded
    order = jnp.argsort(e_flat, stable=True).astype(jnp.int32)
    cstart = jnp.cumsum(counts) - counts
    rows = jnp.arange(n_rows, dtype=jnp.int32)
    row_e = jnp.minimum(jnp.searchsorted(ends, rows, side='right'), N_EXPERTS - 1).astype(jnp.int32)
    j = rows - offs[row_e]
    live = j < counts[row_e]
    slot = order[jnp.clip(cstart[row_e] + j, 0, n_slots - 1)]
    spare = n_slots + (rows // tm) % 2 * tm + rows % tm
    src_rows = jnp.where(live, slot % T, 0).astype(jnp.int32).reshape(n_tiles, 1, tm)
    dst_rows = jnp.where(live, slot, spare).astype(jnp.int32).reshape(n_tiles, 1, tm)
    gate_rows = jnp.where(live, g_flat[slot], 0.0).reshape(n_rows, 1)
    tile_start = jnp.arange(n_tiles, dtype=jnp.int32) * tm
    n_valid = (ends[-1] // tm).astype(jnp.int32).reshape(1)
    tile_expert = jnp.minimum(jnp.searchsorted(ends, tile_start, side='right'), N_EXPERTS - 1)
    last_e = tile_expert[jnp.maximum(n_valid[0] - 1, 0)]
    tile_expert = jnp.where(tile_start < ends[-1], tile_expert, last_e).astype(jnp.int32)
    return tile_expert, n_valid, src_rows, dst_rows, gate_rows


def ffn_moe(xa, xb, mods_a, mods_b, g, wr_t, w1, w3, w2, seg_len_b, tm=TM_PROJ, tme=TM_MOE, tmc=TM_COMB,
            tf=TF_MOE):
    na, nb = xa.shape[0], xb.shape[0]
    mods = jnp.concatenate([mods_a, mods_b], axis=0)
    h, idx, gw = moe_router(xa, xb, mods, g, wr_t, seg_len_b, tm)
    tile_expert, n_valid, src_rows, dst_rows, gate_rows = moe_dispatch_plan(idx, gw, tme)
    y = moe_experts(h, tile_expert, n_valid, src_rows, dst_rows, gate_rows, w1, w3, w2, tme, tf)
    return (moe_combine(xa, mods_a, y, 0, na + nb, na, tmc),
            moe_combine(xb, mods_b, y, na, na + nb, seg_len_b, tmc))


def axial_rope(L):
    rows = L // GRID_W
    r = jnp.repeat(jnp.arange(rows, dtype=F32), GRID_W)
    col = jnp.tile(jnp.arange(GRID_W, dtype=F32), rows)
    n = HEAD_DIM // 4
    freqs = ROPE_BASE ** (-jnp.arange(n, dtype=F32) / n)
    ang = jnp.concatenate([r[:, None] * freqs, col[:, None] * freqs], axis=-1)
    return jnp.cos(ang), jnp.sin(ang)


def rope_tables(L):
    cos, sin = axial_rope(L)
    n = HEAD_DIM // 4
    cos_h = jnp.concatenate([cos[:, :n], cos[:, :n], cos[:, n:], cos[:, n:]], axis=1)
    sin_h = jnp.concatenate([-sin[:, :n], sin[:, :n], -sin[:, n:], sin[:, n:]], axis=1)
    reps = LANES // HEAD_DIM
    return jnp.tile(cos_h, (1, reps)), jnp.tile(sin_h, (1, reps))


def block_diag_ones(width, block):
    i = jnp.arange(width) // block
    return (i[:, None] == i[None, :]).astype(BF16)


def gla_gate_params(gw, gb):
    w = jnp.zeros((LANES, 2 * C_WIDTH), F32)
    w = w.at[:GLA_RANK, :C_WIDTH].set(gw[0]).at[GLA_RANK:2 * GLA_RANK, C_WIDTH:].set(gw[1])
    return w.astype(BF16), jnp.concatenate([gb[0], gb[1]])[None, :]


def gla_state_to_blockdiag(st):
    B = st.shape[0]
    out = jnp.zeros((B, 2, C_HEADS, C_DV, C_HEADS, C_DK), F32)
    for h in range(C_HEADS):
        out = out.at[:, :, h, :, h, :].set(jnp.swapaxes(st[:, :, h], -1, -2))
    return out.reshape(B, 2, C_WIDTH, C_WIDTH)


def gla_state_from_blockdiag(sT):
    B = sT.shape[0]
    s6 = sT.reshape(B, 2, C_HEADS, C_DV, C_HEADS, C_DK)
    return jnp.stack([jnp.swapaxes(s6[:, :, h, :, h, :], -1, -2) for h in range(C_HEADS)], axis=2)


def mod_table(cvec, w_ada, b_ada):
    m = jax.nn.silu(cvec) @ w_ada + b_ada
    m = m.reshape(cvec.shape[0], 6, D_MODEL)
    return jnp.pad(m, ((0, 0), (0, MOD_ROWS - 6), (0, 0)))


def kernel(x_prompt, x_sample, cache_k, cache_v, state_gla, c, c_ctx, norm1_g, norm2_g, w_ada, b_ada,
           w_in, w_out, q_norm_g, k_norm_g, attn_sink, hy_conv_w, hy_conv_b, hy_w1, hy_b1, hy_freq1,
           hy_w2, hy_b2, hy_freq2, hy_w3, hy_decay, hy_d, gla_gate_w, gla_gate_b, gla_norm_g,
           ffn_w1, ffn_w3, ffn_w2, moe_router, moe_w1, moe_w3, moe_w2):
    D = D_MODEL
    xp = x_prompt.reshape(BATCH * SEQ, D)
    xs = x_sample.reshape(DEC_BATCH * DEC_SEQ, D)
    streams = [
        dict(x=xp, B=BATCH, L=SEQ, seg_len=BATCH * SEQ, cvec=c_ctx[None, :], latent=False),
        dict(x=xs, B=DEC_BATCH, L=DEC_SEQ, seg_len=DEC_SEQ, cvec=c, latent=True),
    ]
    ks_list, vs_list, st_list = [], [], []
    bd_q = block_diag_ones(A_WIDTH, HEAD_DIM)
    dft = {s['L']: dft_half_matrices(s['L'], min(TM_DFT, s['L'] // 2)) for s in streams}
    for l in range(DEPTH):
        lp = {
            'hy_conv_w': hy_conv_w[l], 'hy_conv_b': hy_conv_b[l],
            'hy_w1': hy_w1[l], 'hy_b1': hy_b1[l], 'hy_freq1': hy_freq1[l], 'hy_w2': hy_w2[l],
            'hy_b2': hy_b2[l], 'hy_freq2': hy_freq2[l], 'hy_w3': hy_w3[l], 'hy_decay': hy_decay[l],
            'hy_d': hy_d[l], 'gla_gate_w': gla_gate_w[l], 'gla_gate_b': gla_gate_b[l],
            'gla_norm_g': gla_norm_g[l],
        }
        w_in_l = jnp.pad(w_in[l], ((0, 0), (0, D_PROJ_PAD - D_PROJ))).astype(BF16)
        w_out_l = w_out[l].astype(BF16)
        g1 = norm1_g[l][None, :]
        g2 = norm2_g[l][None, :]
        qg = jnp.tile(q_norm_g[l], A_HEADS)[None, :]
        kg = jnp.tile(k_norm_g[l], A_KV_HEADS)[None, :]
        gg = jnp.tile(gla_norm_g[l], C_HEADS)[None, :]
        gate_w, gate_b = gla_gate_params(gla_gate_w[l], gla_gate_b[l])
        j = l // 2
        for s in streams:
            B, L = s['B'], s['L']
            mods = mod_table(s['cvec'], w_ada[l], b_ada[l])
            rope = rope_tables(L) if s['latent'] else None
            q, k, v, hy, cq, ck, cv, cg, la = in_proj(s['x'], mods, g1, w_in_l, bd_q, qg, kg, gate_w, gate_b,
                                                      s['seg_len'], rope, L)
            seq = lambda t: t.reshape(B, L, t.shape[-1])
            if s['latent']:
                kc = cache_k[:, l].reshape(DEC_BATCH, PAST_LEN, LANES)
                vc = cache_v[:, l].reshape(DEC_BATCH, PAST_LEN, LANES)
                a_out = attention(seq(q), seq(k), seq(v), attn_sink[l], kc, vc)
                s0 = gla_state_to_blockdiag(state_gla[:, l])
                o_f, o_b, _ = gla(seq(cq), seq(ck), seq(cv), seq(la), s0)
            else:
                a_out = attention(seq(q), seq(k), seq(v), attn_sink[l])
                o_f, o_b, sT = gla(seq(cq), seq(ck), seq(cv), seq(la))
                ks_list.append(k.reshape(B, L, A_KV_HEADS, HEAD_DIM))
                vs_list.append(v.reshape(B, L, A_KV_HEADS, HEAD_DIM))
                st_list.append(gla_state_from_blockdiag(sT))
            h_out = hyena_half(seq(hy), lp, dft[L])
            flat = lambda t: t.reshape(B * L, t.shape[-1])
            x1 = out_proj(s['x'], flat(a_out), [flat(h) for h in h_out], flat(o_f), flat(o_b), cg, mods, w_out_l,
                          bd_q[:C_WIDTH, :C_WIDTH], gg, s['seg_len'])
            if l % 2 == 0:
                s['x'] = ffn_dense(x1, mods, g2, ffn_w1[j].astype(BF16), ffn_w3[j].astype(BF16),
                                   ffn_w2[j].astype(BF16), s['seg_len'])
            else:
                s['x'], s['mods'] = x1, mods
        if l % 2 == 1:
            sa, sb = streams
            sa['x'], sb['x'] = ffn_moe(sa['x'], sb['x'], sa['mods'], sb['mods'], g2, moe_router[j].T,
                                       moe_w1[j].astype(BF16), moe_w3[j].astype(BF16), moe_w2[j].astype(BF16),
                                       sb['seg_len'])
    y_prompt = streams[0]['x'].reshape(BATCH, SEQ, D)
    y_sample = streams[1]['x'].reshape(DEC_BATCH, DEC_SEQ, D)
    new_cache_k = jnp.stack(ks_list, axis=1)
    new_cache_v = jnp.stack(vs_list, axis=1)
    new_state_gla = jnp.stack(st_list, axis=1)
    return (y_prompt, y_sample, new_cache_k, new_cache_v, new_state_gla)
```

```python
import math
import functools
import jax
import jax.numpy as jnp
from jax import lax
import numpy as np
from jax.experimental import pallas as pl
from jax.experimental.pallas import tpu as pltpu

D_MODEL = 1024
BATCH = 32
SEQ = 256
DEPTH = 2
DEC_BATCH = 8
DEC_SEQ = 4096
PAST_LEN = 512

GRID_W = 64
HEAD_DIM = 64
D_MIX = D_MODEL
A_HEADS = 8
A_KV_HEADS = 2
A_GROUP = A_HEADS // A_KV_HEADS
A_WIDTH = A_HEADS * HEAD_DIM
WINDOW = 128
BLK = 128
ROPE_BASE = 10000.0
HY_CH = 256
HY_ORDER = 2
HY_BANDS = 16
HY_EMB = 1 + 2 * HY_BANDS
HY_HID = 64
C_HEADS = 4
C_DK = 64
C_DV = 64
C_WIDTH = C_HEADS * C_DV
GLA_RANK = 16
GLA_TAU = 16.0
GLA_CHUNK = 64
D_FF = 2816
N_EXPERTS = 8
TOP_K = 2
D_FF_EXPERT = 3584

PROJ_SIZES = (A_WIDTH, A_KV_HEADS * HEAD_DIM, A_KV_HEADS * HEAD_DIM, 3 * HY_CH,
              C_HEADS * C_DK, C_HEADS * C_DK, C_WIDTH, C_WIDTH, 2 * GLA_RANK)
PROJ_SPLITS = tuple(int(s) for s in np.cumsum(PROJ_SIZES)[:-1])
D_PROJ = int(sum(PROJ_SIZES))

F32 = jnp.float32
BF16 = jnp.bfloat16
ATT_SCALE = HEAD_DIM ** -0.5
NEG_INF = -1e30
EPS = 1e-6

LANES = 128
SUBLANES = 8
D_PROJ_PAD = -(-D_PROJ // LANES) * LANES
VMEM_LIMIT_BYTES = 56 * 1024 * 1024

MOD_ROWS = SUBLANES
ROW_SHIFT1, ROW_SCALE1, ROW_GATE1, ROW_SHIFT2, ROW_SCALE2, ROW_GATE2 = range(6)

TM_PROJ = 512
TM_MOE = 1024
TF_MOE = 896
TM_COMB = 512


def _cparams(*sem):
    return pltpu.CompilerParams(dimension_semantics=sem, vmem_limit_bytes=VMEM_LIMIT_BYTES)


def _adaln_rows(x, g, shift, scale):
    ms = jnp.mean(x * x, axis=-1, keepdims=True)
    return (x * lax.rsqrt(ms + EPS) * g) * (1.0 + scale) + shift


def _silu(a):
    return a * jax.nn.sigmoid(a)


def _group_mean_sq(x, ones_bd):
    sq = x * x
    hi = sq.astype(BF16)
    lo = (sq - hi.astype(F32)).astype(BF16)
    s = jnp.dot(hi, ones_bd, preferred_element_type=F32) + jnp.dot(lo, ones_bd, preferred_element_type=F32)
    return s * (1.0 / HEAD_DIM)


def _rope_rows(x, cos_t, sin_t):
    q4 = HEAD_DIM // 4
    lane = lax.broadcasted_iota(jnp.int32, x.shape, 1)
    partner = jnp.where((lane % (2 * q4)) < q4, pltpu.roll(x, LANES - q4, 1), pltpu.roll(x, q4, 1))
    return x * cos_t + partner * sin_t


def _log_sigmoid(x):
    return jnp.minimum(x, 0.0) - jnp.log(1.0 + jnp.exp(-jnp.abs(x)))


def _inproj_body(*refs, latent):
    if latent:
        (x_ref, mod_ref, g_ref, w_ref, bd_ref, qg_ref, kg_ref, gw_ref, gb_ref, cos_ref, sin_ref,
         q_ref, k_ref, v_ref, hy_ref, cq_ref, ck_ref, cv_ref, cg_ref, la_ref) = refs
    else:
        (x_ref, mod_ref, g_ref, w_ref, bd_ref, qg_ref, kg_ref, gw_ref, gb_ref,
         q_ref, k_ref, v_ref, hy_ref, cq_ref, ck_ref, cv_ref, cg_ref, la_ref) = refs
    h = _adaln_rows(x_ref[...], g_ref[...], mod_ref[0, ROW_SHIFT1:ROW_SHIFT1 + 1, :],
                    mod_ref[0, ROW_SCALE1:ROW_SCALE1 + 1, :])
    acc = jnp.dot(h.astype(BF16), w_ref[...], preferred_element_type=F32)
    o = 0
    q = acc[:, o:o + A_WIDTH]
    o += A_WIDTH
    k = acc[:, o:o + LANES]
    o += LANES
    v_ref[...] = acc[:, o:o + LANES]
    o += LANES
    hy_ref[...] = acc[:, o:o + 3 * HY_CH]
    o += 3 * HY_CH
    cq_ref[...] = acc[:, o:o + C_WIDTH] * (C_DK ** -0.5)
    o += C_WIDTH
    ck_ref[...] = acc[:, o:o + C_WIDTH]
    o += C_WIDTH
    cv_ref[...] = acc[:, o:o + C_WIDTH]
    o += C_WIDTH
    cg_ref[...] = acc[:, o:o + C_WIDTH]
    o += C_WIDTH
    r = acc[:, o:o + LANES]
    la_ref[...] = _log_sigmoid(jnp.dot(r.astype(BF16), gw_ref[...], preferred_element_type=F32)
                               + gb_ref[...]) * (1.0 / GLA_TAU)
    q = q * lax.rsqrt(_group_mean_sq(q, bd_ref[...]) + EPS) * qg_ref[...]
    k = k * lax.rsqrt(_group_mean_sq(k, bd_ref[0:LANES, 0:LANES]) + EPS) * kg_ref[...]
    if latent:
        cos_t = cos_ref[...]
        sin_t = sin_ref[...]
        q = jnp.concatenate([_rope_rows(q[:, j * LANES:(j + 1) * LANES], cos_t, sin_t)
                             for j in range(A_WIDTH // LANES)], axis=1)
        k = _rope_rows(k, cos_t, sin_t)
    q_ref[...] = (q * ATT_SCALE).astype(BF16)
    k_ref[...] = k


def in_proj(x, mods, g, w, bd, qg, kg, gw, gb, seg_len, rope=None, seq_len=None, tm=TM_PROJ):
    T, D = x.shape
    N = w.shape[1]
    latent = rope is not None
    const = lambda shape: pl.BlockSpec(shape, lambda i: (0,) * len(shape))
    in_specs = [
        pl.BlockSpec((tm, D), lambda i: (i, 0)),
        pl.BlockSpec((1, MOD_ROWS, D), lambda i: (i * tm // seg_len, 0, 0)),
        const((1, D)), const((D, N)), const((A_WIDTH, A_WIDTH)), const((1, A_WIDTH)), const((1, LANES)),
        const((LANES, 2 * C_WIDTH)), const((1, 2 * C_WIDTH)),
    ]
    args = [x, mods, g, w, bd, qg, kg, gw, gb]
    if latent:
        tiles_per_seq = seq_len // tm
        in_specs += [pl.BlockSpec((tm, LANES), lambda i: (i % tiles_per_seq, 0))] * 2
        args += list(rope)
    widths = [A_WIDTH, LANES, LANES, 3 * HY_CH, C_WIDTH, C_WIDTH, C_WIDTH, C_WIDTH, 2 * C_WIDTH]
    dtypes = [BF16] + [F32] * 8
    return pl.pallas_call(
        functools.partial(_inproj_body, latent=latent),
        grid=(T // tm,),
        in_specs=in_specs,
        out_specs=[pl.BlockSpec((tm, wd), lambda i: (i, 0)) for wd in widths],
        out_shape=[jax.ShapeDtypeStruct((T, wd), dt) for wd, dt in zip(widths, dtypes)],
        compiler_params=_cparams("parallel"),
        name="in_proj_latent" if latent else "in_proj_context",
    )(*args)


def _outproj_body(x_ref, a_ref, h0_ref, h1_ref, of_ref, ob_ref, cg_ref, mod_ref, w_ref, bd_ref, gg_ref, o_ref):
    o = of_ref[...] + ob_ref[...]
    g_out = o * lax.rsqrt(_group_mean_sq(o, bd_ref[...]) + EPS) * gg_ref[...] * _silu(cg_ref[...])
    h0 = A_WIDTH
    g0 = A_WIDTH + HY_CH
    mix = jnp.dot(a_ref[...], w_ref[0:h0, :], preferred_element_type=F32)
    h = jnp.concatenate([h0_ref[...], h1_ref[...]], axis=1)
    mix += jnp.dot(h.astype(BF16), w_ref[h0:g0, :], preferred_element_type=F32)
    mix += jnp.dot(g_out.astype(BF16), w_ref[g0:, :], preferred_element_type=F32)
    o_ref[...] = x_ref[...] + mod_ref[0, ROW_GATE1:ROW_GATE1 + 1, :] * mix


def out_proj(x, a, h, of, ob, cg, mods, w, bd, gg, seg_len, tm=TM_PROJ):
    T, D = x.shape
    row = lambda wd: pl.BlockSpec((tm, wd), lambda i: (i, 0))
    const = lambda shape: pl.BlockSpec(shape, lambda i: (0,) * len(shape))
    return pl.pallas_call(
        _outproj_body,
        grid=(T // tm,),
        in_specs=[
            row(D), row(A_WIDTH), row(LANES), row(LANES), row(C_WIDTH), row(C_WIDTH), row(C_WIDTH),
            pl.BlockSpec((1, MOD_ROWS, D), lambda i: (i * tm // seg_len, 0, 0)),
            const((D, D)), const((C_WIDTH, C_WIDTH)), const((1, C_WIDTH)),
        ],
        out_specs=row(D),
        out_shape=jax.ShapeDtypeStruct((T, D), F32),
        compiler_params=_cparams("parallel"),
        name="out_proj",
    )(x, a, h[0], h[1], of, ob, cg, mods, w, bd, gg)


TQ_ATT = 256
HEADS_PER_STACK = 2
KWIN_ATT = TQ_ATT + 2 * WINDOW


def _dup_heads(x, g):
    lane = lax.broadcasted_iota(jnp.int32, x.shape, 1)
    rolled = pltpu.roll(x, HEAD_DIM, 1)
    keep = (lane < HEAD_DIM) if g == 0 else (lane >= HEAD_DIM)
    return jnp.where(keep, x, rolled)


def _attn_body(*refs, latent, seq_len):
    if latent:
        sink_ref, q_ref, k_ref, v_ref, kc_ref, vc_ref, o_ref = refs
    else:
        sink_ref, q_ref, k_ref, v_ref, o_ref = refs
    tq = q_ref.shape[1]
    i = pl.program_id(1)
    q = q_ref[0]
    lane = lax.broadcasted_iota(jnp.int32, (tq, LANES), 1)
    low = lane < HEAD_DIM
    if latent:
        start = jnp.clip(i * tq - WINDOW, 0, seq_len - KWIN_ATT)
        start = pl.multiple_of(start, WINDOW)
        kl = k_ref[0, pl.ds(start, KWIN_ATT), :]
        vl = v_ref[0, pl.ds(start, KWIN_ATT), :]
        qpos = i * tq + lax.broadcasted_iota(jnp.int32, (tq, KWIN_ATT), 0)
        kpos = start + lax.broadcasted_iota(jnp.int32, (tq, KWIN_ATT), 1)
        bias = jnp.where(jnp.abs(qpos - kpos) <= WINDOW, 0.0, NEG_INF).astype(F32)
        bias = jnp.concatenate([bias] * A_GROUP, axis=0)
        kc = kc_ref[0]
        vc = vc_ref[0]
    else:
        kl = k_ref[0]
        vl = v_ref[0]
    nt = (((1,), (1,)), ((), ()))
    zero = jnp.zeros_like(q[:, :LANES])
    outs = []
    for g in range(A_KV_HEADS):
        kl_g = _dup_heads(kl, g).astype(BF16)
        vl_g = _dup_heads(vl, g).astype(BF16)
        if latent:
            kc_g = _dup_heads(kc, g).astype(BF16)
            vc_g = _dup_heads(vc, g).astype(BF16)
        for j in range(HEADS_PER_STACK // 2 - 1, A_GROUP // 2, HEADS_PER_STACK // 2):
            heads = range(g * A_GROUP + 2 * j + 2 - HEADS_PER_STACK, g * A_GROUP + 2 * j + 2)
            qs = jnp.concatenate([jnp.where(low if h % 2 == 0 else jnp.logical_not(low),
                                            q[:, (h // 2) * LANES:(h // 2 + 1) * LANES], zero) for h in heads], axis=0)
            sink = jnp.concatenate([jnp.full((tq, 1), sink_ref[h], F32) for h in heads], axis=0)
            s_loc = lax.dot_general(qs, kl_g, nt, preferred_element_type=F32)
            m = sink
            if latent:
                s_loc = s_loc + bias[:len(heads) * tq]
                s_ctx = lax.dot_general(qs, kc_g, nt, preferred_element_type=F32)
                m = jnp.maximum(m, jnp.max(s_ctx, axis=-1, keepdims=True))
            m = jnp.maximum(m, jnp.max(s_loc, axis=-1, keepdims=True))
            p_loc = jnp.exp(s_loc - m)
            den = jnp.exp(sink - m) + jnp.sum(p_loc, axis=-1, keepdims=True)
            acc = jnp.dot(p_loc.astype(BF16), vl_g, preferred_element_type=F32)
            if latent:
                p_ctx = jnp.exp(s_ctx - m)
                den = den + jnp.sum(p_ctx, axis=-1, keepdims=True)
                acc = acc + jnp.dot(p_ctx.astype(BF16), vc_g, preferred_element_type=F32)
            og = acc / den
            for jj in range(len(heads) // 2):
                outs.append(jnp.where(low, og[(2 * jj) * tq:(2 * jj + 1) * tq],
                                      og[(2 * jj + 1) * tq:(2 * jj + 2) * tq]))
    o_ref[0] = jnp.concatenate(outs, axis=1).astype(o_ref.dtype)


def attention(q, k, v, sink, kc=None, vc=None, tq=TQ_ATT):
    B, L, _ = q.shape
    latent = kc is not None
    tq = min(tq, L)
    seq = lambda wd: pl.BlockSpec((1, L, wd), lambda b, i: (b, 0, 0))
    in_specs = [
        pl.BlockSpec(memory_space=pltpu.SMEM),
        pl.BlockSpec((1, tq, A_WIDTH), lambda b, i: (b, i, 0)),
        seq(LANES), seq(LANES),
    ]
    args = [sink, q, k, v]
    if latent:
        P = kc.shape[1]
        in_specs += [pl.BlockSpec((1, P, LANES), lambda b, i: (b, 0, 0))] * 2
        args += [kc, vc]
    return pl.pallas_call(
        functools.partial(_attn_body, latent=latent, seq_len=L),
        grid=(B, L // tq),
        in_specs=in_specs,
        out_specs=pl.BlockSpec((1, tq, A_WIDTH), lambda b, i: (b, i, 0)),
        out_shape=jax.ShapeDtypeStruct((B, L, A_WIDTH), BF16),
        compiler_params=_cparams("parallel", "arbitrary"),
        name="attention_latent" if latent else "attention_context",
    )(*args)


def _split3(x):
    hi = x.astype(BF16)
    r = x - hi.astype(F32)
    mid = r.astype(BF16)
    lo = (r - mid.astype(F32)).astype(BF16)
    return hi, mid, lo


def _gla_group(q_ref, k_ref, v_ref, la_ref, o_ref, st_ref, b, d, reverse, n_chunks):
    C = GLA_CHUNK
    W = C_WIDTH
    ti = lax.broadcasted_iota(jnp.int32, (C, C), 0)
    si = lax.broadcasted_iota(jnp.int32, (C, C), 1)
    tri = (si >= ti) if reverse else (si <= ti)
    tri_b = tri.astype(BF16)
    tri4 = jnp.concatenate([tri] * C_HEADS, axis=0)
    r4 = lax.broadcasted_iota(jnp.int32, (C_HEADS * C, W), 0) // C
    c4 = lax.broadcasted_iota(jnp.int32, (C_HEADS * C, W), 1) // C_DK
    same_head = r4 == c4
    nt = (((1,), (1,)), ((), ()))
    tn = (((0,), (0,)), ((), ()))
    chunks = range(n_chunks)
    rows = [pl.ds(c * C, C) for c in chunks]
    vbs = [v_ref[b, r, :].astype(BF16) for r in rows]
    parts = [_split3(la_ref[b, r, :]) for r in rows]
    bsums = [jnp.dot(tri_b, hi, preferred_element_type=F32) + jnp.dot(tri_b, mid, preferred_element_type=F32)
             + jnp.dot(tri_b, lo, preferred_element_type=F32) for hi, mid, lo in parts]
    b_lasts = [s[0:1] if reverse else s[C - 1:C] for s in bsums]
    qgs = [q_ref[b, r, :] * jnp.exp(s) for r, s in zip(rows, bsums)]
    kgs = [(k_ref[b, r, :] * jnp.exp(-s)).astype(BF16) for r, s in zip(rows, bsums)]
    kds = [(k_ref[b, r, :] * jnp.exp(bl - s)).astype(BF16) for r, s, bl in zip(rows, bsums, b_lasts)]
    decays = [jnp.exp(bl) for bl in b_lasts]
    q_bds = [jnp.where(same_head, jnp.concatenate([qg] * C_HEADS, axis=0), 0.0).astype(BF16) for qg in qgs]
    uts = [jnp.where(same_head, lax.dot_general(vb, kd, tn, preferred_element_type=F32), 0.0)
           for vb, kd in zip(vbs, kds)]
    a_s = [jnp.where(tri4, lax.dot_general(qb, kg, nt, preferred_element_type=F32), 0.0).astype(BF16)
           for qb, kg in zip(q_bds, kgs)]
    r_s = [jnp.where(same_head, jnp.dot(a, vb, preferred_element_type=F32), 0.0)
           for a, vb in zip(a_s, vbs)]
    o_intra = [sum([r[h * C:(h + 1) * C] for h in range(1, C_HEADS)], r[0:C]) for r in r_s]
    qgb = [qg.astype(BF16) for qg in qgs]
    st = st_ref[d]
    for c in (reversed(chunks) if reverse else chunks):
        o_ref[b, rows[c], :] = o_intra[c] + lax.dot_general(qgb[c], st.astype(BF16), nt,
                                                            preferred_element_type=F32)
        st = st * decays[c] + uts[c]
    st_ref[d] = st


def _gla_body(*refs, has_state, n_chunks):
    if has_state:
        (qf, kf, vf, lf, qb, kb, vb, lb, s0_ref, of_ref, ob_ref, sT_ref, st) = refs
    else:
        (qf, kf, vf, lf, qb, kb, vb, lb, of_ref, ob_ref, sT_ref, st) = refs
    j = pl.program_id(1)

    nb = qf.shape[0]

    @pl.when(j == 0)
    def _init():
        if has_state:
            st[...] = s0_ref[...].reshape(st.shape)
        else:
            st[...] = jnp.zeros_like(st)

    for b in range(nb):
        _gla_group(qf, kf, vf, lf, of_ref, st, b, 2 * b, False, n_chunks)
        _gla_group(qb, kb, vb, lb, ob_ref, st, b, 2 * b + 1, True, n_chunks)

    @pl.when(j == pl.num_programs(1) - 1)
    def _final():
        sT_ref[...] = st[...].reshape(sT_ref.shape)


NB_GLA = 1


def gla(cq, ck, cv, la, s0=None, rows=512, nb=NB_GLA):
    B, L, W = cq.shape
    assert B % nb == 0
    rows = min(rows, L)
    ng = L // rows
    has_state = s0 is not None
    fwd = lambda: pl.BlockSpec((nb, rows, W), lambda b, j: (b, j, 0))
    bwd = lambda: pl.BlockSpec((nb, rows, W), lambda b, j: (b, ng - 1 - j, 0))
    state = lambda: pl.BlockSpec((nb, 2, W, W), lambda b, j: (b, 0, 0, 0))
    in_specs = [fwd(), fwd(), fwd(), pl.BlockSpec((nb, rows, W), lambda b, j: (b, j, 0)),
                bwd(), bwd(), bwd(), pl.BlockSpec((nb, rows, W), lambda b, j: (b, ng - 1 - j, 1))]
    args = [cq, ck, cv, la, cq, ck, cv, la]
    if has_state:
        in_specs.append(state())
        args.append(s0)
    return pl.pallas_call(
        functools.partial(_gla_body, has_state=has_state, n_chunks=rows // GLA_CHUNK),
        grid=(B // nb, ng),
        in_specs=in_specs,
        out_specs=[fwd(), bwd(), state()],
        out_shape=[jax.ShapeDtypeStruct((B, L, W), F32), jax.ShapeDtypeStruct((B, L, W), F32),
                   jax.ShapeDtypeStruct((B, 2, W, W), F32)],
        scratch_shapes=[pltpu.VMEM((2 * nb, W, W), F32)],
        compiler_params=_cparams("parallel", "arbitrary"),
        name="gla",
    )(*args)


TM_DFT = 512
BG_DFT = 4


def _conv3_body(x_ref, w_ref, b_ref, u_ref, ub_ref):
    x = x_ref[0]
    L = x.shape[0]
    row = lax.broadcasted_iota(jnp.int32, (L, 1), 0)
    prev = jnp.where(row == 0, 0.0, pltpu.roll(x, 1, 0))
    nxt = jnp.where(row == L - 1, 0.0, pltpu.roll(x, L - 1, 0))
    u = prev * w_ref[0:1, :] + x * w_ref[1:2, :] + nxt * w_ref[2:3, :] + b_ref[...]
    u_ref[0] = u
    ub_ref[0] = u.astype(BF16)


def hyena_conv3(hy, w, b):
    B, L, C3 = hy.shape
    blk = pl.BlockSpec((1, L, HY_CH), lambda b_, j: (b_, 0, j))
    return pl.pallas_call(
        _conv3_body,
        grid=(B, C3 // HY_CH),
        in_specs=[blk, pl.BlockSpec((SUBLANES, HY_CH), lambda b_, j: (0, j)),
                  pl.BlockSpec((1, HY_CH), lambda b_, j: (0, j))],
        out_specs=[blk, blk],
        out_shape=[jax.ShapeDtypeStruct(hy.shape, F32), jax.ShapeDtypeStruct(hy.shape, BF16)],
        compiler_params=_cparams("parallel", "parallel"),
        name="hyena_conv3",
    )(hy, w, b)


def _freq_weight(i, tm, n):
    k = i * tm + lax.broadcasted_iota(jnp.int32, (tm, 1), 0)
    return jnp.where(k == 0, 1.0 / n, 2.0 / n).astype(F32)


def _filt_body(fr_ref, fi_ref, sg_ref, hs_ref, hd_ref, hr_ref, hi_ref, hny_ref, *, n):
    i = pl.program_id(0)
    wk = _freq_weight(i, fr_ref.shape[0], n)
    fr = fr_ref[...]
    fi = fi_ref[...]
    hr_ref[...] = wk * (jnp.dot(fr, hs_ref[0], preferred_element_type=F32)
                        + jnp.dot(fr, hs_ref[1], preferred_element_type=F32))
    hi_ref[...] = wk * (jnp.dot(fi, hd_ref[0], preferred_element_type=F32)
                        + jnp.dot(fi, hd_ref[1], preferred_element_type=F32))

    @pl.when(i == 0)
    def _nyquist():
        sg = sg_ref[...]
        hny_ref[...] = (jnp.dot(sg, hs_ref[0], preferred_element_type=F32)
                        + jnp.dot(sg, hs_ref[1], preferred_element_type=F32)) * (1.0 / n)


def hyena_filter_spectrum(fr, fi, sg, hs, hd, tm=TM_DFT):
    L = fr.shape[0]
    W = hs.shape[2]
    tm = min(tm, L)
    whole = lambda shape: pl.BlockSpec(shape, lambda i: (0,) * len(shape))
    return pl.pallas_call(
        functools.partial(_filt_body, n=2 * L),
        grid=(L // tm,),
        in_specs=[pl.BlockSpec((tm, L), lambda i: (i, 0)), pl.BlockSpec((tm, L), lambda i: (i, 0)),
                  whole((SUBLANES, L)), whole((2, L, W)), whole((2, L, W))],
        out_specs=[pl.BlockSpec((tm, W), lambda i: (i, 0)), pl.BlockSpec((tm, W), lambda i: (i, 0)),
                   whole((SUBLANES, W))],
        out_shape=[jax.ShapeDtypeStruct((L, W), F32), jax.ShapeDtypeStruct((L, W), F32),
                   jax.ShapeDtypeStruct((SUBLANES, W), F32)],
        compiler_params=_cparams("arbitrary"),
        name="hyena_filter_spectrum",
    )(fr, fi, sg, hs, hd)


def _dft_fwd_body(fr_ref, fi_ref, sg_ref, z_ref, hr_ref, hi_ref, hny_ref, yr_ref, yi_ref, yny_ref):
    nb = z_ref.shape[0]
    hr = hr_ref[...]
    hi = hi_ref[...]

    def body(b, carry):
        zb = z_ref[b]
        xr = jnp.dot(fr_ref[...], zb, preferred_element_type=F32)
        xi = jnp.dot(fi_ref[...], zb, preferred_element_type=F32)
        yr_ref[b] = (xr * hr - xi * hi).astype(BF16)
        yi_ref[b] = (xr * hi + xi * hr).astype(BF16)
        return carry
    lax.fori_loop(0, nb, body, 0)

    @pl.when(pl.program_id(1) == 0)
    def _nyquist():
        def nyq(b, carry):
            yny_ref[b] = jnp.dot(sg_ref[...], z_ref[b], preferred_element_type=F32) * hny_ref[0:1, :]
            return carry
        lax.fori_loop(0, nb, nyq, 0)


def hyena_dft_fwd(fr, fi, sg, zb, zcol, hr, hi, hny, order, tm=TM_DFT, bg=BG_DFT):
    B, L, _ = zb.shape
    C = HY_CH
    tm = min(tm, L)
    bg = B if L < TM_DFT else bg
    ftile = pl.BlockSpec((tm, L), lambda g, i: (i, 0))
    ytile = pl.BlockSpec((bg, tm, C), lambda g, i: (g, i, 0))
    return pl.pallas_call(
        _dft_fwd_body,
        grid=(B // bg, L // tm),
        in_specs=[ftile, ftile, pl.BlockSpec((SUBLANES, L), lambda g, i: (0, 0)),
                  pl.BlockSpec((bg, L, C), lambda g, i: (g, 0, zcol)),
                  pl.BlockSpec((tm, C), lambda g, i: (i, order)), pl.BlockSpec((tm, C), lambda g, i: (i, order)),
                  pl.BlockSpec((SUBLANES, C), lambda g, i: (0, order))],
        out_specs=[ytile, ytile, pl.BlockSpec((bg, SUBLANES, C), lambda g, i: (g, 0, 0))],
        out_shape=[jax.ShapeDtypeStruct((B, L, C), BF16), jax.ShapeDtypeStruct((B, L, C), BF16),
                   jax.ShapeDtypeStruct((B, SUBLANES, C), F32)],
        compiler_params=_cparams("parallel", "arbitrary"),
        name="hyena_dft_fwd",
    )(fr, fi, sg, zb, hr, hi, hny)


def _dft_inv_body(fr_ref, fi_ref, yr_ref, yi_ref, yny_ref, z_ref, gate_ref, d_ref, zo_ref, zob_ref):
    nb = z_ref.shape[0]
    tm = fr_ref.shape[0]
    t = pl.program_id(1) * tm + lax.broadcasted_iota(jnp.int32, (tm, 1), 0)
    sign = jnp.where((t & 1) == 0, 1.0, -1.0).astype(F32)
    d = d_ref[...]

    def body(b, carry):
        conv = (jnp.dot(fr_ref[...], yr_ref[b], preferred_element_type=F32)
                + jnp.dot(fi_ref[...], yi_ref[b], preferred_element_type=F32) + sign * yny_ref[b][0:1, :])
        zn = gate_ref[b] * (conv + d * z_ref[b])
        zo_ref[b] = zn
        zob_ref[b] = zn.astype(BF16)
        return carry
    lax.fori_loop(0, nb, body, 0)


def hyena_dft_inv(fr, fi, yr, yi, yny, z, zcol, gate, gcol, d, tm=TM_DFT, bg=BG_DFT):
    B, L, C = yr.shape
    tm = min(tm, L)
    bg = B if L < TM_DFT else bg
    ftile = pl.BlockSpec((tm, L), lambda g, i: (i, 0))
    whole_y = lambda: pl.BlockSpec((bg, L, C), lambda g, i: (g, 0, 0), pipeline_mode=pl.Buffered(1))
    otile = pl.BlockSpec((bg, tm, C), lambda g, i: (g, i, 0))
    return pl.pallas_call(
        _dft_inv_body,
        grid=(B // bg, L // tm),
        in_specs=[ftile, ftile, whole_y(), whole_y(),
                  pl.BlockSpec((bg, SUBLANES, C), lambda g, i: (g, 0, 0)),
                  pl.BlockSpec((bg, tm, C), lambda g, i: (g, i, zcol)),
                  pl.BlockSpec((bg, tm, C), lambda g, i: (g, i, gcol)),
                  pl.BlockSpec((1, C), lambda g, i: (0, 0))],
        out_specs=[otile, otile],
        out_shape=[jax.ShapeDtypeStruct((B, L, C), F32), jax.ShapeDtypeStruct((B, L, C), BF16)],
        compiler_params=_cparams("parallel", "arbitrary"),
        name="hyena_dft_inv",
    )(fr, fi, yr, yi, yny, z, gate, d)


def dft_matrices(L):
    r = 1 << (max(L.bit_length() - 1, 0) // 2)
    t = jnp.arange(L, dtype=jnp.int32)

    def table(k):
        ang = ((k[:, None] * t[None, :]) % (2 * L)).astype(F32) * (math.pi / L)
        return jnp.cos(ang), jnp.sin(ang)
    ca, sa = table(jnp.arange(L // r, dtype=jnp.int32) * r)
    cb, sb = table(jnp.arange(r, dtype=jnp.int32))
    fr = ca[:, None, :] * cb[None, :, :] - sa[:, None, :] * sb[None, :, :]
    fi = -(sa[:, None, :] * cb[None, :, :] + ca[:, None, :] * sb[None, :, :])
    sg = jnp.zeros((SUBLANES, L), F32).at[0].set(jnp.where(t % 2 == 0, 1.0, -1.0))
    return fr.reshape(L, L).astype(BF16), fi.reshape(L, L).astype(BF16), sg.astype(BF16)


def hyena_time_filters(L, lp):
    t = jnp.arange(L, dtype=F32)
    t_norm = t / max(L - 1, 1)
    w = (2.0 * math.pi / L) * t
    f = jnp.linspace(1e-4, HY_BANDS - 1, HY_BANDS, dtype=F32)
    fw = w[:, None] * f[None, :]
    feat = jnp.concatenate([t_norm[:, None], jnp.cos(fw), -jnp.sin(fw)], axis=-1)
    z = jnp.sin(lp['hy_freq1'] * (feat @ lp['hy_w1'] + lp['hy_b1']))
    z = jnp.sin(lp['hy_freq2'] * (z @ lp['hy_w2'] + lp['hy_b2']))
    hf = (z @ lp['hy_w3']).astype(F32).reshape(L, 2, HY_ORDER, HY_CH)
    hf = hf * jnp.exp(-t_norm[:, None, None, None] * jnp.abs(lp['hy_decay'].astype(F32)))
    return hf / (jnp.sum(jnp.abs(hf), axis=(0, 1), keepdims=True) + EPS)


def _hi_lo(x):
    hi = x.astype(BF16)
    return jnp.stack([hi, (x - hi.astype(F32)).astype(BF16)])


def hyena(hy, lp, mats):
    B, L, _ = hy.shape
    fr, fi, sg = mats
    hf = hyena_time_filters(L, lp)
    h_fwd = hf[:, 0].reshape(L, HY_ORDER * HY_CH)
    h_bwd = hf[:, 1].at[0].set(0.0).reshape(L, HY_ORDER * HY_CH)
    hr, hi, hny = hyena_filter_spectrum(fr, fi, sg, _hi_lo(h_fwd + h_bwd), _hi_lo(h_fwd - h_bwd))
    w = jnp.pad(lp['hy_conv_w'], ((0, SUBLANES - 3), (0, 0)))
    u, ub = hyena_conv3(hy, w, lp['hy_conv_b'][None, :])
    d = lp['hy_d'].astype(F32)
    z, zb, zcol = u, ub, 0
    for o in range(HY_ORDER):
        yr, yi, yny = hyena_dft_fwd(fr, fi, sg, zb, zcol, hr, hi, hny, o)
        z, zb = hyena_dft_inv(fr, fi, yr, yi, yny, z, zcol, u, 1 + o, d[o][None, :])
        zcol = 0
    return z


BG_HALF = 2


def _shift_rows(x, down):
    n = x.shape[0]
    row = lax.broadcasted_iota(jnp.int32, (n, 1), 0)
    if down:
        return jnp.where(row == 0, 0.0, pltpu.roll(x, 1, 0))
    return jnp.where(row == n - 1, 0.0, pltpu.roll(x, n - 1, 0))


def _conv3_planes_body(x_ref, w_ref, b_ref, u_ref, ub_ref):
    H = x_ref.shape[0] // 2
    e = x_ref[pl.ds(0, H, stride=2), :]
    o = x_ref[pl.ds(1, H, stride=2), :]
    w0, w1, w2 = w_ref[0:1, :], w_ref[1:2, :], w_ref[2:3, :]
    ue = _shift_rows(o, True) * w0 + e * w1 + o * w2 + b_ref[...]
    uo = e * w0 + o * w1 + _shift_rows(e, False) * w2 + b_ref[...]
    u_ref[0, 0] = ue
    u_ref[0, 1] = uo
    ub_ref[0, 0] = ue.astype(BF16)
    ub_ref[0, 1] = uo.astype(BF16)


def hyena_conv3_planes(hy, w, b):
    B, L, C3 = hy.shape
    H = L // 2
    out = pl.BlockSpec((1, 2, H, LANES), lambda b_, j: (b_, 0, 0, j))
    return pl.pallas_call(
        _conv3_planes_body,
        grid=(B, C3 // LANES),
        in_specs=[pl.BlockSpec((None, L, LANES), lambda b_, j: (b_, 0, j)),
                  pl.BlockSpec((SUBLANES, LANES), lambda b_, j: (0, j)),
                  pl.BlockSpec((1, LANES), lambda b_, j: (0, j))],
        out_specs=[out, out],
        out_shape=[jax.ShapeDtypeStruct((B, 2, H, C3), F32), jax.ShapeDtypeStruct((B, 2, H, C3), BF16)],
        compiler_params=_cparams("parallel", "parallel"),
        name="hyena_conv3",
    )(hy, w, b)


def _half_filt_body(fe_ref, fo_ref, sg_ref, hse_ref, hso_ref, hde_ref, hdo_ref,
                    lr_ref, li_ref, hr_ref, hi_ref, sp_ref, *, n):
    i = pl.program_id(0)
    tm = lr_ref.shape[0]
    wk = _freq_weight(i, tm, n)

    def two_pass(m, parts_ref):
        return (jnp.dot(m, parts_ref[0], preferred_element_type=F32)
                + jnp.dot(m, parts_ref[1], preferred_element_type=F32))
    a = two_pass(fe_ref[0], hse_ref)
    b = two_pass(fo_ref[0], hso_ref)
    c = two_pass(fe_ref[0], hde_ref)
    d = two_pass(fo_ref[0], hdo_ref)
    lr_ref[...] = wk * (a[:tm] + b[:tm])
    hr_ref[...] = wk * (a[:tm] - b[:tm])
    li_ref[...] = wk * (c[tm:] + d[tm:])
    hi_ref[...] = wk * (d[tm:] - c[tm:])

    @pl.when(i == 0)
    def _middle_bin():
        sr = two_pass(sg_ref[...], hse_ref)[0:1, :]
        si = -two_pass(sg_ref[...], hdo_ref)[0:1, :]
        row = lax.broadcasted_iota(jnp.int32, sp_ref.shape, 0)
        sp_ref[...] = jnp.where(row == 0, sr, jnp.where(row == 1, si, 0.0)) * (2.0 / n)


def hyena_half_filter_spectrum(fe, fo, sg, hs, hd, tm):
    H = fe.shape[2]
    W = hs.shape[2]
    planes = lambda h: (h.reshape(2, H, 2, W)[:, :, 0], h.reshape(2, H, 2, W)[:, :, 1])
    hse, hso = planes(hs)
    hde, hdo = planes(hd)
    whole = lambda shape: pl.BlockSpec(shape, lambda i: (0,) * len(shape))
    ftile = pl.BlockSpec((1, 2 * tm, H), lambda i: (i, 0, 0))
    otile = pl.BlockSpec((tm, W), lambda i: (i, 0))
    out = jax.ShapeDtypeStruct((H, W), F32)
    return pl.pallas_call(
        functools.partial(_half_filt_body, n=4 * H),
        grid=(H // tm,),
        in_specs=[ftile, ftile, whole((SUBLANES, H))] + [whole((2, H, W))] * 4,
        out_specs=[otile, otile, otile, otile, whole((SUBLANES, W))],
        out_shape=[out, out, out, out, jax.ShapeDtypeStruct((SUBLANES, W), F32)],
        compiler_params=_cparams("arbitrary"),
        name="hyena_filter_spectrum",
    )(fe, fo, sg, hse, hso, hde, hdo)


def _cmul(xr, xi, hr, hi):
    return xr * hr - xi * hi, xr * hi + xi * hr


def _half_fwd_body(fe_ref, fo_ref, sg_ref, ze_ref, zo_ref, lr_ref, li_ref, hr_ref, hi_ref, sp_ref,
                   per_ref, pei_ref, por_ref, poi_ref, ysp_ref):
    nb = ze_ref.shape[0]
    tm = lr_ref.shape[0]

    def body(b, carry):
        a = jnp.dot(fe_ref[0], ze_ref[b], preferred_element_type=F32)
        o = jnp.dot(fo_ref[0], zo_ref[b], preferred_element_type=F32)
        ylr, yli = _cmul(a[:tm] + o[:tm], a[tm:] + o[tm:], lr_ref[...], li_ref[...])
        yhr, yhi = _cmul(a[:tm] - o[:tm], o[tm:] - a[tm:], hr_ref[...], hi_ref[...])
        per_ref[b] = (ylr + yhr).astype(BF16)
        pei_ref[b] = (yli - yhi).astype(BF16)
        por_ref[b] = (ylr - yhr).astype(BF16)
        poi_ref[b] = (yli + yhi).astype(BF16)
        return carry
    lax.fori_loop(0, nb, body, 0)

    @pl.when(pl.program_id(1) == 0)
    def _middle_bin():
        def mid(b, carry):
            xr = jnp.dot(sg_ref[...], ze_ref[b], preferred_element_type=F32)[0:1, :]
            xi = -jnp.dot(sg_ref[...], zo_ref[b], preferred_element_type=F32)[0:1, :]
            yr, yi = _cmul(xr, xi, sp_ref[0:1, :], sp_ref[1:2, :])
            row = lax.broadcasted_iota(jnp.int32, ysp_ref.shape[1:], 0)
            ysp_ref[b] = jnp.where(row == 0, yr, jnp.where(row == 1, yi, 0.0))
            return carry
        lax.fori_loop(0, nb, mid, 0)


def hyena_half_fwd(fe, fo, sg, zb, zcol, spec, order, tm, bg):
    B, _, H, _ = zb.shape
    C = HY_CH
    lr, li, hr, hi, sp = spec
    ftile = pl.BlockSpec((1, 2 * tm, H), lambda g, i: (i, 0, 0))
    plane = lambda p: pl.BlockSpec((bg, None, H, C), lambda g, i: (g, p, 0, zcol))
    stile = pl.BlockSpec((tm, C), lambda g, i: (i, order))
    ptile = pl.BlockSpec((bg, tm, C), lambda g, i: (g, i, 0))
    pshape = jax.ShapeDtypeStruct((B, H, C), BF16)
    return pl.pallas_call(
        _half_fwd_body,
        grid=(B // bg, H // tm),
        in_specs=[ftile, ftile, pl.BlockSpec((SUBLANES, H), lambda g, i: (0, 0)), plane(0), plane(1),
                  stile, stile, stile, stile, pl.BlockSpec((SUBLANES, C), lambda g, i: (0, order))],
        out_specs=[ptile, ptile, ptile, ptile, pl.BlockSpec((bg, SUBLANES, C), lambda g, i: (g, 0, 0))],
        out_shape=[pshape, pshape, pshape, pshape, jax.ShapeDtypeStruct((B, SUBLANES, C), F32)],
        compiler_params=_cparams("parallel", "arbitrary"),
        name="hyena_dft_fwd",
    )(fe, fo, sg, zb, zb, lr, li, hr, hi, sp)


def _half_inv_body(fer_ref, fei_ref, fotr_ref, foti_ref, per_ref, pei_ref, por_ref, poi_ref, ysp_ref,
                   ze_ref, zo_ref, ge_ref, go_ref, d_ref, *out_refs, natural):
    nb = ze_ref.shape[0]
    tm = fer_ref.shape[0]
    u = pl.program_id(1) * tm + lax.broadcasted_iota(jnp.int32, (tm, 1), 0)
    sign = jnp.where((u & 1) == 0, 1.0, -1.0).astype(F32)
    d = d_ref[...]

    def body(b, carry):
        ce = (jnp.dot(fer_ref[...], per_ref[b], preferred_element_type=F32)
              + jnp.dot(fei_ref[...], pei_ref[b], preferred_element_type=F32) + sign * ysp_ref[b][0:1, :])
        co = (jnp.dot(fotr_ref[...], por_ref[b], preferred_element_type=F32)
              + jnp.dot(foti_ref[...], poi_ref[b], preferred_element_type=F32) - sign * ysp_ref[b][1:2, :])
        zne = ge_ref[b] * (ce + d * ze_ref[b])
        zno = go_ref[b] * (co + d * zo_ref[b])
        if natural:
            bi = jnp.asarray(b, jnp.int32)
            for c, o_ref in enumerate(out_refs):
                o_ref.at[bi][pl.ds(0, tm, stride=2), :] = zne[:, c * LANES:(c + 1) * LANES]
                o_ref.at[bi][pl.ds(1, tm, stride=2), :] = zno[:, c * LANES:(c + 1) * LANES]
        else:
            z_ref, zb_ref = out_refs
            z_ref[b, 0] = zne
            z_ref[b, 1] = zno
            zb_ref[b, 0] = zne.astype(BF16)
            zb_ref[b, 1] = zno.astype(BF16)
        return carry
    lax.fori_loop(0, nb, body, 0)


def hyena_half_inv(mats, ps, ysp, z, zcol, gate, gcol, d, tm, bg, natural=False):
    fer, fei, fotr, foti = mats
    per = ps[0]
    B, H, C = per.shape
    ftile = pl.BlockSpec((tm, H), lambda g, i: (i, 0))
    whole_p = lambda: pl.BlockSpec((bg, H, C), lambda g, i: (g, 0, 0), pipeline_mode=pl.Buffered(1))
    tile = lambda p, col: pl.BlockSpec((bg, None, tm, C), lambda g, i: (g, p, i, col))
    if natural:
        out_specs = [pl.BlockSpec((bg, 2 * tm, LANES), lambda g, i: (g, i, 0))] * (C // LANES)
        out_shape = [jax.ShapeDtypeStruct((B, 2 * H, LANES), F32)] * (C // LANES)
    else:
        out_specs = [pl.BlockSpec((bg, 2, tm, C), lambda g, i: (g, 0, i, 0))] * 2
        out_shape = [jax.ShapeDtypeStruct((B, 2, H, C), F32), jax.ShapeDtypeStruct((B, 2, H, C), BF16)]
    return pl.pallas_call(
        functools.partial(_half_inv_body, natural=natural),
        grid=(B // bg, H // tm),
        in_specs=[ftile, ftile, ftile, ftile, whole_p(), whole_p(), whole_p(), whole_p(),
                  pl.BlockSpec((bg, SUBLANES, C), lambda g, i: (g, 0, 0)),
                  tile(0, zcol), tile(1, zcol), tile(0, gcol), tile(1, gcol),
                  pl.BlockSpec((1, C), lambda g, i: (0, 0))],
        out_specs=out_specs,
        out_shape=out_shape,
        compiler_params=_cparams("parallel", "arbitrary"),
        name="hyena_dft_inv",
    )(fer, fei, fotr, foti, *ps, ysp, z, z, gate, gate, d)


def dft_half_matrices(L, tm):
    H = L // 2
    r = 1 << (max(H.bit_length() - 1, 0) // 2)
    u = jnp.arange(H, dtype=jnp.int32)

    hi = jnp.arange(H // r, dtype=jnp.int32) * r
    lo = jnp.arange(r, dtype=jnp.int32)

    def cos_msin(row_hi, row_lo, col):
        def table(k):
            ang = ((k[:, None] * col[None, :]) % (2 * L)).astype(F32) * (math.pi / L)
            return jnp.cos(ang), jnp.sin(ang)
        ca, sa = table(row_hi)
        cb, sb = table(row_lo)
        c = ca[:, None, :] * cb[None, :, :] - sa[:, None, :] * sb[None, :, :]
        s = sa[:, None, :] * cb[None, :, :] + ca[:, None, :] * sb[None, :, :]
        return c.reshape(H, H).astype(BF16), (-s).reshape(H, H).astype(BF16)
    fre, fie = cos_msin(hi, lo, 2 * u)
    fro, fio = cos_msin(hi, lo, 2 * u + 1)
    frot, fiot = cos_msin(2 * hi, 2 * lo + 1, u)
    stack = lambda a, b: jnp.concatenate([a.reshape(H // tm, tm, H), b.reshape(H // tm, tm, H)], axis=1)
    sg = jnp.zeros((SUBLANES, H), F32).at[0].set(jnp.where(u % 2 == 0, 1.0, -1.0)).astype(BF16)
    return dict(fe=stack(fre, fie), fo=stack(fro, fio), sg=sg, inv=(fre, fie, frot, fiot), tm=tm)


def hyena_half(hy, lp, mats):
    B, L, _ = hy.shape
    tm = mats['tm']
    bg = B if L < TM_DFT else BG_HALF
    hf = hyena_time_filters(L, lp)
    h_fwd = hf[:, 0].reshape(L, HY_ORDER * HY_CH)
    h_bwd = hf[:, 1].at[0].set(0.0).reshape(L, HY_ORDER * HY_CH)
    spec = hyena_half_filter_spectrum(mats['fe'], mats['fo'], mats['sg'], _hi_lo(h_fwd + h_bwd),
                                      _hi_lo(h_fwd - h_bwd), tm)
    w = jnp.pad(lp['hy_conv_w'], ((0, SUBLANES - 3), (0, 0)))
    u, ub = hyena_conv3_planes(hy, w, lp['hy_conv_b'][None, :])
    d = lp['hy_d'].astype(F32)
    z, zb, zcol = u, ub, 0
    for o in range(HY_ORDER):
        *ps, ysp = hyena_half_fwd(mats['fe'], mats['fo'], mats['sg'], zb, zcol, spec, o, tm, bg)
        last = o == HY_ORDER - 1
        out = hyena_half_inv(mats['inv'], ps, ysp, z, zcol, u, 1 + o, d[o][None, :], tm, bg, natural=last)
        if last:
            return out
        z, zb = out
        zcol = 0


def _ffn_body(x_ref, mod_ref, g_ref, w1_ref, w3_ref, w2_ref, o_ref):
    x = x_ref[...]
    h = _adaln_rows(x, g_ref[...], mod_ref[0, ROW_SHIFT2:ROW_SHIFT2 + 1, :],
                    mod_ref[0, ROW_SCALE2:ROW_SCALE2 + 1, :]).astype(BF16)
    a = jnp.dot(h, w1_ref[...], preferred_element_type=F32)
    b = jnp.dot(h, w3_ref[...], preferred_element_type=F32)
    act = (_silu(a) * b).astype(BF16)
    ff = jnp.dot(act, w2_ref[...], preferred_element_type=F32)
    o_ref[...] = x + mod_ref[0, ROW_GATE2:ROW_GATE2 + 1, :] * ff


def ffn_dense(x, mods, g, w1, w3, w2, seg_len, tm=TM_PROJ):
    T, D = x.shape
    F = w1.shape[1]
    resident = functools.partial(pl.BlockSpec, pipeline_mode=pl.Buffered(1))
    return pl.pallas_call(
        _ffn_body,
        grid=(T // tm,),
        in_specs=[
            pl.BlockSpec((tm, D), lambda i: (i, 0)),
            pl.BlockSpec((1, MOD_ROWS, D), lambda i: (i * tm // seg_len, 0, 0)),
            pl.BlockSpec((1, D), lambda i: (0, 0)),
            resident((D, F), lambda i: (0, 0)),
            resident((D, F), lambda i: (0, 0)),
            resident((F, D), lambda i: (0, 0)),
        ],
        out_specs=pl.BlockSpec((tm, D), lambda i: (i, 0)),
        out_shape=jax.ShapeDtypeStruct((T, D), F32),
        compiler_params=_cparams("parallel"),
        name="ffn_dense",
    )(x, mods, g, w1, w3, w2)


def _store_token_tiles(ref, x):
    n = x.shape[0]
    for s in range(SUBLANES):
        ref[pl.ds(s, n, stride=SUBLANES), :] = x[:, s * LANES:(s + 1) * LANES]


def _load_token_tiles(ref, n):
    return jnp.concatenate([ref[pl.ds(s, n, stride=SUBLANES), :] for s in range(SUBLANES)], axis=1)


def _router_body(xa_ref, xb_ref, mod_ref, g_ref, wr_ref, h_ref, idx_ref, gw_ref, *, na):
    x = jnp.where(pl.program_id(0) < na, xa_ref[...], xb_ref[...])
    h = _adaln_rows(x, g_ref[...], mod_ref[0, ROW_SHIFT2:ROW_SHIFT2 + 1, :],
                    mod_ref[0, ROW_SCALE2:ROW_SCALE2 + 1, :])
    _store_token_tiles(h_ref, h)
    logits = lax.dot_general(wr_ref[...], h, (((1,), (1,)), ((), ())),
                             precision=lax.Precision.HIGHEST, preferred_element_type=F32)
    eidx = lax.broadcasted_iota(jnp.int32, logits.shape, 0)
    m1 = jnp.max(logits, axis=0, keepdims=True)
    i1 = jnp.min(jnp.where(logits == m1, eidx, N_EXPERTS), axis=0, keepdims=True)
    rest = jnp.where(eidx == i1, -jnp.inf, logits)
    m2 = jnp.max(rest, axis=0, keepdims=True)
    i2 = jnp.min(jnp.where(rest == m2, eidx, N_EXPERTS), axis=0, keepdims=True)
    e2 = jnp.exp(m2 - m1)
    den = 1.0 + e2
    row = lax.broadcasted_iota(jnp.int32, logits.shape, 0)
    idx_ref[...] = jnp.where(row == 0, i1, jnp.where(row == 1, i2, 0))
    gw_ref[...] = jnp.transpose(jnp.where(row == 0, 1.0 / den, jnp.where(row == 1, e2 / den, 0.0)))


def _merged_seg(i, na, tm, seg_len_b):
    return jnp.where(i < na, 0, 1 + jnp.maximum(i - na, 0) * tm // seg_len_b)


def moe_router(xa, xb, mods, g, wr_t, seg_len_b, tm=TM_PROJ):
    D = xa.shape[1]
    na = xa.shape[0] // tm
    T = xa.shape[0] + xb.shape[0]
    return pl.pallas_call(
        functools.partial(_router_body, na=na),
        grid=(T // tm,),
        in_specs=[
            pl.BlockSpec((tm, D), lambda i: (jnp.minimum(i, na - 1), 0)),
            pl.BlockSpec((tm, D), lambda i: (jnp.maximum(i - na, 0), 0)),
            pl.BlockSpec((1, MOD_ROWS, D), lambda i: (_merged_seg(i, na, tm, seg_len_b), 0, 0)),
            pl.BlockSpec((1, D), lambda i: (0, 0)),
            pl.BlockSpec((N_EXPERTS, D), lambda i: (0, 0)),
        ],
        out_specs=[
            pl.BlockSpec((tm * SUBLANES, LANES), lambda i: (i, 0)),
            pl.BlockSpec((N_EXPERTS, tm), lambda i: (0, i)),
            pl.BlockSpec((tm, N_EXPERTS), lambda i: (i, 0)),
        ],
        out_shape=[
            jax.ShapeDtypeStruct((T * SUBLANES, LANES), F32),
            jax.ShapeDtypeStruct((N_EXPERTS, T), jnp.int32),
            jax.ShapeDtypeStruct((T, N_EXPERTS), F32),
        ],
        compiler_params=_cparams("parallel"),
        name="moe_router",
    )(xa, xb, mods, g, wr_t)


def _experts_body(te_ref, nv_ref, src0_ref, srcn_ref, dst_ref, h_hbm, w1_ref, w3_ref, w2_ref,
                  y_hbm, hrows, hb, acc, ybuf, sem_g, sem_s, *, tm, nf, n_slots):
    i = pl.program_id(0)
    f = pl.program_id(1)
    nv = nv_ref[0]
    valid = i < nv
    slot = lax.rem(i, 2)

    def gather_issue(idx_ref, s):
        def body(r, carry):
            src = pl.multiple_of(idx_ref[0, 0, r] * SUBLANES, SUBLANES)
            dst = pl.multiple_of(r * SUBLANES, SUBLANES)
            pltpu.make_async_copy(h_hbm.at[pl.ds(src, SUBLANES)], hrows.at[s, pl.ds(dst, SUBLANES)],
                                  sem_g.at[s]).start()
            return carry
        lax.fori_loop(0, tm, body, 0, unroll=8)

    def gather_wait(s):
        pltpu.make_async_copy(h_hbm.at[pl.ds(0, tm * SUBLANES)], hrows.at[s], sem_g.at[s]).wait()

    def scatter_wait(s):
        pltpu.make_async_copy(ybuf.at[s], y_hbm.at[pl.ds(0, tm * SUBLANES)], sem_s.at[s]).wait()

    @pl.when(jnp.logical_and(i == 0, f == 0))
    def _first():
        ybuf[0] = jnp.zeros(ybuf.shape[1:], ybuf.dtype)
        spare = lambda p: y_hbm.at[pl.ds((n_slots + p * tm) * SUBLANES, tm * SUBLANES)]
        for p in range(2):
            pltpu.make_async_copy(ybuf.at[0], spare(p), sem_s.at[p]).start()
        for p in range(2):
            pltpu.make_async_copy(ybuf.at[0], spare(p), sem_s.at[p]).wait()
        gather_issue(src0_ref, 0)

    @pl.when(jnp.logical_and(valid, f == 0))
    def _stage():
        gather_wait(slot)
        hb[...] = _load_token_tiles(hrows.at[slot], tm).astype(BF16)
        acc[...] = jnp.zeros_like(acc)

        @pl.when(i + 1 < nv)
        def _prefetch():
            gather_issue(srcn_ref, 1 - slot)

    @pl.when(valid)
    def _compute():
        h = hb[...]
        a = jnp.dot(h, w1_ref[0], preferred_element_type=F32)
        b = jnp.dot(h, w3_ref[0], preferred_element_type=F32)
        act = (_silu(a) * b).astype(BF16)
        acc[...] += jnp.dot(act, w2_ref[0], preferred_element_type=F32)

    @pl.when(jnp.logical_and(valid, f == nf - 1))
    def _emit():
        @pl.when(i >= 2)
        def _reuse():
            scatter_wait(slot)
        _store_token_tiles(ybuf.at[slot], acc[...])

        def body(r, carry):
            src = pl.multiple_of(r * SUBLANES, SUBLANES)
            dst = pl.multiple_of(dst_ref[0, 0, r] * SUBLANES, SUBLANES)
            pltpu.make_async_copy(ybuf.at[slot, pl.ds(src, SUBLANES)], y_hbm.at[pl.ds(dst, SUBLANES)],
                                  sem_s.at[slot]).start()
            return carry
        lax.fori_loop(0, tm, body, 0, unroll=8)

    @pl.when(jnp.logical_and(i == nv, f == 0))
    def _drain():
        scatter_wait(lax.rem(nv + 1, 2))

        @pl.when(nv >= 2)
        def _older():
            scatter_wait(lax.rem(nv, 2))


def moe_experts(h, tile_expert, n_valid, src_rows, dst_rows, w1, w3, w2, tm=TM_MOE, tf=TF_MOE):
    T = h.shape[0] // SUBLANES
    D = w1.shape[1]
    F = w1.shape[2]
    n_tiles = src_rows.shape[0]
    n_slots = TOP_K * T
    nf = F // tf

    def wcol(i, f, te, nv):
        return (te[i], 0, jnp.where(i < nv[0], f, nf - 1))

    def wrow(i, f, te, nv):
        return (te[i], jnp.where(i < nv[0], f, nf - 1), 0)

    smem_tile = lambda fn: pl.BlockSpec((1, 1, tm), fn, memory_space=pltpu.SMEM)
    grid_spec = pltpu.PrefetchScalarGridSpec(
        num_scalar_prefetch=2,
        grid=(n_tiles, nf),
        in_specs=[
            smem_tile(lambda i, f, te, nv: (0, 0, 0)),
            smem_tile(lambda i, f, te, nv: (jnp.minimum(i + 1, n_tiles - 1), 0, 0)),
            smem_tile(lambda i, f, te, nv: (i, 0, 0)),
            pl.BlockSpec(memory_space=pl.ANY),
            pl.BlockSpec((1, D, tf), wcol),
            pl.BlockSpec((1, D, tf), wcol),
            pl.BlockSpec((1, tf, D), wrow),
        ],
        out_specs=pl.BlockSpec(memory_space=pl.ANY),
        scratch_shapes=[
            pltpu.VMEM((2, tm * SUBLANES, LANES), F32),
            pltpu.VMEM((tm, D), BF16),
            pltpu.VMEM((tm, D), F32),
            pltpu.VMEM((2, tm * SUBLANES, LANES), F32),
            pltpu.SemaphoreType.DMA((2,)),
            pltpu.SemaphoreType.DMA((2,)),
        ],
    )
    return pl.pallas_call(
        functools.partial(_experts_body, tm=tm, nf=nf, n_slots=n_slots),
        grid_spec=grid_spec,
        out_shape=jax.ShapeDtypeStruct(((n_slots + 2 * tm) * SUBLANES, LANES), F32),
        compiler_params=_cparams("arbitrary", "arbitrary"),
        name="moe_experts",
    )(tile_expert, n_valid, src_rows, src_rows, dst_rows, h, w1, w3, w2)


def _combine_body(x_ref, mod_ref, gw_ref, y0_ref, y1_ref, o_ref):
    n = x_ref.shape[0]
    gw = gw_ref[...]
    y = gw[:, 0:1] * _load_token_tiles(y0_ref, n) + gw[:, 1:2] * _load_token_tiles(y1_ref, n)
    o_ref[...] = x_ref[...] + mod_ref[0, ROW_GATE2:ROW_GATE2 + 1, :] * y


def moe_combine(x, mods, gw, y, row0, n_tok, seg_len, tm=TM_COMB):
    T, D = x.shape
    off = row0 // tm
    return pl.pallas_call(
        _combine_body,
        grid=(T // tm,),
        in_specs=[
            pl.BlockSpec((tm, D), lambda i: (i, 0)),
            pl.BlockSpec((1, MOD_ROWS, D), lambda i: (i * tm // seg_len, 0, 0)),
            pl.BlockSpec((tm, N_EXPERTS), lambda i: (off + i, 0)),
            pl.BlockSpec((tm * SUBLANES, LANES), lambda i: (off + i, 0)),
            pl.BlockSpec((tm * SUBLANES, LANES), lambda i: (off + n_tok // tm + i, 0)),
        ],
        out_specs=pl.BlockSpec((tm, D), lambda i: (i, 0)),
        out_shape=jax.ShapeDtypeStruct((T, D), F32),
        compiler_params=_cparams("parallel"),
        name="moe_combine",
    )(x, mods, gw, y, y)


def moe_dispatch_plan(idx, tm=TM_MOE):
    T = idx.shape[1]
    n_slots = TOP_K * T
    n_tiles = n_slots // tm + N_EXPERTS
    n_rows = n_tiles * tm
    experts = jnp.arange(N_EXPERTS, dtype=jnp.int32)[None, :]
    e_flat = idx[:TOP_K].reshape(n_slots)
    counts = jnp.sum((e_flat[:, None] == experts).astype(jnp.int32), axis=0)
    padded = (counts + tm - 1) // tm * tm
    ends = jnp.cumsum(padded)
    offs = ends - padded
    order = jnp.argsort(e_flat, stable=True).astype(jnp.int32)
    cstart = jnp.cumsum(counts) - counts
    rows = jnp.arange(n_rows, dtype=jnp.int32)
    row_e = jnp.minimum(jnp.sum((rows[:, None] >= ends[None, :]).astype(jnp.int32), axis=1), N_EXPERTS - 1)
    pick = lambda table: jnp.sum(jnp.where(row_e[:, None] == experts, table[None, :], 0), axis=1)
    j = rows - pick(offs)
    live = j < pick(counts)
    slot = order[jnp.clip(pick(cstart) + j, 0, n_slots - 1)]
    spare = n_slots + (rows // tm) % 2 * tm + rows % tm
    src_rows = jnp.where(live, slot % T, 0).astype(jnp.int32).reshape(n_tiles, 1, tm)
    dst_rows = jnp.where(live, slot, spare).astype(jnp.int32).reshape(n_tiles, 1, tm)
    tile_start = jnp.arange(n_tiles, dtype=jnp.int32) * tm
    n_valid = (ends[-1] // tm).astype(jnp.int32).reshape(1)
    tile_expert = jnp.minimum(jnp.sum((tile_start[:, None] >= ends[None, :]).astype(jnp.int32), axis=1),
                              N_EXPERTS - 1)
    last_e = tile_expert[jnp.maximum(n_valid[0] - 1, 0)]
    tile_expert = jnp.where(tile_start < ends[-1], tile_expert, last_e).astype(jnp.int32)
    return tile_expert, n_valid, src_rows, dst_rows


def ffn_moe(xa, xb, mods_a, mods_b, g, wr_t, w1, w3, w2, seg_len_b, tm=TM_PROJ, tme=TM_MOE, tmc=TM_COMB,
            tf=TF_MOE):
    na, nb = xa.shape[0], xb.shape[0]
    mods = jnp.concatenate([mods_a, mods_b], axis=0)
    h, idx, gw = moe_router(xa, xb, mods, g, wr_t, seg_len_b, tm)
    tile_expert, n_valid, src_rows, dst_rows = moe_dispatch_plan(idx, tme)
    y = moe_experts(h, tile_expert, n_valid, src_rows, dst_rows, w1, w3, w2, tme, tf)
    return (moe_combine(xa, mods_a, gw, y, 0, na + nb, na, tmc),
            moe_combine(xb, mods_b, gw, y, na, na + nb, seg_len_b, tmc))


def axial_rope(L):
    rows = L // GRID_W
    r = jnp.repeat(jnp.arange(rows, dtype=F32), GRID_W)
    col = jnp.tile(jnp.arange(GRID_W, dtype=F32), rows)
    n = HEAD_DIM // 4
    freqs = ROPE_BASE ** (-jnp.arange(n, dtype=F32) / n)
    ang = jnp.concatenate([r[:, None] * freqs, col[:, None] * freqs], axis=-1)
    return jnp.cos(ang), jnp.sin(ang)


def rope_tables(L):
    cos, sin = axial_rope(L)
    n = HEAD_DIM // 4
    cos_h = jnp.concatenate([cos[:, :n], cos[:, :n], cos[:, n:], cos[:, n:]], axis=1)
    sin_h = jnp.concatenate([-sin[:, :n], sin[:, :n], -sin[:, n:], sin[:, n:]], axis=1)
    reps = LANES // HEAD_DIM
    return jnp.tile(cos_h, (1, reps)), jnp.tile(sin_h, (1, reps))


def block_diag_ones(width, block):
    i = jnp.arange(width) // block
    return (i[:, None] == i[None, :]).astype(BF16)


def gla_gate_params(gw, gb):
    w = jnp.zeros((LANES, 2 * C_WIDTH), F32)
    w = w.at[:GLA_RANK, :C_WIDTH].set(gw[0]).at[GLA_RANK:2 * GLA_RANK, C_WIDTH:].set(gw[1])
    return w.astype(BF16), jnp.concatenate([gb[0], gb[1]])[None, :]


def gla_state_to_blockdiag(st):
    B = st.shape[0]
    out = jnp.zeros((B, 2, C_HEADS, C_DV, C_HEADS, C_DK), F32)
    for h in range(C_HEADS):
        out = out.at[:, :, h, :, h, :].set(jnp.swapaxes(st[:, :, h], -1, -2))
    return out.reshape(B, 2, C_WIDTH, C_WIDTH)


def gla_state_from_blockdiag(sT):
    B = sT.shape[0]
    s6 = sT.reshape(B, 2, C_HEADS, C_DV, C_HEADS, C_DK)
    return jnp.stack([jnp.swapaxes(s6[:, :, h, :, h, :], -1, -2) for h in range(C_HEADS)], axis=2)


def mod_table(cvec, w_ada, b_ada):
    m = jax.nn.silu(cvec) @ w_ada + b_ada
    m = m.reshape(cvec.shape[0], 6, D_MODEL)
    return jnp.pad(m, ((0, 0), (0, MOD_ROWS - 6), (0, 0)))


def kernel(x_prompt, x_sample, cache_k, cache_v, state_gla, c, c_ctx, norm1_g, norm2_g, w_ada, b_ada,
           w_in, w_out, q_norm_g, k_norm_g, attn_sink, hy_conv_w, hy_conv_b, hy_w1, hy_b1, hy_freq1,
           hy_w2, hy_b2, hy_freq2, hy_w3, hy_decay, hy_d, gla_gate_w, gla_gate_b, gla_norm_g,
           ffn_w1, ffn_w3, ffn_w2, moe_router, moe_w1, moe_w3, moe_w2):
    D = D_MODEL
    xp = x_prompt.reshape(BATCH * SEQ, D)
    xs = x_sample.reshape(DEC_BATCH * DEC_SEQ, D)
    streams = [
        dict(x=xp, B=BATCH, L=SEQ, seg_len=BATCH * SEQ, cvec=c_ctx[None, :], latent=False),
        dict(x=xs, B=DEC_BATCH, L=DEC_SEQ, seg_len=DEC_SEQ, cvec=c, latent=True),
    ]
    ks_list, vs_list, st_list = [], [], []
    bd_q = block_diag_ones(A_WIDTH, HEAD_DIM)
    dft = {s['L']: dft_half_matrices(s['L'], min(TM_DFT, s['L'] // 2)) for s in streams}
    for l in range(DEPTH):
        lp = {
            'hy_conv_w': hy_conv_w[l], 'hy_conv_b': hy_conv_b[l],
            'hy_w1': hy_w1[l], 'hy_b1': hy_b1[l], 'hy_freq1': hy_freq1[l], 'hy_w2': hy_w2[l],
            'hy_b2': hy_b2[l], 'hy_freq2': hy_freq2[l], 'hy_w3': hy_w3[l], 'hy_decay': hy_decay[l],
            'hy_d': hy_d[l], 'gla_gate_w': gla_gate_w[l], 'gla_gate_b': gla_gate_b[l],
            'gla_norm_g': gla_norm_g[l],
        }
        w_in_l = jnp.pad(w_in[l], ((0, 0), (0, D_PROJ_PAD - D_PROJ))).astype(BF16)
        w_out_l = w_out[l].astype(BF16)
        g1 = norm1_g[l][None, :]
        g2 = norm2_g[l][None, :]
        qg = jnp.tile(q_norm_g[l], A_HEADS)[None, :]
        kg = jnp.tile(k_norm_g[l], A_KV_HEADS)[None, :]
        gg = jnp.tile(gla_norm_g[l], C_HEADS)[None, :]
        gate_w, gate_b = gla_gate_params(gla_gate_w[l], gla_gate_b[l])
        j = l // 2
        for s in streams:
            B, L = s['B'], s['L']
            mods = mod_table(s['cvec'], w_ada[l], b_ada[l])
            rope = rope_tables(L) if s['latent'] else None
            q, k, v, hy, cq, ck, cv, cg, la = in_proj(s['x'], mods, g1, w_in_l, bd_q, qg, kg, gate_w, gate_b,
                                                      s['seg_len'], rope, L)
            seq = lambda t: t.reshape(B, L, t.shape[-1])
            if s['latent']:
                kc = cache_k[:, l].reshape(DEC_BATCH, PAST_LEN, LANES)
                vc = cache_v[:, l].reshape(DEC_BATCH, PAST_LEN, LANES)
                a_out = attention(seq(q), seq(k), seq(v), attn_sink[l], kc, vc)
                s0 = gla_state_to_blockdiag(state_gla[:, l])
                o_f, o_b, _ = gla(seq(cq), seq(ck), seq(cv), seq(la), s0)
            else:
                a_out = attention(seq(q), seq(k), seq(v), attn_sink[l])
                o_f, o_b, sT = gla(seq(cq), seq(ck), seq(cv), seq(la))
                ks_list.append(k.reshape(B, L, A_KV_HEADS, HEAD_DIM))
                vs_list.append(v.reshape(B, L, A_KV_HEADS, HEAD_DIM))
                st_list.append(gla_state_from_blockdiag(sT))
            h_out = hyena_half(seq(hy), lp, dft[L])
            flat = lambda t: t.reshape(B * L, t.shape[-1])
            x1 = out_proj(s['x'], flat(a_out), [flat(h) for h in h_out], flat(o_f), flat(o_b), cg, mods, w_out_l,
                          bd_q[:C_WIDTH, :C_WIDTH], gg, s['seg_len'])
            if l % 2 == 0:
                s['x'] = ffn_dense(x1, mods, g2, ffn_w1[j].astype(BF16), ffn_w3[j].astype(BF16),
                                   ffn_w2[j].astype(BF16), s['seg_len'])
            else:
                s['x'], s['mods'] = x1, mods
        if l % 2 == 1:
            sa, sb = streams
            sa['x'], sb['x'] = ffn_moe(sa['x'], sb['x'], sa['mods'], sb['mods'], g2, moe_router[j].T,
                                       moe_w1[j].astype(BF16), moe_w3[j].astype(BF16), moe_w2[j].astype(BF16),
                                       sb['seg_len'])
    y_prompt = streams[0]['x'].reshape(BATCH, SEQ, D)
    y_sample = streams[1]['x'].reshape(DEC_BATCH, DEC_SEQ, D)
    new_cache_k = jnp.stack(ks_list, axis=1)
    new_cache_v = jnp.stack(vs_list, axis=1)
    new_state_gla = jnp.stack(st_list, axis=1)
    return (y_prompt, y_sample, new_cache_k, new_cache_v, new_state_gla)
```

```python
import math
import functools
import jax
import jax.numpy as jnp
from jax import lax
import numpy as np
from jax.experimental import pallas as pl
from jax.experimental.pallas import tpu as pltpu

D_MODEL = 1024
BATCH = 32
SEQ = 256
DEPTH = 2
DEC_BATCH = 8
DEC_SEQ = 4096
PAST_LEN = 512

GRID_W = 64
HEAD_DIM = 64
D_MIX = D_MODEL
A_HEADS = 8
A_KV_HEADS = 2
A_GROUP = A_HEADS // A_KV_HEADS
A_WIDTH = A_HEADS * HEAD_DIM
WINDOW = 128
BLK = 128
ROPE_BASE = 10000.0
HY_CH = 256
HY_ORDER = 2
HY_BANDS = 16
HY_EMB = 1 + 2 * HY_BANDS
HY_HID = 64
C_HEADS = 4
C_DK = 64
C_DV = 64
C_WIDTH = C_HEADS * C_DV
GLA_RANK = 16
GLA_TAU = 16.0
GLA_CHUNK = 64
D_FF = 2816
N_EXPERTS = 8
TOP_K = 2
D_FF_EXPERT = 3584

PROJ_SIZES = (A_WIDTH, A_KV_HEADS * HEAD_DIM, A_KV_HEADS * HEAD_DIM, 3 * HY_CH,
              C_HEADS * C_DK, C_HEADS * C_DK, C_WIDTH, C_WIDTH, 2 * GLA_RANK)
PROJ_SPLITS = tuple(int(s) for s in np.cumsum(PROJ_SIZES)[:-1])
D_PROJ = int(sum(PROJ_SIZES))

F32 = jnp.float32
BF16 = jnp.bfloat16
ATT_SCALE = HEAD_DIM ** -0.5
NEG_INF = -1e30
EPS = 1e-6

LANES = 128
SUBLANES = 8
D_PROJ_PAD = -(-D_PROJ // LANES) * LANES
VMEM_LIMIT_BYTES = 56 * 1024 * 1024

MOD_ROWS = SUBLANES
ROW_SHIFT1, ROW_SCALE1, ROW_GATE1, ROW_SHIFT2, ROW_SCALE2, ROW_GATE2 = range(6)

TM_PROJ = 512
TM_MOE = 1024
TF_MOE = 896
TM_COMB = 512


def _cparams(*sem):
    return pltpu.CompilerParams(dimension_semantics=sem, vmem_limit_bytes=VMEM_LIMIT_BYTES)


def _adaln_rows(x, g, shift, scale):
    ms = jnp.mean(x * x, axis=-1, keepdims=True)
    return (x * lax.rsqrt(ms + EPS) * g) * (1.0 + scale) + shift


def _silu(a):
    return a * jax.nn.sigmoid(a)


def _group_mean_sq(x, ones_bd):
    sq = x * x
    hi = sq.astype(BF16)
    lo = (sq - hi.astype(F32)).astype(BF16)
    s = jnp.dot(hi, ones_bd, preferred_element_type=F32) + jnp.dot(lo, ones_bd, preferred_element_type=F32)
    return s * (1.0 / HEAD_DIM)


def _rope_rows(x, cos_t, sin_t):
    q4 = HEAD_DIM // 4
    lane = lax.broadcasted_iota(jnp.int32, x.shape, 1)
    partner = jnp.where((lane % (2 * q4)) < q4, pltpu.roll(x, LANES - q4, 1), pltpu.roll(x, q4, 1))
    return x * cos_t + partner * sin_t


def _log_sigmoid(x):
    return jnp.minimum(x, 0.0) - jnp.log(1.0 + jnp.exp(-jnp.abs(x)))


def _inproj_body(*refs, latent):
    if latent:
        (x_ref, mod_ref, g_ref, w_ref, bd_ref, qg_ref, kg_ref, gw_ref, gb_ref, cos_ref, sin_ref,
         q_ref, k_ref, v_ref, hy_ref, cq_ref, ck_ref, cv_ref, cg_ref, la_ref) = refs
    else:
        (x_ref, mod_ref, g_ref, w_ref, bd_ref, qg_ref, kg_ref, gw_ref, gb_ref,
         q_ref, k_ref, v_ref, hy_ref, cq_ref, ck_ref, cv_ref, cg_ref, la_ref) = refs
    h = _adaln_rows(x_ref[...], g_ref[...], mod_ref[0, ROW_SHIFT1:ROW_SHIFT1 + 1, :],
                    mod_ref[0, ROW_SCALE1:ROW_SCALE1 + 1, :])
    acc = jnp.dot(h.astype(BF16), w_ref[...], preferred_element_type=F32)
    o = 0
    q = acc[:, o:o + A_WIDTH]
    o += A_WIDTH
    k = acc[:, o:o + LANES]
    o += LANES
    v_ref[...] = acc[:, o:o + LANES]
    o += LANES
    hy_ref[...] = acc[:, o:o + 3 * HY_CH]
    o += 3 * HY_CH
    cq_ref[...] = acc[:, o:o + C_WIDTH] * (C_DK ** -0.5)
    o += C_WIDTH
    ck_ref[...] = acc[:, o:o + C_WIDTH]
    o += C_WIDTH
    cv_ref[...] = acc[:, o:o + C_WIDTH]
    o += C_WIDTH
    cg_ref[...] = acc[:, o:o + C_WIDTH]
    o += C_WIDTH
    r = acc[:, o:o + LANES]
    la_ref[...] = _log_sigmoid(jnp.dot(r.astype(BF16), gw_ref[...], preferred_element_type=F32)
                               + gb_ref[...]) * (1.0 / GLA_TAU)
    q = q * lax.rsqrt(_group_mean_sq(q, bd_ref[...]) + EPS) * qg_ref[...]
    k = k * lax.rsqrt(_group_mean_sq(k, bd_ref[0:LANES, 0:LANES]) + EPS) * kg_ref[...]
    if latent:
        cos_t = cos_ref[...]
        sin_t = sin_ref[...]
        q = jnp.concatenate([_rope_rows(q[:, j * LANES:(j + 1) * LANES], cos_t, sin_t)
                             for j in range(A_WIDTH // LANES)], axis=1)
        k = _rope_rows(k, cos_t, sin_t)
    q_ref[...] = (q * ATT_SCALE).astype(BF16)
    k_ref[...] = k


def in_proj(x, mods, g, w, bd, qg, kg, gw, gb, seg_len, rope=None, seq_len=None, tm=TM_PROJ):
    T, D = x.shape
    N = w.shape[1]
    latent = rope is not None
    const = lambda shape: pl.BlockSpec(shape, lambda i: (0,) * len(shape))
    in_specs = [
        pl.BlockSpec((tm, D), lambda i: (i, 0)),
        pl.BlockSpec((1, MOD_ROWS, D), lambda i: (i * tm // seg_len, 0, 0)),
        const((1, D)), const((D, N)), const((A_WIDTH, A_WIDTH)), const((1, A_WIDTH)), const((1, LANES)),
        const((LANES, 2 * C_WIDTH)), const((1, 2 * C_WIDTH)),
    ]
    args = [x, mods, g, w, bd, qg, kg, gw, gb]
    if latent:
        tiles_per_seq = seq_len // tm
        in_specs += [pl.BlockSpec((tm, LANES), lambda i: (i % tiles_per_seq, 0))] * 2
        args += list(rope)
    widths = [A_WIDTH, LANES, LANES, 3 * HY_CH, C_WIDTH, C_WIDTH, C_WIDTH, C_WIDTH, 2 * C_WIDTH]
    dtypes = [BF16] + [F32] * 8
    return pl.pallas_call(
        functools.partial(_inproj_body, latent=latent),
        grid=(T // tm,),
        in_specs=in_specs,
        out_specs=[pl.BlockSpec((tm, wd), lambda i: (i, 0)) for wd in widths],
        out_shape=[jax.ShapeDtypeStruct((T, wd), dt) for wd, dt in zip(widths, dtypes)],
        compiler_params=_cparams("parallel"),
        name="in_proj_latent" if latent else "in_proj_context",
    )(*args)


def _outproj_body(x_ref, a_ref, h0_ref, h1_ref, of_ref, ob_ref, cg_ref, mod_ref, w_ref, bd_ref, gg_ref, o_ref):
    o = of_ref[...] + ob_ref[...]
    g_out = o * lax.rsqrt(_group_mean_sq(o, bd_ref[...]) + EPS) * gg_ref[...] * _silu(cg_ref[...])
    h0 = A_WIDTH
    g0 = A_WIDTH + HY_CH
    mix = jnp.dot(a_ref[...], w_ref[0:h0, :], preferred_element_type=F32)
    h = jnp.concatenate([h0_ref[...], h1_ref[...]], axis=1)
    mix += jnp.dot(h.astype(BF16), w_ref[h0:g0, :], preferred_element_type=F32)
    mix += jnp.dot(g_out.astype(BF16), w_ref[g0:, :], preferred_element_type=F32)
    o_ref[...] = x_ref[...] + mod_ref[0, ROW_GATE1:ROW_GATE1 + 1, :] * mix


def out_proj(x, a, h, of, ob, cg, mods, w, bd, gg, seg_len, tm=TM_PROJ):
    T, D = x.shape
    row = lambda wd: pl.BlockSpec((tm, wd), lambda i: (i, 0))
    const = lambda shape: pl.BlockSpec(shape, lambda i: (0,) * len(shape))
    return pl.pallas_call(
        _outproj_body,
        grid=(T // tm,),
        in_specs=[
            row(D), row(A_WIDTH), row(LANES), row(LANES), row(C_WIDTH), row(C_WIDTH), row(C_WIDTH),
            pl.BlockSpec((1, MOD_ROWS, D), lambda i: (i * tm // seg_len, 0, 0)),
            const((D, D)), const((C_WIDTH, C_WIDTH)), const((1, C_WIDTH)),
        ],
        out_specs=row(D),
        out_shape=jax.ShapeDtypeStruct((T, D), F32),
        compiler_params=_cparams("parallel"),
        name="out_proj",
    )(x, a, h[0], h[1], of, ob, cg, mods, w, bd, gg)


TQ_ATT = 256
HEADS_PER_STACK = 2
KWIN_ATT = TQ_ATT + 2 * WINDOW


def _dup_heads(x, g):
    lane = lax.broadcasted_iota(jnp.int32, x.shape, 1)
    rolled = pltpu.roll(x, HEAD_DIM, 1)
    keep = (lane < HEAD_DIM) if g == 0 else (lane >= HEAD_DIM)
    return jnp.where(keep, x, rolled)


def _attn_body(*refs, latent, seq_len):
    if latent:
        sink_ref, q_ref, k_ref, v_ref, kc_ref, vc_ref, o_ref = refs
    else:
        sink_ref, q_ref, k_ref, v_ref, o_ref = refs
    tq = q_ref.shape[1]
    i = pl.program_id(1)
    q = q_ref[0]
    lane = lax.broadcasted_iota(jnp.int32, (tq, LANES), 1)
    low = lane < HEAD_DIM
    if latent:
        start = jnp.clip(i * tq - WINDOW, 0, seq_len - KWIN_ATT)
        start = pl.multiple_of(start, WINDOW)
        kl = k_ref[0, pl.ds(start, KWIN_ATT), :]
        vl = v_ref[0, pl.ds(start, KWIN_ATT), :]
        qpos = i * tq + lax.broadcasted_iota(jnp.int32, (tq, KWIN_ATT), 0)
        kpos = start + lax.broadcasted_iota(jnp.int32, (tq, KWIN_ATT), 1)
        bias = jnp.where(jnp.abs(qpos - kpos) <= WINDOW, 0.0, NEG_INF).astype(F32)
        bias = jnp.concatenate([bias] * A_GROUP, axis=0)
        kc = kc_ref[0]
        vc = vc_ref[0]
    else:
        kl = k_ref[0]
        vl = v_ref[0]
    nt = (((1,), (1,)), ((), ()))
    zero = jnp.zeros_like(q[:, :LANES])
    outs = []
    for g in range(A_KV_HEADS):
        kl_g = _dup_heads(kl, g).astype(BF16)
        vl_g = _dup_heads(vl, g).astype(BF16)
        if latent:
            kc_g = _dup_heads(kc, g).astype(BF16)
            vc_g = _dup_heads(vc, g).astype(BF16)
        for j in range(HEADS_PER_STACK // 2 - 1, A_GROUP // 2, HEADS_PER_STACK // 2):
            heads = range(g * A_GROUP + 2 * j + 2 - HEADS_PER_STACK, g * A_GROUP + 2 * j + 2)
            qs = jnp.concatenate([jnp.where(low if h % 2 == 0 else jnp.logical_not(low),
                                            q[:, (h // 2) * LANES:(h // 2 + 1) * LANES], zero) for h in heads], axis=0)
            sink = jnp.concatenate([jnp.full((tq, 1), sink_ref[h], F32) for h in heads], axis=0)
            s_loc = lax.dot_general(qs, kl_g, nt, preferred_element_type=F32)
            m = sink
            if latent:
                s_loc = s_loc + bias[:len(heads) * tq]
                s_ctx = lax.dot_general(qs, kc_g, nt, preferred_element_type=F32)
                m = jnp.maximum(m, jnp.max(s_ctx, axis=-1, keepdims=True))
            m = jnp.maximum(m, jnp.max(s_loc, axis=-1, keepdims=True))
            p_loc = jnp.exp(s_loc - m)
            den = jnp.exp(sink - m) + jnp.sum(p_loc, axis=-1, keepdims=True)
            acc = jnp.dot(p_loc.astype(BF16), vl_g, preferred_element_type=F32)
            if latent:
                p_ctx = jnp.exp(s_ctx - m)
                den = den + jnp.sum(p_ctx, axis=-1, keepdims=True)
                acc = acc + jnp.dot(p_ctx.astype(BF16), vc_g, preferred_element_type=F32)
            og = acc / den
            for jj in range(len(heads) // 2):
                outs.append(jnp.where(low, og[(2 * jj) * tq:(2 * jj + 1) * tq],
                                      og[(2 * jj + 1) * tq:(2 * jj + 2) * tq]))
    o_ref[0] = jnp.concatenate(outs, axis=1).astype(o_ref.dtype)


def attention(q, k, v, sink, kc=None, vc=None, tq=TQ_ATT):
    B, L, _ = q.shape
    latent = kc is not None
    tq = min(tq, L)
    seq = lambda wd: pl.BlockSpec((1, L, wd), lambda b, i: (b, 0, 0))
    in_specs = [
        pl.BlockSpec(memory_space=pltpu.SMEM),
        pl.BlockSpec((1, tq, A_WIDTH), lambda b, i: (b, i, 0)),
        seq(LANES), seq(LANES),
    ]
    args = [sink, q, k, v]
    if latent:
        P = kc.shape[1]
        in_specs += [pl.BlockSpec((1, P, LANES), lambda b, i: (b, 0, 0))] * 2
        args += [kc, vc]
    return pl.pallas_call(
        functools.partial(_attn_body, latent=latent, seq_len=L),
        grid=(B, L // tq),
        in_specs=in_specs,
        out_specs=pl.BlockSpec((1, tq, A_WIDTH), lambda b, i: (b, i, 0)),
        out_shape=jax.ShapeDtypeStruct((B, L, A_WIDTH), BF16),
        compiler_params=_cparams("parallel", "arbitrary"),
        name="attention_latent" if latent else "attention_context",
    )(*args)


def _split3(x):
    hi = x.astype(BF16)
    r = x - hi.astype(F32)
    mid = r.astype(BF16)
    lo = (r - mid.astype(F32)).astype(BF16)
    return hi, mid, lo


def _gla_group(q_ref, k_ref, v_ref, la_ref, o_ref, st_ref, b, d, reverse, n_chunks):
    C = GLA_CHUNK
    W = C_WIDTH
    ti = lax.broadcasted_iota(jnp.int32, (C, C), 0)
    si = lax.broadcasted_iota(jnp.int32, (C, C), 1)
    tri = (si >= ti) if reverse else (si <= ti)
    tri_b = tri.astype(BF16)
    tri4 = jnp.concatenate([tri] * C_HEADS, axis=0)
    r4 = lax.broadcasted_iota(jnp.int32, (C_HEADS * C, W), 0) // C
    c4 = lax.broadcasted_iota(jnp.int32, (C_HEADS * C, W), 1) // C_DK
    same_head = r4 == c4
    nt = (((1,), (1,)), ((), ()))
    tn = (((0,), (0,)), ((), ()))
    chunks = range(n_chunks)
    rows = [pl.ds(c * C, C) for c in chunks]
    vbs = [v_ref[b, r, :].astype(BF16) for r in rows]
    parts = [_split3(la_ref[b, r, :]) for r in rows]
    bsums = [jnp.dot(tri_b, hi, preferred_element_type=F32) + jnp.dot(tri_b, mid, preferred_element_type=F32)
             + jnp.dot(tri_b, lo, preferred_element_type=F32) for hi, mid, lo in parts]
    b_lasts = [s[0:1] if reverse else s[C - 1:C] for s in bsums]
    qgs = [q_ref[b, r, :] * jnp.exp(s) for r, s in zip(rows, bsums)]
    kgs = [(k_ref[b, r, :] * jnp.exp(-s)).astype(BF16) for r, s in zip(rows, bsums)]
    kds = [(k_ref[b, r, :] * jnp.exp(bl - s)).astype(BF16) for r, s, bl in zip(rows, bsums, b_lasts)]
    decays = [jnp.exp(bl) for bl in b_lasts]
    q_bds = [jnp.where(same_head, jnp.concatenate([qg] * C_HEADS, axis=0), 0.0).astype(BF16) for qg in qgs]
    uts = [jnp.where(same_head, lax.dot_general(vb, kd, tn, preferred_element_type=F32), 0.0)
           for vb, kd in zip(vbs, kds)]
    a_s = [jnp.where(tri4, lax.dot_general(qb, kg, nt, preferred_element_type=F32), 0.0).astype(BF16)
           for qb, kg in zip(q_bds, kgs)]
    r_s = [jnp.where(same_head, jnp.dot(a, vb, preferred_element_type=F32), 0.0)
           for a, vb in zip(a_s, vbs)]
    o_intra = [sum([r[h * C:(h + 1) * C] for h in range(1, C_HEADS)], r[0:C]) for r in r_s]
    qgb = [qg.astype(BF16) for qg in qgs]
    st = st_ref[d]
    for c in (reversed(chunks) if reverse else chunks):
        o_ref[b, rows[c], :] = o_intra[c] + lax.dot_general(qgb[c], st.astype(BF16), nt,
                                                            preferred_element_type=F32)
        st = st * decays[c] + uts[c]
    st_ref[d] = st


def _gla_body(*refs, has_state, n_chunks):
    if has_state:
        (qf, kf, vf, lf, qb, kb, vb, lb, s0_ref, of_ref, ob_ref, sT_ref, st) = refs
    else:
        (qf, kf, vf, lf, qb, kb, vb, lb, of_ref, ob_ref, sT_ref, st) = refs
    j = pl.program_id(1)

    nb = qf.shape[0]

    @pl.when(j == 0)
    def _init():
        if has_state:
            st[...] = s0_ref[...].reshape(st.shape)
        else:
            st[...] = jnp.zeros_like(st)

    for b in range(nb):
        _gla_group(qf, kf, vf, lf, of_ref, st, b, 2 * b, False, n_chunks)
        _gla_group(qb, kb, vb, lb, ob_ref, st, b, 2 * b + 1, True, n_chunks)

    @pl.when(j == pl.num_programs(1) - 1)
    def _final():
        sT_ref[...] = st[...].reshape(sT_ref.shape)


NB_GLA = 1


def gla(cq, ck, cv, la, s0=None, rows=512, nb=NB_GLA):
    B, L, W = cq.shape
    assert B % nb == 0
    rows = min(rows, L)
    ng = L // rows
    has_state = s0 is not None
    fwd = lambda: pl.BlockSpec((nb, rows, W), lambda b, j: (b, j, 0))
    bwd = lambda: pl.BlockSpec((nb, rows, W), lambda b, j: (b, ng - 1 - j, 0))
    state = lambda: pl.BlockSpec((nb, 2, W, W), lambda b, j: (b, 0, 0, 0))
    in_specs = [fwd(), fwd(), fwd(), pl.BlockSpec((nb, rows, W), lambda b, j: (b, j, 0)),
                bwd(), bwd(), bwd(), pl.BlockSpec((nb, rows, W), lambda b, j: (b, ng - 1 - j, 1))]
    args = [cq, ck, cv, la, cq, ck, cv, la]
    if has_state:
        in_specs.append(state())
        args.append(s0)
    return pl.pallas_call(
        functools.partial(_gla_body, has_state=has_state, n_chunks=rows // GLA_CHUNK),
        grid=(B // nb, ng),
        in_specs=in_specs,
        out_specs=[fwd(), bwd(), state()],
        out_shape=[jax.ShapeDtypeStruct((B, L, W), F32), jax.ShapeDtypeStruct((B, L, W), F32),
                   jax.ShapeDtypeStruct((B, 2, W, W), F32)],
        scratch_shapes=[pltpu.VMEM((2 * nb, W, W), F32)],
        compiler_params=_cparams("parallel", "arbitrary"),
        name="gla",
    )(*args)


TM_DFT = 512
BG_HALF = 2


def _freq_weight(i, tm, n):
    k = i * tm + lax.broadcasted_iota(jnp.int32, (tm, 1), 0)
    return jnp.where(k == 0, 1.0 / n, 2.0 / n).astype(F32)


def hyena_time_filters(L, lp):
    t = jnp.arange(L, dtype=F32)
    t_norm = t / max(L - 1, 1)
    w = (2.0 * math.pi / L) * t
    f = jnp.linspace(1e-4, HY_BANDS - 1, HY_BANDS, dtype=F32)
    fw = w[:, None] * f[None, :]
    feat = jnp.concatenate([t_norm[:, None], jnp.cos(fw), -jnp.sin(fw)], axis=-1)
    z = jnp.sin(lp['hy_freq1'] * (feat @ lp['hy_w1'] + lp['hy_b1']))
    z = jnp.sin(lp['hy_freq2'] * (z @ lp['hy_w2'] + lp['hy_b2']))
    hf = (z @ lp['hy_w3']).astype(F32).reshape(L, 2, HY_ORDER, HY_CH)
    hf = hf * jnp.exp(-t_norm[:, None, None, None] * jnp.abs(lp['hy_decay'].astype(F32)))
    return hf / (jnp.sum(jnp.abs(hf), axis=(0, 1), keepdims=True) + EPS)


def _hi_lo(x):
    hi = x.astype(BF16)
    return jnp.stack([hi, (x - hi.astype(F32)).astype(BF16)])


def _shift_rows(x, down):
    n = x.shape[0]
    row = lax.broadcasted_iota(jnp.int32, (n, 1), 0)
    if down:
        return jnp.where(row == 0, 0.0, pltpu.roll(x, 1, 0))
    return jnp.where(row == n - 1, 0.0, pltpu.roll(x, n - 1, 0))


def _conv3_planes_body(x_ref, w_ref, b_ref, u_ref, ub_ref):
    H = x_ref.shape[0] // 2
    e = x_ref[pl.ds(0, H, stride=2), :]
    o = x_ref[pl.ds(1, H, stride=2), :]
    w0, w1, w2 = w_ref[0:1, :], w_ref[1:2, :], w_ref[2:3, :]
    ue = _shift_rows(o, True) * w0 + e * w1 + o * w2 + b_ref[...]
    uo = e * w0 + o * w1 + _shift_rows(e, False) * w2 + b_ref[...]
    u_ref[0, 0] = ue
    u_ref[0, 1] = uo
    ub_ref[0, 0] = ue.astype(BF16)
    ub_ref[0, 1] = uo.astype(BF16)


def hyena_conv3_planes(hy, w, b):
    B, L, C3 = hy.shape
    H = L // 2
    out = pl.BlockSpec((1, 2, H, LANES), lambda b_, j: (b_, 0, 0, j))
    return pl.pallas_call(
        _conv3_planes_body,
        grid=(B, C3 // LANES),
        in_specs=[pl.BlockSpec((None, L, LANES), lambda b_, j: (b_, 0, j)),
                  pl.BlockSpec((SUBLANES, LANES), lambda b_, j: (0, j)),
                  pl.BlockSpec((1, LANES), lambda b_, j: (0, j))],
        out_specs=[out, out],
        out_shape=[jax.ShapeDtypeStruct((B, 2, H, C3), F32), jax.ShapeDtypeStruct((B, 2, H, C3), BF16)],
        compiler_params=_cparams("parallel", "parallel"),
        name="hyena_conv3",
    )(hy, w, b)


def _half_filt_body(fe_ref, fo_ref, sg_ref, hse_ref, hso_ref, hde_ref, hdo_ref,
                    lr_ref, li_ref, hr_ref, hi_ref, sp_ref, *, n):
    i = pl.program_id(0)
    tm = lr_ref.shape[0]
    wk = _freq_weight(i, tm, n)

    def two_pass(m, parts_ref):
        return (jnp.dot(m, parts_ref[0], preferred_element_type=F32)
                + jnp.dot(m, parts_ref[1], preferred_element_type=F32))
    a = two_pass(fe_ref[0, :tm], hse_ref)
    b = two_pass(fo_ref[0, :tm], hso_ref)
    c = two_pass(fe_ref[0, tm:], hde_ref)
    d = two_pass(fo_ref[0, tm:], hdo_ref)
    lr_ref[...] = wk * (a + b)
    hr_ref[...] = wk * (a - b)
    li_ref[...] = wk * (c + d)
    hi_ref[...] = wk * (d - c)

    @pl.when(i == 0)
    def _middle_bin():
        sr = two_pass(sg_ref[...], hse_ref)[0:1, :]
        si = -two_pass(sg_ref[...], hdo_ref)[0:1, :]
        row = lax.broadcasted_iota(jnp.int32, sp_ref.shape, 0)
        sp_ref[...] = jnp.where(row == 0, sr, jnp.where(row == 1, si, 0.0)) * (2.0 / n)


def hyena_half_filter_spectrum(fe, fo, sg, hs, hd, tm):
    H = fe.shape[2]
    W = hs.shape[2]
    planes = lambda h: (h.reshape(2, H, 2, W)[:, :, 0], h.reshape(2, H, 2, W)[:, :, 1])
    hse, hso = planes(hs)
    hde, hdo = planes(hd)
    whole = lambda shape: pl.BlockSpec(shape, lambda i: (0,) * len(shape))
    ftile = pl.BlockSpec((1, 2 * tm, H), lambda i: (i, 0, 0))
    otile = pl.BlockSpec((tm, W), lambda i: (i, 0))
    out = jax.ShapeDtypeStruct((H, W), F32)
    return pl.pallas_call(
        functools.partial(_half_filt_body, n=4 * H),
        grid=(H // tm,),
        in_specs=[ftile, ftile, whole((SUBLANES, H))] + [whole((2, H, W))] * 4,
        out_specs=[otile, otile, otile, otile, whole((SUBLANES, W))],
        out_shape=[out, out, out, out, jax.ShapeDtypeStruct((SUBLANES, W), F32)],
        compiler_params=_cparams("arbitrary"),
        name="hyena_filter_spectrum",
    )(fe, fo, sg, hse, hso, hde, hdo)


def _cmul(xr, xi, hr, hi):
    return xr * hr - xi * hi, xr * hi + xi * hr


def _half_fwd_body(fe_ref, fo_ref, sg_ref, ze_ref, zo_ref, lr_ref, li_ref, hr_ref, hi_ref, sp_ref,
                   per_ref, pei_ref, por_ref, poi_ref, ysp_ref):
    nb = ze_ref.shape[0]
    tm = lr_ref.shape[0]

    def body(b, carry):
        a = jnp.dot(fe_ref[0], ze_ref[b], preferred_element_type=F32)
        o = jnp.dot(fo_ref[0], zo_ref[b], preferred_element_type=F32)
        ylr, yli = _cmul(a[:tm] + o[:tm], a[tm:] + o[tm:], lr_ref[...], li_ref[...])
        yhr, yhi = _cmul(a[:tm] - o[:tm], o[tm:] - a[tm:], hr_ref[...], hi_ref[...])
        per_ref[b] = (ylr + yhr).astype(BF16)
        pei_ref[b] = (yli - yhi).astype(BF16)
        por_ref[b] = (ylr - yhr).astype(BF16)
        poi_ref[b] = (yli + yhi).astype(BF16)
        return carry
    lax.fori_loop(0, nb, body, 0)

    @pl.when(pl.program_id(1) == 0)
    def _middle_bin():
        def mid(b, carry):
            xr = jnp.dot(sg_ref[...], ze_ref[b], preferred_element_type=F32)[0:1, :]
            xi = -jnp.dot(sg_ref[...], zo_ref[b], preferred_element_type=F32)[0:1, :]
            yr, yi = _cmul(xr, xi, sp_ref[0:1, :], sp_ref[1:2, :])
            row = lax.broadcasted_iota(jnp.int32, ysp_ref.shape[1:], 0)
            ysp_ref[b] = jnp.where(row == 0, yr, jnp.where(row == 1, yi, 0.0))
            return carry
        lax.fori_loop(0, nb, mid, 0)


def hyena_half_fwd(fe, fo, sg, zb, zcol, spec, order, tm, bg):
    B, _, H, _ = zb.shape
    C = HY_CH
    lr, li, hr, hi, sp = spec
    ftile = pl.BlockSpec((1, 2 * tm, H), lambda g, i: (i, 0, 0))
    plane = lambda p: pl.BlockSpec((bg, None, H, C), lambda g, i: (g, p, 0, zcol))
    stile = pl.BlockSpec((tm, C), lambda g, i: (i, order))
    ptile = pl.BlockSpec((bg, tm, C), lambda g, i: (g, i, 0))
    pshape = jax.ShapeDtypeStruct((B, H, C), BF16)
    return pl.pallas_call(
        _half_fwd_body,
        grid=(B // bg, H // tm),
        in_specs=[ftile, ftile, pl.BlockSpec((SUBLANES, H), lambda g, i: (0, 0)), plane(0), plane(1),
                  stile, stile, stile, stile, pl.BlockSpec((SUBLANES, C), lambda g, i: (0, order))],
        out_specs=[ptile, ptile, ptile, ptile, pl.BlockSpec((bg, SUBLANES, C), lambda g, i: (g, 0, 0))],
        out_shape=[pshape, pshape, pshape, pshape, jax.ShapeDtypeStruct((B, SUBLANES, C), F32)],
        compiler_params=_cparams("parallel", "arbitrary"),
        name="hyena_dft_fwd",
    )(fe, fo, sg, zb, zb, lr, li, hr, hi, sp)


def _half_inv_body(fer_ref, fei_ref, fotr_ref, foti_ref, per_ref, pei_ref, por_ref, poi_ref, ysp_ref,
                   ze_ref, zo_ref, ge_ref, go_ref, d_ref, *out_refs, natural):
    nb = ze_ref.shape[0]
    tm = fer_ref.shape[0]
    u = pl.program_id(1) * tm + lax.broadcasted_iota(jnp.int32, (tm, 1), 0)
    sign = jnp.where((u & 1) == 0, 1.0, -1.0).astype(F32)
    d = d_ref[...]

    def body(b, carry):
        ce = (jnp.dot(fer_ref[...], per_ref[b], preferred_element_type=F32)
              + jnp.dot(fei_ref[...], pei_ref[b], preferred_element_type=F32) + sign * ysp_ref[b][0:1, :])
        co = (jnp.dot(fotr_ref[...], por_ref[b], preferred_element_type=F32)
              + jnp.dot(foti_ref[...], poi_ref[b], preferred_element_type=F32) - sign * ysp_ref[b][1:2, :])
        zne = ge_ref[b] * (ce + d * ze_ref[b])
        zno = go_ref[b] * (co + d * zo_ref[b])
        if natural:
            bi = jnp.asarray(b, jnp.int32)
            for c, o_ref in enumerate(out_refs):
                o_ref.at[bi][pl.ds(0, tm, stride=2), :] = zne[:, c * LANES:(c + 1) * LANES]
                o_ref.at[bi][pl.ds(1, tm, stride=2), :] = zno[:, c * LANES:(c + 1) * LANES]
        else:
            z_ref, zb_ref = out_refs
            z_ref[b, 0] = zne
            z_ref[b, 1] = zno
            zb_ref[b, 0] = zne.astype(BF16)
            zb_ref[b, 1] = zno.astype(BF16)
        return carry
    lax.fori_loop(0, nb, body, 0)


def hyena_half_inv(mats, ps, ysp, z, zcol, gate, gcol, d, tm, bg, natural=False):
    fer, fei, fotr, foti = mats
    per = ps[0]
    B, H, C = per.shape
    ftile = pl.BlockSpec((tm, H), lambda g, i: (i, 0))
    whole_p = lambda: pl.BlockSpec((bg, H, C), lambda g, i: (g, 0, 0), pipeline_mode=pl.Buffered(1))
    tile = lambda p, col: pl.BlockSpec((bg, None, tm, C), lambda g, i: (g, p, i, col))
    if natural:
        out_specs = [pl.BlockSpec((bg, 2 * tm, LANES), lambda g, i: (g, i, 0))] * (C // LANES)
        out_shape = [jax.ShapeDtypeStruct((B, 2 * H, LANES), F32)] * (C // LANES)
    else:
        out_specs = [pl.BlockSpec((bg, 2, tm, C), lambda g, i: (g, 0, i, 0))] * 2
        out_shape = [jax.ShapeDtypeStruct((B, 2, H, C), F32), jax.ShapeDtypeStruct((B, 2, H, C), BF16)]
    return pl.pallas_call(
        functools.partial(_half_inv_body, natural=natural),
        grid=(B // bg, H // tm),
        in_specs=[ftile, ftile, ftile, ftile, whole_p(), whole_p(), whole_p(), whole_p(),
                  pl.BlockSpec((bg, SUBLANES, C), lambda g, i: (g, 0, 0)),
                  tile(0, zcol), tile(1, zcol), tile(0, gcol), tile(1, gcol),
                  pl.BlockSpec((1, C), lambda g, i: (0, 0))],
        out_specs=out_specs,
        out_shape=out_shape,
        compiler_params=_cparams("parallel", "arbitrary"),
        name="hyena_dft_inv",
    )(fer, fei, fotr, foti, *ps, ysp, z, z, gate, gate, d)


def dft_half_matrices(L, tm):
    H = L // 2
    r = 1 << (max(H.bit_length() - 1, 0) // 2)
    u = jnp.arange(H, dtype=jnp.int32)

    hi = jnp.arange(H // r, dtype=jnp.int32) * r
    lo = jnp.arange(r, dtype=jnp.int32)

    def cos_msin(row_hi, row_lo, col):
        def table(k):
            ang = ((k[:, None] * col[None, :]) % (2 * L)).astype(F32) * (math.pi / L)
            return jnp.cos(ang), jnp.sin(ang)
        ca, sa = table(row_hi)
        cb, sb = table(row_lo)
        c = ca[:, None, :] * cb[None, :, :] - sa[:, None, :] * sb[None, :, :]
        s = sa[:, None, :] * cb[None, :, :] + ca[:, None, :] * sb[None, :, :]
        return c.reshape(H, H).astype(BF16), (-s).reshape(H, H).astype(BF16)
    fre, fie = cos_msin(hi, lo, 2 * u)
    fro, fio = cos_msin(hi, lo, 2 * u + 1)
    frot, fiot = cos_msin(2 * hi, 2 * lo + 1, u)
    stack = lambda a, b: jnp.concatenate([a.reshape(H // tm, tm, H), b.reshape(H // tm, tm, H)], axis=1)
    sg = jnp.zeros((SUBLANES, H), F32).at[0].set(jnp.where(u % 2 == 0, 1.0, -1.0)).astype(BF16)
    return dict(fe=stack(fre, fie), fo=stack(fro, fio), sg=sg, inv=(fre, fie, frot, fiot), tm=tm)


def hyena_half(hy, lp, mats):
    B, L, _ = hy.shape
    tm = mats['tm']
    bg = B if L < TM_DFT else BG_HALF
    hf = hyena_time_filters(L, lp)
    h_fwd = hf[:, 0].reshape(L, HY_ORDER * HY_CH)
    h_bwd = hf[:, 1].at[0].set(0.0).reshape(L, HY_ORDER * HY_CH)
    spec = hyena_half_filter_spectrum(mats['fe'], mats['fo'], mats['sg'], _hi_lo(h_fwd + h_bwd),
                                      _hi_lo(h_fwd - h_bwd), tm)
    w = jnp.pad(lp['hy_conv_w'], ((0, SUBLANES - 3), (0, 0)))
    u, ub = hyena_conv3_planes(hy, w, lp['hy_conv_b'][None, :])
    d = lp['hy_d'].astype(F32)
    z, zb, zcol = u, ub, 0
    for o in range(HY_ORDER):
        *ps, ysp = hyena_half_fwd(mats['fe'], mats['fo'], mats['sg'], zb, zcol, spec, o, tm, bg)
        last = o == HY_ORDER - 1
        out = hyena_half_inv(mats['inv'], ps, ysp, z, zcol, u, 1 + o, d[o][None, :], tm, bg, natural=last)
        if last:
            return out
        z, zb = out
        zcol = 0


def _ffn_body(x_ref, mod_ref, g_ref, w1_ref, w3_ref, w2_ref, o_ref):
    x = x_ref[...]
    h = _adaln_rows(x, g_ref[...], mod_ref[0, ROW_SHIFT2:ROW_SHIFT2 + 1, :],
                    mod_ref[0, ROW_SCALE2:ROW_SCALE2 + 1, :]).astype(BF16)
    a = jnp.dot(h, w1_ref[...], preferred_element_type=F32)
    b = jnp.dot(h, w3_ref[...], preferred_element_type=F32)
    act = (_silu(a) * b).astype(BF16)
    ff = jnp.dot(act, w2_ref[...], preferred_element_type=F32)
    o_ref[...] = x + mod_ref[0, ROW_GATE2:ROW_GATE2 + 1, :] * ff


def ffn_dense(x, mods, g, w1, w3, w2, seg_len, tm=TM_PROJ):
    T, D = x.shape
    F = w1.shape[1]
    resident = functools.partial(pl.BlockSpec, pipeline_mode=pl.Buffered(1))
    return pl.pallas_call(
        _ffn_body,
        grid=(T // tm,),
        in_specs=[
            pl.BlockSpec((tm, D), lambda i: (i, 0)),
            pl.BlockSpec((1, MOD_ROWS, D), lambda i: (i * tm // seg_len, 0, 0)),
            pl.BlockSpec((1, D), lambda i: (0, 0)),
            resident((D, F), lambda i: (0, 0)),
            resident((D, F), lambda i: (0, 0)),
            resident((F, D), lambda i: (0, 0)),
        ],
        out_specs=pl.BlockSpec((tm, D), lambda i: (i, 0)),
        out_shape=jax.ShapeDtypeStruct((T, D), F32),
        compiler_params=_cparams("parallel"),
        name="ffn_dense",
    )(x, mods, g, w1, w3, w2)


def _store_token_tiles(ref, x):
    n = x.shape[0]
    for s in range(SUBLANES):
        ref[pl.ds(s, n, stride=SUBLANES), :] = x[:, s * LANES:(s + 1) * LANES]


def _load_token_tiles(ref, n):
    return jnp.concatenate([ref[pl.ds(s, n, stride=SUBLANES), :] for s in range(SUBLANES)], axis=1)


def _router_body(xa_ref, xb_ref, mod_ref, g_ref, wr_ref, h_ref, idx_ref, gw_ref, *, na):
    x = jnp.where(pl.program_id(0) < na, xa_ref[...], xb_ref[...])
    h = _adaln_rows(x, g_ref[...], mod_ref[0, ROW_SHIFT2:ROW_SHIFT2 + 1, :],
                    mod_ref[0, ROW_SCALE2:ROW_SCALE2 + 1, :])
    _store_token_tiles(h_ref, h)
    logits = lax.dot_general(wr_ref[...], h, (((1,), (1,)), ((), ())),
                             precision=lax.Precision.HIGHEST, preferred_element_type=F32)
    eidx = lax.broadcasted_iota(jnp.int32, logits.shape, 0)
    m1 = jnp.max(logits, axis=0, keepdims=True)
    i1 = jnp.min(jnp.where(logits == m1, eidx, N_EXPERTS), axis=0, keepdims=True)
    rest = jnp.where(eidx == i1, -jnp.inf, logits)
    m2 = jnp.max(rest, axis=0, keepdims=True)
    i2 = jnp.min(jnp.where(rest == m2, eidx, N_EXPERTS), axis=0, keepdims=True)
    e2 = jnp.exp(m2 - m1)
    den = 1.0 + e2
    row = lax.broadcasted_iota(jnp.int32, logits.shape, 0)
    idx_ref[...] = jnp.where(row == 0, i1, jnp.where(row == 1, i2, 0))
    gw_ref[...] = jnp.transpose(jnp.where(row == 0, 1.0 / den, jnp.where(row == 1, e2 / den, 0.0)))


def _merged_seg(i, na, tm, seg_len_b):
    return jnp.where(i < na, 0, 1 + jnp.maximum(i - na, 0) * tm // seg_len_b)


def moe_router(xa, xb, mods, g, wr_t, seg_len_b, tm=TM_PROJ):
    D = xa.shape[1]
    na = xa.shape[0] // tm
    T = xa.shape[0] + xb.shape[0]
    return pl.pallas_call(
        functools.partial(_router_body, na=na),
        grid=(T // tm,),
        in_specs=[
            pl.BlockSpec((tm, D), lambda i: (jnp.minimum(i, na - 1), 0)),
            pl.BlockSpec((tm, D), lambda i: (jnp.maximum(i - na, 0), 0)),
            pl.BlockSpec((1, MOD_ROWS, D), lambda i: (_merged_seg(i, na, tm, seg_len_b), 0, 0)),
            pl.BlockSpec((1, D), lambda i: (0, 0)),
            pl.BlockSpec((N_EXPERTS, D), lambda i: (0, 0)),
        ],
        out_specs=[
            pl.BlockSpec((tm * SUBLANES, LANES), lambda i: (i, 0)),
            pl.BlockSpec((N_EXPERTS, tm), lambda i: (0, i)),
            pl.BlockSpec((tm, N_EXPERTS), lambda i: (i, 0)),
        ],
        out_shape=[
            jax.ShapeDtypeStruct((T * SUBLANES, LANES), F32),
            jax.ShapeDtypeStruct((N_EXPERTS, T), jnp.int32),
            jax.ShapeDtypeStruct((T, N_EXPERTS), F32),
        ],
        compiler_params=_cparams("parallel"),
        name="moe_router",
    )(xa, xb, mods, g, wr_t)


def _experts_body(te_ref, nv_ref, src0_ref, srcn_ref, dst_ref, h_hbm, w1_ref, w3_ref, w2_ref,
                  y_hbm, hrows, hb, acc, ybuf, sem_g, sem_s, *, tm, nf, n_slots):
    i = pl.program_id(0)
    f = pl.program_id(1)
    nv = nv_ref[0]
    valid = i < nv
    slot = lax.rem(i, 2)

    def gather_issue(idx_ref, s):
        def body(r, carry):
            src = pl.multiple_of(idx_ref[0, 0, r] * SUBLANES, SUBLANES)
            dst = pl.multiple_of(r * SUBLANES, SUBLANES)
            pltpu.make_async_copy(h_hbm.at[pl.ds(src, SUBLANES)], hrows.at[s, pl.ds(dst, SUBLANES)],
                                  sem_g.at[s]).start()
            return carry
        lax.fori_loop(0, tm, body, 0, unroll=8)

    def gather_wait(s):
        pltpu.make_async_copy(h_hbm.at[pl.ds(0, tm * SUBLANES)], hrows.at[s], sem_g.at[s]).wait()

    def scatter_wait(s):
        pltpu.make_async_copy(ybuf.at[s], y_hbm.at[pl.ds(0, tm * SUBLANES)], sem_s.at[s]).wait()

    @pl.when(jnp.logical_and(i == 0, f == 0))
    def _first():
        ybuf[0] = jnp.zeros(ybuf.shape[1:], ybuf.dtype)
        spare = lambda p: y_hbm.at[pl.ds((n_slots + p * tm) * SUBLANES, tm * SUBLANES)]
        for p in range(2):
            pltpu.make_async_copy(ybuf.at[0], spare(p), sem_s.at[p]).start()
        for p in range(2):
            pltpu.make_async_copy(ybuf.at[0], spare(p), sem_s.at[p]).wait()
        gather_issue(src0_ref, 0)

    @pl.when(jnp.logical_and(valid, f == 0))
    def _stage():
        gather_wait(slot)
        hb[...] = _load_token_tiles(hrows.at[slot], tm).astype(BF16)
        acc[...] = jnp.zeros_like(acc)

        @pl.when(i + 1 < nv)
        def _prefetch():
            gather_issue(srcn_ref, 1 - slot)

    @pl.when(valid)
    def _compute():
        h = hb[...]
        a = jnp.dot(h, w1_ref[0], preferred_element_type=F32)
        b = jnp.dot(h, w3_ref[0], preferred_element_type=F32)
        act = (_silu(a) * b).astype(BF16)
        acc[...] += jnp.dot(act, w2_ref[0], preferred_element_type=F32)

    @pl.when(jnp.logical_and(valid, f == nf - 1))
    def _emit():
        @pl.when(i >= 2)
        def _reuse():
            scatter_wait(slot)
        _store_token_tiles(ybuf.at[slot], acc[...])

        def body(r, carry):
            src = pl.multiple_of(r * SUBLANES, SUBLANES)
            dst = pl.multiple_of(dst_ref[0, 0, r] * SUBLANES, SUBLANES)
            pltpu.make_async_copy(ybuf.at[slot, pl.ds(src, SUBLANES)], y_hbm.at[pl.ds(dst, SUBLANES)],
                                  sem_s.at[slot]).start()
            return carry
        lax.fori_loop(0, tm, body, 0, unroll=8)

    @pl.when(jnp.logical_and(i == nv, f == 0))
    def _drain():
        scatter_wait(lax.rem(nv + 1, 2))

        @pl.when(nv >= 2)
        def _older():
            scatter_wait(lax.rem(nv, 2))


def moe_experts(h, tile_expert, n_valid, src_rows, dst_rows, w1, w3, w2, tm=TM_MOE, tf=TF_MOE):
    T = h.shape[0] // SUBLANES
    D = w1.shape[1]
    F = w1.shape[2]
    n_tiles = src_rows.shape[0]
    n_slots = TOP_K * T
    nf = F // tf

    def wcol(i, f, te, nv):
        return (te[i], 0, jnp.where(i < nv[0], f, nf - 1))

    def wrow(i, f, te, nv):
        return (te[i], jnp.where(i < nv[0], f, nf - 1), 0)

    smem_tile = lambda fn: pl.BlockSpec((1, 1, tm), fn, memory_space=pltpu.SMEM)
    grid_spec = pltpu.PrefetchScalarGridSpec(
        num_scalar_prefetch=2,
        grid=(n_tiles, nf),
        in_specs=[
            smem_tile(lambda i, f, te, nv: (0, 0, 0)),
            smem_tile(lambda i, f, te, nv: (jnp.minimum(i + 1, n_tiles - 1), 0, 0)),
            smem_tile(lambda i, f, te, nv: (i, 0, 0)),
            pl.BlockSpec(memory_space=pl.ANY),
            pl.BlockSpec((1, D, tf), wcol),
            pl.BlockSpec((1, D, tf), wcol),
            pl.BlockSpec((1, tf, D), wrow),
        ],
        out_specs=pl.BlockSpec(memory_space=pl.ANY),
        scratch_shapes=[
            pltpu.VMEM((2, tm * SUBLANES, LANES), F32),
            pltpu.VMEM((tm, D), BF16),
            pltpu.VMEM((tm, D), F32),
            pltpu.VMEM((2, tm * SUBLANES, LANES), F32),
            pltpu.SemaphoreType.DMA((2,)),
            pltpu.SemaphoreType.DMA((2,)),
        ],
    )
    return pl.pallas_call(
        functools.partial(_experts_body, tm=tm, nf=nf, n_slots=n_slots),
        grid_spec=grid_spec,
        out_shape=jax.ShapeDtypeStruct(((n_slots + 2 * tm) * SUBLANES, LANES), F32),
        compiler_params=_cparams("arbitrary", "arbitrary"),
        name="moe_experts",
    )(tile_expert, n_valid, src_rows, src_rows, dst_rows, h, w1, w3, w2)


def _combine_body(x_ref, mod_ref, gw_ref, y0_ref, y1_ref, o_ref):
    n = x_ref.shape[0]
    gw = gw_ref[...]
    y = gw[:, 0:1] * _load_token_tiles(y0_ref, n) + gw[:, 1:2] * _load_token_tiles(y1_ref, n)
    o_ref[...] = x_ref[...] + mod_ref[0, ROW_GATE2:ROW_GATE2 + 1, :] * y


def moe_combine(x, mods, gw, y, row0, n_tok, seg_len, tm=TM_COMB):
    T, D = x.shape
    off = row0 // tm
    return pl.pallas_call(
        _combine_body,
        grid=(T // tm,),
        in_specs=[
            pl.BlockSpec((tm, D), lambda i: (i, 0)),
            pl.BlockSpec((1, MOD_ROWS, D), lambda i: (i * tm // seg_len, 0, 0)),
            pl.BlockSpec((tm, N_EXPERTS), lambda i: (off + i, 0)),
            pl.BlockSpec((tm * SUBLANES, LANES), lambda i: (off + i, 0)),
            pl.BlockSpec((tm * SUBLANES, LANES), lambda i: (off + n_tok // tm + i, 0)),
        ],
        out_specs=pl.BlockSpec((tm, D), lambda i: (i, 0)),
        out_shape=jax.ShapeDtypeStruct((T, D), F32),
        compiler_params=_cparams("parallel"),
        name="moe_combine",
    )(x, mods, gw, y, y)


def moe_dispatch_plan(idx, tm=TM_MOE):
    T = idx.shape[1]
    n_slots = TOP_K * T
    n_tiles = n_slots // tm + N_EXPERTS
    n_rows = n_tiles * tm
    experts = jnp.arange(N_EXPERTS, dtype=jnp.int32)[None, :]
    e_flat = idx[:TOP_K].reshape(n_slots)
    counts = jnp.sum((e_flat[:, None] == experts).astype(jnp.int32), axis=0)
    padded = (counts + tm - 1) // tm * tm
    ends = jnp.cumsum(padded)
    offs = ends - padded
    order = jnp.argsort(e_flat, stable=True).astype(jnp.int32)
    cstart = jnp.cumsum(counts) - counts
    rows = jnp.arange(n_rows, dtype=jnp.int32)
    row_e = jnp.minimum(jnp.sum((rows[:, None] >= ends[None, :]).astype(jnp.int32), axis=1), N_EXPERTS - 1)
    pick = lambda table: jnp.sum(jnp.where(row_e[:, None] == experts, table[None, :], 0), axis=1)
    j = rows - pick(offs)
    live = j < pick(counts)
    slot = order[jnp.clip(pick(cstart) + j, 0, n_slots - 1)]
    spare = n_slots + (rows // tm) % 2 * tm + rows % tm
    src_rows = jnp.where(live, slot % T, 0).astype(jnp.int32).reshape(n_tiles, 1, tm)
    dst_rows = jnp.where(live, slot, spare).astype(jnp.int32).reshape(n_tiles, 1, tm)
    tile_start = jnp.arange(n_tiles, dtype=jnp.int32) * tm
    n_valid = (ends[-1] // tm).astype(jnp.int32).reshape(1)
    tile_expert = jnp.minimum(jnp.sum((tile_start[:, None] >= ends[None, :]).astype(jnp.int32), axis=1),
                              N_EXPERTS - 1)
    last_e = tile_expert[jnp.maximum(n_valid[0] - 1, 0)]
    tile_expert = jnp.where(tile_start < ends[-1], tile_expert, last_e).astype(jnp.int32)
    return tile_expert, n_valid, src_rows, dst_rows


def ffn_moe(xa, xb, mods_a, mods_b, g, wr_t, w1, w3, w2, seg_len_b, tm=TM_PROJ, tme=TM_MOE, tmc=TM_COMB,
            tf=TF_MOE):
    na, nb = xa.shape[0], xb.shape[0]
    mods = jnp.concatenate([mods_a, mods_b], axis=0)
    h, idx, gw = moe_router(xa, xb, mods, g, wr_t, seg_len_b, tm)
    tile_expert, n_valid, src_rows, dst_rows = moe_dispatch_plan(idx, tme)
    y = moe_experts(h, tile_expert, n_valid, src_rows, dst_rows, w1, w3, w2, tme, tf)
    return (moe_combine(xa, mods_a, gw, y, 0, na + nb, na, tmc),
            moe_combine(xb, mods_b, gw, y, na, na + nb, seg_len_b, tmc))


def axial_rope(L):
    rows = L // GRID_W
    r = jnp.repeat(jnp.arange(rows, dtype=F32), GRID_W)
    col = jnp.tile(jnp.arange(GRID_W, dtype=F32), rows)
    n = HEAD_DIM // 4
    freqs = ROPE_BASE ** (-jnp.arange(n, dtype=F32) / n)
    ang = jnp.concatenate([r[:, None] * freqs, col[:, None] * freqs], axis=-1)
    return jnp.cos(ang), jnp.sin(ang)


def rope_tables(L):
    cos, sin = axial_rope(L)
    n = HEAD_DIM // 4
    cos_h = jnp.concatenate([cos[:, :n], cos[:, :n], cos[:, n:], cos[:, n:]], axis=1)
    sin_h = jnp.concatenate([-sin[:, :n], sin[:, :n], -sin[:, n:], sin[:, n:]], axis=1)
    reps = LANES // HEAD_DIM
    return jnp.tile(cos_h, (1, reps)), jnp.tile(sin_h, (1, reps))


def block_diag_ones(width, block):
    i = jnp.arange(width) // block
    return (i[:, None] == i[None, :]).astype(BF16)


def gla_gate_params(gw, gb):
    w = jnp.zeros((LANES, 2 * C_WIDTH), F32)
    w = w.at[:GLA_RANK, :C_WIDTH].set(gw[0]).at[GLA_RANK:2 * GLA_RANK, C_WIDTH:].set(gw[1])
    return w.astype(BF16), jnp.concatenate([gb[0], gb[1]])[None, :]


def gla_state_to_blockdiag(st):
    B = st.shape[0]
    out = jnp.zeros((B, 2, C_HEADS, C_DV, C_HEADS, C_DK), F32)
    for h in range(C_HEADS):
        out = out.at[:, :, h, :, h, :].set(jnp.swapaxes(st[:, :, h], -1, -2))
    return out.reshape(B, 2, C_WIDTH, C_WIDTH)


def gla_state_from_blockdiag(sT):
    B = sT.shape[0]
    s6 = sT.reshape(B, 2, C_HEADS, C_DV, C_HEADS, C_DK)
    return jnp.stack([jnp.swapaxes(s6[:, :, h, :, h, :], -1, -2) for h in range(C_HEADS)], axis=2)


def mod_table(cvec, w_ada, b_ada):
    m = jax.nn.silu(cvec) @ w_ada + b_ada
    m = m.reshape(cvec.shape[0], 6, D_MODEL)
    return jnp.pad(m, ((0, 0), (0, MOD_ROWS - 6), (0, 0)))


def kernel(x_prompt, x_sample, cache_k, cache_v, state_gla, c, c_ctx, norm1_g, norm2_g, w_ada, b_ada,
           w_in, w_out, q_norm_g, k_norm_g, attn_sink, hy_conv_w, hy_conv_b, hy_w1, hy_b1, hy_freq1,
           hy_w2, hy_b2, hy_freq2, hy_w3, hy_decay, hy_d, gla_gate_w, gla_gate_b, gla_norm_g,
           ffn_w1, ffn_w3, ffn_w2, moe_router, moe_w1, moe_w3, moe_w2):
    D = D_MODEL
    xp = x_prompt.reshape(BATCH * SEQ, D)
    xs = x_sample.reshape(DEC_BATCH * DEC_SEQ, D)
    streams = [
        dict(x=xp, B=BATCH, L=SEQ, seg_len=BATCH * SEQ, cvec=c_ctx[None, :], latent=False),
        dict(x=xs, B=DEC_BATCH, L=DEC_SEQ, seg_len=DEC_SEQ, cvec=c, latent=True),
    ]
    ks_list, vs_list, st_list = [], [], []
    bd_q = block_diag_ones(A_WIDTH, HEAD_DIM)
    dft = {s['L']: dft_half_matrices(s['L'], min(TM_DFT, s['L'] // 2)) for s in streams}
    for l in range(DEPTH):
        lp = {
            'hy_conv_w': hy_conv_w[l], 'hy_conv_b': hy_conv_b[l],
            'hy_w1': hy_w1[l], 'hy_b1': hy_b1[l], 'hy_freq1': hy_freq1[l], 'hy_w2': hy_w2[l],
            'hy_b2': hy_b2[l], 'hy_freq2': hy_freq2[l], 'hy_w3': hy_w3[l], 'hy_decay': hy_decay[l],
            'hy_d': hy_d[l],
        }
        w_in_l = jnp.pad(w_in[l], ((0, 0), (0, D_PROJ_PAD - D_PROJ))).astype(BF16)
        w_out_l = w_out[l].astype(BF16)
        g1 = norm1_g[l][None, :]
        g2 = norm2_g[l][None, :]
        qg = jnp.tile(q_norm_g[l], A_HEADS)[None, :]
        kg = jnp.tile(k_norm_g[l], A_KV_HEADS)[None, :]
        gg = jnp.tile(gla_norm_g[l], C_HEADS)[None, :]
        gate_w, gate_b = gla_gate_params(gla_gate_w[l], gla_gate_b[l])
        j = l // 2
        for s in streams:
            B, L = s['B'], s['L']
            mods = mod_table(s['cvec'], w_ada[l], b_ada[l])
            rope = rope_tables(L) if s['latent'] else None
            q, k, v, hy, cq, ck, cv, cg, la = in_proj(s['x'], mods, g1, w_in_l, bd_q, qg, kg, gate_w, gate_b,
                                                      s['seg_len'], rope, L)
            seq = lambda t: t.reshape(B, L, t.shape[-1])
            if s['latent']:
                kc = cache_k[:, l].reshape(DEC_BATCH, PAST_LEN, LANES)
                vc = cache_v[:, l].reshape(DEC_BATCH, PAST_LEN, LANES)
                a_out = attention(seq(q), seq(k), seq(v), attn_sink[l], kc, vc)
                s0 = gla_state_to_blockdiag(state_gla[:, l])
                o_f, o_b, _ = gla(seq(cq), seq(ck), seq(cv), seq(la), s0)
            else:
                a_out = attention(seq(q), seq(k), seq(v), attn_sink[l])
                o_f, o_b, sT = gla(seq(cq), seq(ck), seq(cv), seq(la))
                ks_list.append(k.reshape(B, L, A_KV_HEADS, HEAD_DIM))
                vs_list.append(v.reshape(B, L, A_KV_HEADS, HEAD_DIM))
                st_list.append(gla_state_from_blockdiag(sT))
            h_out = hyena_half(seq(hy), lp, dft[L])
            flat = lambda t: t.reshape(B * L, t.shape[-1])
            x1 = out_proj(s['x'], flat(a_out), [flat(h) for h in h_out], flat(o_f), flat(o_b), cg, mods, w_out_l,
                          bd_q[:C_WIDTH, :C_WIDTH], gg, s['seg_len'])
            if l % 2 == 0:
                s['x'] = ffn_dense(x1, mods, g2, ffn_w1[j].astype(BF16), ffn_w3[j].astype(BF16),
                                   ffn_w2[j].astype(BF16), s['seg_len'])
            else:
                s['x'], s['mods'] = x1, mods
        if l % 2 == 1:
            sa, sb = streams
            sa['x'], sb['x'] = ffn_moe(sa['x'], sb['x'], sa['mods'], sb['mods'], g2, moe_router[j].T,
                                       moe_w1[j].astype(BF16), moe_w3[j].astype(BF16), moe_w2[j].astype(BF16),
                                       sb['seg_len'])
    y_prompt = streams[0]['x'].reshape(BATCH, SEQ, D)
    y_sample = streams[1]['x'].reshape(DEC_BATCH, DEC_SEQ, D)
    new_cache_k = jnp.stack(ks_list, axis=1)
    new_cache_v = jnp.stack(vs_list, axis=1)
    new_state_gla = jnp.stack(st_list, axis=1)
    return (y_prompt, y_sample, new_cache_k, new_cache_v, new_state_gla)
```

```python
import math
import functools
import jax
import jax.numpy as jnp
from jax import lax
import numpy as np
from jax.experimental import pallas as pl
from jax.experimental.pallas import tpu as pltpu

D_MODEL = 1024
BATCH = 32
SEQ = 256
DEPTH = 2
DEC_BATCH = 8
DEC_SEQ = 4096
PAST_LEN = 512

GRID_W = 64
HEAD_DIM = 64
D_MIX = D_MODEL
A_HEADS = 8
A_KV_HEADS = 2
A_GROUP = A_HEADS // A_KV_HEADS
A_WIDTH = A_HEADS * HEAD_DIM
WINDOW = 128
BLK = 128
ROPE_BASE = 10000.0
HY_CH = 256
HY_ORDER = 2
HY_BANDS = 16
HY_EMB = 1 + 2 * HY_BANDS
HY_HID = 64
C_HEADS = 4
C_DK = 64
C_DV = 64
C_WIDTH = C_HEADS * C_DV
GLA_RANK = 16
GLA_TAU = 16.0
GLA_CHUNK = 64
D_FF = 2816
N_EXPERTS = 8
TOP_K = 2
D_FF_EXPERT = 3584

PROJ_SIZES = (A_WIDTH, A_KV_HEADS * HEAD_DIM, A_KV_HEADS * HEAD_DIM, 3 * HY_CH,
              C_HEADS * C_DK, C_HEADS * C_DK, C_WIDTH, C_WIDTH, 2 * GLA_RANK)
PROJ_SPLITS = tuple(int(s) for s in np.cumsum(PROJ_SIZES)[:-1])
D_PROJ = int(sum(PROJ_SIZES))

F32 = jnp.float32
BF16 = jnp.bfloat16
ATT_SCALE = HEAD_DIM ** -0.5
NEG_INF = -1e30
EPS = 1e-6

LANES = 128
SUBLANES = 8
D_PROJ_PAD = -(-D_PROJ // LANES) * LANES
VMEM_LIMIT_BYTES = 56 * 1024 * 1024

MOD_ROWS = SUBLANES
ROW_SHIFT1, ROW_SCALE1, ROW_GATE1, ROW_SHIFT2, ROW_SCALE2, ROW_GATE2 = range(6)

TM_PROJ = 512
TM_MOE = 512
TF_MOE = 1792
TM_COMB = 512


def _cparams(*sem):
    return pltpu.CompilerParams(dimension_semantics=sem, vmem_limit_bytes=VMEM_LIMIT_BYTES)


def _adaln_rows(x, g, shift, scale):
    ms = jnp.mean(x * x, axis=-1, keepdims=True)
    return (x * lax.rsqrt(ms + EPS) * g) * (1.0 + scale) + shift


def _silu(a):
    return a * jax.nn.sigmoid(a)


def _group_mean_sq(x, ones_bd):
    sq = x * x
    hi = sq.astype(BF16)
    lo = (sq - hi.astype(F32)).astype(BF16)
    s = jnp.dot(hi, ones_bd, preferred_element_type=F32) + jnp.dot(lo, ones_bd, preferred_element_type=F32)
    return s * (1.0 / HEAD_DIM)


def _rope_rows(x, cos_t, sin_t):
    q4 = HEAD_DIM // 4
    lane = lax.broadcasted_iota(jnp.int32, x.shape, 1)
    partner = jnp.where((lane % (2 * q4)) < q4, pltpu.roll(x, LANES - q4, 1), pltpu.roll(x, q4, 1))
    return x * cos_t + partner * sin_t


def _log_sigmoid(x):
    return jnp.minimum(x, 0.0) - jnp.log(1.0 + jnp.exp(-jnp.abs(x)))


def _inproj_body(*refs, latent):
    if latent:
        (x_ref, mod_ref, g_ref, w_ref, bd_ref, qg_ref, kg_ref, gw_ref, gb_ref, cos_ref, sin_ref,
         q_ref, k_ref, v_ref, hy_ref, cq_ref, ck_ref, cv_ref, cg_ref, la_ref) = refs
    else:
        (x_ref, mod_ref, g_ref, w_ref, bd_ref, qg_ref, kg_ref, gw_ref, gb_ref,
         q_ref, k_ref, v_ref, hy_ref, cq_ref, ck_ref, cv_ref, cg_ref, la_ref) = refs
    h = _adaln_rows(x_ref[...], g_ref[...], mod_ref[0, ROW_SHIFT1:ROW_SHIFT1 + 1, :],
                    mod_ref[0, ROW_SCALE1:ROW_SCALE1 + 1, :])
    acc = jnp.dot(h.astype(BF16), w_ref[...], preferred_element_type=F32)
    o = 0
    q = acc[:, o:o + A_WIDTH]
    o += A_WIDTH
    k = acc[:, o:o + LANES]
    o += LANES
    v_ref[...] = acc[:, o:o + LANES]
    o += LANES
    hy_ref[...] = acc[:, o:o + 3 * HY_CH]
    o += 3 * HY_CH
    cq_ref[...] = acc[:, o:o + C_WIDTH] * (C_DK ** -0.5)
    o += C_WIDTH
    ck_ref[...] = acc[:, o:o + C_WIDTH]
    o += C_WIDTH
    cv_ref[...] = acc[:, o:o + C_WIDTH]
    o += C_WIDTH
    cg_ref[...] = acc[:, o:o + C_WIDTH]
    o += C_WIDTH
    r = acc[:, o:o + LANES]
    la_ref[...] = _log_sigmoid(jnp.dot(r.astype(BF16), gw_ref[...], preferred_element_type=F32)
                               + gb_ref[...]) * (1.0 / GLA_TAU)
    q = q * lax.rsqrt(_group_mean_sq(q, bd_ref[...]) + EPS) * qg_ref[...]
    k = k * lax.rsqrt(_group_mean_sq(k, bd_ref[0:LANES, 0:LANES]) + EPS) * kg_ref[...]
    if latent:
        cos_t = cos_ref[...]
        sin_t = sin_ref[...]
        q = jnp.concatenate([_rope_rows(q[:, j * LANES:(j + 1) * LANES], cos_t, sin_t)
                             for j in range(A_WIDTH // LANES)], axis=1)
        k = _rope_rows(k, cos_t, sin_t)
    q_ref[...] = (q * ATT_SCALE).astype(BF16)
    k_ref[...] = k


def in_proj(x, mods, g, w, bd, qg, kg, gw, gb, seg_len, rope=None, seq_len=None, tm=TM_PROJ):
    T, D = x.shape
    N = w.shape[1]
    latent = rope is not None
    const = lambda shape: pl.BlockSpec(shape, lambda i: (0,) * len(shape))
    in_specs = [
        pl.BlockSpec((tm, D), lambda i: (i, 0)),
        pl.BlockSpec((1, MOD_ROWS, D), lambda i: (i * tm // seg_len, 0, 0)),
        const((1, D)), const((D, N)), const((A_WIDTH, A_WIDTH)), const((1, A_WIDTH)), const((1, LANES)),
        const((LANES, 2 * C_WIDTH)), const((1, 2 * C_WIDTH)),
    ]
    args = [x, mods, g, w, bd, qg, kg, gw, gb]
    if latent:
        tiles_per_seq = seq_len // tm
        in_specs += [pl.BlockSpec((tm, LANES), lambda i: (i % tiles_per_seq, 0))] * 2
        args += list(rope)
    widths = [A_WIDTH, LANES, LANES, 3 * HY_CH, C_WIDTH, C_WIDTH, C_WIDTH, C_WIDTH, 2 * C_WIDTH]
    dtypes = [BF16] + [F32] * 8
    return pl.pallas_call(
        functools.partial(_inproj_body, latent=latent),
        grid=(T // tm,),
        in_specs=in_specs,
        out_specs=[pl.BlockSpec((tm, wd), lambda i: (i, 0)) for wd in widths],
        out_shape=[jax.ShapeDtypeStruct((T, wd), dt) for wd, dt in zip(widths, dtypes)],
        compiler_params=_cparams("parallel"),
        name="in_proj_latent" if latent else "in_proj_context",
    )(*args)


def _outproj_body(x_ref, a_ref, h0_ref, h1_ref, of_ref, ob_ref, cg_ref, mod_ref, w_ref, bd_ref, gg_ref, o_ref):
    o = of_ref[...] + ob_ref[...]
    g_out = o * lax.rsqrt(_group_mean_sq(o, bd_ref[...]) + EPS) * gg_ref[...] * _silu(cg_ref[...])
    h0 = A_WIDTH
    g0 = A_WIDTH + HY_CH
    mix = jnp.dot(a_ref[...], w_ref[0:h0, :], preferred_element_type=F32)
    h = jnp.concatenate([h0_ref[...], h1_ref[...]], axis=1)
    mix += jnp.dot(h.astype(BF16), w_ref[h0:g0, :], preferred_element_type=F32)
    mix += jnp.dot(g_out.astype(BF16), w_ref[g0:, :], preferred_element_type=F32)
    o_ref[...] = x_ref[...] + mod_ref[0, ROW_GATE1:ROW_GATE1 + 1, :] * mix


def out_proj(x, a, h, of, ob, cg, mods, w, bd, gg, seg_len, tm=TM_PROJ):
    T, D = x.shape
    row = lambda wd: pl.BlockSpec((tm, wd), lambda i: (i, 0))
    const = lambda shape: pl.BlockSpec(shape, lambda i: (0,) * len(shape))
    return pl.pallas_call(
        _outproj_body,
        grid=(T // tm,),
        in_specs=[
            row(D), row(A_WIDTH), row(LANES), row(LANES), row(C_WIDTH), row(C_WIDTH), row(C_WIDTH),
            pl.BlockSpec((1, MOD_ROWS, D), lambda i: (i * tm // seg_len, 0, 0)),
            const((D, D)), const((C_WIDTH, C_WIDTH)), const((1, C_WIDTH)),
        ],
        out_specs=row(D),
        out_shape=jax.ShapeDtypeStruct((T, D), F32),
        compiler_params=_cparams("parallel"),
        name="out_proj",
    )(x, a, h[0], h[1], of, ob, cg, mods, w, bd, gg)


TQ_ATT = 256
HEADS_PER_STACK = 2
KWIN_ATT = TQ_ATT + 2 * WINDOW


def _dup_heads(x, g):
    lane = lax.broadcasted_iota(jnp.int32, x.shape, 1)
    rolled = pltpu.roll(x, HEAD_DIM, 1)
    keep = (lane < HEAD_DIM) if g == 0 else (lane >= HEAD_DIM)
    return jnp.where(keep, x, rolled)


def _attn_body(*refs, latent, seq_len):
    if latent:
        sink_ref, q_ref, k_ref, v_ref, kc_ref, vc_ref, o_ref = refs
    else:
        sink_ref, q_ref, k_ref, v_ref, o_ref = refs
    tq = q_ref.shape[1]
    i = pl.program_id(1)
    q = q_ref[0]
    lane = lax.broadcasted_iota(jnp.int32, (tq, LANES), 1)
    low = lane < HEAD_DIM
    if latent:
        start = jnp.clip(i * tq - WINDOW, 0, seq_len - KWIN_ATT)
        start = pl.multiple_of(start, WINDOW)
        kl = k_ref[0, pl.ds(start, KWIN_ATT), :]
        vl = v_ref[0, pl.ds(start, KWIN_ATT), :]
        qpos = i * tq + lax.broadcasted_iota(jnp.int32, (tq, KWIN_ATT), 0)
        kpos = start + lax.broadcasted_iota(jnp.int32, (tq, KWIN_ATT), 1)
        bias = jnp.where(jnp.abs(qpos - kpos) <= WINDOW, 0.0, NEG_INF).astype(F32)
        bias = jnp.concatenate([bias] * A_GROUP, axis=0)
        kc = kc_ref[0]
        vc = vc_ref[0]
    else:
        kl = k_ref[0]
        vl = v_ref[0]
    nt = (((1,), (1,)), ((), ()))
    zero = jnp.zeros_like(q[:, :LANES])
    kls = [_dup_heads(kl, g).astype(BF16) for g in range(A_KV_HEADS)]
    vls = [_dup_heads(vl, g).astype(BF16) for g in range(A_KV_HEADS)]
    if latent:
        kcs = [_dup_heads(kc, g).astype(BF16) for g in range(A_KV_HEADS)]
        vcs = [_dup_heads(vc, g).astype(BF16) for g in range(A_KV_HEADS)]

    def run(stacks):
        gs = [g for g, _ in stacks]
        n = range(len(stacks))
        qs = [jnp.concatenate([jnp.where(low if h % 2 == 0 else jnp.logical_not(low),
                                         q[:, (h // 2) * LANES:(h // 2 + 1) * LANES], zero) for h in heads], axis=0)
              for _, heads in stacks]
        sinks = [jnp.concatenate([jnp.full((tq, 1), sink_ref[h], F32) for h in heads], axis=0)
                 for _, heads in stacks]
        s_loc = [lax.dot_general(qs[i], kls[gs[i]], nt, preferred_element_type=F32) for i in n]
        m = sinks
        if latent:
            s_loc = [s + bias[:s.shape[0]] for s in s_loc]
            s_ctx = [lax.dot_general(qs[i], kcs[gs[i]], nt, preferred_element_type=F32) for i in n]
            m = [jnp.maximum(m[i], jnp.max(s_ctx[i], axis=-1, keepdims=True)) for i in n]
        m = [jnp.maximum(m[i], jnp.max(s_loc[i], axis=-1, keepdims=True)) for i in n]
        p_loc = [jnp.exp(s_loc[i] - m[i]) for i in n]
        den = [jnp.exp(sinks[i] - m[i]) + jnp.sum(p_loc[i], axis=-1, keepdims=True) for i in n]
        acc = [jnp.dot(p_loc[i].astype(BF16), vls[gs[i]], preferred_element_type=F32) for i in n]
        if latent:
            p_ctx = [jnp.exp(s_ctx[i] - m[i]) for i in n]
            den = [den[i] + jnp.sum(p_ctx[i], axis=-1, keepdims=True) for i in n]
            acc = [acc[i] + jnp.dot(p_ctx[i].astype(BF16), vcs[gs[i]], preferred_element_type=F32) for i in n]
        outs = []
        for i in n:
            og = acc[i] / den[i]
            for jj in range(len(stacks[i][1]) // 2):
                outs.append(jnp.where(low, og[(2 * jj) * tq:(2 * jj + 1) * tq],
                                      og[(2 * jj + 1) * tq:(2 * jj + 2) * tq]))
        return outs

    stacks = [(g, range(g * A_GROUP + j, g * A_GROUP + j + HEADS_PER_STACK))
              for g in range(A_KV_HEADS) for j in range(0, A_GROUP, HEADS_PER_STACK)]
    if latent:
        outs = [o for st in stacks for o in run([st])]
    else:
        outs = run(stacks)
    o_ref[0] = jnp.concatenate(outs, axis=1).astype(o_ref.dtype)


def attention(q, k, v, sink, kc=None, vc=None, tq=TQ_ATT):
    B, L, _ = q.shape
    latent = kc is not None
    tq = min(tq, L)
    seq = lambda wd: pl.BlockSpec((1, L, wd), lambda b, i: (b, 0, 0))
    in_specs = [
        pl.BlockSpec(memory_space=pltpu.SMEM),
        pl.BlockSpec((1, tq, A_WIDTH), lambda b, i: (b, i, 0)),
        seq(LANES), seq(LANES),
    ]
    args = [sink, q, k, v]
    if latent:
        P = kc.shape[1]
        in_specs += [pl.BlockSpec((1, P, LANES), lambda b, i: (b, 0, 0))] * 2
        args += [kc, vc]
    return pl.pallas_call(
        functools.partial(_attn_body, latent=latent, seq_len=L),
        grid=(B, L // tq),
        in_specs=in_specs,
        out_specs=pl.BlockSpec((1, tq, A_WIDTH), lambda b, i: (b, i, 0)),
        out_shape=jax.ShapeDtypeStruct((B, L, A_WIDTH), BF16),
        compiler_params=_cparams("parallel", "arbitrary"),
        name="attention_latent" if latent else "attention_context",
    )(*args)


def _split3(x):
    hi = x.astype(BF16)
    r = x - hi.astype(F32)
    mid = r.astype(BF16)
    lo = (r - mid.astype(F32)).astype(BF16)
    return hi, mid, lo


def _gla_group(q_ref, k_ref, v_ref, la_ref, o_ref, st_ref, b, d, reverse, n_chunks):
    C = GLA_CHUNK
    W = C_WIDTH
    ti = lax.broadcasted_iota(jnp.int32, (C, C), 0)
    si = lax.broadcasted_iota(jnp.int32, (C, C), 1)
    tri = (si >= ti) if reverse else (si <= ti)
    tri_b = tri.astype(BF16)
    tri4 = jnp.concatenate([tri] * C_HEADS, axis=0)
    r4 = lax.broadcasted_iota(jnp.int32, (C_HEADS * C, W), 0) // C
    c4 = lax.broadcasted_iota(jnp.int32, (C_HEADS * C, W), 1) // C_DK
    same_head = r4 == c4
    nt = (((1,), (1,)), ((), ()))
    tn = (((0,), (0,)), ((), ()))
    chunks = range(n_chunks)
    rows = [pl.ds(c * C, C) for c in chunks]
    vbs = [v_ref[b, r, :].astype(BF16) for r in rows]
    parts = [_split3(la_ref[b, r, :]) for r in rows]
    bsums = [jnp.dot(tri_b, hi, preferred_element_type=F32) + jnp.dot(tri_b, mid, preferred_element_type=F32)
             + jnp.dot(tri_b, lo, preferred_element_type=F32) for hi, mid, lo in parts]
    b_lasts = [s[0:1] if reverse else s[C - 1:C] for s in bsums]
    qgs = [q_ref[b, r, :] * jnp.exp(s) for r, s in zip(rows, bsums)]
    kgs = [(k_ref[b, r, :] * jnp.exp(-s)).astype(BF16) for r, s in zip(rows, bsums)]
    kds = [(k_ref[b, r, :] * jnp.exp(bl - s)).astype(BF16) for r, s, bl in zip(rows, bsums, b_lasts)]
    decays = [jnp.exp(bl) for bl in b_lasts]
    q_bds = [jnp.where(same_head, jnp.concatenate([qg] * C_HEADS, axis=0), 0.0).astype(BF16) for qg in qgs]
    uts = [jnp.where(same_head, lax.dot_general(vb, kd, tn, preferred_element_type=F32), 0.0)
           for vb, kd in zip(vbs, kds)]
    a_s = [jnp.where(tri4, lax.dot_general(qb, kg, nt, preferred_element_type=F32), 0.0).astype(BF16)
           for qb, kg in zip(q_bds, kgs)]
    r_s = [jnp.where(same_head, jnp.dot(a, vb, preferred_element_type=F32), 0.0)
           for a, vb in zip(a_s, vbs)]
    o_intra = [sum([r[h * C:(h + 1) * C] for h in range(1, C_HEADS)], r[0:C]) for r in r_s]
    qgb = [qg.astype(BF16) for qg in qgs]
    st = st_ref[d]
    for c in (reversed(chunks) if reverse else chunks):
        o_ref[b, rows[c], :] = o_intra[c] + lax.dot_general(qgb[c], st.astype(BF16), nt,
                                                            preferred_element_type=F32)
        st = st * decays[c] + uts[c]
    st_ref[d] = st


def _gla_body(*refs, has_state, n_chunks):
    if has_state:
        (qf, kf, vf, lf, qb, kb, vb, lb, s0_ref, of_ref, ob_ref, sT_ref, st) = refs
    else:
        (qf, kf, vf, lf, qb, kb, vb, lb, of_ref, ob_ref, sT_ref, st) = refs
    j = pl.program_id(1)

    nb = qf.shape[0]

    @pl.when(j == 0)
    def _init():
        if has_state:
            st[...] = s0_ref[...].reshape(st.shape)
        else:
            st[...] = jnp.zeros_like(st)

    for b in range(nb):
        _gla_group(qf, kf, vf, lf, of_ref, st, b, 2 * b, False, n_chunks)
        _gla_group(qb, kb, vb, lb, ob_ref, st, b, 2 * b + 1, True, n_chunks)

    @pl.when(j == pl.num_programs(1) - 1)
    def _final():
        sT_ref[...] = st[...].reshape(sT_ref.shape)


NB_GLA = 1


def gla(cq, ck, cv, la, s0=None, rows=512, nb=NB_GLA):
    B, L, W = cq.shape
    assert B % nb == 0
    rows = min(rows, L)
    ng = L // rows
    has_state = s0 is not None
    fwd = lambda: pl.BlockSpec((nb, rows, W), lambda b, j: (b, j, 0))
    bwd = lambda: pl.BlockSpec((nb, rows, W), lambda b, j: (b, ng - 1 - j, 0))
    state = lambda: pl.BlockSpec((nb, 2, W, W), lambda b, j: (b, 0, 0, 0))
    in_specs = [fwd(), fwd(), fwd(), pl.BlockSpec((nb, rows, W), lambda b, j: (b, j, 0)),
                bwd(), bwd(), bwd(), pl.BlockSpec((nb, rows, W), lambda b, j: (b, ng - 1 - j, 1))]
    args = [cq, ck, cv, la, cq, ck, cv, la]
    if has_state:
        in_specs.append(state())
        args.append(s0)
    return pl.pallas_call(
        functools.partial(_gla_body, has_state=has_state, n_chunks=rows // GLA_CHUNK),
        grid=(B // nb, ng),
        in_specs=in_specs,
        out_specs=[fwd(), bwd(), state()],
        out_shape=[jax.ShapeDtypeStruct((B, L, W), F32), jax.ShapeDtypeStruct((B, L, W), F32),
                   jax.ShapeDtypeStruct((B, 2, W, W), F32)],
        scratch_shapes=[pltpu.VMEM((2 * nb, W, W), F32)],
        compiler_params=_cparams("parallel", "arbitrary"),
        name="gla",
    )(*args)


TM_DFT = 512
BG_HALF = 2


def _freq_weight(i, tm, n):
    k = i * tm + lax.broadcasted_iota(jnp.int32, (tm, 1), 0)
    return jnp.where(k == 0, 1.0 / n, 2.0 / n).astype(F32)


def hyena_time_filters(L, lp):
    t = jnp.arange(L, dtype=F32)
    t_norm = t / max(L - 1, 1)
    w = (2.0 * math.pi / L) * t
    f = jnp.linspace(1e-4, HY_BANDS - 1, HY_BANDS, dtype=F32)
    fw = w[:, None] * f[None, :]
    feat = jnp.concatenate([t_norm[:, None], jnp.cos(fw), -jnp.sin(fw)], axis=-1)
    z = jnp.sin(lp['hy_freq1'] * (feat @ lp['hy_w1'] + lp['hy_b1']))
    z = jnp.sin(lp['hy_freq2'] * (z @ lp['hy_w2'] + lp['hy_b2']))
    hf = (z @ lp['hy_w3']).astype(F32).reshape(L, 2, HY_ORDER, HY_CH)
    hf = hf * jnp.exp(-t_norm[:, None, None, None] * jnp.abs(lp['hy_decay'].astype(F32)))
    return hf / (jnp.sum(jnp.abs(hf), axis=(0, 1), keepdims=True) + EPS)


def _hi_lo(x):
    hi = x.astype(BF16)
    return jnp.stack([hi, (x - hi.astype(F32)).astype(BF16)])


def _shift_rows(x, down):
    n = x.shape[0]
    row = lax.broadcasted_iota(jnp.int32, (n, 1), 0)
    if down:
        return jnp.where(row == 0, 0.0, pltpu.roll(x, 1, 0))
    return jnp.where(row == n - 1, 0.0, pltpu.roll(x, n - 1, 0))


def _conv3_planes_body(x_ref, w_ref, b_ref, u_ref, ub_ref):
    H = x_ref.shape[0] // 2
    e = x_ref[pl.ds(0, H, stride=2), :]
    o = x_ref[pl.ds(1, H, stride=2), :]
    w0, w1, w2 = w_ref[0:1, :], w_ref[1:2, :], w_ref[2:3, :]
    ue = _shift_rows(o, True) * w0 + e * w1 + o * w2 + b_ref[...]
    uo = e * w0 + o * w1 + _shift_rows(e, False) * w2 + b_ref[...]
    u_ref[0, 0] = ue
    u_ref[0, 1] = uo
    ub_ref[0, 0] = ue.astype(BF16)
    ub_ref[0, 1] = uo.astype(BF16)


def hyena_conv3_planes(hy, w, b):
    B, L, C3 = hy.shape
    H = L // 2
    out = pl.BlockSpec((1, 2, H, LANES), lambda b_, j: (b_, 0, 0, j))
    return pl.pallas_call(
        _conv3_planes_body,
        grid=(B, C3 // LANES),
        in_specs=[pl.BlockSpec((None, L, LANES), lambda b_, j: (b_, 0, j)),
                  pl.BlockSpec((SUBLANES, LANES), lambda b_, j: (0, j)),
                  pl.BlockSpec((1, LANES), lambda b_, j: (0, j))],
        out_specs=[out, out],
        out_shape=[jax.ShapeDtypeStruct((B, 2, H, C3), F32), jax.ShapeDtypeStruct((B, 2, H, C3), BF16)],
        compiler_params=_cparams("parallel", "parallel"),
        name="hyena_conv3",
    )(hy, w, b)


def _half_filt_body(fe_ref, fo_ref, sg_ref, hse_ref, hso_ref, hde_ref, hdo_ref,
                    lr_ref, li_ref, hr_ref, hi_ref, sp_ref, *, n):
    i = pl.program_id(0)
    tm = lr_ref.shape[0]
    wk = _freq_weight(i, tm, n)

    def two_pass(m, parts_ref):
        return (jnp.dot(m, parts_ref[0], preferred_element_type=F32)
                + jnp.dot(m, parts_ref[1], preferred_element_type=F32))
    a = two_pass(fe_ref[0, :tm], hse_ref)
    b = two_pass(fo_ref[0, :tm], hso_ref)
    c = two_pass(fe_ref[0, tm:], hde_ref)
    d = two_pass(fo_ref[0, tm:], hdo_ref)
    lr_ref[...] = wk * (a + b)
    hr_ref[...] = wk * (a - b)
    li_ref[...] = wk * (c + d)
    hi_ref[...] = wk * (d - c)

    @pl.when(i == 0)
    def _middle_bin():
        sr = two_pass(sg_ref[...], hse_ref)[0:1, :]
        si = -two_pass(sg_ref[...], hdo_ref)[0:1, :]
        row = lax.broadcasted_iota(jnp.int32, sp_ref.shape, 0)
        sp_ref[...] = jnp.where(row == 0, sr, jnp.where(row == 1, si, 0.0)) * (2.0 / n)


def hyena_half_filter_spectrum(fe, fo, sg, hs, hd, tm):
    H = fe.shape[2]
    W = hs.shape[2]
    planes = lambda h: (h.reshape(2, H, 2, W)[:, :, 0], h.reshape(2, H, 2, W)[:, :, 1])
    hse, hso = planes(hs)
    hde, hdo = planes(hd)
    whole = lambda shape: pl.BlockSpec(shape, lambda i: (0,) * len(shape))
    ftile = pl.BlockSpec((1, 2 * tm, H), lambda i: (i, 0, 0))
    otile = pl.BlockSpec((tm, W), lambda i: (i, 0))
    out = jax.ShapeDtypeStruct((H, W), F32)
    return pl.pallas_call(
        functools.partial(_half_filt_body, n=4 * H),
        grid=(H // tm,),
        in_specs=[ftile, ftile, whole((SUBLANES, H))] + [whole((2, H, W))] * 4,
        out_specs=[otile, otile, otile, otile, whole((SUBLANES, W))],
        out_shape=[out, out, out, out, jax.ShapeDtypeStruct((SUBLANES, W), F32)],
        compiler_params=_cparams("arbitrary"),
        name="hyena_filter_spectrum",
    )(fe, fo, sg, hse, hso, hde, hdo)


def _cmul(xr, xi, hr, hi):
    return xr * hr - xi * hi, xr * hi + xi * hr


def _half_fwd_body(fe_ref, fo_ref, sg_ref, ze_ref, zo_ref, lr_ref, li_ref, hr_ref, hi_ref, sp_ref,
                   per_ref, pei_ref, por_ref, poi_ref, ysp_ref):
    nb = ze_ref.shape[0]
    tm = lr_ref.shape[0]

    def body(b, carry):
        a = jnp.dot(fe_ref[0], ze_ref[b], preferred_element_type=F32)
        o = jnp.dot(fo_ref[0], zo_ref[b], preferred_element_type=F32)
        ylr, yli = _cmul(a[:tm] + o[:tm], a[tm:] + o[tm:], lr_ref[...], li_ref[...])
        yhr, yhi = _cmul(a[:tm] - o[:tm], o[tm:] - a[tm:], hr_ref[...], hi_ref[...])
        per_ref[b] = (ylr + yhr).astype(BF16)
        pei_ref[b] = (yli - yhi).astype(BF16)
        por_ref[b] = (ylr - yhr).astype(BF16)
        poi_ref[b] = (yli + yhi).astype(BF16)
        return carry
    lax.fori_loop(0, nb, body, 0)

    @pl.when(pl.program_id(1) == 0)
    def _middle_bin():
        def mid(b, carry):
            xr = jnp.dot(sg_ref[...], ze_ref[b], preferred_element_type=F32)[0:1, :]
            xi = -jnp.dot(sg_ref[...], zo_ref[b], preferred_element_type=F32)[0:1, :]
            yr, yi = _cmul(xr, xi, sp_ref[0:1, :], sp_ref[1:2, :])
            row = lax.broadcasted_iota(jnp.int32, ysp_ref.shape[1:], 0)
            ysp_ref[b] = jnp.where(row == 0, yr, jnp.where(row == 1, yi, 0.0))
            return carry
        lax.fori_loop(0, nb, mid, 0)


def hyena_half_fwd(fe, fo, sg, zb, zcol, spec, order, tm, bg):
    B, _, H, _ = zb.shape
    C = HY_CH
    lr, li, hr, hi, sp = spec
    ftile = pl.BlockSpec((1, 2 * tm, H), lambda g, i: (i, 0, 0))
    plane = lambda p: pl.BlockSpec((bg, None, H, C), lambda g, i: (g, p, 0, zcol))
    stile = pl.BlockSpec((tm, C), lambda g, i: (i, order))
    ptile = pl.BlockSpec((bg, tm, C), lambda g, i: (g, i, 0))
    pshape = jax.ShapeDtypeStruct((B, H, C), BF16)
    return pl.pallas_call(
        _half_fwd_body,
        grid=(B // bg, H // tm),
        in_specs=[ftile, ftile, pl.BlockSpec((SUBLANES, H), lambda g, i: (0, 0)), plane(0), plane(1),
                  stile, stile, stile, stile, pl.BlockSpec((SUBLANES, C), lambda g, i: (0, order))],
        out_specs=[ptile, ptile, ptile, ptile, pl.BlockSpec((bg, SUBLANES, C), lambda g, i: (g, 0, 0))],
        out_shape=[pshape, pshape, pshape, pshape, jax.ShapeDtypeStruct((B, SUBLANES, C), F32)],
        compiler_params=_cparams("parallel", "arbitrary"),
        name="hyena_dft_fwd",
    )(fe, fo, sg, zb, zb, lr, li, hr, hi, sp)


def _half_inv_body(fer_ref, fei_ref, fotr_ref, foti_ref, per_ref, pei_ref, por_ref, poi_ref, ysp_ref,
                   ze_ref, zo_ref, ge_ref, go_ref, d_ref, *out_refs, natural):
    nb = ze_ref.shape[0]
    tm = fer_ref.shape[0]
    u = pl.program_id(1) * tm + lax.broadcasted_iota(jnp.int32, (tm, 1), 0)
    sign = jnp.where((u & 1) == 0, 1.0, -1.0).astype(F32)
    d = d_ref[...]

    def body(b, carry):
        ce = (jnp.dot(fer_ref[...], per_ref[b], preferred_element_type=F32)
              + jnp.dot(fei_ref[...], pei_ref[b], preferred_element_type=F32) + sign * ysp_ref[b][0:1, :])
        co = (jnp.dot(fotr_ref[...], por_ref[b], preferred_element_type=F32)
              + jnp.dot(foti_ref[...], poi_ref[b], preferred_element_type=F32) - sign * ysp_ref[b][1:2, :])
        zne = ge_ref[b] * (ce + d * ze_ref[b])
        zno = go_ref[b] * (co + d * zo_ref[b])
        if natural:
            bi = jnp.asarray(b, jnp.int32)
            for c, o_ref in enumerate(out_refs):
                o_ref.at[bi][pl.ds(0, tm, stride=2), :] = zne[:, c * LANES:(c + 1) * LANES]
                o_ref.at[bi][pl.ds(1, tm, stride=2), :] = zno[:, c * LANES:(c + 1) * LANES]
        else:
            z_ref, zb_ref = out_refs
            z_ref[b, 0] = zne
            z_ref[b, 1] = zno
            zb_ref[b, 0] = zne.astype(BF16)
            zb_ref[b, 1] = zno.astype(BF16)
        return carry
    lax.fori_loop(0, nb, body, 0)


def hyena_half_inv(mats, ps, ysp, z, zcol, gate, gcol, d, tm, bg, natural=False):
    fer, fei, fotr, foti = mats
    per = ps[0]
    B, H, C = per.shape
    ftile = pl.BlockSpec((tm, H), lambda g, i: (i, 0))
    whole_p = lambda: pl.BlockSpec((bg, H, C), lambda g, i: (g, 0, 0), pipeline_mode=pl.Buffered(1))
    tile = lambda p, col: pl.BlockSpec((bg, None, tm, C), lambda g, i: (g, p, i, col))
    if natural:
        out_specs = [pl.BlockSpec((bg, 2 * tm, LANES), lambda g, i: (g, i, 0))] * (C // LANES)
        out_shape = [jax.ShapeDtypeStruct((B, 2 * H, LANES), F32)] * (C // LANES)
    else:
        out_specs = [pl.BlockSpec((bg, 2, tm, C), lambda g, i: (g, 0, i, 0))] * 2
        out_shape = [jax.ShapeDtypeStruct((B, 2, H, C), F32), jax.ShapeDtypeStruct((B, 2, H, C), BF16)]
    return pl.pallas_call(
        functools.partial(_half_inv_body, natural=natural),
        grid=(B // bg, H // tm),
        in_specs=[ftile, ftile, ftile, ftile, whole_p(), whole_p(), whole_p(), whole_p(),
                  pl.BlockSpec((bg, SUBLANES, C), lambda g, i: (g, 0, 0)),
                  tile(0, zcol), tile(1, zcol), tile(0, gcol), tile(1, gcol),
                  pl.BlockSpec((1, C), lambda g, i: (0, 0))],
        out_specs=out_specs,
        out_shape=out_shape,
        compiler_params=_cparams("parallel", "arbitrary"),
        name="hyena_dft_inv",
    )(fer, fei, fotr, foti, *ps, ysp, z, z, gate, gate, d)


def dft_half_matrices(L, tm):
    H = L // 2
    r = 1 << (max(H.bit_length() - 1, 0) // 2)
    u = jnp.arange(H, dtype=jnp.int32)

    hi = jnp.arange(H // r, dtype=jnp.int32) * r
    lo = jnp.arange(r, dtype=jnp.int32)

    def cos_msin(row_hi, row_lo, col):
        def table(k):
            ang = ((k[:, None] * col[None, :]) % (2 * L)).astype(F32) * (math.pi / L)
            return jnp.cos(ang), jnp.sin(ang)
        ca, sa = table(row_hi)
        cb, sb = table(row_lo)
        c = ca[:, None, :] * cb[None, :, :] - sa[:, None, :] * sb[None, :, :]
        s = sa[:, None, :] * cb[None, :, :] + ca[:, None, :] * sb[None, :, :]
        return c.reshape(H, H).astype(BF16), (-s).reshape(H, H).astype(BF16)
    fre, fie = cos_msin(hi, lo, 2 * u)
    fro, fio = cos_msin(hi, lo, 2 * u + 1)
    frot, fiot = cos_msin(2 * hi, 2 * lo + 1, u)
    stack = lambda a, b: jnp.concatenate([a.reshape(H // tm, tm, H), b.reshape(H // tm, tm, H)], axis=1)
    sg = jnp.zeros((SUBLANES, H), F32).at[0].set(jnp.where(u % 2 == 0, 1.0, -1.0)).astype(BF16)
    return dict(fe=stack(fre, fie), fo=stack(fro, fio), sg=sg, inv=(fre, fie, frot, fiot), tm=tm)


def hyena_half(hy, lp, mats):
    B, L, _ = hy.shape
    tm = mats['tm']
    bg = B if L < TM_DFT else BG_HALF
    hf = hyena_time_filters(L, lp)
    h_fwd = hf[:, 0].reshape(L, HY_ORDER * HY_CH)
    h_bwd = hf[:, 1].at[0].set(0.0).reshape(L, HY_ORDER * HY_CH)
    spec = hyena_half_filter_spectrum(mats['fe'], mats['fo'], mats['sg'], _hi_lo(h_fwd + h_bwd),
                                      _hi_lo(h_fwd - h_bwd), tm)
    w = jnp.pad(lp['hy_conv_w'], ((0, SUBLANES - 3), (0, 0)))
    u, ub = hyena_conv3_planes(hy, w, lp['hy_conv_b'][None, :])
    d = lp['hy_d'].astype(F32)
    z, zb, zcol = u, ub, 0
    for o in range(HY_ORDER):
        *ps, ysp = hyena_half_fwd(mats['fe'], mats['fo'], mats['sg'], zb, zcol, spec, o, tm, bg)
        last = o == HY_ORDER - 1
        out = hyena_half_inv(mats['inv'], ps, ysp, z, zcol, u, 1 + o, d[o][None, :], tm, bg, natural=last)
        if last:
            return out
        z, zb = out
        zcol = 0


def _ffn_body(x_ref, mod_ref, g_ref, w1_ref, w3_ref, w2_ref, o_ref):
    x = x_ref[...]
    h = _adaln_rows(x, g_ref[...], mod_ref[0, ROW_SHIFT2:ROW_SHIFT2 + 1, :],
                    mod_ref[0, ROW_SCALE2:ROW_SCALE2 + 1, :]).astype(BF16)
    a = jnp.dot(h, w1_ref[...], preferred_element_type=F32)
    b = jnp.dot(h, w3_ref[...], preferred_element_type=F32)
    act = (_silu(a) * b).astype(BF16)
    ff = jnp.dot(act, w2_ref[...], preferred_element_type=F32)
    o_ref[...] = x + mod_ref[0, ROW_GATE2:ROW_GATE2 + 1, :] * ff


def ffn_dense(x, mods, g, w1, w3, w2, seg_len, tm=TM_PROJ):
    T, D = x.shape
    F = w1.shape[1]
    resident = functools.partial(pl.BlockSpec, pipeline_mode=pl.Buffered(1))
    return pl.pallas_call(
        _ffn_body,
        grid=(T // tm,),
        in_specs=[
            pl.BlockSpec((tm, D), lambda i: (i, 0)),
            pl.BlockSpec((1, MOD_ROWS, D), lambda i: (i * tm // seg_len, 0, 0)),
            pl.BlockSpec((1, D), lambda i: (0, 0)),
            resident((D, F), lambda i: (0, 0)),
            resident((D, F), lambda i: (0, 0)),
            resident((F, D), lambda i: (0, 0)),
        ],
        out_specs=pl.BlockSpec((tm, D), lambda i: (i, 0)),
        out_shape=jax.ShapeDtypeStruct((T, D), F32),
        compiler_params=_cparams("parallel"),
        name="ffn_dense",
    )(x, mods, g, w1, w3, w2)


def _store_token_tiles(ref, x):
    n = x.shape[0]
    for s in range(SUBLANES):
        ref[pl.ds(s, n, stride=SUBLANES), :] = x[:, s * LANES:(s + 1) * LANES]


def _load_token_tiles(ref, n):
    return jnp.concatenate([ref[pl.ds(s, n, stride=SUBLANES), :] for s in range(SUBLANES)], axis=1)


def _router_body(xa_ref, xb_ref, mod_ref, g_ref, wr_ref, h_ref, idx_ref, gw_ref, *, na):
    x = jnp.where(pl.program_id(0) < na, xa_ref[...], xb_ref[...])
    h = _adaln_rows(x, g_ref[...], mod_ref[0, ROW_SHIFT2:ROW_SHIFT2 + 1, :],
                    mod_ref[0, ROW_SCALE2:ROW_SCALE2 + 1, :])
    _store_token_tiles(h_ref, h)
    logits = lax.dot_general(wr_ref[...], h, (((1,), (1,)), ((), ())),
                             precision=lax.Precision.HIGHEST, preferred_element_type=F32)
    eidx = lax.broadcasted_iota(jnp.int32, logits.shape, 0)
    m1 = jnp.max(logits, axis=0, keepdims=True)
    i1 = jnp.min(jnp.where(logits == m1, eidx, N_EXPERTS), axis=0, keepdims=True)
    rest = jnp.where(eidx == i1, -jnp.inf, logits)
    m2 = jnp.max(rest, axis=0, keepdims=True)
    i2 = jnp.min(jnp.where(rest == m2, eidx, N_EXPERTS), axis=0, keepdims=True)
    e2 = jnp.exp(m2 - m1)
    den = 1.0 + e2
    row = lax.broadcasted_iota(jnp.int32, logits.shape, 0)
    idx_ref[...] = jnp.where(row == 0, i1, jnp.where(row == 1, i2, 0))
    gw_ref[...] = jnp.transpose(jnp.where(row == 0, 1.0 / den, jnp.where(row == 1, e2 / den, 0.0)))


def _merged_seg(i, na, tm, seg_len_b):
    return jnp.where(i < na, 0, 1 + jnp.maximum(i - na, 0) * tm // seg_len_b)


def moe_router(xa, xb, mods, g, wr_t, seg_len_b, tm=TM_PROJ):
    D = xa.shape[1]
    na = xa.shape[0] // tm
    T = xa.shape[0] + xb.shape[0]
    return pl.pallas_call(
        functools.partial(_router_body, na=na),
        grid=(T // tm,),
        in_specs=[
            pl.BlockSpec((tm, D), lambda i: (jnp.minimum(i, na - 1), 0)),
            pl.BlockSpec((tm, D), lambda i: (jnp.maximum(i - na, 0), 0)),
            pl.BlockSpec((1, MOD_ROWS, D), lambda i: (_merged_seg(i, na, tm, seg_len_b), 0, 0)),
            pl.BlockSpec((1, D), lambda i: (0, 0)),
            pl.BlockSpec((N_EXPERTS, D), lambda i: (0, 0)),
        ],
        out_specs=[
            pl.BlockSpec((tm * SUBLANES, LANES), lambda i: (i, 0)),
            pl.BlockSpec((N_EXPERTS, tm), lambda i: (0, i)),
            pl.BlockSpec((tm, N_EXPERTS), lambda i: (i, 0)),
        ],
        out_shape=[
            jax.ShapeDtypeStruct((T * SUBLANES, LANES), F32),
            jax.ShapeDtypeStruct((N_EXPERTS, T), jnp.int32),
            jax.ShapeDtypeStruct((T, N_EXPERTS), F32),
        ],
        compiler_params=_cparams("parallel"),
        name="moe_router",
    )(xa, xb, mods, g, wr_t)


def _experts_body(te_ref, nv_ref, src0_ref, srcn_ref, dst_ref, h_hbm, w1_ref, w3_ref, w2_ref,
                  y_hbm, hrows, hb, acc, ybuf, sem_g, sem_s, *, tm, nf, n_slots):
    i = pl.program_id(0)
    f = pl.program_id(1)
    nv = nv_ref[0]
    valid = i < nv
    slot = lax.rem(i, 2)

    def gather_issue(idx_ref, s):
        def body(r, carry):
            src = pl.multiple_of(idx_ref[0, 0, r] * SUBLANES, SUBLANES)
            dst = pl.multiple_of(r * SUBLANES, SUBLANES)
            pltpu.make_async_copy(h_hbm.at[pl.ds(src, SUBLANES)], hrows.at[s, pl.ds(dst, SUBLANES)],
                                  sem_g.at[s]).start()
            return carry
        lax.fori_loop(0, tm, body, 0, unroll=8)

    def gather_wait(s):
        pltpu.make_async_copy(h_hbm.at[pl.ds(0, tm * SUBLANES)], hrows.at[s], sem_g.at[s]).wait()

    def scatter_wait(s):
        pltpu.make_async_copy(ybuf.at[s], y_hbm.at[pl.ds(0, tm * SUBLANES)], sem_s.at[s]).wait()

    @pl.when(jnp.logical_and(i == 0, f == 0))
    def _first():
        ybuf[0] = jnp.zeros(ybuf.shape[1:], ybuf.dtype)
        spare = lambda p: y_hbm.at[pl.ds((n_slots + p * tm) * SUBLANES, tm * SUBLANES)]
        for p in range(2):
            pltpu.make_async_copy(ybuf.at[0], spare(p), sem_s.at[p]).start()
        for p in range(2):
            pltpu.make_async_copy(ybuf.at[0], spare(p), sem_s.at[p]).wait()
        gather_issue(src0_ref, 0)

    @pl.when(jnp.logical_and(valid, f == 0))
    def _stage():
        gather_wait(slot)
        hb[...] = _load_token_tiles(hrows.at[slot], tm).astype(BF16)
        acc[...] = jnp.zeros_like(acc)

        @pl.when(i + 1 < nv)
        def _prefetch():
            gather_issue(srcn_ref, 1 - slot)

    @pl.when(valid)
    def _compute():
        h = hb[...]
        a = jnp.dot(h, w1_ref[0], preferred_element_type=F32)
        b = jnp.dot(h, w3_ref[0], preferred_element_type=F32)
        act = (_silu(a) * b).astype(BF16)
        acc[...] += jnp.dot(act, w2_ref[0], preferred_element_type=F32)

    @pl.when(jnp.logical_and(valid, f == nf - 1))
    def _emit():
        @pl.when(i >= 2)
        def _reuse():
            scatter_wait(slot)
        _store_token_tiles(ybuf.at[slot], acc[...])

        def body(r, carry):
            src = pl.multiple_of(r * SUBLANES, SUBLANES)
            dst = pl.multiple_of(dst_ref[0, 0, r] * SUBLANES, SUBLANES)
            pltpu.make_async_copy(ybuf.at[slot, pl.ds(src, SUBLANES)], y_hbm.at[pl.ds(dst, SUBLANES)],
                                  sem_s.at[slot]).start()
            return carry
        lax.fori_loop(0, tm, body, 0, unroll=8)

    @pl.when(jnp.logical_and(i == nv, f == 0))
    def _drain():
        scatter_wait(lax.rem(nv + 1, 2))

        @pl.when(nv >= 2)
        def _older():
            scatter_wait(lax.rem(nv, 2))


def moe_experts(h, tile_expert, n_valid, src_rows, dst_rows, w1, w3, w2, tm=TM_MOE, tf=TF_MOE):
    T = h.shape[0] // SUBLANES
    D = w1.shape[1]
    F = w1.shape[2]
    n_tiles = src_rows.shape[0]
    n_slots = TOP_K * T
    nf = F // tf

    def wcol(i, f, te, nv):
        return (te[i], 0, jnp.where(i < nv[0], f, nf - 1))

    def wrow(i, f, te, nv):
        return (te[i], jnp.where(i < nv[0], f, nf - 1), 0)

    smem_tile = lambda fn: pl.BlockSpec((1, 1, tm), fn, memory_space=pltpu.SMEM)
    grid_spec = pltpu.PrefetchScalarGridSpec(
        num_scalar_prefetch=2,
        grid=(n_tiles, nf),
        in_specs=[
            smem_tile(lambda i, f, te, nv: (0, 0, 0)),
            smem_tile(lambda i, f, te, nv: (jnp.minimum(i + 1, n_tiles - 1), 0, 0)),
            smem_tile(lambda i, f, te, nv: (i, 0, 0)),
            pl.BlockSpec(memory_space=pl.ANY),
            pl.BlockSpec((1, D, tf), wcol),
            pl.BlockSpec((1, D, tf), wcol),
            pl.BlockSpec((1, tf, D), wrow),
        ],
        out_specs=pl.BlockSpec(memory_space=pl.ANY),
        scratch_shapes=[
            pltpu.VMEM((2, tm * SUBLANES, LANES), F32),
            pltpu.VMEM((tm, D), BF16),
            pltpu.VMEM((tm, D), F32),
            pltpu.VMEM((2, tm * SUBLANES, LANES), F32),
            pltpu.SemaphoreType.DMA((2,)),
            pltpu.SemaphoreType.DMA((2,)),
        ],
    )
    return pl.pallas_call(
        functools.partial(_experts_body, tm=tm, nf=nf, n_slots=n_slots),
        grid_spec=grid_spec,
        out_shape=jax.ShapeDtypeStruct(((n_slots + 2 * tm) * SUBLANES, LANES), F32),
        compiler_params=_cparams("arbitrary", "arbitrary"),
        name="moe_experts",
    )(tile_expert, n_valid, src_rows, src_rows, dst_rows, h, w1, w3, w2)


def _combine_body(x_ref, mod_ref, gw_ref, y0_ref, y1_ref, o_ref):
    n = x_ref.shape[0]
    gw = gw_ref[...]
    y = gw[:, 0:1] * _load_token_tiles(y0_ref, n) + gw[:, 1:2] * _load_token_tiles(y1_ref, n)
    o_ref[...] = x_ref[...] + mod_ref[0, ROW_GATE2:ROW_GATE2 + 1, :] * y


def moe_combine(x, mods, gw, y, row0, n_tok, seg_len, tm=TM_COMB):
    T, D = x.shape
    off = row0 // tm
    return pl.pallas_call(
        _combine_body,
        grid=(T // tm,),
        in_specs=[
            pl.BlockSpec((tm, D), lambda i: (i, 0)),
            pl.BlockSpec((1, MOD_ROWS, D), lambda i: (i * tm // seg_len, 0, 0)),
            pl.BlockSpec((tm, N_EXPERTS), lambda i: (off + i, 0)),
            pl.BlockSpec((tm * SUBLANES, LANES), lambda i: (off + i, 0)),
            pl.BlockSpec((tm * SUBLANES, LANES), lambda i: (off + n_tok // tm + i, 0)),
        ],
        out_specs=pl.BlockSpec((tm, D), lambda i: (i, 0)),
        out_shape=jax.ShapeDtypeStruct((T, D), F32),
        compiler_params=_cparams("parallel"),
        name="moe_combine",
    )(x, mods, gw, y, y)


def moe_dispatch_plan(idx, tm=TM_MOE):
    T = idx.shape[1]
    n_slots = TOP_K * T
    n_tiles = n_slots // tm + N_EXPERTS
    n_rows = n_tiles * tm
    experts = jnp.arange(N_EXPERTS, dtype=jnp.int32)[None, :]
    e_flat = idx[:TOP_K].reshape(n_slots)
    counts = jnp.sum((e_flat[:, None] == experts).astype(jnp.int32), axis=0)
    padded = (counts + tm - 1) // tm * tm
    ends = jnp.cumsum(padded)
    offs = ends - padded
    order = jnp.argsort(e_flat, stable=True).astype(jnp.int32)
    cstart = jnp.cumsum(counts) - counts
    rows = jnp.arange(n_rows, dtype=jnp.int32)
    row_e = jnp.minimum(jnp.sum((rows[:, None] >= ends[None, :]).astype(jnp.int32), axis=1), N_EXPERTS - 1)
    pick = lambda table: jnp.sum(jnp.where(row_e[:, None] == experts, table[None, :], 0), axis=1)
    j = rows - pick(offs)
    live = j < pick(counts)
    slot = order[jnp.clip(pick(cstart) + j, 0, n_slots - 1)]
    spare = n_slots + (rows // tm) % 2 * tm + rows % tm
    src_rows = jnp.where(live, slot % T, 0).astype(jnp.int32).reshape(n_tiles, 1, tm)
    dst_rows = jnp.where(live, slot, spare).astype(jnp.int32).reshape(n_tiles, 1, tm)
    tile_start = jnp.arange(n_tiles, dtype=jnp.int32) * tm
    n_valid = (ends[-1] // tm).astype(jnp.int32).reshape(1)
    tile_expert = jnp.minimum(jnp.sum((tile_start[:, None] >= ends[None, :]).astype(jnp.int32), axis=1),
                              N_EXPERTS - 1)
    last_e = tile_expert[jnp.maximum(n_valid[0] - 1, 0)]
    tile_expert = jnp.where(tile_start < ends[-1], tile_expert, last_e).astype(jnp.int32)
    return tile_expert, n_valid, src_rows, dst_rows


def ffn_moe(xa, xb, mods_a, mods_b, g, wr_t, w1, w3, w2, seg_len_b, tm=TM_PROJ, tme=TM_MOE, tmc=TM_COMB,
            tf=TF_MOE):
    na, nb = xa.shape[0], xb.shape[0]
    mods = jnp.concatenate([mods_a, mods_b], axis=0)
    h, idx, gw = moe_router(xa, xb, mods, g, wr_t, seg_len_b, tm)
    tile_expert, n_valid, src_rows, dst_rows = moe_dispatch_plan(idx, tme)
    y = moe_experts(h, tile_expert, n_valid, src_rows, dst_rows, w1, w3, w2, tme, tf)
    return (moe_combine(xa, mods_a, gw, y, 0, na + nb, na, tmc),
            moe_combine(xb, mods_b, gw, y, na, na + nb, seg_len_b, tmc))


def axial_rope(L):
    rows = L // GRID_W
    r = jnp.repeat(jnp.arange(rows, dtype=F32), GRID_W)
    col = jnp.tile(jnp.arange(GRID_W, dtype=F32), rows)
    n = HEAD_DIM // 4
    freqs = ROPE_BASE ** (-jnp.arange(n, dtype=F32) / n)
    ang = jnp.concatenate([r[:, None] * freqs, col[:, None] * freqs], axis=-1)
    return jnp.cos(ang), jnp.sin(ang)


def rope_tables(L):
    cos, sin = axial_rope(L)
    n = HEAD_DIM // 4
    cos_h = jnp.concatenate([cos[:, :n], cos[:, :n], cos[:, n:], cos[:, n:]], axis=1)
    sin_h = jnp.concatenate([-sin[:, :n], sin[:, :n], -sin[:, n:], sin[:, n:]], axis=1)
    reps = LANES // HEAD_DIM
    return jnp.tile(cos_h, (1, reps)), jnp.tile(sin_h, (1, reps))


def block_diag_ones(width, block):
    i = jnp.arange(width) // block
    return (i[:, None] == i[None, :]).astype(BF16)


def gla_gate_params(gw, gb):
    w = jnp.zeros((LANES, 2 * C_WIDTH), F32)
    w = w.at[:GLA_RANK, :C_WIDTH].set(gw[0]).at[GLA_RANK:2 * GLA_RANK, C_WIDTH:].set(gw[1])
    return w.astype(BF16), jnp.concatenate([gb[0], gb[1]])[None, :]


def gla_state_to_blockdiag(st):
    B = st.shape[0]
    out = jnp.zeros((B, 2, C_HEADS, C_DV, C_HEADS, C_DK), F32)
    for h in range(C_HEADS):
        out = out.at[:, :, h, :, h, :].set(jnp.swapaxes(st[:, :, h], -1, -2))
    return out.reshape(B, 2, C_WIDTH, C_WIDTH)


def gla_state_from_blockdiag(sT):
    B = sT.shape[0]
    s6 = sT.reshape(B, 2, C_HEADS, C_DV, C_HEADS, C_DK)
    return jnp.stack([jnp.swapaxes(s6[:, :, h, :, h, :], -1, -2) for h in range(C_HEADS)], axis=2)


def mod_table(cvec, w_ada, b_ada):
    m = jax.nn.silu(cvec) @ w_ada + b_ada
    m = m.reshape(cvec.shape[0], 6, D_MODEL)
    return jnp.pad(m, ((0, 0), (0, MOD_ROWS - 6), (0, 0)))


def kernel(x_prompt, x_sample, cache_k, cache_v, state_gla, c, c_ctx, norm1_g, norm2_g, w_ada, b_ada,
           w_in, w_out, q_norm_g, k_norm_g, attn_sink, hy_conv_w, hy_conv_b, hy_w1, hy_b1, hy_freq1,
           hy_w2, hy_b2, hy_freq2, hy_w3, hy_decay, hy_d, gla_gate_w, gla_gate_b, gla_norm_g,
           ffn_w1, ffn_w3, ffn_w2, moe_router, moe_w1, moe_w3, moe_w2):
    D = D_MODEL
    xp = x_prompt.reshape(BATCH * SEQ, D)
    xs = x_sample.reshape(DEC_BATCH * DEC_SEQ, D)
    streams = [
        dict(x=xp, B=BATCH, L=SEQ, seg_len=BATCH * SEQ, cvec=c_ctx[None, :], latent=False),
        dict(x=xs, B=DEC_BATCH, L=DEC_SEQ, seg_len=DEC_SEQ, cvec=c, latent=True),
    ]
    ks_list, vs_list, st_list = [], [], []
    bd_q = block_diag_ones(A_WIDTH, HEAD_DIM)
    dft = {s['L']: dft_half_matrices(s['L'], min(TM_DFT, s['L'] // 2)) for s in streams}
    for l in range(DEPTH):
        lp = {
            'hy_conv_w': hy_conv_w[l], 'hy_conv_b': hy_conv_b[l],
            'hy_w1': hy_w1[l], 'hy_b1': hy_b1[l], 'hy_freq1': hy_freq1[l], 'hy_w2': hy_w2[l],
            'hy_b2': hy_b2[l], 'hy_freq2': hy_freq2[l], 'hy_w3': hy_w3[l], 'hy_decay': hy_decay[l],
            'hy_d': hy_d[l],
        }
        w_in_l = jnp.pad(w_in[l], ((0, 0), (0, D_PROJ_PAD - D_PROJ))).astype(BF16)
        w_out_l = w_out[l].astype(BF16)
        g1 = norm1_g[l][None, :]
        g2 = norm2_g[l][None, :]
        qg = jnp.tile(q_norm_g[l], A_HEADS)[None, :]
        kg = jnp.tile(k_norm_g[l], A_KV_HEADS)[None, :]
        gg = jnp.tile(gla_norm_g[l], C_HEADS)[None, :]
        gate_w, gate_b = gla_gate_params(gla_gate_w[l], gla_gate_b[l])
        j = l // 2
        for s in streams:
            B, L = s['B'], s['L']
            mods = mod_table(s['cvec'], w_ada[l], b_ada[l])
            rope = rope_tables(L) if s['latent'] else None
            q, k, v, hy, cq, ck, cv, cg, la = in_proj(s['x'], mods, g1, w_in_l, bd_q, qg, kg, gate_w, gate_b,
                                                      s['seg_len'], rope, L)
            seq = lambda t: t.reshape(B, L, t.shape[-1])
            if s['latent']:
                kc = cache_k[:, l].reshape(DEC_BATCH, PAST_LEN, LANES)
                vc = cache_v[:, l].reshape(DEC_BATCH, PAST_LEN, LANES)
                a_out = attention(seq(q), seq(k), seq(v), attn_sink[l], kc, vc)
                s0 = gla_state_to_blockdiag(state_gla[:, l])
                o_f, o_b, _ = gla(seq(cq), seq(ck), seq(cv), seq(la), s0)
            else:
                a_out = attention(seq(q), seq(k), seq(v), attn_sink[l])
                o_f, o_b, sT = gla(seq(cq), seq(ck), seq(cv), seq(la))
                ks_list.append(k.reshape(B, L, A_KV_HEADS, HEAD_DIM))
                vs_list.append(v.reshape(B, L, A_KV_HEADS, HEAD_DIM))
                st_list.append(gla_state_from_blockdiag(sT))
            h_out = hyena_half(seq(hy), lp, dft[L])
            flat = lambda t: t.reshape(B * L, t.shape[-1])
            x1 = out_proj(s['x'], flat(a_out), [flat(h) for h in h_out], flat(o_f), flat(o_b), cg, mods, w_out_l,
                          bd_q[:C_WIDTH, :C_WIDTH], gg, s['seg_len'])
            if l % 2 == 0:
                s['x'] = ffn_dense(x1, mods, g2, ffn_w1[j].astype(BF16), ffn_w3[j].astype(BF16),
                                   ffn_w2[j].astype(BF16), s['seg_len'])
            else:
                s['x'], s['mods'] = x1, mods
        if l % 2 == 1:
            sa, sb = streams
            sa['x'], sb['x'] = ffn_moe(sa['x'], sb['x'], sa['mods'], sb['mods'], g2, moe_router[j].T,
                                       moe_w1[j].astype(BF16), moe_w3[j].astype(BF16), moe_w2[j].astype(BF16),
                                       sb['seg_len'])
    y_prompt = streams[0]['x'].reshape(BATCH, SEQ, D)
    y_sample = streams[1]['x'].reshape(DEC_BATCH, DEC_SEQ, D)
    new_cache_k = jnp.stack(ks_list, axis=1)
    new_cache_v = jnp.stack(vs_list, axis=1)
    new_state_gla = jnp.stack(st_list, axis=1)
    return (y_prompt, y_sample, new_cache_k, new_cache_v, new_state_gla)
```

```python
import math
import functools
import jax
import jax.numpy as jnp
from jax import lax
from jax.experimental import pallas as pl
from jax.experimental.pallas import tpu as pltpu

D_MODEL = 1024
BATCH = 32
SEQ = 256
DEPTH = 2
DEC_BATCH = 8
DEC_SEQ = 4096
PAST_LEN = 512

GRID_W = 64
HEAD_DIM = 64
A_HEADS = 8
A_KV_HEADS = 2
A_GROUP = A_HEADS // A_KV_HEADS
A_WIDTH = A_HEADS * HEAD_DIM
WINDOW = 128
ROPE_BASE = 10000.0
HY_CH = 256
HY_ORDER = 2
HY_BANDS = 16
HY_EMB = 1 + 2 * HY_BANDS
HY_HID = 64
C_HEADS = 4
C_DK = 64
C_DV = 64
C_WIDTH = C_HEADS * C_DV
GLA_RANK = 16
GLA_TAU = 16.0
GLA_CHUNK = 64
D_FF = 2816
N_EXPERTS = 8
TOP_K = 2
D_FF_EXPERT = 3584

PROJ_SIZES = (A_WIDTH, A_KV_HEADS * HEAD_DIM, A_KV_HEADS * HEAD_DIM, 3 * HY_CH,
              C_HEADS * C_DK, C_HEADS * C_DK, C_WIDTH, C_WIDTH, 2 * GLA_RANK)
D_PROJ = int(sum(PROJ_SIZES))

F32 = jnp.float32
BF16 = jnp.bfloat16
ATT_SCALE = HEAD_DIM ** -0.5
NEG_INF = -1e30
EPS = 1e-6

LANES = 128
SUBLANES = 8
D_PROJ_PAD = -(-D_PROJ // LANES) * LANES
VMEM_LIMIT_BYTES = 56 * 1024 * 1024

MOD_ROWS = SUBLANES
ROW_SHIFT1, ROW_SCALE1, ROW_GATE1, ROW_SHIFT2, ROW_SCALE2, ROW_GATE2 = range(6)

TM_PROJ = 512
TM_MOE = 512
TF_MOE = 1792
TM_COMB = 512


def _cparams(*sem):
    return pltpu.CompilerParams(dimension_semantics=sem, vmem_limit_bytes=VMEM_LIMIT_BYTES)


def _adaln_rows(x, g, shift, scale):
    ms = jnp.mean(x * x, axis=-1, keepdims=True)
    return (x * lax.rsqrt(ms + EPS) * g) * (1.0 + scale) + shift


def _silu(a):
    return a * jax.nn.sigmoid(a)


def _group_mean_sq(x, ones_bd):
    sq = x * x
    hi = sq.astype(BF16)
    lo = (sq - hi.astype(F32)).astype(BF16)
    s = jnp.dot(hi, ones_bd, preferred_element_type=F32) + jnp.dot(lo, ones_bd, preferred_element_type=F32)
    return s * (1.0 / HEAD_DIM)


def _rope_rows(x, cos_t, sin_t):
    q4 = HEAD_DIM // 4
    lane = lax.broadcasted_iota(jnp.int32, x.shape, 1)
    partner = jnp.where((lane % (2 * q4)) < q4, pltpu.roll(x, LANES - q4, 1), pltpu.roll(x, q4, 1))
    return x * cos_t + partner * sin_t


def _log_sigmoid(x):
    return jnp.minimum(x, 0.0) - jnp.log(1.0 + jnp.exp(-jnp.abs(x)))


def _inproj_body(*refs, latent):
    if latent:
        (x_ref, mod_ref, g_ref, w_ref, bd_ref, qg_ref, kg_ref, gw_ref, gb_ref, cos_ref, sin_ref,
         q_ref, k_ref, v_ref, hy_ref, cq_ref, ck_ref, cv_ref, cg_ref, la_ref) = refs
    else:
        (x_ref, mod_ref, g_ref, w_ref, bd_ref, qg_ref, kg_ref, gw_ref, gb_ref,
         q_ref, k_ref, v_ref, hy_ref, cq_ref, ck_ref, cv_ref, cg_ref, la_ref) = refs
    h = _adaln_rows(x_ref[...], g_ref[...], mod_ref[0, ROW_SHIFT1:ROW_SHIFT1 + 1, :],
                    mod_ref[0, ROW_SCALE1:ROW_SCALE1 + 1, :])
    acc = jnp.dot(h.astype(BF16), w_ref[...], preferred_element_type=F32)
    o = 0
    q = acc[:, o:o + A_WIDTH]
    o += A_WIDTH
    k = acc[:, o:o + LANES]
    o += LANES
    v_ref[...] = acc[:, o:o + LANES]
    o += LANES
    hy_ref[...] = acc[:, o:o + 3 * HY_CH]
    o += 3 * HY_CH
    cq_ref[...] = acc[:, o:o + C_WIDTH] * (C_DK ** -0.5)
    o += C_WIDTH
    ck_ref[...] = acc[:, o:o + C_WIDTH]
    o += C_WIDTH
    cv_ref[...] = acc[:, o:o + C_WIDTH]
    o += C_WIDTH
    cg_ref[...] = acc[:, o:o + C_WIDTH]
    o += C_WIDTH
    r = acc[:, o:o + LANES]
    la_ref[...] = _log_sigmoid(jnp.dot(r.astype(BF16), gw_ref[...], preferred_element_type=F32)
                               + gb_ref[...]) * (1.0 / GLA_TAU)
    q = q * lax.rsqrt(_group_mean_sq(q, bd_ref[...]) + EPS) * qg_ref[...]
    k = k * lax.rsqrt(_group_mean_sq(k, bd_ref[0:LANES, 0:LANES]) + EPS) * kg_ref[...]
    if latent:
        cos_t = cos_ref[...]
        sin_t = sin_ref[...]
        q = jnp.concatenate([_rope_rows(q[:, j * LANES:(j + 1) * LANES], cos_t, sin_t)
                             for j in range(A_WIDTH // LANES)], axis=1)
        k = _rope_rows(k, cos_t, sin_t)
    q_ref[...] = (q * ATT_SCALE).astype(BF16)
    k_ref[...] = k


def in_proj(x, mods, g, w, bd, qg, kg, gw, gb, seg_len, rope=None, seq_len=None, tm=TM_PROJ):
    T, D = x.shape
    N = w.shape[1]
    latent = rope is not None
    const = lambda shape: pl.BlockSpec(shape, lambda i: (0,) * len(shape))
    in_specs = [
        pl.BlockSpec((tm, D), lambda i: (i, 0)),
        pl.BlockSpec((1, MOD_ROWS, D), lambda i: (i * tm // seg_len, 0, 0)),
        const((1, D)), const((D, N)), const((A_WIDTH, A_WIDTH)), const((1, A_WIDTH)), const((1, LANES)),
        const((LANES, 2 * C_WIDTH)), const((1, 2 * C_WIDTH)),
    ]
    args = [x, mods, g, w, bd, qg, kg, gw, gb]
    if latent:
        tiles_per_seq = seq_len // tm
        in_specs += [pl.BlockSpec((tm, LANES), lambda i: (i % tiles_per_seq, 0))] * 2
        args += list(rope)
    widths = [A_WIDTH, LANES, LANES, 3 * HY_CH, C_WIDTH, C_WIDTH, C_WIDTH, C_WIDTH, 2 * C_WIDTH]
    dtypes = [BF16] + [F32] * 8
    return pl.pallas_call(
        functools.partial(_inproj_body, latent=latent),
        grid=(T // tm,),
        in_specs=in_specs,
        out_specs=[pl.BlockSpec((tm, wd), lambda i: (i, 0)) for wd in widths],
        out_shape=[jax.ShapeDtypeStruct((T, wd), dt) for wd, dt in zip(widths, dtypes)],
        compiler_params=_cparams("parallel"),
        name="in_proj_latent" if latent else "in_proj_context",
    )(*args)


def _outproj_body(x_ref, a_ref, h0_ref, h1_ref, of_ref, ob_ref, cg_ref, mod_ref, w_ref, bd_ref, gg_ref, o_ref):
    o = of_ref[...] + ob_ref[...]
    g_out = o * lax.rsqrt(_group_mean_sq(o, bd_ref[...]) + EPS) * gg_ref[...] * _silu(cg_ref[...])
    h0 = A_WIDTH
    g0 = A_WIDTH + HY_CH
    mix = jnp.dot(a_ref[...], w_ref[0:h0, :], preferred_element_type=F32)
    h = jnp.concatenate([h0_ref[...], h1_ref[...]], axis=1)
    mix += jnp.dot(h.astype(BF16), w_ref[h0:g0, :], preferred_element_type=F32)
    mix += jnp.dot(g_out.astype(BF16), w_ref[g0:, :], preferred_element_type=F32)
    o_ref[...] = x_ref[...] + mod_ref[0, ROW_GATE1:ROW_GATE1 + 1, :] * mix


def out_proj(x, a, h, of, ob, cg, mods, w, bd, gg, seg_len, tm=TM_PROJ):
    T, D = x.shape
    row = lambda wd: pl.BlockSpec((tm, wd), lambda i: (i, 0))
    const = lambda shape: pl.BlockSpec(shape, lambda i: (0,) * len(shape))
    return pl.pallas_call(
        _outproj_body,
        grid=(T // tm,),
        in_specs=[
            row(D), row(A_WIDTH), row(LANES), row(LANES), row(C_WIDTH), row(C_WIDTH), row(C_WIDTH),
            pl.BlockSpec((1, MOD_ROWS, D), lambda i: (i * tm // seg_len, 0, 0)),
            const((D, D)), const((C_WIDTH, C_WIDTH)), const((1, C_WIDTH)),
        ],
        out_specs=row(D),
        out_shape=jax.ShapeDtypeStruct((T, D), F32),
        compiler_params=_cparams("parallel"),
        name="out_proj",
    )(x, a, h[0], h[1], of, ob, cg, mods, w, bd, gg)


TQ_ATT = 256
HEADS_PER_STACK = 2
KWIN_ATT = TQ_ATT + 2 * WINDOW


def _dup_heads(x, g):
    lane = lax.broadcasted_iota(jnp.int32, x.shape, 1)
    rolled = pltpu.roll(x, HEAD_DIM, 1)
    keep = (lane < HEAD_DIM) if g == 0 else (lane >= HEAD_DIM)
    return jnp.where(keep, x, rolled)


def _attn_body(*refs, latent, seq_len):
    if latent:
        sink_ref, q_ref, k_ref, v_ref, kc_ref, vc_ref, o_ref = refs
    else:
        sink_ref, q_ref, k_ref, v_ref, o_ref = refs
    tq = q_ref.shape[1]
    i = pl.program_id(1)
    q = q_ref[0]
    lane = lax.broadcasted_iota(jnp.int32, (tq, LANES), 1)
    low = lane < HEAD_DIM
    if latent:
        start = jnp.clip(i * tq - WINDOW, 0, seq_len - KWIN_ATT)
        start = pl.multiple_of(start, WINDOW)
        kl = k_ref[0, pl.ds(start, KWIN_ATT), :]
        vl = v_ref[0, pl.ds(start, KWIN_ATT), :]
        qpos = i * tq + lax.broadcasted_iota(jnp.int32, (tq, KWIN_ATT), 0)
        kpos = start + lax.broadcasted_iota(jnp.int32, (tq, KWIN_ATT), 1)
        bias = jnp.where(jnp.abs(qpos - kpos) <= WINDOW, 0.0, NEG_INF).astype(F32)
        bias = jnp.concatenate([bias] * A_GROUP, axis=0)
        kc = kc_ref[0]
        vc = vc_ref[0]
    else:
        kl = k_ref[0]
        vl = v_ref[0]
    nt = (((1,), (1,)), ((), ()))
    zero = jnp.zeros_like(q[:, :LANES])
    kls = [_dup_heads(kl, g).astype(BF16) for g in range(A_KV_HEADS)]
    vls = [_dup_heads(vl, g).astype(BF16) for g in range(A_KV_HEADS)]
    if latent:
        kcs = [_dup_heads(kc, g).astype(BF16) for g in range(A_KV_HEADS)]
        vcs = [_dup_heads(vc, g).astype(BF16) for g in range(A_KV_HEADS)]

    def run(stacks):
        gs = [g for g, _ in stacks]
        n = range(len(stacks))
        qs = [jnp.concatenate([jnp.where(low if h % 2 == 0 else jnp.logical_not(low),
                                         q[:, (h // 2) * LANES:(h // 2 + 1) * LANES], zero) for h in heads], axis=0)
              for _, heads in stacks]
        sinks = [jnp.concatenate([jnp.full((tq, 1), sink_ref[h], F32) for h in heads], axis=0)
                 for _, heads in stacks]
        s_loc = [lax.dot_general(qs[i], kls[gs[i]], nt, preferred_element_type=F32) for i in n]
        m = sinks
        if latent:
            s_loc = [s + bias[:s.shape[0]] for s in s_loc]
            s_ctx = [lax.dot_general(qs[i], kcs[gs[i]], nt, preferred_element_type=F32) for i in n]
            m = [jnp.maximum(m[i], jnp.max(s_ctx[i], axis=-1, keepdims=True)) for i in n]
        m = [jnp.maximum(m[i], jnp.max(s_loc[i], axis=-1, keepdims=True)) for i in n]
        p_loc = [jnp.exp(s_loc[i] - m[i]) for i in n]
        den = [jnp.exp(sinks[i] - m[i]) + jnp.sum(p_loc[i], axis=-1, keepdims=True) for i in n]
        acc = [jnp.dot(p_loc[i].astype(BF16), vls[gs[i]], preferred_element_type=F32) for i in n]
        if latent:
            p_ctx = [jnp.exp(s_ctx[i] - m[i]) for i in n]
            den = [den[i] + jnp.sum(p_ctx[i], axis=-1, keepdims=True) for i in n]
            acc = [acc[i] + jnp.dot(p_ctx[i].astype(BF16), vcs[gs[i]], preferred_element_type=F32) for i in n]
        outs = []
        for i in n:
            og = acc[i] / den[i]
            for jj in range(len(stacks[i][1]) // 2):
                outs.append(jnp.where(low, og[(2 * jj) * tq:(2 * jj + 1) * tq],
                                      og[(2 * jj + 1) * tq:(2 * jj + 2) * tq]))
        return outs

    stacks = [(g, range(g * A_GROUP + j, g * A_GROUP + j + HEADS_PER_STACK))
              for g in range(A_KV_HEADS) for j in range(0, A_GROUP, HEADS_PER_STACK)]
    if latent:
        outs = [o for st in stacks for o in run([st])]
    else:
        outs = run(stacks)
    o_ref[0] = jnp.concatenate(outs, axis=1).astype(o_ref.dtype)


def attention(q, k, v, sink, kc=None, vc=None, tq=TQ_ATT):
    B, L, _ = q.shape
    latent = kc is not None
    tq = min(tq, L)
    seq = lambda wd: pl.BlockSpec((1, L, wd), lambda b, i: (b, 0, 0))
    in_specs = [
        pl.BlockSpec(memory_space=pltpu.SMEM),
        pl.BlockSpec((1, tq, A_WIDTH), lambda b, i: (b, i, 0)),
        seq(LANES), seq(LANES),
    ]
    args = [sink, q, k, v]
    if latent:
        P = kc.shape[1]
        in_specs += [pl.BlockSpec((1, P, LANES), lambda b, i: (b, 0, 0))] * 2
        args += [kc, vc]
    return pl.pallas_call(
        functools.partial(_attn_body, latent=latent, seq_len=L),
        grid=(B, L // tq),
        in_specs=in_specs,
        out_specs=pl.BlockSpec((1, tq, A_WIDTH), lambda b, i: (b, i, 0)),
        out_shape=jax.ShapeDtypeStruct((B, L, A_WIDTH), BF16),
        compiler_params=_cparams("parallel", "arbitrary"),
        name="attention_latent" if latent else "attention_context",
    )(*args)


def _split3(x):
    hi = x.astype(BF16)
    r = x - hi.astype(F32)
    mid = r.astype(BF16)
    lo = (r - mid.astype(F32)).astype(BF16)
    return hi, mid, lo


def _gla_group(q_ref, k_ref, v_ref, la_ref, o_ref, st_ref, b, d, reverse, n_chunks):
    C = GLA_CHUNK
    W = C_WIDTH
    ti = lax.broadcasted_iota(jnp.int32, (C, C), 0)
    si = lax.broadcasted_iota(jnp.int32, (C, C), 1)
    tri = (si >= ti) if reverse else (si <= ti)
    tri_b = tri.astype(BF16)
    tri4 = jnp.concatenate([tri] * C_HEADS, axis=0)
    r4 = lax.broadcasted_iota(jnp.int32, (C_HEADS * C, W), 0) // C
    c4 = lax.broadcasted_iota(jnp.int32, (C_HEADS * C, W), 1) // C_DK
    same_head = r4 == c4
    nt = (((1,), (1,)), ((), ()))
    tn = (((0,), (0,)), ((), ()))
    chunks = range(n_chunks)
    rows = [pl.ds(c * C, C) for c in chunks]
    vbs = [v_ref[b, r, :].astype(BF16) for r in rows]
    parts = [_split3(la_ref[b, r, :]) for r in rows]
    bsums = [jnp.dot(tri_b, hi, preferred_element_type=F32) + jnp.dot(tri_b, mid, preferred_element_type=F32)
             + jnp.dot(tri_b, lo, preferred_element_type=F32) for hi, mid, lo in parts]
    b_lasts = [s[0:1] if reverse else s[C - 1:C] for s in bsums]
    qgs = [q_ref[b, r, :] * jnp.exp(s) for r, s in zip(rows, bsums)]
    kgs = [(k_ref[b, r, :] * jnp.exp(-s)).astype(BF16) for r, s in zip(rows, bsums)]
    kds = [(k_ref[b, r, :] * jnp.exp(bl - s)).astype(BF16) for r, s, bl in zip(rows, bsums, b_lasts)]
    decays = [jnp.exp(bl) for bl in b_lasts]
    q_bds = [jnp.where(same_head, jnp.concatenate([qg] * C_HEADS, axis=0), 0.0).astype(BF16) for qg in qgs]
    uts = [jnp.where(same_head, lax.dot_general(vb, kd, tn, preferred_element_type=F32), 0.0)
           for vb, kd in zip(vbs, kds)]
    a_s = [jnp.where(tri4, lax.dot_general(qb, kg, nt, preferred_element_type=F32), 0.0).astype(BF16)
           for qb, kg in zip(q_bds, kgs)]
    r_s = [jnp.where(same_head, jnp.dot(a, vb, preferred_element_type=F32), 0.0)
           for a, vb in zip(a_s, vbs)]
    o_intra = [sum([r[h * C:(h + 1) * C] for h in range(1, C_HEADS)], r[0:C]) for r in r_s]
    qgb = [qg.astype(BF16) for qg in qgs]
    st = st_ref[d]
    for c in (reversed(chunks) if reverse else chunks):
        o_ref[b, rows[c], :] = o_intra[c] + lax.dot_general(qgb[c], st.astype(BF16), nt,
                                                            preferred_element_type=F32)
        st = st * decays[c] + uts[c]
    st_ref[d] = st


def _gla_body(*refs, has_state, n_chunks):
    if has_state:
        (qf, kf, vf, lf, qb, kb, vb, lb, s0_ref, of_ref, ob_ref, sT_ref, st) = refs
    else:
        (qf, kf, vf, lf, qb, kb, vb, lb, of_ref, ob_ref, sT_ref, st) = refs
    j = pl.program_id(1)

    nb = qf.shape[0]

    @pl.when(j == 0)
    def _init():
        if has_state:
            st[...] = s0_ref[...].reshape(st.shape)
        else:
            st[...] = jnp.zeros_like(st)

    for b in range(nb):
        _gla_group(qf, kf, vf, lf, of_ref, st, b, 2 * b, False, n_chunks)
        _gla_group(qb, kb, vb, lb, ob_ref, st, b, 2 * b + 1, True, n_chunks)

    @pl.when(j == pl.num_programs(1) - 1)
    def _final():
        sT_ref[...] = st[...].reshape(sT_ref.shape)


NB_GLA = 1


def gla(cq, ck, cv, la, s0=None, rows=512, nb=NB_GLA):
    B, L, W = cq.shape
    assert B % nb == 0
    rows = min(rows, L)
    ng = L // rows
    has_state = s0 is not None
    fwd = lambda: pl.BlockSpec((nb, rows, W), lambda b, j: (b, j, 0))
    bwd = lambda: pl.BlockSpec((nb, rows, W), lambda b, j: (b, ng - 1 - j, 0))
    state = lambda: pl.BlockSpec((nb, 2, W, W), lambda b, j: (b, 0, 0, 0))
    in_specs = [fwd(), fwd(), fwd(), pl.BlockSpec((nb, rows, W), lambda b, j: (b, j, 0)),
                bwd(), bwd(), bwd(), pl.BlockSpec((nb, rows, W), lambda b, j: (b, ng - 1 - j, 1))]
    args = [cq, ck, cv, la, cq, ck, cv, la]
    if has_state:
        in_specs.append(state())
        args.append(s0)
    return pl.pallas_call(
        functools.partial(_gla_body, has_state=has_state, n_chunks=rows // GLA_CHUNK),
        grid=(B // nb, ng),
        in_specs=in_specs,
        out_specs=[fwd(), bwd(), state()],
        out_shape=[jax.ShapeDtypeStruct((B, L, W), F32), jax.ShapeDtypeStruct((B, L, W), F32),
                   jax.ShapeDtypeStruct((B, 2, W, W), F32)],
        scratch_shapes=[pltpu.VMEM((2 * nb, W, W), F32)],
        compiler_params=_cparams("parallel", "arbitrary"),
        name="gla",
    )(*args)


TM_DFT = 512
BG_HALF = 2


def _freq_weight(i, tm, n):
    k = i * tm + lax.broadcasted_iota(jnp.int32, (tm, 1), 0)
    return jnp.where(k == 0, 1.0 / n, 2.0 / n).astype(F32)


def hyena_time_filters(L, lp):
    t = jnp.arange(L, dtype=F32)
    t_norm = t / max(L - 1, 1)
    w = (2.0 * math.pi / L) * t
    f = jnp.linspace(1e-4, HY_BANDS - 1, HY_BANDS, dtype=F32)
    fw = w[:, None] * f[None, :]
    feat = jnp.concatenate([t_norm[:, None], jnp.cos(fw), -jnp.sin(fw)], axis=-1)
    z = jnp.sin(lp['hy_freq1'] * (feat @ lp['hy_w1'] + lp['hy_b1']))
    z = jnp.sin(lp['hy_freq2'] * (z @ lp['hy_w2'] + lp['hy_b2']))
    hf = (z @ lp['hy_w3']).astype(F32).reshape(L, 2, HY_ORDER, HY_CH)
    hf = hf * jnp.exp(-t_norm[:, None, None, None] * jnp.abs(lp['hy_decay'].astype(F32)))
    return hf / (jnp.sum(jnp.abs(hf), axis=(0, 1), keepdims=True) + EPS)


def _hi_lo(x):
    hi = x.astype(BF16)
    return jnp.stack([hi, (x - hi.astype(F32)).astype(BF16)])


def _shift_rows(x, down):
    n = x.shape[0]
    row = lax.broadcasted_iota(jnp.int32, (n, 1), 0)
    if down:
        return jnp.where(row == 0, 0.0, pltpu.roll(x, 1, 0))
    return jnp.where(row == n - 1, 0.0, pltpu.roll(x, n - 1, 0))


def _conv3_planes_body(x_ref, w_ref, b_ref, u_ref, ub_ref):
    H = x_ref.shape[0] // 2
    e = x_ref[pl.ds(0, H, stride=2), :]
    o = x_ref[pl.ds(1, H, stride=2), :]
    w0, w1, w2 = w_ref[0:1, :], w_ref[1:2, :], w_ref[2:3, :]
    ue = _shift_rows(o, True) * w0 + e * w1 + o * w2 + b_ref[...]
    uo = e * w0 + o * w1 + _shift_rows(e, False) * w2 + b_ref[...]
    u_ref[0, 0] = ue
    u_ref[0, 1] = uo
    ub_ref[0, 0] = ue.astype(BF16)
    ub_ref[0, 1] = uo.astype(BF16)


def hyena_conv3_planes(hy, w, b):
    B, L, C3 = hy.shape
    H = L // 2
    out = pl.BlockSpec((1, 2, H, LANES), lambda b_, j: (b_, 0, 0, j))
    return pl.pallas_call(
        _conv3_planes_body,
        grid=(B, C3 // LANES),
        in_specs=[pl.BlockSpec((None, L, LANES), lambda b_, j: (b_, 0, j)),
                  pl.BlockSpec((SUBLANES, LANES), lambda b_, j: (0, j)),
                  pl.BlockSpec((1, LANES), lambda b_, j: (0, j))],
        out_specs=[out, out],
        out_shape=[jax.ShapeDtypeStruct((B, 2, H, C3), F32), jax.ShapeDtypeStruct((B, 2, H, C3), BF16)],
        compiler_params=_cparams("parallel", "parallel"),
        name="hyena_conv3",
    )(hy, w, b)


def _half_filt_body(fe_ref, fo_ref, sg_ref, hse_ref, hso_ref, hde_ref, hdo_ref,
                    lr_ref, li_ref, hr_ref, hi_ref, sp_ref, *, n):
    i = pl.program_id(0)
    tm = lr_ref.shape[0]
    wk = _freq_weight(i, tm, n)

    def two_pass(m, parts_ref):
        return (jnp.dot(m, parts_ref[0], preferred_element_type=F32)
                + jnp.dot(m, parts_ref[1], preferred_element_type=F32))
    a = two_pass(fe_ref[0, :tm], hse_ref)
    b = two_pass(fo_ref[0, :tm], hso_ref)
    c = two_pass(fe_ref[0, tm:], hde_ref)
    d = two_pass(fo_ref[0, tm:], hdo_ref)
    lr_ref[...] = wk * (a + b)
    hr_ref[...] = wk * (a - b)
    li_ref[...] = wk * (c + d)
    hi_ref[...] = wk * (d - c)

    @pl.when(i == 0)
    def _middle_bin():
        sr = two_pass(sg_ref[...], hse_ref)[0:1, :]
        si = -two_pass(sg_ref[...], hdo_ref)[0:1, :]
        row = lax.broadcasted_iota(jnp.int32, sp_ref.shape, 0)
        sp_ref[...] = jnp.where(row == 0, sr, jnp.where(row == 1, si, 0.0)) * (2.0 / n)


def hyena_half_filter_spectrum(fe, fo, sg, hs, hd, tm):
    H = fe.shape[2]
    W = hs.shape[2]
    planes = lambda h: (h.reshape(2, H, 2, W)[:, :, 0], h.reshape(2, H, 2, W)[:, :, 1])
    hse, hso = planes(hs)
    hde, hdo = planes(hd)
    whole = lambda shape: pl.BlockSpec(shape, lambda i: (0,) * len(shape))
    ftile = pl.BlockSpec((1, 2 * tm, H), lambda i: (i, 0, 0))
    otile = pl.BlockSpec((tm, W), lambda i: (i, 0))
    out = jax.ShapeDtypeStruct((H, W), F32)
    return pl.pallas_call(
        functools.partial(_half_filt_body, n=4 * H),
        grid=(H // tm,),
        in_specs=[ftile, ftile, whole((SUBLANES, H))] + [whole((2, H, W))] * 4,
        out_specs=[otile, otile, otile, otile, whole((SUBLANES, W))],
        out_shape=[out, out, out, out, jax.ShapeDtypeStruct((SUBLANES, W), F32)],
        compiler_params=_cparams("arbitrary"),
        name="hyena_filter_spectrum",
    )(fe, fo, sg, hse, hso, hde, hdo)


def _cmul(xr, xi, hr, hi):
    return xr * hr - xi * hi, xr * hi + xi * hr


def _half_fwd_body(fe_ref, fo_ref, sg_ref, ze_ref, zo_ref, lr_ref, li_ref, hr_ref, hi_ref, sp_ref,
                   per_ref, pei_ref, por_ref, poi_ref, ysp_ref):
    nb = ze_ref.shape[0]
    tm = lr_ref.shape[0]

    def body(b, carry):
        a = jnp.dot(fe_ref[0], ze_ref[b], preferred_element_type=F32)
        o = jnp.dot(fo_ref[0], zo_ref[b], preferred_element_type=F32)
        ylr, yli = _cmul(a[:tm] + o[:tm], a[tm:] + o[tm:], lr_ref[...], li_ref[...])
        yhr, yhi = _cmul(a[:tm] - o[:tm], o[tm:] - a[tm:], hr_ref[...], hi_ref[...])
        per_ref[b] = (ylr + yhr).astype(BF16)
        pei_ref[b] = (yli - yhi).astype(BF16)
        por_ref[b] = (ylr - yhr).astype(BF16)
        poi_ref[b] = (yli + yhi).astype(BF16)
        return carry
    lax.fori_loop(0, nb, body, 0)

    @pl.when(pl.program_id(1) == 0)
    def _middle_bin():
        def mid(b, carry):
            xr = jnp.dot(sg_ref[...], ze_ref[b], preferred_element_type=F32)[0:1, :]
            xi = -jnp.dot(sg_ref[...], zo_ref[b], preferred_element_type=F32)[0:1, :]
            yr, yi = _cmul(xr, xi, sp_ref[0:1, :], sp_ref[1:2, :])
            row = lax.broadcasted_iota(jnp.int32, ysp_ref.shape[1:], 0)
            ysp_ref[b] = jnp.where(row == 0, yr, jnp.where(row == 1, yi, 0.0))
            return carry
        lax.fori_loop(0, nb, mid, 0)


def hyena_half_fwd(fe, fo, sg, zb, zcol, spec, order, tm, bg):
    B, _, H, _ = zb.shape
    C = HY_CH
    lr, li, hr, hi, sp = spec
    ftile = pl.BlockSpec((1, 2 * tm, H), lambda g, i: (i, 0, 0))
    plane = lambda p: pl.BlockSpec((bg, None, H, C), lambda g, i: (g, p, 0, zcol))
    stile = pl.BlockSpec((tm, C), lambda g, i: (i, order))
    ptile = pl.BlockSpec((bg, tm, C), lambda g, i: (g, i, 0))
    pshape = jax.ShapeDtypeStruct((B, H, C), BF16)
    return pl.pallas_call(
        _half_fwd_body,
        grid=(B // bg, H // tm),
        in_specs=[ftile, ftile, pl.BlockSpec((SUBLANES, H), lambda g, i: (0, 0)), plane(0), plane(1),
                  stile, stile, stile, stile, pl.BlockSpec((SUBLANES, C), lambda g, i: (0, order))],
        out_specs=[ptile, ptile, ptile, ptile, pl.BlockSpec((bg, SUBLANES, C), lambda g, i: (g, 0, 0))],
        out_shape=[pshape, pshape, pshape, pshape, jax.ShapeDtypeStruct((B, SUBLANES, C), F32)],
        compiler_params=_cparams("parallel", "arbitrary"),
        name="hyena_dft_fwd",
    )(fe, fo, sg, zb, zb, lr, li, hr, hi, sp)


def _half_inv_body(fer_ref, fei_ref, fotr_ref, foti_ref, per_ref, pei_ref, por_ref, poi_ref, ysp_ref,
                   ze_ref, zo_ref, ge_ref, go_ref, d_ref, *out_refs, natural):
    nb = ze_ref.shape[0]
    tm = fer_ref.shape[0]
    u = pl.program_id(1) * tm + lax.broadcasted_iota(jnp.int32, (tm, 1), 0)
    sign = jnp.where((u & 1) == 0, 1.0, -1.0).astype(F32)
    d = d_ref[...]

    def body(b, carry):
        ce = (jnp.dot(fer_ref[...], per_ref[b], preferred_element_type=F32)
              + jnp.dot(fei_ref[...], pei_ref[b], preferred_element_type=F32) + sign * ysp_ref[b][0:1, :])
        co = (jnp.dot(fotr_ref[...], por_ref[b], preferred_element_type=F32)
              + jnp.dot(foti_ref[...], poi_ref[b], preferred_element_type=F32) - sign * ysp_ref[b][1:2, :])
        zne = ge_ref[b] * (ce + d * ze_ref[b])
        zno = go_ref[b] * (co + d * zo_ref[b])
        if natural:
            bi = jnp.asarray(b, jnp.int32)
            for c, o_ref in enumerate(out_refs):
                o_ref.at[bi][pl.ds(0, tm, stride=2), :] = zne[:, c * LANES:(c + 1) * LANES]
                o_ref.at[bi][pl.ds(1, tm, stride=2), :] = zno[:, c * LANES:(c + 1) * LANES]
        else:
            z_ref, zb_ref = out_refs
            z_ref[b, 0] = zne
            z_ref[b, 1] = zno
            zb_ref[b, 0] = zne.astype(BF16)
            zb_ref[b, 1] = zno.astype(BF16)
        return carry
    lax.fori_loop(0, nb, body, 0)


def hyena_half_inv(mats, ps, ysp, z, zcol, gate, gcol, d, tm, bg, natural=False):
    fer, fei, fotr, foti = mats
    per = ps[0]
    B, H, C = per.shape
    ftile = pl.BlockSpec((tm, H), lambda g, i: (i, 0))
    whole_p = lambda: pl.BlockSpec((bg, H, C), lambda g, i: (g, 0, 0), pipeline_mode=pl.Buffered(1))
    tile = lambda p, col: pl.BlockSpec((bg, None, tm, C), lambda g, i: (g, p, i, col))
    if natural:
        out_specs = [pl.BlockSpec((bg, 2 * tm, LANES), lambda g, i: (g, i, 0))] * (C // LANES)
        out_shape = [jax.ShapeDtypeStruct((B, 2 * H, LANES), F32)] * (C // LANES)
    else:
        out_specs = [pl.BlockSpec((bg, 2, tm, C), lambda g, i: (g, 0, i, 0))] * 2
        out_shape = [jax.ShapeDtypeStruct((B, 2, H, C), F32), jax.ShapeDtypeStruct((B, 2, H, C), BF16)]
    return pl.pallas_call(
        functools.partial(_half_inv_body, natural=natural),
        grid=(B // bg, H // tm),
        in_specs=[ftile, ftile, ftile, ftile, whole_p(), whole_p(), whole_p(), whole_p(),
                  pl.BlockSpec((bg, SUBLANES, C), lambda g, i: (g, 0, 0)),
                  tile(0, zcol), tile(1, zcol), tile(0, gcol), tile(1, gcol),
                  pl.BlockSpec((1, C), lambda g, i: (0, 0))],
        out_specs=out_specs,
        out_shape=out_shape,
        compiler_params=_cparams("parallel", "arbitrary"),
        name="hyena_dft_inv",
    )(fer, fei, fotr, foti, *ps, ysp, z, z, gate, gate, d)


def dft_half_matrices(L, tm):
    H = L // 2
    r = 1 << (max(H.bit_length() - 1, 0) // 2)
    u = jnp.arange(H, dtype=jnp.int32)

    hi = jnp.arange(H // r, dtype=jnp.int32) * r
    lo = jnp.arange(r, dtype=jnp.int32)

    def cos_msin(row_hi, row_lo, col):
        def table(k):
            ang = ((k[:, None] * col[None, :]) % (2 * L)).astype(F32) * (math.pi / L)
            return jnp.cos(ang), jnp.sin(ang)
        ca, sa = table(row_hi)
        cb, sb = table(row_lo)
        c = ca[:, None, :] * cb[None, :, :] - sa[:, None, :] * sb[None, :, :]
        s = sa[:, None, :] * cb[None, :, :] + ca[:, None, :] * sb[None, :, :]
        return c.reshape(H, H).astype(BF16), (-s).reshape(H, H).astype(BF16)
    fre, fie = cos_msin(hi, lo, 2 * u)
    fro, fio = cos_msin(hi, lo, 2 * u + 1)
    frot, fiot = cos_msin(2 * hi, 2 * lo + 1, u)
    stack = lambda a, b: jnp.concatenate([a.reshape(H // tm, tm, H), b.reshape(H // tm, tm, H)], axis=1)
    sg = jnp.zeros((SUBLANES, H), F32).at[0].set(jnp.where(u % 2 == 0, 1.0, -1.0)).astype(BF16)
    return dict(fe=stack(fre, fie), fo=stack(fro, fio), sg=sg, inv=(fre, fie, frot, fiot), tm=tm)


def hyena_half(hy, lp, mats):
    B, L, _ = hy.shape
    tm = mats['tm']
    bg = B if L < TM_DFT else BG_HALF
    hf = hyena_time_filters(L, lp)
    h_fwd = hf[:, 0].reshape(L, HY_ORDER * HY_CH)
    h_bwd = hf[:, 1].at[0].set(0.0).reshape(L, HY_ORDER * HY_CH)
    spec = hyena_half_filter_spectrum(mats['fe'], mats['fo'], mats['sg'], _hi_lo(h_fwd + h_bwd),
                                      _hi_lo(h_fwd - h_bwd), tm)
    w = jnp.pad(lp['hy_conv_w'], ((0, SUBLANES - 3), (0, 0)))
    u, ub = hyena_conv3_planes(hy, w, lp['hy_conv_b'][None, :])
    d = lp['hy_d'].astype(F32)
    z, zb, zcol = u, ub, 0
    for o in range(HY_ORDER):
        *ps, ysp = hyena_half_fwd(mats['fe'], mats['fo'], mats['sg'], zb, zcol, spec, o, tm, bg)
        last = o == HY_ORDER - 1
        out = hyena_half_inv(mats['inv'], ps, ysp, z, zcol, u, 1 + o, d[o][None, :], tm, bg, natural=last)
        if last:
            return out
        z, zb = out
        zcol = 0


def _ffn_body(x_ref, mod_ref, g_ref, w1_ref, w3_ref, w2_ref, o_ref):
    x = x_ref[...]
    h = _adaln_rows(x, g_ref[...], mod_ref[0, ROW_SHIFT2:ROW_SHIFT2 + 1, :],
                    mod_ref[0, ROW_SCALE2:ROW_SCALE2 + 1, :]).astype(BF16)
    a = jnp.dot(h, w1_ref[...], preferred_element_type=F32)
    b = jnp.dot(h, w3_ref[...], preferred_element_type=F32)
    act = (_silu(a) * b).astype(BF16)
    ff = jnp.dot(act, w2_ref[...], preferred_element_type=F32)
    o_ref[...] = x + mod_ref[0, ROW_GATE2:ROW_GATE2 + 1, :] * ff


def ffn_dense(x, mods, g, w1, w3, w2, seg_len, tm=TM_PROJ):
    T, D = x.shape
    F = w1.shape[1]
    resident = functools.partial(pl.BlockSpec, pipeline_mode=pl.Buffered(1))
    return pl.pallas_call(
        _ffn_body,
        grid=(T // tm,),
        in_specs=[
            pl.BlockSpec((tm, D), lambda i: (i, 0)),
            pl.BlockSpec((1, MOD_ROWS, D), lambda i: (i * tm // seg_len, 0, 0)),
            pl.BlockSpec((1, D), lambda i: (0, 0)),
            resident((D, F), lambda i: (0, 0)),
            resident((D, F), lambda i: (0, 0)),
            resident((F, D), lambda i: (0, 0)),
        ],
        out_specs=pl.BlockSpec((tm, D), lambda i: (i, 0)),
        out_shape=jax.ShapeDtypeStruct((T, D), F32),
        compiler_params=_cparams("parallel"),
        name="ffn_dense",
    )(x, mods, g, w1, w3, w2)


def _store_token_tiles(ref, x):
    n = x.shape[0]
    for s in range(SUBLANES):
        ref[pl.ds(s, n, stride=SUBLANES), :] = x[:, s * LANES:(s + 1) * LANES]


def _load_token_tiles(ref, n):
    return jnp.concatenate([ref[pl.ds(s, n, stride=SUBLANES), :] for s in range(SUBLANES)], axis=1)


def _router_body(xa_ref, xb_ref, mod_ref, g_ref, wr_ref, h_ref, idx_ref, gw_ref, *, na):
    x = jnp.where(pl.program_id(0) < na, xa_ref[...], xb_ref[...])
    h = _adaln_rows(x, g_ref[...], mod_ref[0, ROW_SHIFT2:ROW_SHIFT2 + 1, :],
                    mod_ref[0, ROW_SCALE2:ROW_SCALE2 + 1, :])
    _store_token_tiles(h_ref, h)
    logits = lax.dot_general(wr_ref[...], h, (((1,), (1,)), ((), ())),
                             precision=lax.Precision.HIGHEST, preferred_element_type=F32)
    eidx = lax.broadcasted_iota(jnp.int32, logits.shape, 0)
    m1 = jnp.max(logits, axis=0, keepdims=True)
    i1 = jnp.min(jnp.where(logits == m1, eidx, N_EXPERTS), axis=0, keepdims=True)
    rest = jnp.where(eidx == i1, -jnp.inf, logits)
    m2 = jnp.max(rest, axis=0, keepdims=True)
    i2 = jnp.min(jnp.where(rest == m2, eidx, N_EXPERTS), axis=0, keepdims=True)
    e2 = jnp.exp(m2 - m1)
    den = 1.0 + e2
    row = lax.broadcasted_iota(jnp.int32, logits.shape, 0)
    idx_ref[...] = jnp.where(row == 0, i1, jnp.where(row == 1, i2, 0))
    gw_ref[...] = jnp.transpose(jnp.where(row == 0, 1.0 / den, jnp.where(row == 1, e2 / den, 0.0)))


def _merged_seg(i, na, tm, seg_len_b):
    return jnp.where(i < na, 0, 1 + jnp.maximum(i - na, 0) * tm // seg_len_b)


def moe_router(xa, xb, mods, g, wr_t, seg_len_b, tm=TM_PROJ):
    D = xa.shape[1]
    na = xa.shape[0] // tm
    T = xa.shape[0] + xb.shape[0]
    return pl.pallas_call(
        functools.partial(_router_body, na=na),
        grid=(T // tm,),
        in_specs=[
            pl.BlockSpec((tm, D), lambda i: (jnp.minimum(i, na - 1), 0)),
            pl.BlockSpec((tm, D), lambda i: (jnp.maximum(i - na, 0), 0)),
            pl.BlockSpec((1, MOD_ROWS, D), lambda i: (_merged_seg(i, na, tm, seg_len_b), 0, 0)),
            pl.BlockSpec((1, D), lambda i: (0, 0)),
            pl.BlockSpec((N_EXPERTS, D), lambda i: (0, 0)),
        ],
        out_specs=[
            pl.BlockSpec((tm * SUBLANES, LANES), lambda i: (i, 0)),
            pl.BlockSpec((N_EXPERTS, tm), lambda i: (0, i)),
            pl.BlockSpec((tm, N_EXPERTS), lambda i: (i, 0)),
        ],
        out_shape=[
            jax.ShapeDtypeStruct((T * SUBLANES, LANES), F32),
            jax.ShapeDtypeStruct((N_EXPERTS, T), jnp.int32),
            jax.ShapeDtypeStruct((T, N_EXPERTS), F32),
        ],
        compiler_params=_cparams("parallel"),
        name="moe_router",
    )(xa, xb, mods, g, wr_t)


def _experts_body(te_ref, nv_ref, src0_ref, srcn_ref, dst_ref, h_hbm, w1_ref, w3_ref, w2_ref,
                  y_hbm, hrows, hb, acc, ybuf, sem_g, sem_s, *, tm, nf, n_slots):
    i = pl.program_id(0)
    f = pl.program_id(1)
    nv = nv_ref[0]
    valid = i < nv
    slot = lax.rem(i, 2)

    def gather_issue(idx_ref, s):
        def body(r, carry):
            src = pl.multiple_of(idx_ref[0, 0, r] * SUBLANES, SUBLANES)
            dst = pl.multiple_of(r * SUBLANES, SUBLANES)
            pltpu.make_async_copy(h_hbm.at[pl.ds(src, SUBLANES)], hrows.at[s, pl.ds(dst, SUBLANES)],
                                  sem_g.at[s]).start()
            return carry
        lax.fori_loop(0, tm, body, 0, unroll=8)

    def gather_wait(s):
        pltpu.make_async_copy(h_hbm.at[pl.ds(0, tm * SUBLANES)], hrows.at[s], sem_g.at[s]).wait()

    def scatter_wait(s):
        pltpu.make_async_copy(ybuf.at[s], y_hbm.at[pl.ds(0, tm * SUBLANES)], sem_s.at[s]).wait()

    @pl.when(jnp.logical_and(i == 0, f == 0))
    def _first():
        ybuf[0] = jnp.zeros(ybuf.shape[1:], ybuf.dtype)
        spare = lambda p: y_hbm.at[pl.ds((n_slots + p * tm) * SUBLANES, tm * SUBLANES)]
        for p in range(2):
            pltpu.make_async_copy(ybuf.at[0], spare(p), sem_s.at[p]).start()
        for p in range(2):
            pltpu.make_async_copy(ybuf.at[0], spare(p), sem_s.at[p]).wait()
        gather_issue(src0_ref, 0)

    @pl.when(jnp.logical_and(valid, f == 0))
    def _stage():
        gather_wait(slot)
        hb[...] = _load_token_tiles(hrows.at[slot], tm).astype(BF16)

        @pl.when(i + 1 < nv)
        def _prefetch():
            gather_issue(srcn_ref, 1 - slot)

    def partial_ffn():
        h = hb[...]
        a = jnp.dot(h, w1_ref[0], preferred_element_type=F32)
        b = jnp.dot(h, w3_ref[0], preferred_element_type=F32)
        act = (_silu(a) * b).astype(BF16)
        return jnp.dot(act, w2_ref[0], preferred_element_type=F32)

    @pl.when(jnp.logical_and(valid, f == 0))
    def _first_slice():
        acc[...] = partial_ffn()

    @pl.when(jnp.logical_and(valid, jnp.logical_and(f > 0, f < nf - 1)))
    def _middle_slice():
        acc[...] += partial_ffn()

    @pl.when(jnp.logical_and(valid, f == nf - 1))
    def _emit():
        y = acc[...] + partial_ffn()

        @pl.when(i >= 2)
        def _reuse():
            scatter_wait(slot)
        _store_token_tiles(ybuf.at[slot], y)

        def body(r, carry):
            src = pl.multiple_of(r * SUBLANES, SUBLANES)
            dst = pl.multiple_of(dst_ref[0, 0, r] * SUBLANES, SUBLANES)
            pltpu.make_async_copy(ybuf.at[slot, pl.ds(src, SUBLANES)], y_hbm.at[pl.ds(dst, SUBLANES)],
                                  sem_s.at[slot]).start()
            return carry
        lax.fori_loop(0, tm, body, 0, unroll=8)

    @pl.when(jnp.logical_and(i == nv, f == 0))
    def _drain():
        scatter_wait(lax.rem(nv + 1, 2))

        @pl.when(nv >= 2)
        def _older():
            scatter_wait(lax.rem(nv, 2))


def moe_experts(h, tile_expert, n_valid, src_rows, dst_rows, w1, w3, w2, tm=TM_MOE, tf=TF_MOE):
    T = h.shape[0] // SUBLANES
    D = w1.shape[1]
    F = w1.shape[2]
    n_tiles = src_rows.shape[0]
    n_slots = TOP_K * T
    nf = F // tf

    def wcol(i, f, te, nv):
        return (te[i], 0, jnp.where(i < nv[0], f, nf - 1))

    def wrow(i, f, te, nv):
        return (te[i], jnp.where(i < nv[0], f, nf - 1), 0)

    smem_tile = lambda fn: pl.BlockSpec((1, 1, tm), fn, memory_space=pltpu.SMEM)
    grid_spec = pltpu.PrefetchScalarGridSpec(
        num_scalar_prefetch=2,
        grid=(n_tiles, nf),
        in_specs=[
            smem_tile(lambda i, f, te, nv: (0, 0, 0)),
            smem_tile(lambda i, f, te, nv: (jnp.minimum(i + 1, n_tiles - 1), 0, 0)),
            smem_tile(lambda i, f, te, nv: (i, 0, 0)),
            pl.BlockSpec(memory_space=pl.ANY),
            pl.BlockSpec((1, D, tf), wcol),
            pl.BlockSpec((1, D, tf), wcol),
            pl.BlockSpec((1, tf, D), wrow),
        ],
        out_specs=pl.BlockSpec(memory_space=pl.ANY),
        scratch_shapes=[
            pltpu.VMEM((2, tm * SUBLANES, LANES), F32),
            pltpu.VMEM((tm, D), BF16),
            pltpu.VMEM((tm, D), F32),
            pltpu.VMEM((2, tm * SUBLANES, LANES), F32),
            pltpu.SemaphoreType.DMA((2,)),
            pltpu.SemaphoreType.DMA((2,)),
        ],
    )
    return pl.pallas_call(
        functools.partial(_experts_body, tm=tm, nf=nf, n_slots=n_slots),
        grid_spec=grid_spec,
        out_shape=jax.ShapeDtypeStruct(((n_slots + 2 * tm) * SUBLANES, LANES), F32),
        compiler_params=_cparams("arbitrary", "arbitrary"),
        name="moe_experts",
    )(tile_expert, n_valid, src_rows, src_rows, dst_rows, h, w1, w3, w2)


def _combine_body(x_ref, mod_ref, gw_ref, y0_ref, y1_ref, o_ref):
    n = x_ref.shape[0]
    gw = gw_ref[...]
    y = gw[:, 0:1] * _load_token_tiles(y0_ref, n) + gw[:, 1:2] * _load_token_tiles(y1_ref, n)
    o_ref[...] = x_ref[...] + mod_ref[0, ROW_GATE2:ROW_GATE2 + 1, :] * y


def moe_combine(x, mods, gw, y, row0, n_tok, seg_len, tm=TM_COMB):
    T, D = x.shape
    off = row0 // tm
    return pl.pallas_call(
        _combine_body,
        grid=(T // tm,),
        in_specs=[
            pl.BlockSpec((tm, D), lambda i: (i, 0)),
            pl.BlockSpec((1, MOD_ROWS, D), lambda i: (i * tm // seg_len, 0, 0)),
            pl.BlockSpec((tm, N_EXPERTS), lambda i: (off + i, 0)),
            pl.BlockSpec((tm * SUBLANES, LANES), lambda i: (off + i, 0)),
            pl.BlockSpec((tm * SUBLANES, LANES), lambda i: (off + n_tok // tm + i, 0)),
        ],
        out_specs=pl.BlockSpec((tm, D), lambda i: (i, 0)),
        out_shape=jax.ShapeDtypeStruct((T, D), F32),
        compiler_params=_cparams("parallel"),
        name="moe_combine",
    )(x, mods, gw, y, y)


def moe_dispatch_plan(idx, tm=TM_MOE):
    T = idx.shape[1]
    n_slots = TOP_K * T
    n_tiles = n_slots // tm + N_EXPERTS
    n_rows = n_tiles * tm
    experts = jnp.arange(N_EXPERTS, dtype=jnp.int32)[None, :]
    e_flat = idx[:TOP_K].reshape(n_slots)
    counts = jnp.sum((e_flat[:, None] == experts).astype(jnp.int32), axis=0)
    padded = (counts + tm - 1) // tm * tm
    ends = jnp.cumsum(padded)
    offs = ends - padded
    order = jnp.argsort(e_flat, stable=True).astype(jnp.int32)
    cstart = jnp.cumsum(counts) - counts
    rows = jnp.arange(n_rows, dtype=jnp.int32)
    row_e = jnp.minimum(jnp.sum((rows[:, None] >= ends[None, :]).astype(jnp.int32), axis=1), N_EXPERTS - 1)
    pick = lambda table: jnp.sum(jnp.where(row_e[:, None] == experts, table[None, :], 0), axis=1)
    j = rows - pick(offs)
    live = j < pick(counts)
    slot = order[jnp.clip(pick(cstart) + j, 0, n_slots - 1)]
    spare = n_slots + (rows // tm) % 2 * tm + rows % tm
    src_rows = jnp.where(live, slot % T, 0).astype(jnp.int32).reshape(n_tiles, 1, tm)
    dst_rows = jnp.where(live, slot, spare).astype(jnp.int32).reshape(n_tiles, 1, tm)
    tile_start = jnp.arange(n_tiles, dtype=jnp.int32) * tm
    n_valid = (ends[-1] // tm).astype(jnp.int32).reshape(1)
    tile_expert = jnp.minimum(jnp.sum((tile_start[:, None] >= ends[None, :]).astype(jnp.int32), axis=1),
                              N_EXPERTS - 1)
    last_e = tile_expert[jnp.maximum(n_valid[0] - 1, 0)]
    tile_expert = jnp.where(tile_start < ends[-1], tile_expert, last_e).astype(jnp.int32)
    return tile_expert, n_valid, src_rows, dst_rows


def ffn_moe(xa, xb, mods_a, mods_b, g, wr_t, w1, w3, w2, seg_len_b, tm=TM_PROJ, tme=TM_MOE, tmc=TM_COMB,
            tf=TF_MOE):
    na, nb = xa.shape[0], xb.shape[0]
    mods = jnp.concatenate([mods_a, mods_b], axis=0)
    h, idx, gw = moe_router(xa, xb, mods, g, wr_t, seg_len_b, tm)
    tile_expert, n_valid, src_rows, dst_rows = moe_dispatch_plan(idx, tme)
    y = moe_experts(h, tile_expert, n_valid, src_rows, dst_rows, w1, w3, w2, tme, tf)
    return (moe_combine(xa, mods_a, gw, y, 0, na + nb, na, tmc),
            moe_combine(xb, mods_b, gw, y, na, na + nb, seg_len_b, tmc))


def axial_rope(L):
    rows = L // GRID_W
    r = jnp.repeat(jnp.arange(rows, dtype=F32), GRID_W)
    col = jnp.tile(jnp.arange(GRID_W, dtype=F32), rows)
    n = HEAD_DIM // 4
    freqs = ROPE_BASE ** (-jnp.arange(n, dtype=F32) / n)
    ang = jnp.concatenate([r[:, None] * freqs, col[:, None] * freqs], axis=-1)
    return jnp.cos(ang), jnp.sin(ang)


def rope_tables(L):
    cos, sin = axial_rope(L)
    n = HEAD_DIM // 4
    cos_h = jnp.concatenate([cos[:, :n], cos[:, :n], cos[:, n:], cos[:, n:]], axis=1)
    sin_h = jnp.concatenate([-sin[:, :n], sin[:, :n], -sin[:, n:], sin[:, n:]], axis=1)
    reps = LANES // HEAD_DIM
    return jnp.tile(cos_h, (1, reps)), jnp.tile(sin_h, (1, reps))


def block_diag_ones(width, block):
    i = jnp.arange(width) // block
    return (i[:, None] == i[None, :]).astype(BF16)


def gla_gate_params(gw, gb):
    w = jnp.zeros((LANES, 2 * C_WIDTH), F32)
    w = w.at[:GLA_RANK, :C_WIDTH].set(gw[0]).at[GLA_RANK:2 * GLA_RANK, C_WIDTH:].set(gw[1])
    return w.astype(BF16), jnp.concatenate([gb[0], gb[1]])[None, :]


def gla_state_to_blockdiag(st):
    B = st.shape[0]
    out = jnp.zeros((B, 2, C_HEADS, C_DV, C_HEADS, C_DK), F32)
    for h in range(C_HEADS):
        out = out.at[:, :, h, :, h, :].set(jnp.swapaxes(st[:, :, h], -1, -2))
    return out.reshape(B, 2, C_WIDTH, C_WIDTH)


def gla_state_from_blockdiag(sT):
    B = sT.shape[0]
    s6 = sT.reshape(B, 2, C_HEADS, C_DV, C_HEADS, C_DK)
    return jnp.stack([jnp.swapaxes(s6[:, :, h, :, h, :], -1, -2) for h in range(C_HEADS)], axis=2)


def mod_table(cvec, w_ada, b_ada):
    m = jax.nn.silu(cvec) @ w_ada + b_ada
    m = m.reshape(cvec.shape[0], 6, D_MODEL)
    return jnp.pad(m, ((0, 0), (0, MOD_ROWS - 6), (0, 0)))


def kernel(x_prompt, x_sample, cache_k, cache_v, state_gla, c, c_ctx, norm1_g, norm2_g, w_ada, b_ada,
           w_in, w_out, q_norm_g, k_norm_g, attn_sink, hy_conv_w, hy_conv_b, hy_w1, hy_b1, hy_freq1,
           hy_w2, hy_b2, hy_freq2, hy_w3, hy_decay, hy_d, gla_gate_w, gla_gate_b, gla_norm_g,
           ffn_w1, ffn_w3, ffn_w2, moe_router, moe_w1, moe_w3, moe_w2):
    D = D_MODEL
    xp = x_prompt.reshape(BATCH * SEQ, D)
    xs = x_sample.reshape(DEC_BATCH * DEC_SEQ, D)
    streams = [
        dict(x=xp, B=BATCH, L=SEQ, seg_len=BATCH * SEQ, cvec=c_ctx[None, :], latent=False),
        dict(x=xs, B=DEC_BATCH, L=DEC_SEQ, seg_len=DEC_SEQ, cvec=c, latent=True),
    ]
    ks_list, vs_list, st_list = [], [], []
    bd_q = block_diag_ones(A_WIDTH, HEAD_DIM)
    dft = {s['L']: dft_half_matrices(s['L'], min(TM_DFT, s['L'] // 2)) for s in streams}
    for l in range(DEPTH):
        lp = {
            'hy_conv_w': hy_conv_w[l], 'hy_conv_b': hy_conv_b[l],
            'hy_w1': hy_w1[l], 'hy_b1': hy_b1[l], 'hy_freq1': hy_freq1[l], 'hy_w2': hy_w2[l],
            'hy_b2': hy_b2[l], 'hy_freq2': hy_freq2[l], 'hy_w3': hy_w3[l], 'hy_decay': hy_decay[l],
            'hy_d': hy_d[l],
        }
        w_in_l = jnp.pad(w_in[l], ((0, 0), (0, D_PROJ_PAD - D_PROJ))).astype(BF16)
        w_out_l = w_out[l].astype(BF16)
        g1 = norm1_g[l][None, :]
        g2 = norm2_g[l][None, :]
        qg = jnp.tile(q_norm_g[l], A_HEADS)[None, :]
        kg = jnp.tile(k_norm_g[l], A_KV_HEADS)[None, :]
        gg = jnp.tile(gla_norm_g[l], C_HEADS)[None, :]
        gate_w, gate_b = gla_gate_params(gla_gate_w[l], gla_gate_b[l])
        j = l // 2
        for s in streams:
            B, L = s['B'], s['L']
            mods = mod_table(s['cvec'], w_ada[l], b_ada[l])
            rope = rope_tables(L) if s['latent'] else None
            q, k, v, hy, cq, ck, cv, cg, la = in_proj(s['x'], mods, g1, w_in_l, bd_q, qg, kg, gate_w, gate_b,
                                                      s['seg_len'], rope, L)
            seq = lambda t: t.reshape(B, L, t.shape[-1])
            if s['latent']:
                kc = cache_k[:, l].reshape(DEC_BATCH, PAST_LEN, LANES)
                vc = cache_v[:, l].reshape(DEC_BATCH, PAST_LEN, LANES)
                a_out = attention(seq(q), seq(k), seq(v), attn_sink[l], kc, vc)
                s0 = gla_state_to_blockdiag(state_gla[:, l])
                o_f, o_b, _ = gla(seq(cq), seq(ck), seq(cv), seq(la), s0)
            else:
                a_out = attention(seq(q), seq(k), seq(v), attn_sink[l])
                o_f, o_b, sT = gla(seq(cq), seq(ck), seq(cv), seq(la))
                ks_list.append(k.reshape(B, L, A_KV_HEADS, HEAD_DIM))
                vs_list.append(v.reshape(B, L, A_KV_HEADS, HEAD_DIM))
                st_list.append(gla_state_from_blockdiag(sT))
            h_out = hyena_half(seq(hy), lp, dft[L])
            flat = lambda t: t.reshape(B * L, t.shape[-1])
            x1 = out_proj(s['x'], flat(a_out), [flat(h) for h in h_out], flat(o_f), flat(o_b), cg, mods, w_out_l,
                          bd_q[:C_WIDTH, :C_WIDTH], gg, s['seg_len'])
            if l % 2 == 0:
                s['x'] = ffn_dense(x1, mods, g2, ffn_w1[j].astype(BF16), ffn_w3[j].astype(BF16),
                                   ffn_w2[j].astype(BF16), s['seg_len'])
            else:
                s['x'], s['mods'] = x1, mods
        if l % 2 == 1:
            sa, sb = streams
            sa['x'], sb['x'] = ffn_moe(sa['x'], sb['x'], sa['mods'], sb['mods'], g2, moe_router[j].T,
                                       moe_w1[j].astype(BF16), moe_w3[j].astype(BF16), moe_w2[j].astype(BF16),
                                       sb['seg_len'])
    y_prompt = streams[0]['x'].reshape(BATCH, SEQ, D)
    y_sample = streams[1]['x'].reshape(DEC_BATCH, DEC_SEQ, D)
    new_cache_k = jnp.stack(ks_list, axis=1)
    new_cache_v = jnp.stack(vs_list, axis=1)
    new_state_gla = jnp.stack(st_list, axis=1)
    return (y_prompt, y_sample, new_cache_k, new_cache_v, new_state_gla)
```

```python
import math
import functools
import jax
import jax.numpy as jnp
from jax import lax
from jax.experimental import pallas as pl
from jax.experimental.pallas import tpu as pltpu

D_MODEL = 1024
BATCH = 32
SEQ = 256
DEPTH = 2
DEC_BATCH = 8
DEC_SEQ = 4096
PAST_LEN = 512

GRID_W = 64
HEAD_DIM = 64
A_HEADS = 8
A_KV_HEADS = 2
A_GROUP = A_HEADS // A_KV_HEADS
A_WIDTH = A_HEADS * HEAD_DIM
WINDOW = 128
ROPE_BASE = 10000.0
HY_CH = 256
HY_ORDER = 2
HY_BANDS = 16
HY_EMB = 1 + 2 * HY_BANDS
HY_HID = 64
C_HEADS = 4
C_DK = 64
C_DV = 64
C_WIDTH = C_HEADS * C_DV
GLA_RANK = 16
GLA_TAU = 16.0
GLA_CHUNK = 64
D_FF = 2816
N_EXPERTS = 8
TOP_K = 2
D_FF_EXPERT = 3584

PROJ_SIZES = (A_WIDTH, A_KV_HEADS * HEAD_DIM, A_KV_HEADS * HEAD_DIM, 3 * HY_CH,
              C_HEADS * C_DK, C_HEADS * C_DK, C_WIDTH, C_WIDTH, 2 * GLA_RANK)
D_PROJ = int(sum(PROJ_SIZES))

F32 = jnp.float32
BF16 = jnp.bfloat16
ATT_SCALE = HEAD_DIM ** -0.5
NEG_INF = -1e30
EPS = 1e-6

LANES = 128
SUBLANES = 8
D_PROJ_PAD = -(-D_PROJ // LANES) * LANES
VMEM_LIMIT_BYTES = 56 * 1024 * 1024

MOD_ROWS = SUBLANES
ROW_SHIFT1, ROW_SCALE1, ROW_GATE1, ROW_SHIFT2, ROW_SCALE2, ROW_GATE2 = range(6)

TM_PROJ = 512
TM_MOE = 512
TF_MOE = 1792
TM_COMB = 512


def _cparams(*sem):
    return pltpu.CompilerParams(dimension_semantics=sem, vmem_limit_bytes=VMEM_LIMIT_BYTES)


def _adaln_rows(x, g, shift, scale):
    ms = jnp.mean(x * x, axis=-1, keepdims=True)
    return (x * lax.rsqrt(ms + EPS) * g) * (1.0 + scale) + shift


def _silu(a):
    return a * jax.nn.sigmoid(a)


def _group_mean_sq(x, ones_bd):
    sq = x * x
    hi = sq.astype(BF16)
    lo = (sq - hi.astype(F32)).astype(BF16)
    s = jnp.dot(hi, ones_bd, preferred_element_type=F32) + jnp.dot(lo, ones_bd, preferred_element_type=F32)
    return s * (1.0 / HEAD_DIM)


def _rope_rows(x, cos_t, sin_t):
    q4 = HEAD_DIM // 4
    lane = lax.broadcasted_iota(jnp.int32, x.shape, 1)
    partner = jnp.where((lane % (2 * q4)) < q4, pltpu.roll(x, LANES - q4, 1), pltpu.roll(x, q4, 1))
    return x * cos_t + partner * sin_t


def _log_sigmoid(x):
    return jnp.minimum(x, 0.0) - jnp.log(1.0 + jnp.exp(-jnp.abs(x)))


def _inproj_body(*refs, latent):
    if latent:
        (x_ref, mod_ref, g_ref, w_ref, bd_ref, qg_ref, kg_ref, gw_ref, gb_ref, cos_ref, sin_ref,
         q_ref, k_ref, v_ref, hy_ref, cq_ref, ck_ref, cv_ref, cg_ref, la_ref) = refs
    else:
        (x_ref, mod_ref, g_ref, w_ref, bd_ref, qg_ref, kg_ref, gw_ref, gb_ref,
         q_ref, k_ref, v_ref, hy_ref, cq_ref, ck_ref, cv_ref, cg_ref, la_ref) = refs
    tm = x_ref.shape[0]
    for half in range(2):
        rs = pl.ds(half * (tm // 2), tm // 2)
        h = _adaln_rows(x_ref[rs, :], g_ref[...], mod_ref[0, ROW_SHIFT1:ROW_SHIFT1 + 1, :],
                        mod_ref[0, ROW_SCALE1:ROW_SCALE1 + 1, :])
        acc = jnp.dot(h.astype(BF16), w_ref[...], preferred_element_type=F32)
        o = 0
        q = acc[:, o:o + A_WIDTH]
        o += A_WIDTH
        k = acc[:, o:o + LANES]
        o += LANES
        v_ref[rs, :] = acc[:, o:o + LANES]
        o += LANES
        hy_ref[rs, :] = acc[:, o:o + 3 * HY_CH]
        o += 3 * HY_CH
        cq_ref[rs, :] = acc[:, o:o + C_WIDTH] * (C_DK ** -0.5)
        o += C_WIDTH
        ck_ref[rs, :] = acc[:, o:o + C_WIDTH]
        o += C_WIDTH
        cv_ref[rs, :] = acc[:, o:o + C_WIDTH].astype(cv_ref.dtype)
        o += C_WIDTH
        cg_ref[rs, :] = acc[:, o:o + C_WIDTH]
        o += C_WIDTH
        r = acc[:, o:o + LANES]
        la_ref[rs, :] = _log_sigmoid(jnp.dot(r.astype(BF16), gw_ref[...], preferred_element_type=F32)
                                     + gb_ref[...]) * (1.0 / GLA_TAU)
        q = q * lax.rsqrt(_group_mean_sq(q, bd_ref[...]) + EPS) * qg_ref[...]
        k = k * lax.rsqrt(_group_mean_sq(k, bd_ref[0:LANES, 0:LANES]) + EPS) * kg_ref[...]
        if latent:
            cos_t = cos_ref[rs, :]
            sin_t = sin_ref[rs, :]
            q = jnp.concatenate([_rope_rows(q[:, j * LANES:(j + 1) * LANES], cos_t, sin_t)
                                 for j in range(A_WIDTH // LANES)], axis=1)
            k = _rope_rows(k, cos_t, sin_t)
        q_ref[rs, :] = (q * ATT_SCALE).astype(BF16)
        k_ref[rs, :] = k


def in_proj(x, mods, g, w, bd, qg, kg, gw, gb, seg_len, rope=None, seq_len=None, tm=TM_PROJ):
    T, D = x.shape
    N = w.shape[1]
    latent = rope is not None
    const = lambda shape: pl.BlockSpec(shape, lambda i: (0,) * len(shape))
    in_specs = [
        pl.BlockSpec((tm, D), lambda i: (i, 0)),
        pl.BlockSpec((1, MOD_ROWS, D), lambda i: (i * tm // seg_len, 0, 0)),
        const((1, D)), const((D, N)), const((A_WIDTH, A_WIDTH)), const((1, A_WIDTH)), const((1, LANES)),
        const((LANES, 2 * C_WIDTH)), const((1, 2 * C_WIDTH)),
    ]
    args = [x, mods, g, w, bd, qg, kg, gw, gb]
    if latent:
        tiles_per_seq = seq_len // tm
        in_specs += [pl.BlockSpec((tm, LANES), lambda i: (i % tiles_per_seq, 0))] * 2
        args += list(rope)
    widths = [A_WIDTH, LANES, LANES, 3 * HY_CH, C_WIDTH, C_WIDTH, C_WIDTH, C_WIDTH, 2 * C_WIDTH]
    dtypes = [BF16, F32, F32, F32, F32, F32, BF16, F32, F32]
    return pl.pallas_call(
        functools.partial(_inproj_body, latent=latent),
        grid=(T // tm,),
        in_specs=in_specs,
        out_specs=[pl.BlockSpec((tm, wd), lambda i: (i, 0)) for wd in widths],
        out_shape=[jax.ShapeDtypeStruct((T, wd), dt) for wd, dt in zip(widths, dtypes)],
        compiler_params=_cparams("parallel"),
        name="in_proj_latent" if latent else "in_proj_context",
    )(*args)


def _outproj_body(x_ref, a_ref, h0_ref, h1_ref, of_ref, ob_ref, cg_ref, mod_ref, w_ref, bd_ref, gg_ref, o_ref):
    o = of_ref[...] + ob_ref[...]
    g_out = o * lax.rsqrt(_group_mean_sq(o, bd_ref[...]) + EPS) * gg_ref[...] * _silu(cg_ref[...])
    h0 = A_WIDTH
    g0 = A_WIDTH + HY_CH
    mix = jnp.dot(a_ref[...], w_ref[0:h0, :], preferred_element_type=F32)
    h = jnp.concatenate([h0_ref[...], h1_ref[...]], axis=1)
    mix += jnp.dot(h.astype(BF16), w_ref[h0:g0, :], preferred_element_type=F32)
    mix += jnp.dot(g_out.astype(BF16), w_ref[g0:, :], preferred_element_type=F32)
    o_ref[...] = x_ref[...] + mod_ref[0, ROW_GATE1:ROW_GATE1 + 1, :] * mix


def out_proj(x, a, h, of, ob, cg, mods, w, bd, gg, seg_len, tm=TM_PROJ):
    T, D = x.shape
    row = lambda wd: pl.BlockSpec((tm, wd), lambda i: (i, 0))
    const = lambda shape: pl.BlockSpec(shape, lambda i: (0,) * len(shape))
    return pl.pallas_call(
        _outproj_body,
        grid=(T // tm,),
        in_specs=[
            row(D), row(A_WIDTH), row(LANES), row(LANES), row(C_WIDTH), row(C_WIDTH), row(C_WIDTH),
            pl.BlockSpec((1, MOD_ROWS, D), lambda i: (i * tm // seg_len, 0, 0)),
            const((D, D)), const((C_WIDTH, C_WIDTH)), const((1, C_WIDTH)),
        ],
        out_specs=row(D),
        out_shape=jax.ShapeDtypeStruct((T, D), F32),
        compiler_params=_cparams("parallel"),
        name="out_proj",
    )(x, a, h[0], h[1], of, ob, cg, mods, w, bd, gg)


TQ_ATT = 256
HEADS_PER_STACK = 2
KWIN_ATT = TQ_ATT + 2 * WINDOW


def _dup_heads(x, g):
    lane = lax.broadcasted_iota(jnp.int32, x.shape, 1)
    rolled = pltpu.roll(x, HEAD_DIM, 1)
    keep = (lane < HEAD_DIM) if g == 0 else (lane >= HEAD_DIM)
    return jnp.where(keep, x, rolled)


def _attn_body(*refs, latent, seq_len):
    if latent:
        sink_ref, q_ref, k_ref, v_ref, kc_ref, vc_ref, o_ref = refs
    else:
        sink_ref, q_ref, k_ref, v_ref, o_ref = refs
    tq = q_ref.shape[1]
    i = pl.program_id(1)
    q = q_ref[0]
    lane = lax.broadcasted_iota(jnp.int32, (tq, LANES), 1)
    low = lane < HEAD_DIM
    if latent:
        start = jnp.clip(i * tq - WINDOW, 0, seq_len - KWIN_ATT)
        start = pl.multiple_of(start, WINDOW)
        kl = k_ref[0, pl.ds(start, KWIN_ATT), :]
        vl = v_ref[0, pl.ds(start, KWIN_ATT), :]
        qpos = i * tq + lax.broadcasted_iota(jnp.int32, (tq, KWIN_ATT), 0)
        kpos = start + lax.broadcasted_iota(jnp.int32, (tq, KWIN_ATT), 1)
        bias = jnp.where(jnp.abs(qpos - kpos) <= WINDOW, 0.0, NEG_INF).astype(F32)
        bias = jnp.concatenate([bias] * A_GROUP, axis=0)
        kc = kc_ref[0]
        vc = vc_ref[0]
    else:
        kl = k_ref[0]
        vl = v_ref[0]
    nt = (((1,), (1,)), ((), ()))
    zero = jnp.zeros_like(q[:, :LANES])
    kls = [_dup_heads(kl, g).astype(BF16) for g in range(A_KV_HEADS)]
    vls = [_dup_heads(vl, g).astype(BF16) for g in range(A_KV_HEADS)]
    if latent:
        kcs = [_dup_heads(kc, g).astype(BF16) for g in range(A_KV_HEADS)]
        vcs = [_dup_heads(vc, g).astype(BF16) for g in range(A_KV_HEADS)]

    def run(stacks):
        gs = [g for g, _ in stacks]
        n = range(len(stacks))
        qs = [jnp.concatenate([jnp.where(low if h % 2 == 0 else jnp.logical_not(low),
                                         q[:, (h // 2) * LANES:(h // 2 + 1) * LANES], zero) for h in heads], axis=0)
              for _, heads in stacks]
        sinks = [jnp.concatenate([jnp.full((tq, 1), sink_ref[h], F32) for h in heads], axis=0)
                 for _, heads in stacks]
        s_loc = [lax.dot_general(qs[i], kls[gs[i]], nt, preferred_element_type=F32) for i in n]
        m = sinks
        if latent:
            s_loc = [s + bias[:s.shape[0]] for s in s_loc]
            s_ctx = [lax.dot_general(qs[i], kcs[gs[i]], nt, preferred_element_type=F32) for i in n]
            m = [jnp.maximum(m[i], jnp.max(s_ctx[i], axis=-1, keepdims=True)) for i in n]
        m = [jnp.maximum(m[i], jnp.max(s_loc[i], axis=-1, keepdims=True)) for i in n]
        p_loc = [jnp.exp(s_loc[i] - m[i]) for i in n]
        den = [jnp.exp(sinks[i] - m[i]) + jnp.sum(p_loc[i], axis=-1, keepdims=True) for i in n]
        acc = [jnp.dot(p_loc[i].astype(BF16), vls[gs[i]], preferred_element_type=F32) for i in n]
        if latent:
            p_ctx = [jnp.exp(s_ctx[i] - m[i]) for i in n]
            den = [den[i] + jnp.sum(p_ctx[i], axis=-1, keepdims=True) for i in n]
            acc = [acc[i] + jnp.dot(p_ctx[i].astype(BF16), vcs[gs[i]], preferred_element_type=F32) for i in n]
        outs = []
        for i in n:
            og = acc[i] / den[i]
            for jj in range(len(stacks[i][1]) // 2):
                outs.append(jnp.where(low, og[(2 * jj) * tq:(2 * jj + 1) * tq],
                                      og[(2 * jj + 1) * tq:(2 * jj + 2) * tq]))
        return outs

    stacks = [(g, range(g * A_GROUP + j, g * A_GROUP + j + HEADS_PER_STACK))
              for g in range(A_KV_HEADS) for j in range(0, A_GROUP, HEADS_PER_STACK)]
    if latent:
        outs = [o for st in stacks for o in run([st])]
    else:
        outs = run(stacks)
    o_ref[0] = jnp.concatenate(outs, axis=1).astype(o_ref.dtype)


def attention(q, k, v, sink, kc=None, vc=None, tq=TQ_ATT):
    B, L, _ = q.shape
    latent = kc is not None
    tq = min(tq, L)
    seq = lambda wd: pl.BlockSpec((1, L, wd), lambda b, i: (b, 0, 0))
    in_specs = [
        pl.BlockSpec(memory_space=pltpu.SMEM),
        pl.BlockSpec((1, tq, A_WIDTH), lambda b, i: (b, i, 0)),
        seq(LANES), seq(LANES),
    ]
    args = [sink, q, k, v]
    if latent:
        P = kc.shape[1]
        in_specs += [pl.BlockSpec((1, P, LANES), lambda b, i: (b, 0, 0))] * 2
        args += [kc, vc]
    return pl.pallas_call(
        functools.partial(_attn_body, latent=latent, seq_len=L),
        grid=(B, L // tq),
        in_specs=in_specs,
        out_specs=pl.BlockSpec((1, tq, A_WIDTH), lambda b, i: (b, i, 0)),
        out_shape=jax.ShapeDtypeStruct((B, L, A_WIDTH), BF16),
        compiler_params=_cparams("parallel", "arbitrary"),
        name="attention_latent" if latent else "attention_context",
    )(*args)


def _split3(x):
    hi = x.astype(BF16)
    r = x - hi.astype(F32)
    mid = r.astype(BF16)
    lo = (r - mid.astype(F32)).astype(BF16)
    return hi, mid, lo


def _gla_group(q_ref, k_ref, v_ref, la_ref, o_ref, st_ref, b, d, reverse, n_chunks):
    C = GLA_CHUNK
    W = C_WIDTH
    ti = lax.broadcasted_iota(jnp.int32, (C, C), 0)
    si = lax.broadcasted_iota(jnp.int32, (C, C), 1)
    tri = (si >= ti) if reverse else (si <= ti)
    tri_b = tri.astype(BF16)
    tri4 = jnp.concatenate([tri] * C_HEADS, axis=0)
    r4 = lax.broadcasted_iota(jnp.int32, (C_HEADS * C, W), 0) // C
    c4 = lax.broadcasted_iota(jnp.int32, (C_HEADS * C, W), 1) // C_DK
    same_head = r4 == c4
    nt = (((1,), (1,)), ((), ()))
    tn = (((0,), (0,)), ((), ()))
    chunks = range(n_chunks)
    rows = [pl.ds(c * C, C) for c in chunks]
    vbs = [v_ref[b, r, :].astype(BF16) for r in rows]
    parts = [_split3(la_ref[b, r, :]) for r in rows]
    bsums = [jnp.dot(tri_b, hi, preferred_element_type=F32) + jnp.dot(tri_b, mid, preferred_element_type=F32)
             + jnp.dot(tri_b, lo, preferred_element_type=F32) for hi, mid, lo in parts]
    b_lasts = [s[0:1] if reverse else s[C - 1:C] for s in bsums]
    qgs = [q_ref[b, r, :] * jnp.exp(s) for r, s in zip(rows, bsums)]
    kgs = [(k_ref[b, r, :] * jnp.exp(-s)).astype(BF16) for r, s in zip(rows, bsums)]
    kds = [(k_ref[b, r, :] * jnp.exp(bl - s)).astype(BF16) for r, s, bl in zip(rows, bsums, b_lasts)]
    decays = [jnp.exp(bl) for bl in b_lasts]
    q_bds = [jnp.where(same_head, jnp.concatenate([qg] * C_HEADS, axis=0), 0.0).astype(BF16) for qg in qgs]
    uts = [jnp.where(same_head, lax.dot_general(vb, kd, tn, preferred_element_type=F32), 0.0)
           for vb, kd in zip(vbs, kds)]
    a_s = [jnp.where(tri4, lax.dot_general(qb, kg, nt, preferred_element_type=F32), 0.0).astype(BF16)
           for qb, kg in zip(q_bds, kgs)]
    r_s = [jnp.where(same_head, jnp.dot(a, vb, preferred_element_type=F32), 0.0)
           for a, vb in zip(a_s, vbs)]
    o_intra = [sum([r[h * C:(h + 1) * C] for h in range(1, C_HEADS)], r[0:C]) for r in r_s]
    qgb = [qg.astype(BF16) for qg in qgs]
    st = st_ref[d]
    for c in (reversed(chunks) if reverse else chunks):
        o_ref[b, rows[c], :] = o_intra[c] + lax.dot_general(qgb[c], st.astype(BF16), nt,
                                                            preferred_element_type=F32)
        st = st * decays[c] + uts[c]
    st_ref[d] = st


def _gla_body(*refs, has_state, n_chunks):
    if has_state:
        (qf, kf, vf, lf, qb, kb, vb, lb, s0_ref, of_ref, ob_ref, sT_ref, st) = refs
    else:
        (qf, kf, vf, lf, qb, kb, vb, lb, of_ref, ob_ref, sT_ref, st) = refs
    j = pl.program_id(1)

    nb = qf.shape[0]

    @pl.when(j == 0)
    def _init():
        if has_state:
            st[...] = s0_ref[...].reshape(st.shape)
        else:
            st[...] = jnp.zeros_like(st)

    for b in range(nb):
        _gla_group(qf, kf, vf, lf, of_ref, st, b, 2 * b, False, n_chunks)
        _gla_group(qb, kb, vb, lb, ob_ref, st, b, 2 * b + 1, True, n_chunks)

    @pl.when(j == pl.num_programs(1) - 1)
    def _final():
        sT_ref[...] = st[...].reshape(sT_ref.shape)


NB_GLA = 1


def gla(cq, ck, cv, la, s0=None, rows=512, nb=NB_GLA):
    B, L, W = cq.shape
    assert B % nb == 0
    rows = min(rows, L)
    ng = L // rows
    has_state = s0 is not None
    fwd = lambda: pl.BlockSpec((nb, rows, W), lambda b, j: (b, j, 0))
    bwd = lambda: pl.BlockSpec((nb, rows, W), lambda b, j: (b, ng - 1 - j, 0))
    state = lambda: pl.BlockSpec((nb, 2, W, W), lambda b, j: (b, 0, 0, 0))
    in_specs = [fwd(), fwd(), fwd(), pl.BlockSpec((nb, rows, W), lambda b, j: (b, j, 0)),
                bwd(), bwd(), bwd(), pl.BlockSpec((nb, rows, W), lambda b, j: (b, ng - 1 - j, 1))]
    args = [cq, ck, cv, la, cq, ck, cv, la]
    if has_state:
        in_specs.append(state())
        args.append(s0)
    return pl.pallas_call(
        functools.partial(_gla_body, has_state=has_state, n_chunks=rows // GLA_CHUNK),
        grid=(B // nb, ng),
        in_specs=in_specs,
        out_specs=[fwd(), bwd(), state()],
        out_shape=[jax.ShapeDtypeStruct((B, L, W), F32), jax.ShapeDtypeStruct((B, L, W), F32),
                   jax.ShapeDtypeStruct((B, 2, W, W), F32)],
        scratch_shapes=[pltpu.VMEM((2 * nb, W, W), F32)],
        compiler_params=_cparams("parallel", "arbitrary"),
        name="gla",
    )(*args)


TM_DFT = 512
BG_HALF = 2


def _freq_weight(i, tm, n):
    k = i * tm + lax.broadcasted_iota(jnp.int32, (tm, 1), 0)
    return jnp.where(k == 0, 1.0 / n, 2.0 / n).astype(F32)


def hyena_time_filters(L, lp):
    t = jnp.arange(L, dtype=F32)
    t_norm = t / max(L - 1, 1)
    w = (2.0 * math.pi / L) * t
    f = jnp.linspace(1e-4, HY_BANDS - 1, HY_BANDS, dtype=F32)
    fw = w[:, None] * f[None, :]
    feat = jnp.concatenate([t_norm[:, None], jnp.cos(fw), -jnp.sin(fw)], axis=-1)
    z = jnp.sin(lp['hy_freq1'] * (feat @ lp['hy_w1'] + lp['hy_b1']))
    z = jnp.sin(lp['hy_freq2'] * (z @ lp['hy_w2'] + lp['hy_b2']))
    hf = (z @ lp['hy_w3']).astype(F32).reshape(L, 2, HY_ORDER, HY_CH)
    hf = hf * jnp.exp(-t_norm[:, None, None, None] * jnp.abs(lp['hy_decay'].astype(F32)))
    return hf / (jnp.sum(jnp.abs(hf), axis=(0, 1), keepdims=True) + EPS)


def _hi_lo(x):
    hi = x.astype(BF16)
    return jnp.stack([hi, (x - hi.astype(F32)).astype(BF16)])


def _shift_rows(x, down):
    n = x.shape[0]
    row = lax.broadcasted_iota(jnp.int32, (n, 1), 0)
    if down:
        return jnp.where(row == 0, 0.0, pltpu.roll(x, 1, 0))
    return jnp.where(row == n - 1, 0.0, pltpu.roll(x, n - 1, 0))


def _conv3_planes_body(x_ref, w_ref, b_ref, u_ref, ub_ref):
    H = x_ref.shape[0] // 2
    e = x_ref[pl.ds(0, H, stride=2), :]
    o = x_ref[pl.ds(1, H, stride=2), :]
    w0, w1, w2 = w_ref[0:1, :], w_ref[1:2, :], w_ref[2:3, :]
    ue = _shift_rows(o, True) * w0 + e * w1 + o * w2 + b_ref[...]
    uo = e * w0 + o * w1 + _shift_rows(e, False) * w2 + b_ref[...]
    u_ref[0, 0] = ue
    u_ref[0, 1] = uo
    ub_ref[0, 0] = ue.astype(BF16)
    ub_ref[0, 1] = uo.astype(BF16)


def hyena_conv3_planes(hy, w, b):
    B, L, C3 = hy.shape
    H = L // 2
    out = pl.BlockSpec((1, 2, H, LANES), lambda b_, j: (b_, 0, 0, j))
    return pl.pallas_call(
        _conv3_planes_body,
        grid=(B, C3 // LANES),
        in_specs=[pl.BlockSpec((None, L, LANES), lambda b_, j: (b_, 0, j)),
                  pl.BlockSpec((SUBLANES, LANES), lambda b_, j: (0, j)),
                  pl.BlockSpec((1, LANES), lambda b_, j: (0, j))],
        out_specs=[out, out],
        out_shape=[jax.ShapeDtypeStruct((B, 2, H, C3), F32), jax.ShapeDtypeStruct((B, 2, H, C3), BF16)],
        compiler_params=_cparams("parallel", "parallel"),
        name="hyena_conv3",
    )(hy, w, b)


def _half_filt_body(fe_ref, fo_ref, sg_ref, hse_ref, hso_ref, hde_ref, hdo_ref,
                    lr_ref, li_ref, hr_ref, hi_ref, sp_ref, *, n):
    i = pl.program_id(0)
    tm = lr_ref.shape[0]
    wk = _freq_weight(i, tm, n)

    def two_pass(m, parts_ref):
        return (jnp.dot(m, parts_ref[0], preferred_element_type=F32)
                + jnp.dot(m, parts_ref[1], preferred_element_type=F32))
    a = two_pass(fe_ref[0, :tm], hse_ref)
    b = two_pass(fo_ref[0, :tm], hso_ref)
    c = two_pass(fe_ref[0, tm:], hde_ref)
    d = two_pass(fo_ref[0, tm:], hdo_ref)
    lr_ref[...] = wk * (a + b)
    hr_ref[...] = wk * (a - b)
    li_ref[...] = wk * (c + d)
    hi_ref[...] = wk * (d - c)

    @pl.when(i == 0)
    def _middle_bin():
        sr = two_pass(sg_ref[...], hse_ref)[0:1, :]
        si = -two_pass(sg_ref[...], hdo_ref)[0:1, :]
        row = lax.broadcasted_iota(jnp.int32, sp_ref.shape, 0)
        sp_ref[...] = jnp.where(row == 0, sr, jnp.where(row == 1, si, 0.0)) * (2.0 / n)


def hyena_half_filter_spectrum(fe, fo, sg, hs, hd, tm):
    H = fe.shape[2]
    W = hs.shape[2]
    planes = lambda h: (h.reshape(2, H, 2, W)[:, :, 0], h.reshape(2, H, 2, W)[:, :, 1])
    hse, hso = planes(hs)
    hde, hdo = planes(hd)
    whole = lambda shape: pl.BlockSpec(shape, lambda i: (0,) * len(shape))
    ftile = pl.BlockSpec((1, 2 * tm, H), lambda i: (i, 0, 0))
    otile = pl.BlockSpec((tm, W), lambda i: (i, 0))
    out = jax.ShapeDtypeStruct((H, W), F32)
    return pl.pallas_call(
        functools.partial(_half_filt_body, n=4 * H),
        grid=(H // tm,),
        in_specs=[ftile, ftile, whole((SUBLANES, H))] + [whole((2, H, W))] * 4,
        out_specs=[otile, otile, otile, otile, whole((SUBLANES, W))],
        out_shape=[out, out, out, out, jax.ShapeDtypeStruct((SUBLANES, W), F32)],
        compiler_params=_cparams("arbitrary"),
        name="hyena_filter_spectrum",
    )(fe, fo, sg, hse, hso, hde, hdo)


def _cmul(xr, xi, hr, hi):
    return xr * hr - xi * hi, xr * hi + xi * hr


def _half_fwd_body(fe_ref, fo_ref, sg_ref, ze_ref, zo_ref, lr_ref, li_ref, hr_ref, hi_ref, sp_ref,
                   per_ref, pei_ref, por_ref, poi_ref, ysp_ref):
    nb = ze_ref.shape[0]
    tm = lr_ref.shape[0]

    def body(b, carry):
        a = jnp.dot(fe_ref[0], ze_ref[b], preferred_element_type=F32)
        o = jnp.dot(fo_ref[0], zo_ref[b], preferred_element_type=F32)
        ylr, yli = _cmul(a[:tm] + o[:tm], a[tm:] + o[tm:], lr_ref[...], li_ref[...])
        yhr, yhi = _cmul(a[:tm] - o[:tm], o[tm:] - a[tm:], hr_ref[...], hi_ref[...])
        per_ref[b] = (ylr + yhr).astype(BF16)
        pei_ref[b] = (yli - yhi).astype(BF16)
        por_ref[b] = (ylr - yhr).astype(BF16)
        poi_ref[b] = (yli + yhi).astype(BF16)
        return carry
    lax.fori_loop(0, nb, body, 0)

    @pl.when(pl.program_id(1) == 0)
    def _middle_bin():
        def mid(b, carry):
            xr = jnp.dot(sg_ref[...], ze_ref[b], preferred_element_type=F32)[0:1, :]
            xi = -jnp.dot(sg_ref[...], zo_ref[b], preferred_element_type=F32)[0:1, :]
            yr, yi = _cmul(xr, xi, sp_ref[0:1, :], sp_ref[1:2, :])
            row = lax.broadcasted_iota(jnp.int32, ysp_ref.shape[1:], 0)
            ysp_ref[b] = jnp.where(row == 0, yr, jnp.where(row == 1, yi, 0.0))
            return carry
        lax.fori_loop(0, nb, mid, 0)


def hyena_half_fwd(fe, fo, sg, zb, zcol, spec, order, tm, bg):
    B, _, H, _ = zb.shape
    C = HY_CH
    lr, li, hr, hi, sp = spec
    ftile = pl.BlockSpec((1, 2 * tm, H), lambda g, i: (i, 0, 0))
    plane = lambda p: pl.BlockSpec((bg, None, H, C), lambda g, i: (g, p, 0, zcol))
    stile = pl.BlockSpec((tm, C), lambda g, i: (i, order))
    ptile = pl.BlockSpec((bg, tm, C), lambda g, i: (g, i, 0))
    pshape = jax.ShapeDtypeStruct((B, H, C), BF16)
    return pl.pallas_call(
        _half_fwd_body,
        grid=(B // bg, H // tm),
        in_specs=[ftile, ftile, pl.BlockSpec((SUBLANES, H), lambda g, i: (0, 0)), plane(0), plane(1),
                  stile, stile, stile, stile, pl.BlockSpec((SUBLANES, C), lambda g, i: (0, order))],
        out_specs=[ptile, ptile, ptile, ptile, pl.BlockSpec((bg, SUBLANES, C), lambda g, i: (g, 0, 0))],
        out_shape=[pshape, pshape, pshape, pshape, jax.ShapeDtypeStruct((B, SUBLANES, C), F32)],
        compiler_params=_cparams("parallel", "arbitrary"),
        name="hyena_dft_fwd",
    )(fe, fo, sg, zb, zb, lr, li, hr, hi, sp)


def _half_inv_body(fer_ref, fei_ref, fotr_ref, foti_ref, per_ref, pei_ref, por_ref, poi_ref, ysp_ref,
                   ze_ref, zo_ref, ge_ref, go_ref, d_ref, *out_refs, natural):
    nb = ze_ref.shape[0]
    tm = fer_ref.shape[0]
    u = pl.program_id(1) * tm + lax.broadcasted_iota(jnp.int32, (tm, 1), 0)
    sign = jnp.where((u & 1) == 0, 1.0, -1.0).astype(F32)
    d = d_ref[...]

    def body(b, carry):
        ce = (jnp.dot(fer_ref[...], per_ref[b], preferred_element_type=F32)
              + jnp.dot(fei_ref[...], pei_ref[b], preferred_element_type=F32) + sign * ysp_ref[b][0:1, :])
        co = (jnp.dot(fotr_ref[...], por_ref[b], preferred_element_type=F32)
              + jnp.dot(foti_ref[...], poi_ref[b], preferred_element_type=F32) - sign * ysp_ref[b][1:2, :])
        zne = ge_ref[b] * (ce + d * ze_ref[b])
        zno = go_ref[b] * (co + d * zo_ref[b])
        if natural:
            bi = jnp.asarray(b, jnp.int32)
            for c, o_ref in enumerate(out_refs):
                o_ref.at[bi][pl.ds(0, tm, stride=2), :] = zne[:, c * LANES:(c + 1) * LANES]
                o_ref.at[bi][pl.ds(1, tm, stride=2), :] = zno[:, c * LANES:(c + 1) * LANES]
        else:
            z_ref, zb_ref = out_refs
            z_ref[b, 0] = zne
            z_ref[b, 1] = zno
            zb_ref[b, 0] = zne.astype(BF16)
            zb_ref[b, 1] = zno.astype(BF16)
        return carry
    lax.fori_loop(0, nb, body, 0)


def hyena_half_inv(mats, ps, ysp, z, zcol, gate, gcol, d, tm, bg, natural=False):
    fer, fei, fotr, foti = mats
    per = ps[0]
    B, H, C = per.shape
    ftile = pl.BlockSpec((tm, H), lambda g, i: (i, 0))
    whole_p = lambda: pl.BlockSpec((bg, H, C), lambda g, i: (g, 0, 0), pipeline_mode=pl.Buffered(1))
    tile = lambda p, col: pl.BlockSpec((bg, None, tm, C), lambda g, i: (g, p, i, col))
    if natural:
        out_specs = [pl.BlockSpec((bg, 2 * tm, LANES), lambda g, i: (g, i, 0))] * (C // LANES)
        out_shape = [jax.ShapeDtypeStruct((B, 2 * H, LANES), F32)] * (C // LANES)
    else:
        out_specs = [pl.BlockSpec((bg, 2, tm, C), lambda g, i: (g, 0, i, 0))] * 2
        out_shape = [jax.ShapeDtypeStruct((B, 2, H, C), F32), jax.ShapeDtypeStruct((B, 2, H, C), BF16)]
    return pl.pallas_call(
        functools.partial(_half_inv_body, natural=natural),
        grid=(B // bg, H // tm),
        in_specs=[ftile, ftile, ftile, ftile, whole_p(), whole_p(), whole_p(), whole_p(),
                  pl.BlockSpec((bg, SUBLANES, C), lambda g, i: (g, 0, 0)),
                  tile(0, zcol), tile(1, zcol), tile(0, gcol), tile(1, gcol),
                  pl.BlockSpec((1, C), lambda g, i: (0, 0))],
        out_specs=out_specs,
        out_shape=out_shape,
        compiler_params=_cparams("parallel", "arbitrary"),
        name="hyena_dft_inv",
    )(fer, fei, fotr, foti, *ps, ysp, z, z, gate, gate, d)


def dft_half_matrices(L, tm):
    H = L // 2
    r = 1 << (max(H.bit_length() - 1, 0) // 2)
    u = jnp.arange(H, dtype=jnp.int32)

    hi = jnp.arange(H // r, dtype=jnp.int32) * r
    lo = jnp.arange(r, dtype=jnp.int32)

    def cos_msin(row_hi, row_lo, col):
        def table(k):
            ang = ((k[:, None] * col[None, :]) % (2 * L)).astype(F32) * (math.pi / L)
            return jnp.cos(ang), jnp.sin(ang)
        ca, sa = table(row_hi)
        cb, sb = table(row_lo)
        c = ca[:, None, :] * cb[None, :, :] - sa[:, None, :] * sb[None, :, :]
        s = sa[:, None, :] * cb[None, :, :] + ca[:, None, :] * sb[None, :, :]
        return c.reshape(H, H).astype(BF16), (-s).reshape(H, H).astype(BF16)
    fre, fie = cos_msin(hi, lo, 2 * u)
    fro, fio = cos_msin(hi, lo, 2 * u + 1)
    frot, fiot = cos_msin(2 * hi, 2 * lo + 1, u)
    stack = lambda a, b: jnp.concatenate([a.reshape(H // tm, tm, H), b.reshape(H // tm, tm, H)], axis=1)
    sg = jnp.zeros((SUBLANES, H), F32).at[0].set(jnp.where(u % 2 == 0, 1.0, -1.0)).astype(BF16)
    return dict(fe=stack(fre, fie), fo=stack(fro, fio), sg=sg, inv=(fre, fie, frot, fiot), tm=tm)


def hyena_half(hy, lp, mats):
    B, L, _ = hy.shape
    tm = mats['tm']
    bg = B if L < TM_DFT else BG_HALF
    hf = hyena_time_filters(L, lp)
    h_fwd = hf[:, 0].reshape(L, HY_ORDER * HY_CH)
    h_bwd = hf[:, 1].at[0].set(0.0).reshape(L, HY_ORDER * HY_CH)
    spec = hyena_half_filter_spectrum(mats['fe'], mats['fo'], mats['sg'], _hi_lo(h_fwd + h_bwd),
                                      _hi_lo(h_fwd - h_bwd), tm)
    w = jnp.pad(lp['hy_conv_w'], ((0, SUBLANES - 3), (0, 0)))
    u, ub = hyena_conv3_planes(hy, w, lp['hy_conv_b'][None, :])
    d = lp['hy_d'].astype(F32)
    z, zb, zcol = u, ub, 0
    for o in range(HY_ORDER):
        *ps, ysp = hyena_half_fwd(mats['fe'], mats['fo'], mats['sg'], zb, zcol, spec, o, tm, bg)
        last = o == HY_ORDER - 1
        out = hyena_half_inv(mats['inv'], ps, ysp, z, zcol, u, 1 + o, d[o][None, :], tm, bg, natural=last)
        if last:
            return out
        z, zb = out
        zcol = 0


def _ffn_body(x_ref, mod_ref, g_ref, w1_ref, w3_ref, w2_ref, o_ref):
    x = x_ref[...]
    h = _adaln_rows(x, g_ref[...], mod_ref[0, ROW_SHIFT2:ROW_SHIFT2 + 1, :],
                    mod_ref[0, ROW_SCALE2:ROW_SCALE2 + 1, :]).astype(BF16)
    a = jnp.dot(h, w1_ref[...], preferred_element_type=F32)
    b = jnp.dot(h, w3_ref[...], preferred_element_type=F32)
    act = (_silu(a) * b).astype(BF16)
    ff = jnp.dot(act, w2_ref[...], preferred_element_type=F32)
    o_ref[...] = x + mod_ref[0, ROW_GATE2:ROW_GATE2 + 1, :] * ff


def ffn_dense(x, mods, g, w1, w3, w2, seg_len, tm=TM_PROJ):
    T, D = x.shape
    F = w1.shape[1]
    resident = functools.partial(pl.BlockSpec, pipeline_mode=pl.Buffered(1))
    return pl.pallas_call(
        _ffn_body,
        grid=(T // tm,),
        in_specs=[
            pl.BlockSpec((tm, D), lambda i: (i, 0)),
            pl.BlockSpec((1, MOD_ROWS, D), lambda i: (i * tm // seg_len, 0, 0)),
            pl.BlockSpec((1, D), lambda i: (0, 0)),
            resident((D, F), lambda i: (0, 0)),
            resident((D, F), lambda i: (0, 0)),
            resident((F, D), lambda i: (0, 0)),
        ],
        out_specs=pl.BlockSpec((tm, D), lambda i: (i, 0)),
        out_shape=jax.ShapeDtypeStruct((T, D), F32),
        compiler_params=_cparams("parallel"),
        name="ffn_dense",
    )(x, mods, g, w1, w3, w2)


def _store_token_tiles(ref, x):
    n = x.shape[0]
    for s in range(SUBLANES):
        ref[pl.ds(s, n, stride=SUBLANES), :] = x[:, s * LANES:(s + 1) * LANES]


def _load_token_tiles(ref, n):
    return jnp.concatenate([ref[pl.ds(s, n, stride=SUBLANES), :] for s in range(SUBLANES)], axis=1)


def _router_body(xa_ref, xb_ref, mod_ref, g_ref, wr_ref, h_ref, idx_ref, gw_ref, *, na):
    x = jnp.where(pl.program_id(0) < na, xa_ref[...], xb_ref[...])
    h = _adaln_rows(x, g_ref[...], mod_ref[0, ROW_SHIFT2:ROW_SHIFT2 + 1, :],
                    mod_ref[0, ROW_SCALE2:ROW_SCALE2 + 1, :])
    _store_token_tiles(h_ref, h)
    logits = lax.dot_general(wr_ref[...], h, (((1,), (1,)), ((), ())),
                             precision=lax.Precision.HIGHEST, preferred_element_type=F32)
    eidx = lax.broadcasted_iota(jnp.int32, logits.shape, 0)
    m1 = jnp.max(logits, axis=0, keepdims=True)
    i1 = jnp.min(jnp.where(logits == m1, eidx, N_EXPERTS), axis=0, keepdims=True)
    rest = jnp.where(eidx == i1, -jnp.inf, logits)
    m2 = jnp.max(rest, axis=0, keepdims=True)
    i2 = jnp.min(jnp.where(rest == m2, eidx, N_EXPERTS), axis=0, keepdims=True)
    e2 = jnp.exp(m2 - m1)
    den = 1.0 + e2
    row = lax.broadcasted_iota(jnp.int32, logits.shape, 0)
    idx_ref[...] = jnp.where(row == 0, i1, jnp.where(row == 1, i2, 0))
    gw_ref[...] = jnp.transpose(jnp.where(row == 0, 1.0 / den, jnp.where(row == 1, e2 / den, 0.0)))


def _merged_seg(i, na, tm, seg_len_b):
    return jnp.where(i < na, 0, 1 + jnp.maximum(i - na, 0) * tm // seg_len_b)


def moe_router(xa, xb, mods, g, wr_t, seg_len_b, tm=TM_PROJ):
    D = xa.shape[1]
    na = xa.shape[0] // tm
    T = xa.shape[0] + xb.shape[0]
    return pl.pallas_call(
        functools.partial(_router_body, na=na),
        grid=(T // tm,),
        in_specs=[
            pl.BlockSpec((tm, D), lambda i: (jnp.minimum(i, na - 1), 0)),
            pl.BlockSpec((tm, D), lambda i: (jnp.maximum(i - na, 0), 0)),
            pl.BlockSpec((1, MOD_ROWS, D), lambda i: (_merged_seg(i, na, tm, seg_len_b), 0, 0)),
            pl.BlockSpec((1, D), lambda i: (0, 0)),
            pl.BlockSpec((N_EXPERTS, D), lambda i: (0, 0)),
        ],
        out_specs=[
            pl.BlockSpec((tm * SUBLANES, LANES), lambda i: (i, 0)),
            pl.BlockSpec((N_EXPERTS, tm), lambda i: (0, i)),
            pl.BlockSpec((tm, N_EXPERTS), lambda i: (i, 0)),
        ],
        out_shape=[
            jax.ShapeDtypeStruct((T * SUBLANES, LANES), F32),
            jax.ShapeDtypeStruct((N_EXPERTS, T), jnp.int32),
            jax.ShapeDtypeStruct((T, N_EXPERTS), F32),
        ],
        compiler_params=_cparams("parallel"),
        name="moe_router",
    )(xa, xb, mods, g, wr_t)


def _experts_body(te_ref, nv_ref, src0_ref, srcn_ref, dst_ref, h_hbm, w1_ref, w3_ref, w2_ref,
                  y_hbm, hrows, hb, acc, ybuf, sem_g, sem_s, *, tm, nf, n_slots):
    i = pl.program_id(0)
    f = pl.program_id(1)
    nv = nv_ref[0]
    valid = i < nv
    slot = lax.rem(i, 2)

    def gather_issue(idx_ref, s):
        def body(r, carry):
            src = pl.multiple_of(idx_ref[0, 0, r] * SUBLANES, SUBLANES)
            dst = pl.multiple_of(r * SUBLANES, SUBLANES)
            pltpu.make_async_copy(h_hbm.at[pl.ds(src, SUBLANES)], hrows.at[s, pl.ds(dst, SUBLANES)],
                                  sem_g.at[s]).start()
            return carry
        lax.fori_loop(0, tm, body, 0, unroll=8)

    def gather_wait(s):
        pltpu.make_async_copy(h_hbm.at[pl.ds(0, tm * SUBLANES)], hrows.at[s], sem_g.at[s]).wait()

    def scatter_wait(s):
        pltpu.make_async_copy(ybuf.at[s], y_hbm.at[pl.ds(0, tm * SUBLANES)], sem_s.at[s]).wait()

    @pl.when(jnp.logical_and(i == 0, f == 0))
    def _first():
        ybuf[0] = jnp.zeros(ybuf.shape[1:], ybuf.dtype)
        spare = lambda p: y_hbm.at[pl.ds((n_slots + p * tm) * SUBLANES, tm * SUBLANES)]
        for p in range(2):
            pltpu.make_async_copy(ybuf.at[0], spare(p), sem_s.at[p]).start()
        for p in range(2):
            pltpu.make_async_copy(ybuf.at[0], spare(p), sem_s.at[p]).wait()
        gather_issue(src0_ref, 0)

    @pl.when(jnp.logical_and(valid, f == 0))
    def _stage():
        gather_wait(slot)
        hb[...] = _load_token_tiles(hrows.at[slot], tm).astype(BF16)

        @pl.when(i + 1 < nv)
        def _prefetch():
            gather_issue(srcn_ref, 1 - slot)

    def partial_ffn():
        h = hb[...]
        a = jnp.dot(h, w1_ref[0], preferred_element_type=F32)
        b = jnp.dot(h, w3_ref[0], preferred_element_type=F32)
        act = (_silu(a) * b).astype(BF16)
        return jnp.dot(act, w2_ref[0], preferred_element_type=F32)

    @pl.when(jnp.logical_and(valid, f == 0))
    def _first_slice():
        acc[...] = partial_ffn()

    @pl.when(jnp.logical_and(valid, jnp.logical_and(f > 0, f < nf - 1)))
    def _middle_slice():
        acc[...] += partial_ffn()

    @pl.when(jnp.logical_and(valid, f == nf - 1))
    def _emit():
        y = acc[...] + partial_ffn()

        @pl.when(i >= 2)
        def _reuse():
            scatter_wait(slot)
        _store_token_tiles(ybuf.at[slot], y)

        def body(r, carry):
            src = pl.multiple_of(r * SUBLANES, SUBLANES)
            dst = pl.multiple_of(dst_ref[0, 0, r] * SUBLANES, SUBLANES)
            pltpu.make_async_copy(ybuf.at[slot, pl.ds(src, SUBLANES)], y_hbm.at[pl.ds(dst, SUBLANES)],
                                  sem_s.at[slot]).start()
            return carry
        lax.fori_loop(0, tm, body, 0, unroll=8)

    @pl.when(jnp.logical_and(i == nv, f == 0))
    def _drain():
        scatter_wait(lax.rem(nv + 1, 2))

        @pl.when(nv >= 2)
        def _older():
            scatter_wait(lax.rem(nv, 2))


def moe_experts(h, tile_expert, n_valid, src_rows, dst_rows, w1, w3, w2, tm=TM_MOE, tf=TF_MOE):
    T = h.shape[0] // SUBLANES
    D = w1.shape[1]
    F = w1.shape[2]
    n_tiles = src_rows.shape[0]
    n_slots = TOP_K * T
    nf = F // tf

    def wcol(i, f, te, nv):
        return (te[i], 0, jnp.where(i < nv[0], f, nf - 1))

    def wrow(i, f, te, nv):
        return (te[i], jnp.where(i < nv[0], f, nf - 1), 0)

    smem_tile = lambda fn: pl.BlockSpec((1, 1, tm), fn, memory_space=pltpu.SMEM)
    grid_spec = pltpu.PrefetchScalarGridSpec(
        num_scalar_prefetch=2,
        grid=(n_tiles, nf),
        in_specs=[
            smem_tile(lambda i, f, te, nv: (0, 0, 0)),
            smem_tile(lambda i, f, te, nv: (jnp.minimum(i + 1, n_tiles - 1), 0, 0)),
            smem_tile(lambda i, f, te, nv: (i, 0, 0)),
            pl.BlockSpec(memory_space=pl.ANY),
            pl.BlockSpec((1, D, tf), wcol),
            pl.BlockSpec((1, D, tf), wcol),
            pl.BlockSpec((1, tf, D), wrow),
        ],
        out_specs=pl.BlockSpec(memory_space=pl.ANY),
        scratch_shapes=[
            pltpu.VMEM((2, tm * SUBLANES, LANES), F32),
            pltpu.VMEM((tm, D), BF16),
            pltpu.VMEM((tm, D), F32),
            pltpu.VMEM((2, tm * SUBLANES, LANES), F32),
            pltpu.SemaphoreType.DMA((2,)),
            pltpu.SemaphoreType.DMA((2,)),
        ],
    )
    return pl.pallas_call(
        functools.partial(_experts_body, tm=tm, nf=nf, n_slots=n_slots),
        grid_spec=grid_spec,
        out_shape=jax.ShapeDtypeStruct(((n_slots + 2 * tm) * SUBLANES, LANES), F32),
        compiler_params=_cparams("arbitrary", "arbitrary"),
        name="moe_experts",
    )(tile_expert, n_valid, src_rows, src_rows, dst_rows, h, w1, w3, w2)


def _combine_body(x_ref, mod_ref, gw_ref, y0_ref, y1_ref, o_ref):
    n = x_ref.shape[0]
    gw = gw_ref[...]
    y = gw[:, 0:1] * _load_token_tiles(y0_ref, n) + gw[:, 1:2] * _load_token_tiles(y1_ref, n)
    o_ref[...] = x_ref[...] + mod_ref[0, ROW_GATE2:ROW_GATE2 + 1, :] * y


def moe_combine(x, mods, gw, y, row0, n_tok, seg_len, tm=TM_COMB):
    T, D = x.shape
    off = row0 // tm
    return pl.pallas_call(
        _combine_body,
        grid=(T // tm,),
        in_specs=[
            pl.BlockSpec((tm, D), lambda i: (i, 0)),
            pl.BlockSpec((1, MOD_ROWS, D), lambda i: (i * tm // seg_len, 0, 0)),
            pl.BlockSpec((tm, N_EXPERTS), lambda i: (off + i, 0)),
            pl.BlockSpec((tm * SUBLANES, LANES), lambda i: (off + i, 0)),
            pl.BlockSpec((tm * SUBLANES, LANES), lambda i: (off + n_tok // tm + i, 0)),
        ],
        out_specs=pl.BlockSpec((tm, D), lambda i: (i, 0)),
        out_shape=jax.ShapeDtypeStruct((T, D), F32),
        compiler_params=_cparams("parallel"),
        name="moe_combine",
    )(x, mods, gw, y, y)


def moe_dispatch_plan(idx, tm=TM_MOE):
    T = idx.shape[1]
    n_slots = TOP_K * T
    n_tiles = n_slots // tm + N_EXPERTS
    n_rows = n_tiles * tm
    experts = jnp.arange(N_EXPERTS, dtype=jnp.int32)[None, :]
    e_flat = idx[:TOP_K].reshape(n_slots)
    counts = jnp.sum((e_flat[:, None] == experts).astype(jnp.int32), axis=0)
    padded = (counts + tm - 1) // tm * tm
    ends = jnp.cumsum(padded)
    offs = ends - padded
    order = jnp.argsort(e_flat, stable=True).astype(jnp.int32)
    cstart = jnp.cumsum(counts) - counts
    rows = jnp.arange(n_rows, dtype=jnp.int32)
    row_e = jnp.minimum(jnp.sum((rows[:, None] >= ends[None, :]).astype(jnp.int32), axis=1), N_EXPERTS - 1)
    pick = lambda table: jnp.sum(jnp.where(row_e[:, None] == experts, table[None, :], 0), axis=1)
    j = rows - pick(offs)
    live = j < pick(counts)
    slot = order[jnp.clip(pick(cstart) + j, 0, n_slots - 1)]
    spare = n_slots + (rows // tm) % 2 * tm + rows % tm
    src_rows = jnp.where(live, slot % T, 0).astype(jnp.int32).reshape(n_tiles, 1, tm)
    dst_rows = jnp.where(live, slot, spare).astype(jnp.int32).reshape(n_tiles, 1, tm)
    tile_start = jnp.arange(n_tiles, dtype=jnp.int32) * tm
    n_valid = (ends[-1] // tm).astype(jnp.int32).reshape(1)
    tile_expert = jnp.minimum(jnp.sum((tile_start[:, None] >= ends[None, :]).astype(jnp.int32), axis=1),
                              N_EXPERTS - 1)
    last_e = tile_expert[jnp.maximum(n_valid[0] - 1, 0)]
    tile_expert = jnp.where(tile_start < ends[-1], tile_expert, last_e).astype(jnp.int32)
    return tile_expert, n_valid, src_rows, dst_rows


def ffn_moe(xa, xb, mods_a, mods_b, g, wr_t, w1, w3, w2, seg_len_b, tm=TM_PROJ, tme=TM_MOE, tmc=TM_COMB,
            tf=TF_MOE):
    na, nb = xa.shape[0], xb.shape[0]
    mods = jnp.concatenate([mods_a, mods_b], axis=0)
    h, idx, gw = moe_router(xa, xb, mods, g, wr_t, seg_len_b, tm)
    tile_expert, n_valid, src_rows, dst_rows = moe_dispatch_plan(idx, tme)
    y = moe_experts(h, tile_expert, n_valid, src_rows, dst_rows, w1, w3, w2, tme, tf)
    return (moe_combine(xa, mods_a, gw, y, 0, na + nb, na, tmc),
            moe_combine(xb, mods_b, gw, y, na, na + nb, seg_len_b, tmc))


def axial_rope(L):
    rows = L // GRID_W
    r = jnp.repeat(jnp.arange(rows, dtype=F32), GRID_W)
    col = jnp.tile(jnp.arange(GRID_W, dtype=F32), rows)
    n = HEAD_DIM // 4
    freqs = ROPE_BASE ** (-jnp.arange(n, dtype=F32) / n)
    ang = jnp.concatenate([r[:, None] * freqs, col[:, None] * freqs], axis=-1)
    return jnp.cos(ang), jnp.sin(ang)


def rope_tables(L):
    cos, sin = axial_rope(L)
    n = HEAD_DIM // 4
    cos_h = jnp.concatenate([cos[:, :n], cos[:, :n], cos[:, n:], cos[:, n:]], axis=1)
    sin_h = jnp.concatenate([-sin[:, :n], sin[:, :n], -sin[:, n:], sin[:, n:]], axis=1)
    reps = LANES // HEAD_DIM
    return jnp.tile(cos_h, (1, reps)), jnp.tile(sin_h, (1, reps))


def block_diag_ones(width, block):
    i = jnp.arange(width) // block
    return (i[:, None] == i[None, :]).astype(BF16)


def gla_gate_params(gw, gb):
    w = jnp.zeros((LANES, 2 * C_WIDTH), F32)
    w = w.at[:GLA_RANK, :C_WIDTH].set(gw[0]).at[GLA_RANK:2 * GLA_RANK, C_WIDTH:].set(gw[1])
    return w.astype(BF16), jnp.concatenate([gb[0], gb[1]])[None, :]


def gla_state_to_blockdiag(st):
    B = st.shape[0]
    out = jnp.zeros((B, 2, C_HEADS, C_DV, C_HEADS, C_DK), F32)
    for h in range(C_HEADS):
        out = out.at[:, :, h, :, h, :].set(jnp.swapaxes(st[:, :, h], -1, -2))
    return out.reshape(B, 2, C_WIDTH, C_WIDTH)


def gla_state_from_blockdiag(sT):
    B = sT.shape[0]
    s6 = sT.reshape(B, 2, C_HEADS, C_DV, C_HEADS, C_DK)
    return jnp.stack([jnp.swapaxes(s6[:, :, h, :, h, :], -1, -2) for h in range(C_HEADS)], axis=2)


def mod_table(cvec, w_ada, b_ada):
    m = jax.nn.silu(cvec) @ w_ada + b_ada
    m = m.reshape(cvec.shape[0], 6, D_MODEL)
    return jnp.pad(m, ((0, 0), (0, MOD_ROWS - 6), (0, 0)))


def kernel(x_prompt, x_sample, cache_k, cache_v, state_gla, c, c_ctx, norm1_g, norm2_g, w_ada, b_ada,
           w_in, w_out, q_norm_g, k_norm_g, attn_sink, hy_conv_w, hy_conv_b, hy_w1, hy_b1, hy_freq1,
           hy_w2, hy_b2, hy_freq2, hy_w3, hy_decay, hy_d, gla_gate_w, gla_gate_b, gla_norm_g,
           ffn_w1, ffn_w3, ffn_w2, moe_router, moe_w1, moe_w3, moe_w2):
    D = D_MODEL
    xp = x_prompt.reshape(BATCH * SEQ, D)
    xs = x_sample.reshape(DEC_BATCH * DEC_SEQ, D)
    streams = [
        dict(x=xp, B=BATCH, L=SEQ, seg_len=BATCH * SEQ, cvec=c_ctx[None, :], latent=False),
        dict(x=xs, B=DEC_BATCH, L=DEC_SEQ, seg_len=DEC_SEQ, cvec=c, latent=True),
    ]
    ks_list, vs_list, st_list = [], [], []
    bd_q = block_diag_ones(A_WIDTH, HEAD_DIM)
    dft = {s['L']: dft_half_matrices(s['L'], min(TM_DFT, s['L'] // 2)) for s in streams}
    for l in range(DEPTH):
        lp = {
            'hy_conv_w': hy_conv_w[l], 'hy_conv_b': hy_conv_b[l],
            'hy_w1': hy_w1[l], 'hy_b1': hy_b1[l], 'hy_freq1': hy_freq1[l], 'hy_w2': hy_w2[l],
            'hy_b2': hy_b2[l], 'hy_freq2': hy_freq2[l], 'hy_w3': hy_w3[l], 'hy_decay': hy_decay[l],
            'hy_d': hy_d[l],
        }
        w_in_l = jnp.pad(w_in[l], ((0, 0), (0, D_PROJ_PAD - D_PROJ))).astype(BF16)
        w_out_l = w_out[l].astype(BF16)
        g1 = norm1_g[l][None, :]
        g2 = norm2_g[l][None, :]
        qg = jnp.tile(q_norm_g[l], A_HEADS)[None, :]
        kg = jnp.tile(k_norm_g[l], A_KV_HEADS)[None, :]
        gg = jnp.tile(gla_norm_g[l], C_HEADS)[None, :]
        gate_w, gate_b = gla_gate_params(gla_gate_w[l], gla_gate_b[l])
        j = l // 2
        for s in streams:
            B, L = s['B'], s['L']
            mods = mod_table(s['cvec'], w_ada[l], b_ada[l])
            rope = rope_tables(L) if s['latent'] else None
            q, k, v, hy, cq, ck, cv, cg, la = in_proj(s['x'], mods, g1, w_in_l, bd_q, qg, kg, gate_w, gate_b,
                                                      s['seg_len'], rope, L)
            seq = lambda t: t.reshape(B, L, t.shape[-1])
            if s['latent']:
                kc = cache_k[:, l].reshape(DEC_BATCH, PAST_LEN, LANES)
                vc = cache_v[:, l].reshape(DEC_BATCH, PAST_LEN, LANES)
                a_out = attention(seq(q), seq(k), seq(v), attn_sink[l], kc, vc)
                s0 = gla_state_to_blockdiag(state_gla[:, l])
                o_f, o_b, _ = gla(seq(cq), seq(ck), seq(cv), seq(la), s0)
            else:
                a_out = attention(seq(q), seq(k), seq(v), attn_sink[l])
                o_f, o_b, sT = gla(seq(cq), seq(ck), seq(cv), seq(la))
                ks_list.append(k.reshape(B, L, A_KV_HEADS, HEAD_DIM))
                vs_list.append(v.reshape(B, L, A_KV_HEADS, HEAD_DIM))
                st_list.append(gla_state_from_blockdiag(sT))
            h_out = hyena_half(seq(hy), lp, dft[L])
            flat = lambda t: t.reshape(B * L, t.shape[-1])
            x1 = out_proj(s['x'], flat(a_out), [flat(h) for h in h_out], flat(o_f), flat(o_b), cg, mods, w_out_l,
                          bd_q[:C_WIDTH, :C_WIDTH], gg, s['seg_len'])
            if l % 2 == 0:
                s['x'] = ffn_dense(x1, mods, g2, ffn_w1[j].astype(BF16), ffn_w3[j].astype(BF16),
                                   ffn_w2[j].astype(BF16), s['seg_len'])
            else:
                s['x'], s['mods'] = x1, mods
        if l % 2 == 1:
            sa, sb = streams
            sa['x'], sb['x'] = ffn_moe(sa['x'], sb['x'], sa['mods'], sb['mods'], g2, moe_router[j].T,
                                       moe_w1[j].astype(BF16), moe_w3[j].astype(BF16), moe_w2[j].astype(BF16),
                                       sb['seg_len'])
    y_prompt = streams[0]['x'].reshape(BATCH, SEQ, D)
    y_sample = streams[1]['x'].reshape(DEC_BATCH, DEC_SEQ, D)
    new_cache_k = jnp.stack(ks_list, axis=1)
    new_cache_v = jnp.stack(vs_list, axis=1)
    new_state_gla = jnp.stack(st_list, axis=1)
    return (y_prompt, y_sample, new_cache_k, new_cache_v, new_state_gla)
```

```python
import math
import functools
import jax
import jax.numpy as jnp
from jax import lax
from jax.experimental import pallas as pl
from jax.experimental.pallas import tpu as pltpu

D_MODEL = 1024
BATCH = 32
SEQ = 256
DEPTH = 2
DEC_BATCH = 8
DEC_SEQ = 4096
PAST_LEN = 512

GRID_W = 64
HEAD_DIM = 64
A_HEADS = 8
A_KV_HEADS = 2
A_GROUP = A_HEADS // A_KV_HEADS
A_WIDTH = A_HEADS * HEAD_DIM
WINDOW = 128
ROPE_BASE = 10000.0
HY_CH = 256
HY_ORDER = 2
HY_BANDS = 16
HY_EMB = 1 + 2 * HY_BANDS
HY_HID = 64
C_HEADS = 4
C_DK = 64
C_DV = 64
C_WIDTH = C_HEADS * C_DV
GLA_RANK = 16
GLA_TAU = 16.0
GLA_CHUNK = 64
D_FF = 2816
N_EXPERTS = 8
TOP_K = 2
D_FF_EXPERT = 3584

PROJ_SIZES = (A_WIDTH, A_KV_HEADS * HEAD_DIM, A_KV_HEADS * HEAD_DIM, 3 * HY_CH,
              C_HEADS * C_DK, C_HEADS * C_DK, C_WIDTH, C_WIDTH, 2 * GLA_RANK)
D_PROJ = int(sum(PROJ_SIZES))

F32 = jnp.float32
BF16 = jnp.bfloat16
ATT_SCALE = HEAD_DIM ** -0.5
NEG_INF = -1e30
EPS = 1e-6

LANES = 128
SUBLANES = 8
D_PROJ_PAD = -(-D_PROJ // LANES) * LANES
VMEM_LIMIT_BYTES = 56 * 1024 * 1024

MOD_ROWS = SUBLANES
ROW_SHIFT1, ROW_SCALE1, ROW_GATE1, ROW_SHIFT2, ROW_SCALE2, ROW_GATE2 = range(6)

TM_PROJ = 512
TM_MOE = 512
TF_MOE = 1792
TM_COMB = 512


def _cparams(*sem):
    return pltpu.CompilerParams(dimension_semantics=sem, vmem_limit_bytes=VMEM_LIMIT_BYTES)


def _adaln_rows(x, g, shift, scale):
    ms = jnp.mean(x * x, axis=-1, keepdims=True)
    return (x * lax.rsqrt(ms + EPS) * g) * (1.0 + scale) + shift


def _silu(a):
    return a * jax.nn.sigmoid(a)


def _group_mean_sq(x, ones_bd):
    sq = x * x
    hi = sq.astype(BF16)
    lo = (sq - hi.astype(F32)).astype(BF16)
    s = jnp.dot(hi, ones_bd, preferred_element_type=F32) + jnp.dot(lo, ones_bd, preferred_element_type=F32)
    return s * (1.0 / HEAD_DIM)


def _rope_rows(x, cos_t, sin_t):
    q4 = HEAD_DIM // 4
    lane = lax.broadcasted_iota(jnp.int32, x.shape, 1)
    partner = jnp.where((lane % (2 * q4)) < q4, pltpu.roll(x, LANES - q4, 1), pltpu.roll(x, q4, 1))
    return x * cos_t + partner * sin_t


def _log_sigmoid(x):
    return jnp.minimum(x, 0.0) - jnp.log(1.0 + jnp.exp(-jnp.abs(x)))


def _inproj_body(*refs, latent):
    if latent:
        (x_ref, mod_ref, g_ref, w_ref, bd_ref, qg_ref, kg_ref, gw_ref, gb_ref, cos_ref, sin_ref,
         q_ref, k_ref, v_ref, hy_ref, cq_ref, ck_ref, cv_ref, cg_ref, la_ref) = refs
    else:
        (x_ref, mod_ref, g_ref, w_ref, bd_ref, qg_ref, kg_ref, gw_ref, gb_ref,
         q_ref, k_ref, v_ref, hy_ref, cq_ref, ck_ref, cv_ref, cg_ref, la_ref) = refs
    tm = x_ref.shape[0]
    for half in range(2):
        rs = pl.ds(half * (tm // 2), tm // 2)
        h = _adaln_rows(x_ref[rs, :], g_ref[...], mod_ref[0, ROW_SHIFT1:ROW_SHIFT1 + 1, :],
                        mod_ref[0, ROW_SCALE1:ROW_SCALE1 + 1, :])
        acc = jnp.dot(h.astype(BF16), w_ref[...], preferred_element_type=F32)
        o = 0
        q = acc[:, o:o + A_WIDTH]
        o += A_WIDTH
        k = acc[:, o:o + LANES]
        o += LANES
        v_ref[rs, :] = acc[:, o:o + LANES]
        o += LANES
        hy_ref[rs, :] = acc[:, o:o + 3 * HY_CH]
        o += 3 * HY_CH
        cq_ref[rs, :] = acc[:, o:o + C_WIDTH] * (C_DK ** -0.5)
        o += C_WIDTH
        ck_ref[rs, :] = acc[:, o:o + C_WIDTH]
        o += C_WIDTH
        cv_ref[rs, :] = acc[:, o:o + C_WIDTH].astype(cv_ref.dtype)
        o += C_WIDTH
        cg_ref[rs, :] = acc[:, o:o + C_WIDTH]
        o += C_WIDTH
        r = acc[:, o:o + LANES]
        la_ref[rs, :] = _log_sigmoid(jnp.dot(r.astype(BF16), gw_ref[...], preferred_element_type=F32)
                                     + gb_ref[...]) * (1.0 / GLA_TAU)
        q = q * lax.rsqrt(_group_mean_sq(q, bd_ref[...]) + EPS) * qg_ref[...]
        k = k * lax.rsqrt(_group_mean_sq(k, bd_ref[0:LANES, 0:LANES]) + EPS) * kg_ref[...]
        if latent:
            cos_t = cos_ref[rs, :]
            sin_t = sin_ref[rs, :]
            q = jnp.concatenate([_rope_rows(q[:, j * LANES:(j + 1) * LANES], cos_t, sin_t)
                                 for j in range(A_WIDTH // LANES)], axis=1)
            k = _rope_rows(k, cos_t, sin_t)
        q_ref[rs, :] = (q * ATT_SCALE).astype(BF16)
        k_ref[rs, :] = k


def in_proj(x, mods, g, w, bd, qg, kg, gw, gb, seg_len, rope=None, seq_len=None, tm=TM_PROJ):
    T, D = x.shape
    N = w.shape[1]
    latent = rope is not None
    const = lambda shape: pl.BlockSpec(shape, lambda i: (0,) * len(shape))
    in_specs = [
        pl.BlockSpec((tm, D), lambda i: (i, 0)),
        pl.BlockSpec((1, MOD_ROWS, D), lambda i: (i * tm // seg_len, 0, 0)),
        const((1, D)), const((D, N)), const((A_WIDTH, A_WIDTH)), const((1, A_WIDTH)), const((1, LANES)),
        const((LANES, 2 * C_WIDTH)), const((1, 2 * C_WIDTH)),
    ]
    args = [x, mods, g, w, bd, qg, kg, gw, gb]
    if latent:
        tiles_per_seq = seq_len // tm
        in_specs += [pl.BlockSpec((tm, LANES), lambda i: (i % tiles_per_seq, 0))] * 2
        args += list(rope)
    widths = [A_WIDTH, LANES, LANES, 3 * HY_CH, C_WIDTH, C_WIDTH, C_WIDTH, C_WIDTH, 2 * C_WIDTH]
    dtypes = [BF16, F32, F32, F32, F32, F32, BF16, F32, F32]
    return pl.pallas_call(
        functools.partial(_inproj_body, latent=latent),
        grid=(T // tm,),
        in_specs=in_specs,
        out_specs=[pl.BlockSpec((tm, wd), lambda i: (i, 0)) for wd in widths],
        out_shape=[jax.ShapeDtypeStruct((T, wd), dt) for wd, dt in zip(widths, dtypes)],
        compiler_params=_cparams("parallel"),
        name="in_proj_latent" if latent else "in_proj_context",
    )(*args)


def _outproj_body(x_ref, a_ref, h0_ref, h1_ref, of_ref, ob_ref, cg_ref, mod_ref, w_ref, bd_ref, gg_ref, o_ref):
    o = of_ref[...] + ob_ref[...]
    g_out = o * lax.rsqrt(_group_mean_sq(o, bd_ref[...]) + EPS) * gg_ref[...] * _silu(cg_ref[...])
    h0 = A_WIDTH
    g0 = A_WIDTH + HY_CH
    mix = jnp.dot(a_ref[...], w_ref[0:h0, :], preferred_element_type=F32)
    h = jnp.concatenate([h0_ref[...], h1_ref[...]], axis=1)
    mix += jnp.dot(h.astype(BF16), w_ref[h0:g0, :], preferred_element_type=F32)
    mix += jnp.dot(g_out.astype(BF16), w_ref[g0:, :], preferred_element_type=F32)
    o_ref[...] = x_ref[...] + mod_ref[0, ROW_GATE1:ROW_GATE1 + 1, :] * mix


def out_proj(x, a, h, of, ob, cg, mods, w, bd, gg, seg_len, tm=TM_PROJ):
    T, D = x.shape
    row = lambda wd: pl.BlockSpec((tm, wd), lambda i: (i, 0))
    const = lambda shape: pl.BlockSpec(shape, lambda i: (0,) * len(shape))
    return pl.pallas_call(
        _outproj_body,
        grid=(T // tm,),
        in_specs=[
            row(D), row(A_WIDTH), row(LANES), row(LANES), row(C_WIDTH), row(C_WIDTH), row(C_WIDTH),
            pl.BlockSpec((1, MOD_ROWS, D), lambda i: (i * tm // seg_len, 0, 0)),
            const((D, D)), const((C_WIDTH, C_WIDTH)), const((1, C_WIDTH)),
        ],
        out_specs=row(D),
        out_shape=jax.ShapeDtypeStruct((T, D), F32),
        compiler_params=_cparams("parallel"),
        name="out_proj",
    )(x, a, h[0], h[1], of, ob, cg, mods, w, bd, gg)


TQ_ATT = 256
HEADS_PER_STACK = 2
KWIN_ATT = TQ_ATT + 2 * WINDOW


def _dup_heads(x, g):
    lane = lax.broadcasted_iota(jnp.int32, x.shape, 1)
    rolled = pltpu.roll(x, HEAD_DIM, 1)
    keep = (lane < HEAD_DIM) if g == 0 else (lane >= HEAD_DIM)
    return jnp.where(keep, x, rolled)


def _attn_body(*refs, latent, seq_len):
    if latent:
        sink_ref, q_ref, k_ref, v_ref, kc_ref, vc_ref, o_ref = refs
    else:
        sink_ref, q_ref, k_ref, v_ref, o_ref = refs
    tq = q_ref.shape[1]
    i = pl.program_id(1)
    q = q_ref[0]
    lane = lax.broadcasted_iota(jnp.int32, (tq, LANES), 1)
    low = lane < HEAD_DIM
    if latent:
        start = jnp.clip(i * tq - WINDOW, 0, seq_len - KWIN_ATT)
        start = pl.multiple_of(start, WINDOW)
        kl = k_ref[0, pl.ds(start, KWIN_ATT), :]
        vl = v_ref[0, pl.ds(start, KWIN_ATT), :]
        qpos = i * tq + lax.broadcasted_iota(jnp.int32, (tq, KWIN_ATT), 0)
        kpos = start + lax.broadcasted_iota(jnp.int32, (tq, KWIN_ATT), 1)
        bias = jnp.where(jnp.abs(qpos - kpos) <= WINDOW, 0.0, NEG_INF).astype(F32)
        bias = jnp.concatenate([bias] * A_GROUP, axis=0)
        kc = kc_ref[0]
        vc = vc_ref[0]
    else:
        kl = k_ref[0]
        vl = v_ref[0]
    nt = (((1,), (1,)), ((), ()))
    zero = jnp.zeros_like(q[:, :LANES])
    kls = [_dup_heads(kl, g).astype(BF16) for g in range(A_KV_HEADS)]
    vls = [_dup_heads(vl, g).astype(BF16) for g in range(A_KV_HEADS)]
    if latent:
        kcs = [_dup_heads(kc, g).astype(BF16) for g in range(A_KV_HEADS)]
        vcs = [_dup_heads(vc, g).astype(BF16) for g in range(A_KV_HEADS)]

    def run(stacks):
        gs = [g for g, _ in stacks]
        n = range(len(stacks))
        qs = [jnp.concatenate([jnp.where(low if h % 2 == 0 else jnp.logical_not(low),
                                         q[:, (h // 2) * LANES:(h // 2 + 1) * LANES], zero) for h in heads], axis=0)
              for _, heads in stacks]
        sinks = [jnp.concatenate([jnp.full((tq, 1), sink_ref[h], F32) for h in heads], axis=0)
                 for _, heads in stacks]
        s_loc = [lax.dot_general(qs[i], kls[gs[i]], nt, preferred_element_type=F32) for i in n]
        m = sinks
        if latent:
            s_loc = [s + bias[:s.shape[0]] for s in s_loc]
            s_ctx = [lax.dot_general(qs[i], kcs[gs[i]], nt, preferred_element_type=F32) for i in n]
            m = [jnp.maximum(m[i], jnp.max(s_ctx[i], axis=-1, keepdims=True)) for i in n]
        m = [jnp.maximum(m[i], jnp.max(s_loc[i], axis=-1, keepdims=True)) for i in n]
        p_loc = [jnp.exp(s_loc[i] - m[i]) for i in n]
        den = [jnp.exp(sinks[i] - m[i]) + jnp.sum(p_loc[i], axis=-1, keepdims=True) for i in n]
        acc = [jnp.dot(p_loc[i].astype(BF16), vls[gs[i]], preferred_element_type=F32) for i in n]
        if latent:
            p_ctx = [jnp.exp(s_ctx[i] - m[i]) for i in n]
            den = [den[i] + jnp.sum(p_ctx[i], axis=-1, keepdims=True) for i in n]
            acc = [acc[i] + jnp.dot(p_ctx[i].astype(BF16), vcs[gs[i]], preferred_element_type=F32) for i in n]
        outs = []
        for i in n:
            og = acc[i] / den[i]
            for jj in range(len(stacks[i][1]) // 2):
                outs.append(jnp.where(low, og[(2 * jj) * tq:(2 * jj + 1) * tq],
                                      og[(2 * jj + 1) * tq:(2 * jj + 2) * tq]))
        return outs

    stacks = [(g, range(g * A_GROUP + j, g * A_GROUP + j + HEADS_PER_STACK))
              for g in range(A_KV_HEADS) for j in range(0, A_GROUP, HEADS_PER_STACK)]
    if latent:
        outs = [o for st in stacks for o in run([st])]
    else:
        outs = run(stacks)
    o_ref[0] = jnp.concatenate(outs, axis=1).astype(o_ref.dtype)


def attention(q, k, v, sink, kc=None, vc=None, tq=TQ_ATT):
    B, L, _ = q.shape
    latent = kc is not None
    tq = min(tq, L)
    seq = lambda wd: pl.BlockSpec((1, L, wd), lambda b, i: (b, 0, 0))
    in_specs = [
        pl.BlockSpec(memory_space=pltpu.SMEM),
        pl.BlockSpec((1, tq, A_WIDTH), lambda b, i: (b, i, 0)),
        seq(LANES), seq(LANES),
    ]
    args = [sink, q, k, v]
    if latent:
        P = kc.shape[1]
        in_specs += [pl.BlockSpec((1, P, LANES), lambda b, i: (b, 0, 0))] * 2
        args += [kc, vc]
    return pl.pallas_call(
        functools.partial(_attn_body, latent=latent, seq_len=L),
        grid=(B, L // tq),
        in_specs=in_specs,
        out_specs=pl.BlockSpec((1, tq, A_WIDTH), lambda b, i: (b, i, 0)),
        out_shape=jax.ShapeDtypeStruct((B, L, A_WIDTH), BF16),
        compiler_params=_cparams("parallel", "arbitrary"),
        name="attention_latent" if latent else "attention_context",
    )(*args)


def _split3(x):
    hi = x.astype(BF16)
    r = x - hi.astype(F32)
    mid = r.astype(BF16)
    lo = (r - mid.astype(F32)).astype(BF16)
    return hi, mid, lo


def _gla_group(q_ref, k_ref, v_ref, la_ref, o_ref, st_ref, b, d, reverse, n_chunks):
    C = GLA_CHUNK
    W = C_WIDTH
    ti = lax.broadcasted_iota(jnp.int32, (C, C), 0)
    si = lax.broadcasted_iota(jnp.int32, (C, C), 1)
    tri = (si >= ti) if reverse else (si <= ti)
    tri_b = tri.astype(BF16)
    tri4 = jnp.concatenate([tri] * C_HEADS, axis=0)
    r4 = lax.broadcasted_iota(jnp.int32, (C_HEADS * C, W), 0) // C
    c4 = lax.broadcasted_iota(jnp.int32, (C_HEADS * C, W), 1) // C_DK
    same_head = r4 == c4
    nt = (((1,), (1,)), ((), ()))
    tn = (((0,), (0,)), ((), ()))
    chunks = range(n_chunks)
    rows = [pl.ds(c * C, C) for c in chunks]
    vbs = [v_ref[b, r, :].astype(BF16) for r in rows]
    parts = [_split3(la_ref[b, r, :]) for r in rows]
    bsums = [jnp.dot(tri_b, hi, preferred_element_type=F32) + jnp.dot(tri_b, mid, preferred_element_type=F32)
             + jnp.dot(tri_b, lo, preferred_element_type=F32) for hi, mid, lo in parts]
    b_lasts = [s[0:1] if reverse else s[C - 1:C] for s in bsums]
    qgs = [q_ref[b, r, :] * jnp.exp(s) for r, s in zip(rows, bsums)]
    kgs = [(k_ref[b, r, :] * jnp.exp(-s)).astype(BF16) for r, s in zip(rows, bsums)]
    kds = [(k_ref[b, r, :] * jnp.exp(bl - s)).astype(BF16) for r, s, bl in zip(rows, bsums, b_lasts)]
    decays = [jnp.exp(bl) for bl in b_lasts]
    q_bds = [jnp.where(same_head, jnp.concatenate([qg] * C_HEADS, axis=0), 0.0).astype(BF16) for qg in qgs]
    uts = [jnp.where(same_head, lax.dot_general(vb, kd, tn, preferred_element_type=F32), 0.0)
           for vb, kd in zip(vbs, kds)]
    a_s = [jnp.where(tri4, lax.dot_general(qb, kg, nt, preferred_element_type=F32), 0.0).astype(BF16)
           for qb, kg in zip(q_bds, kgs)]
    r_s = [jnp.where(same_head, jnp.dot(a, vb, preferred_element_type=F32), 0.0)
           for a, vb in zip(a_s, vbs)]
    o_intra = [sum([r[h * C:(h + 1) * C] for h in range(1, C_HEADS)], r[0:C]) for r in r_s]
    qgb = [qg.astype(BF16) for qg in qgs]
    st = st_ref[d]
    for c in (reversed(chunks) if reverse else chunks):
        o_ref[b, rows[c], :] = o_intra[c] + lax.dot_general(qgb[c], st.astype(BF16), nt,
                                                            preferred_element_type=F32)
        st = st * decays[c] + uts[c]
    st_ref[d] = st


def _gla_body(*refs, has_state, n_chunks):
    if has_state:
        (qf, kf, vf, lf, qb, kb, vb, lb, s0_ref, of_ref, ob_ref, sT_ref, st) = refs
    else:
        (qf, kf, vf, lf, qb, kb, vb, lb, of_ref, ob_ref, sT_ref, st) = refs
    j = pl.program_id(1)

    nb = qf.shape[0]

    @pl.when(j == 0)
    def _init():
        if has_state:
            st[...] = s0_ref[...].reshape(st.shape)
        else:
            st[...] = jnp.zeros_like(st)

    for b in range(nb):
        _gla_group(qf, kf, vf, lf, of_ref, st, b, 2 * b, False, n_chunks)
        _gla_group(qb, kb, vb, lb, ob_ref, st, b, 2 * b + 1, True, n_chunks)

    @pl.when(j == pl.num_programs(1) - 1)
    def _final():
        sT_ref[...] = st[...].reshape(sT_ref.shape)


NB_GLA = 1


def gla(cq, ck, cv, la, s0=None, rows=512, nb=NB_GLA):
    B, L, W = cq.shape
    assert B % nb == 0
    rows = min(rows, L)
    ng = L // rows
    has_state = s0 is not None
    fwd = lambda: pl.BlockSpec((nb, rows, W), lambda b, j: (b, j, 0))
    bwd = lambda: pl.BlockSpec((nb, rows, W), lambda b, j: (b, ng - 1 - j, 0))
    state = lambda: pl.BlockSpec((nb, 2, W, W), lambda b, j: (b, 0, 0, 0))
    in_specs = [fwd(), fwd(), fwd(), pl.BlockSpec((nb, rows, W), lambda b, j: (b, j, 0)),
                bwd(), bwd(), bwd(), pl.BlockSpec((nb, rows, W), lambda b, j: (b, ng - 1 - j, 1))]
    args = [cq, ck, cv, la, cq, ck, cv, la]
    if has_state:
        in_specs.append(state())
        args.append(s0)
    return pl.pallas_call(
        functools.partial(_gla_body, has_state=has_state, n_chunks=rows // GLA_CHUNK),
        grid=(B // nb, ng),
        in_specs=in_specs,
        out_specs=[fwd(), bwd(), state()],
        out_shape=[jax.ShapeDtypeStruct((B, L, W), F32), jax.ShapeDtypeStruct((B, L, W), F32),
                   jax.ShapeDtypeStruct((B, 2, W, W), F32)],
        scratch_shapes=[pltpu.VMEM((2 * nb, W, W), F32)],
        compiler_params=_cparams("parallel", "arbitrary"),
        name="gla",
    )(*args)


TM_DFT = 512
BG_HALF = 2


def _freq_weight(i, tm, n):
    k = i * tm + lax.broadcasted_iota(jnp.int32, (tm, 1), 0)
    return jnp.where(k == 0, 1.0 / n, 2.0 / n).astype(F32)


def hyena_time_filters(L, lp):
    t = jnp.arange(L, dtype=F32)
    t_norm = t / max(L - 1, 1)
    w = (2.0 * math.pi / L) * t
    f = jnp.linspace(1e-4, HY_BANDS - 1, HY_BANDS, dtype=F32)
    fw = w[:, None] * f[None, :]
    feat = jnp.concatenate([t_norm[:, None], jnp.cos(fw), -jnp.sin(fw)], axis=-1)
    z = jnp.sin(lp['hy_freq1'] * (feat @ lp['hy_w1'] + lp['hy_b1']))
    z = jnp.sin(lp['hy_freq2'] * (z @ lp['hy_w2'] + lp['hy_b2']))
    hf = (z @ lp['hy_w3']).astype(F32).reshape(L, 2, HY_ORDER, HY_CH)
    hf = hf * jnp.exp(-t_norm[:, None, None, None] * jnp.abs(lp['hy_decay'].astype(F32)))
    return hf / (jnp.sum(jnp.abs(hf), axis=(0, 1), keepdims=True) + EPS)


def _hi_lo(x):
    hi = x.astype(BF16)
    return jnp.stack([hi, (x - hi.astype(F32)).astype(BF16)])


def _shift_rows(x, down):
    n = x.shape[0]
    row = lax.broadcasted_iota(jnp.int32, (n, 1), 0)
    if down:
        return jnp.where(row == 0, 0.0, pltpu.roll(x, 1, 0))
    return jnp.where(row == n - 1, 0.0, pltpu.roll(x, n - 1, 0))


def _conv3_planes_body(x_ref, w_ref, b_ref, u_ref, ub_ref):
    H = x_ref.shape[0] // 2
    e = x_ref[pl.ds(0, H, stride=2), :]
    o = x_ref[pl.ds(1, H, stride=2), :]
    w0, w1, w2 = w_ref[0:1, :], w_ref[1:2, :], w_ref[2:3, :]
    ue = _shift_rows(o, True) * w0 + e * w1 + o * w2 + b_ref[...]
    uo = e * w0 + o * w1 + _shift_rows(e, False) * w2 + b_ref[...]
    u_ref[0, 0] = ue
    u_ref[0, 1] = uo
    ub_ref[0, 0] = ue.astype(BF16)
    ub_ref[0, 1] = uo.astype(BF16)


def hyena_conv3_planes(hy, w, b):
    B, L, C3 = hy.shape
    H = L // 2
    out = pl.BlockSpec((1, 2, H, LANES), lambda b_, j: (b_, 0, 0, j))
    return pl.pallas_call(
        _conv3_planes_body,
        grid=(B, C3 // LANES),
        in_specs=[pl.BlockSpec((None, L, LANES), lambda b_, j: (b_, 0, j)),
                  pl.BlockSpec((SUBLANES, LANES), lambda b_, j: (0, j)),
                  pl.BlockSpec((1, LANES), lambda b_, j: (0, j))],
        out_specs=[out, out],
        out_shape=[jax.ShapeDtypeStruct((B, 2, H, C3), F32), jax.ShapeDtypeStruct((B, 2, H, C3), BF16)],
        compiler_params=_cparams("parallel", "parallel"),
        name="hyena_conv3",
    )(hy, w, b)


def _half_filt_body(fe_ref, fo_ref, sg_ref, hse_ref, hso_ref, hde_ref, hdo_ref,
                    lr_ref, li_ref, hr_ref, hi_ref, sp_ref, *, n):
    i = pl.program_id(0)
    tm = lr_ref.shape[0]
    wk = _freq_weight(i, tm, n)

    def two_pass(m, parts_ref):
        return (jnp.dot(m, parts_ref[0], preferred_element_type=F32)
                + jnp.dot(m, parts_ref[1], preferred_element_type=F32))
    a = two_pass(fe_ref[0, :tm], hse_ref)
    b = two_pass(fo_ref[0, :tm], hso_ref)
    c = two_pass(fe_ref[0, tm:], hde_ref)
    d = two_pass(fo_ref[0, tm:], hdo_ref)
    lr_ref[...] = wk * (a + b)
    hr_ref[...] = wk * (a - b)
    li_ref[...] = wk * (c + d)
    hi_ref[...] = wk * (d - c)

    @pl.when(i == 0)
    def _middle_bin():
        sr = two_pass(sg_ref[...], hse_ref)[0:1, :]
        si = -two_pass(sg_ref[...], hdo_ref)[0:1, :]
        row = lax.broadcasted_iota(jnp.int32, sp_ref.shape, 0)
        sp_ref[...] = jnp.where(row == 0, sr, jnp.where(row == 1, si, 0.0)) * (2.0 / n)


def hyena_half_filter_spectrum(fe, fo, sg, hs, hd, tm):
    H = fe.shape[2]
    W = hs.shape[2]
    planes = lambda h: (h.reshape(2, H, 2, W)[:, :, 0], h.reshape(2, H, 2, W)[:, :, 1])
    hse, hso = planes(hs)
    hde, hdo = planes(hd)
    whole = lambda shape: pl.BlockSpec(shape, lambda i: (0,) * len(shape))
    ftile = pl.BlockSpec((1, 2 * tm, H), lambda i: (i, 0, 0))
    otile = pl.BlockSpec((tm, W), lambda i: (i, 0))
    out = jax.ShapeDtypeStruct((H, W), F32)
    return pl.pallas_call(
        functools.partial(_half_filt_body, n=4 * H),
        grid=(H // tm,),
        in_specs=[ftile, ftile, whole((SUBLANES, H))] + [whole((2, H, W))] * 4,
        out_specs=[otile, otile, otile, otile, whole((SUBLANES, W))],
        out_shape=[out, out, out, out, jax.ShapeDtypeStruct((SUBLANES, W), F32)],
        compiler_params=_cparams("arbitrary"),
        name="hyena_filter_spectrum",
    )(fe, fo, sg, hse, hso, hde, hdo)


def _cmul(xr, xi, hr, hi):
    return xr * hr - xi * hi, xr * hi + xi * hr


def _half_fwd_body(fe_ref, fo_ref, sg_ref, ze_ref, zo_ref, lr_ref, li_ref, hr_ref, hi_ref, sp_ref,
                   per_ref, pei_ref, por_ref, poi_ref, ysp_ref):
    nb = ze_ref.shape[0]
    tm = lr_ref.shape[0]

    def body(b, carry):
        a = jnp.dot(fe_ref[0], ze_ref[b], preferred_element_type=F32)
        o = jnp.dot(fo_ref[0], zo_ref[b], preferred_element_type=F32)
        ylr, yli = _cmul(a[:tm] + o[:tm], a[tm:] + o[tm:], lr_ref[...], li_ref[...])
        yhr, yhi = _cmul(a[:tm] - o[:tm], o[tm:] - a[tm:], hr_ref[...], hi_ref[...])
        per_ref[b] = (ylr + yhr).astype(BF16)
        pei_ref[b] = (yli - yhi).astype(BF16)
        por_ref[b] = (ylr - yhr).astype(BF16)
        poi_ref[b] = (yli + yhi).astype(BF16)
        return carry
    lax.fori_loop(0, nb, body, 0)

    @pl.when(pl.program_id(1) == 0)
    def _middle_bin():
        def mid(b, carry):
            xr = jnp.dot(sg_ref[...], ze_ref[b], preferred_element_type=F32)[0:1, :]
            xi = -jnp.dot(sg_ref[...], zo_ref[b], preferred_element_type=F32)[0:1, :]
            yr, yi = _cmul(xr, xi, sp_ref[0:1, :], sp_ref[1:2, :])
            row = lax.broadcasted_iota(jnp.int32, ysp_ref.shape[1:], 0)
            ysp_ref[b] = jnp.where(row == 0, yr, jnp.where(row == 1, yi, 0.0))
            return carry
        lax.fori_loop(0, nb, mid, 0)


def hyena_half_fwd(fe, fo, sg, zb, zcol, spec, order, tm, bg):
    B, _, H, _ = zb.shape
    C = HY_CH
    lr, li, hr, hi, sp = spec
    ftile = pl.BlockSpec((1, 2 * tm, H), lambda g, i: (i, 0, 0))
    plane = lambda p: pl.BlockSpec((bg, None, H, C), lambda g, i: (g, p, 0, zcol))
    stile = pl.BlockSpec((tm, C), lambda g, i: (i, order))
    ptile = pl.BlockSpec((bg, tm, C), lambda g, i: (g, i, 0))
    pshape = jax.ShapeDtypeStruct((B, H, C), BF16)
    return pl.pallas_call(
        _half_fwd_body,
        grid=(B // bg, H // tm),
        in_specs=[ftile, ftile, pl.BlockSpec((SUBLANES, H), lambda g, i: (0, 0)), plane(0), plane(1),
                  stile, stile, stile, stile, pl.BlockSpec((SUBLANES, C), lambda g, i: (0, order))],
        out_specs=[ptile, ptile, ptile, ptile, pl.BlockSpec((bg, SUBLANES, C), lambda g, i: (g, 0, 0))],
        out_shape=[pshape, pshape, pshape, pshape, jax.ShapeDtypeStruct((B, SUBLANES, C), F32)],
        compiler_params=_cparams("parallel", "arbitrary"),
        name="hyena_dft_fwd",
    )(fe, fo, sg, zb, zb, lr, li, hr, hi, sp)


def _half_inv_body(fer_ref, fei_ref, fotr_ref, foti_ref, per_ref, pei_ref, por_ref, poi_ref, ysp_ref,
                   ze_ref, zo_ref, ge_ref, go_ref, d_ref, *out_refs, natural):
    nb = ze_ref.shape[0]
    tm = fer_ref.shape[0]
    u = pl.program_id(1) * tm + lax.broadcasted_iota(jnp.int32, (tm, 1), 0)
    sign = jnp.where((u & 1) == 0, 1.0, -1.0).astype(F32)
    d = d_ref[...]

    def body(b, carry):
        ce = (jnp.dot(fer_ref[...], per_ref[b], preferred_element_type=F32)
              + jnp.dot(fei_ref[...], pei_ref[b], preferred_element_type=F32) + sign * ysp_ref[b][0:1, :])
        co = (jnp.dot(fotr_ref[...], por_ref[b], preferred_element_type=F32)
              + jnp.dot(foti_ref[...], poi_ref[b], preferred_element_type=F32) - sign * ysp_ref[b][1:2, :])
        zne = ge_ref[b] * (ce + d * ze_ref[b])
        zno = go_ref[b] * (co + d * zo_ref[b])
        if natural:
            bi = jnp.asarray(b, jnp.int32)
            for c, o_ref in enumerate(out_refs):
                o_ref.at[bi][pl.ds(0, tm, stride=2), :] = zne[:, c * LANES:(c + 1) * LANES]
                o_ref.at[bi][pl.ds(1, tm, stride=2), :] = zno[:, c * LANES:(c + 1) * LANES]
        else:
            z_ref, zb_ref = out_refs
            z_ref[b, 0] = zne
            z_ref[b, 1] = zno
            zb_ref[b, 0] = zne.astype(BF16)
            zb_ref[b, 1] = zno.astype(BF16)
        return carry
    lax.fori_loop(0, nb, body, 0)


def hyena_half_inv(mats, ps, ysp, z, zcol, gate, gcol, d, tm, bg, natural=False):
    fer, fei, fotr, foti = mats
    per = ps[0]
    B, H, C = per.shape
    ftile = pl.BlockSpec((tm, H), lambda g, i: (i, 0))
    whole_p = lambda: pl.BlockSpec((bg, H, C), lambda g, i: (g, 0, 0), pipeline_mode=pl.Buffered(1))
    tile = lambda p, col: pl.BlockSpec((bg, None, tm, C), lambda g, i: (g, p, i, col))
    if natural:
        out_specs = [pl.BlockSpec((bg, 2 * tm, LANES), lambda g, i: (g, i, 0))] * (C // LANES)
        out_shape = [jax.ShapeDtypeStruct((B, 2 * H, LANES), F32)] * (C // LANES)
    else:
        out_specs = [pl.BlockSpec((bg, 2, tm, C), lambda g, i: (g, 0, i, 0))] * 2
        out_shape = [jax.ShapeDtypeStruct((B, 2, H, C), F32), jax.ShapeDtypeStruct((B, 2, H, C), BF16)]
    return pl.pallas_call(
        functools.partial(_half_inv_body, natural=natural),
        grid=(B // bg, H // tm),
        in_specs=[ftile, ftile, ftile, ftile, whole_p(), whole_p(), whole_p(), whole_p(),
                  pl.BlockSpec((bg, SUBLANES, C), lambda g, i: (g, 0, 0)),
                  tile(0, zcol), tile(1, zcol), tile(0, gcol), tile(1, gcol),
                  pl.BlockSpec((1, C), lambda g, i: (0, 0))],
        out_specs=out_specs,
        out_shape=out_shape,
        compiler_params=_cparams("parallel", "arbitrary"),
        name="hyena_dft_inv",
    )(fer, fei, fotr, foti, *ps, ysp, z, z, gate, gate, d)


def dft_half_matrices(L, tm):
    H = L // 2
    r = 1 << (max(H.bit_length() - 1, 0) // 2)
    u = jnp.arange(H, dtype=jnp.int32)

    hi = jnp.arange(H // r, dtype=jnp.int32) * r
    lo = jnp.arange(r, dtype=jnp.int32)

    def cos_msin(row_hi, row_lo, col):
        def table(k):
            ang = ((k[:, None] * col[None, :]) % (2 * L)).astype(F32) * (math.pi / L)
            return jnp.cos(ang), jnp.sin(ang)
        ca, sa = table(row_hi)
        cb, sb = table(row_lo)
        c = ca[:, None, :] * cb[None, :, :] - sa[:, None, :] * sb[None, :, :]
        s = sa[:, None, :] * cb[None, :, :] + ca[:, None, :] * sb[None, :, :]
        return c.reshape(H, H).astype(BF16), (-s).reshape(H, H).astype(BF16)
    fre, fie = cos_msin(hi, lo, 2 * u)
    fro, fio = cos_msin(hi, lo, 2 * u + 1)
    frot, fiot = cos_msin(2 * hi, 2 * lo + 1, u)
    stack = lambda a, b: jnp.concatenate([a.reshape(H // tm, tm, H), b.reshape(H // tm, tm, H)], axis=1)
    sg = jnp.zeros((SUBLANES, H), F32).at[0].set(jnp.where(u % 2 == 0, 1.0, -1.0)).astype(BF16)
    return dict(fe=stack(fre, fie), fo=stack(fro, fio), sg=sg, inv=(fre, fie, frot, fiot), tm=tm)


def hyena_half(hy, lp, mats):
    B, L, _ = hy.shape
    tm = mats['tm']
    bg = B if L < TM_DFT else BG_HALF
    hf = hyena_time_filters(L, lp)
    h_fwd = hf[:, 0].reshape(L, HY_ORDER * HY_CH)
    h_bwd = hf[:, 1].at[0].set(0.0).reshape(L, HY_ORDER * HY_CH)
    spec = hyena_half_filter_spectrum(mats['fe'], mats['fo'], mats['sg'], _hi_lo(h_fwd + h_bwd),
                                      _hi_lo(h_fwd - h_bwd), tm)
    w = jnp.pad(lp['hy_conv_w'], ((0, SUBLANES - 3), (0, 0)))
    u, ub = hyena_conv3_planes(hy, w, lp['hy_conv_b'][None, :])
    d = lp['hy_d'].astype(F32)
    z, zb, zcol = u, ub, 0
    for o in range(HY_ORDER):
        *ps, ysp = hyena_half_fwd(mats['fe'], mats['fo'], mats['sg'], zb, zcol, spec, o, tm, bg)
        last = o == HY_ORDER - 1
        out = hyena_half_inv(mats['inv'], ps, ysp, z, zcol, u, 1 + o, d[o][None, :], tm, bg, natural=last)
        if last:
            return out
        z, zb = out
        zcol = 0


def _ffn_body(x_ref, mod_ref, g_ref, w1_ref, w3_ref, w2_ref, o_ref):
    x = x_ref[...]
    h = _adaln_rows(x, g_ref[...], mod_ref[0, ROW_SHIFT2:ROW_SHIFT2 + 1, :],
                    mod_ref[0, ROW_SCALE2:ROW_SCALE2 + 1, :]).astype(BF16)
    a = jnp.dot(h, w1_ref[...], preferred_element_type=F32)
    b = jnp.dot(h, w3_ref[...], preferred_element_type=F32)
    act = (_silu(a) * b).astype(BF16)
    ff = jnp.dot(act, w2_ref[...], preferred_element_type=F32)
    o_ref[...] = x + mod_ref[0, ROW_GATE2:ROW_GATE2 + 1, :] * ff


def ffn_dense(x, mods, g, w1, w3, w2, seg_len, tm=TM_PROJ):
    T, D = x.shape
    F = w1.shape[1]
    resident = functools.partial(pl.BlockSpec, pipeline_mode=pl.Buffered(1))
    return pl.pallas_call(
        _ffn_body,
        grid=(T // tm,),
        in_specs=[
            pl.BlockSpec((tm, D), lambda i: (i, 0)),
            pl.BlockSpec((1, MOD_ROWS, D), lambda i: (i * tm // seg_len, 0, 0)),
            pl.BlockSpec((1, D), lambda i: (0, 0)),
            resident((D, F), lambda i: (0, 0)),
            resident((D, F), lambda i: (0, 0)),
            resident((F, D), lambda i: (0, 0)),
        ],
        out_specs=pl.BlockSpec((tm, D), lambda i: (i, 0)),
        out_shape=jax.ShapeDtypeStruct((T, D), F32),
        compiler_params=_cparams("parallel"),
        name="ffn_dense",
    )(x, mods, g, w1, w3, w2)


def _store_token_tiles(ref, x):
    n = x.shape[0]
    for s in range(SUBLANES):
        ref[pl.ds(s, n, stride=SUBLANES), :] = x[:, s * LANES:(s + 1) * LANES]


def _load_token_tiles(ref, n):
    return jnp.concatenate([ref[pl.ds(s, n, stride=SUBLANES), :] for s in range(SUBLANES)], axis=1)


def _router_body(xa_ref, xb_ref, mod_ref, g_ref, wr_ref, h_ref, idx_ref, gw_ref, *, na):
    x = jnp.where(pl.program_id(0) < na, xa_ref[...], xb_ref[...])
    h = _adaln_rows(x, g_ref[...], mod_ref[0, ROW_SHIFT2:ROW_SHIFT2 + 1, :],
                    mod_ref[0, ROW_SCALE2:ROW_SCALE2 + 1, :])
    _store_token_tiles(h_ref, h)
    logits = lax.dot_general(wr_ref[...], h, (((1,), (1,)), ((), ())),
                             precision=lax.Precision.HIGHEST, preferred_element_type=F32)
    eidx = lax.broadcasted_iota(jnp.int32, logits.shape, 0)
    m1 = jnp.max(logits, axis=0, keepdims=True)
    i1 = jnp.min(jnp.where(logits == m1, eidx, N_EXPERTS), axis=0, keepdims=True)
    rest = jnp.where(eidx == i1, -jnp.inf, logits)
    m2 = jnp.max(rest, axis=0, keepdims=True)
    i2 = jnp.min(jnp.where(rest == m2, eidx, N_EXPERTS), axis=0, keepdims=True)
    e2 = jnp.exp(m2 - m1)
    den = 1.0 + e2
    row = lax.broadcasted_iota(jnp.int32, logits.shape, 0)
    idx_ref[...] = jnp.where(row == 0, i1, jnp.where(row == 1, i2, 0))
    gw_ref[...] = jnp.transpose(jnp.where(row == 0, 1.0 / den, jnp.where(row == 1, e2 / den, 0.0)))


def _merged_seg(i, na, tm, seg_len_b):
    return jnp.where(i < na, 0, 1 + jnp.maximum(i - na, 0) * tm // seg_len_b)


def moe_router(xa, xb, mods, g, wr_t, seg_len_b, tm=TM_PROJ):
    D = xa.shape[1]
    na = xa.shape[0] // tm
    T = xa.shape[0] + xb.shape[0]
    return pl.pallas_call(
        functools.partial(_router_body, na=na),
        grid=(T // tm,),
        in_specs=[
            pl.BlockSpec((tm, D), lambda i: (jnp.minimum(i, na - 1), 0)),
            pl.BlockSpec((tm, D), lambda i: (jnp.maximum(i - na, 0), 0)),
            pl.BlockSpec((1, MOD_ROWS, D), lambda i: (_merged_seg(i, na, tm, seg_len_b), 0, 0)),
            pl.BlockSpec((1, D), lambda i: (0, 0)),
            pl.BlockSpec((N_EXPERTS, D), lambda i: (0, 0)),
        ],
        out_specs=[
            pl.BlockSpec((tm * SUBLANES, LANES), lambda i: (i, 0)),
            pl.BlockSpec((N_EXPERTS, tm), lambda i: (0, i)),
            pl.BlockSpec((tm, N_EXPERTS), lambda i: (i, 0)),
        ],
        out_shape=[
            jax.ShapeDtypeStruct((T * SUBLANES, LANES), F32),
            jax.ShapeDtypeStruct((N_EXPERTS, T), jnp.int32),
            jax.ShapeDtypeStruct((T, N_EXPERTS), F32),
        ],
        compiler_params=_cparams("parallel"),
        name="moe_router",
    )(xa, xb, mods, g, wr_t)


def _experts_body(te_ref, nv_ref, src0_ref, srcn_ref, dst_ref, h_hbm, w1_ref, w3_ref, w2_ref,
                  y_hbm, hrows, hb, acc, ybuf, sem_g, sem_s, *, tm, nf, n_slots):
    i = pl.program_id(0)
    f = pl.program_id(1)
    nv = nv_ref[0]
    valid = i < nv
    slot = lax.rem(i, 2)

    def gather_issue(idx_ref, s):
        def body(r, carry):
            src = pl.multiple_of(idx_ref[0, 0, r] * SUBLANES, SUBLANES)
            dst = pl.multiple_of(r * SUBLANES, SUBLANES)
            pltpu.make_async_copy(h_hbm.at[pl.ds(src, SUBLANES)], hrows.at[s, pl.ds(dst, SUBLANES)],
                                  sem_g.at[s]).start()
            return carry
        lax.fori_loop(0, tm, body, 0, unroll=8)

    def gather_wait(s):
        pltpu.make_async_copy(h_hbm.at[pl.ds(0, tm * SUBLANES)], hrows.at[s], sem_g.at[s]).wait()

    def scatter_wait(s):
        pltpu.make_async_copy(ybuf.at[s], y_hbm.at[pl.ds(0, tm * SUBLANES)], sem_s.at[s]).wait()

    @pl.when(jnp.logical_and(i == 0, f == 0))
    def _first():
        ybuf[0] = jnp.zeros(ybuf.shape[1:], ybuf.dtype)
        spare = lambda p: y_hbm.at[pl.ds((n_slots + p * tm) * SUBLANES, tm * SUBLANES)]
        for p in range(2):
            pltpu.make_async_copy(ybuf.at[0], spare(p), sem_s.at[p]).start()
        for p in range(2):
            pltpu.make_async_copy(ybuf.at[0], spare(p), sem_s.at[p]).wait()
        gather_issue(src0_ref, 0)

    @pl.when(jnp.logical_and(valid, f == 0))
    def _stage():
        gather_wait(slot)
        hb[...] = _load_token_tiles(hrows.at[slot], tm).astype(BF16)

        @pl.when(i + 1 < nv)
        def _prefetch():
            gather_issue(srcn_ref, 1 - slot)

    def partial_ffn():
        h = hb[...]
        a = jnp.dot(h, w1_ref[0], preferred_element_type=F32)
        b = jnp.dot(h, w3_ref[0], preferred_element_type=F32)
        act = (_silu(a) * b).astype(BF16)
        return jnp.dot(act, w2_ref[0], preferred_element_type=F32)

    @pl.when(jnp.logical_and(valid, f == 0))
    def _first_slice():
        acc[...] = partial_ffn()

    @pl.when(jnp.logical_and(valid, jnp.logical_and(f > 0, f < nf - 1)))
    def _middle_slice():
        acc[...] += partial_ffn()

    @pl.when(jnp.logical_and(valid, f == nf - 1))
    def _emit():
        y = acc[...] + partial_ffn()

        @pl.when(i >= 2)
        def _reuse():
            scatter_wait(slot)
        _store_token_tiles(ybuf.at[slot], y)

        def body(r, carry):
            src = pl.multiple_of(r * SUBLANES, SUBLANES)
            dst = pl.multiple_of(dst_ref[0, 0, r] * SUBLANES, SUBLANES)
            pltpu.make_async_copy(ybuf.at[slot, pl.ds(src, SUBLANES)], y_hbm.at[pl.ds(dst, SUBLANES)],
                                  sem_s.at[slot]).start()
            return carry
        lax.fori_loop(0, tm, body, 0, unroll=8)

    @pl.when(jnp.logical_and(i == nv, f == 0))
    def _drain():
        scatter_wait(lax.rem(nv + 1, 2))

        @pl.when(nv >= 2)
        def _older():
            scatter_wait(lax.rem(nv, 2))


def moe_experts(h, tile_expert, n_valid, src_rows, dst_rows, w1, w3, w2, tm=TM_MOE, tf=TF_MOE):
    T = h.shape[0] // SUBLANES
    D = w1.shape[1]
    F = w1.shape[2]
    n_tiles = src_rows.shape[0]
    n_slots = TOP_K * T
    nf = F // tf

    def wcol(i, f, te, nv):
        return (te[i], 0, jnp.where(i < nv[0], f, nf - 1))

    def wrow(i, f, te, nv):
        return (te[i], jnp.where(i < nv[0], f, nf - 1), 0)

    smem_tile = lambda fn: pl.BlockSpec((1, 1, tm), fn, memory_space=pltpu.SMEM)
    grid_spec = pltpu.PrefetchScalarGridSpec(
        num_scalar_prefetch=2,
        grid=(n_tiles, nf),
        in_specs=[
            smem_tile(lambda i, f, te, nv: (0, 0, 0)),
            smem_tile(lambda i, f, te, nv: (jnp.minimum(i + 1, n_tiles - 1), 0, 0)),
            smem_tile(lambda i, f, te, nv: (i, 0, 0)),
            pl.BlockSpec(memory_space=pl.ANY),
            pl.BlockSpec((1, D, tf), wcol),
            pl.BlockSpec((1, D, tf), wcol),
            pl.BlockSpec((1, tf, D), wrow),
        ],
        out_specs=pl.BlockSpec(memory_space=pl.ANY),
        scratch_shapes=[
            pltpu.VMEM((2, tm * SUBLANES, LANES), F32),
            pltpu.VMEM((tm, D), BF16),
            pltpu.VMEM((tm, D), F32),
            pltpu.VMEM((2, tm * SUBLANES, LANES), F32),
            pltpu.SemaphoreType.DMA((2,)),
            pltpu.SemaphoreType.DMA((2,)),
        ],
    )
    return pl.pallas_call(
        functools.partial(_experts_body, tm=tm, nf=nf, n_slots=n_slots),
        grid_spec=grid_spec,
        out_shape=jax.ShapeDtypeStruct(((n_slots + 2 * tm) * SUBLANES, LANES), F32),
        compiler_params=_cparams("arbitrary", "arbitrary"),
        name="moe_experts",
    )(tile_expert, n_valid, src_rows, src_rows, dst_rows, h, w1, w3, w2)


def _combine_body(x_ref, mod_ref, gw_ref, y0_ref, y1_ref, o_ref):
    n = x_ref.shape[0]
    gw = gw_ref[...]
    y = gw[:, 0:1] * _load_token_tiles(y0_ref, n) + gw[:, 1:2] * _load_token_tiles(y1_ref, n)
    o_ref[...] = x_ref[...] + mod_ref[0, ROW_GATE2:ROW_GATE2 + 1, :] * y


def moe_combine(x, mods, gw, y, row0, n_tok, seg_len, tm=TM_COMB):
    T, D = x.shape
    off = row0 // tm
    return pl.pallas_call(
        _combine_body,
        grid=(T // tm,),
        in_specs=[
            pl.BlockSpec((tm, D), lambda i: (i, 0)),
            pl.BlockSpec((1, MOD_ROWS, D), lambda i: (i * tm // seg_len, 0, 0)),
            pl.BlockSpec((tm, N_EXPERTS), lambda i: (off + i, 0)),
            pl.BlockSpec((tm * SUBLANES, LANES), lambda i: (off + i, 0)),
            pl.BlockSpec((tm * SUBLANES, LANES), lambda i: (off + n_tok // tm + i, 0)),
        ],
        out_specs=pl.BlockSpec((tm, D), lambda i: (i, 0)),
        out_shape=jax.ShapeDtypeStruct((T, D), F32),
        compiler_params=_cparams("parallel"),
        name="moe_combine",
    )(x, mods, gw, y, y)


def moe_dispatch_plan(idx, tm=TM_MOE):
    T = idx.shape[1]
    n_slots = TOP_K * T
    n_tiles = n_slots // tm + N_EXPERTS
    n_rows = n_tiles * tm
    experts = jnp.arange(N_EXPERTS, dtype=jnp.int32)[None, :]
    e_flat = idx[:TOP_K].reshape(n_slots)
    counts = jnp.sum((e_flat[:, None] == experts).astype(jnp.int32), axis=0)
    padded = (counts + tm - 1) // tm * tm
    ends = jnp.cumsum(padded)
    offs = ends - padded
    slots = jnp.arange(n_slots, dtype=jnp.int32)
    order = jnp.sort(e_flat * n_slots + slots) % n_slots
    cstart = jnp.cumsum(counts) - counts
    rows = jnp.arange(n_rows, dtype=jnp.int32)
    row_e = jnp.minimum(jnp.sum((rows[:, None] >= ends[None, :]).astype(jnp.int32), axis=1), N_EXPERTS - 1)
    pick = lambda table: jnp.sum(jnp.where(row_e[:, None] == experts, table[None, :], 0), axis=1)
    j = rows - pick(offs)
    live = j < pick(counts)
    slot = order[jnp.clip(pick(cstart) + j, 0, n_slots - 1)]
    spare = n_slots + (rows // tm) % 2 * tm + rows % tm
    src_rows = jnp.where(live, slot % T, 0).astype(jnp.int32).reshape(n_tiles, 1, tm)
    dst_rows = jnp.where(live, slot, spare).astype(jnp.int32).reshape(n_tiles, 1, tm)
    tile_start = jnp.arange(n_tiles, dtype=jnp.int32) * tm
    n_valid = (ends[-1] // tm).astype(jnp.int32).reshape(1)
    tile_expert = jnp.minimum(jnp.sum((tile_start[:, None] >= ends[None, :]).astype(jnp.int32), axis=1),
                              N_EXPERTS - 1)
    last_e = tile_expert[jnp.maximum(n_valid[0] - 1, 0)]
    tile_expert = jnp.where(tile_start < ends[-1], tile_expert, last_e).astype(jnp.int32)
    return tile_expert, n_valid, src_rows, dst_rows


def ffn_moe(xa, xb, mods_a, mods_b, g, wr_t, w1, w3, w2, seg_len_b, tm=TM_PROJ, tme=TM_MOE, tmc=TM_COMB,
            tf=TF_MOE):
    na, nb = xa.shape[0], xb.shape[0]
    mods = jnp.concatenate([mods_a, mods_b], axis=0)
    h, idx, gw = moe_router(xa, xb, mods, g, wr_t, seg_len_b, tm)
    tile_expert, n_valid, src_rows, dst_rows = moe_dispatch_plan(idx, tme)
    y = moe_experts(h, tile_expert, n_valid, src_rows, dst_rows, w1, w3, w2, tme, tf)
    return (moe_combine(xa, mods_a, gw, y, 0, na + nb, na, tmc),
            moe_combine(xb, mods_b, gw, y, na, na + nb, seg_len_b, tmc))


def axial_rope(L):
    rows = L // GRID_W
    r = jnp.repeat(jnp.arange(rows, dtype=F32), GRID_W)
    col = jnp.tile(jnp.arange(GRID_W, dtype=F32), rows)
    n = HEAD_DIM // 4
    freqs = ROPE_BASE ** (-jnp.arange(n, dtype=F32) / n)
    ang = jnp.concatenate([r[:, None] * freqs, col[:, None] * freqs], axis=-1)
    return jnp.cos(ang), jnp.sin(ang)


def rope_tables(L):
    cos, sin = axial_rope(L)
    n = HEAD_DIM // 4
    cos_h = jnp.concatenate([cos[:, :n], cos[:, :n], cos[:, n:], cos[:, n:]], axis=1)
    sin_h = jnp.concatenate([-sin[:, :n], sin[:, :n], -sin[:, n:], sin[:, n:]], axis=1)
    reps = LANES // HEAD_DIM
    return jnp.tile(cos_h, (1, reps)), jnp.tile(sin_h, (1, reps))


def block_diag_ones(width, block):
    i = jnp.arange(width) // block
    return (i[:, None] == i[None, :]).astype(BF16)


def gla_gate_params(gw, gb):
    w = jnp.zeros((LANES, 2 * C_WIDTH), F32)
    w = w.at[:GLA_RANK, :C_WIDTH].set(gw[0]).at[GLA_RANK:2 * GLA_RANK, C_WIDTH:].set(gw[1])
    return w.astype(BF16), jnp.concatenate([gb[0], gb[1]])[None, :]


def gla_state_to_blockdiag(st):
    B = st.shape[0]
    out = jnp.zeros((B, 2, C_HEADS, C_DV, C_HEADS, C_DK), F32)
    for h in range(C_HEADS):
        out = out.at[:, :, h, :, h, :].set(jnp.swapaxes(st[:, :, h], -1, -2))
    return out.reshape(B, 2, C_WIDTH, C_WIDTH)


def gla_state_from_blockdiag(sT):
    B = sT.shape[0]
    s6 = sT.reshape(B, 2, C_HEADS, C_DV, C_HEADS, C_DK)
    return jnp.stack([jnp.swapaxes(s6[:, :, h, :, h, :], -1, -2) for h in range(C_HEADS)], axis=2)


def mod_table(cvec, w_ada, b_ada):
    m = jax.nn.silu(cvec) @ w_ada + b_ada
    m = m.reshape(cvec.shape[0], 6, D_MODEL)
    return jnp.pad(m, ((0, 0), (0, MOD_ROWS - 6), (0, 0)))


def kernel(x_prompt, x_sample, cache_k, cache_v, state_gla, c, c_ctx, norm1_g, norm2_g, w_ada, b_ada,
           w_in, w_out, q_norm_g, k_norm_g, attn_sink, hy_conv_w, hy_conv_b, hy_w1, hy_b1, hy_freq1,
           hy_w2, hy_b2, hy_freq2, hy_w3, hy_decay, hy_d, gla_gate_w, gla_gate_b, gla_norm_g,
           ffn_w1, ffn_w3, ffn_w2, moe_router, moe_w1, moe_w3, moe_w2):
    D = D_MODEL
    xp = x_prompt.reshape(BATCH * SEQ, D)
    xs = x_sample.reshape(DEC_BATCH * DEC_SEQ, D)
    streams = [
        dict(x=xp, B=BATCH, L=SEQ, seg_len=BATCH * SEQ, cvec=c_ctx[None, :], latent=False),
        dict(x=xs, B=DEC_BATCH, L=DEC_SEQ, seg_len=DEC_SEQ, cvec=c, latent=True),
    ]
    ks_list, vs_list, st_list = [], [], []
    bd_q = block_diag_ones(A_WIDTH, HEAD_DIM)
    dft = {s['L']: dft_half_matrices(s['L'], min(TM_DFT, s['L'] // 2)) for s in streams}
    for l in range(DEPTH):
        lp = {
            'hy_conv_w': hy_conv_w[l], 'hy_conv_b': hy_conv_b[l],
            'hy_w1': hy_w1[l], 'hy_b1': hy_b1[l], 'hy_freq1': hy_freq1[l], 'hy_w2': hy_w2[l],
            'hy_b2': hy_b2[l], 'hy_freq2': hy_freq2[l], 'hy_w3': hy_w3[l], 'hy_decay': hy_decay[l],
            'hy_d': hy_d[l],
        }
        w_in_l = jnp.pad(w_in[l], ((0, 0), (0, D_PROJ_PAD - D_PROJ))).astype(BF16)
        w_out_l = w_out[l].astype(BF16)
        g1 = norm1_g[l][None, :]
        g2 = norm2_g[l][None, :]
        qg = jnp.tile(q_norm_g[l], A_HEADS)[None, :]
        kg = jnp.tile(k_norm_g[l], A_KV_HEADS)[None, :]
        gg = jnp.tile(gla_norm_g[l], C_HEADS)[None, :]
        gate_w, gate_b = gla_gate_params(gla_gate_w[l], gla_gate_b[l])
        j = l // 2
        for s in streams:
            B, L = s['B'], s['L']
            mods = mod_table(s['cvec'], w_ada[l], b_ada[l])
            rope = rope_tables(L) if s['latent'] else None
            q, k, v, hy, cq, ck, cv, cg, la = in_proj(s['x'], mods, g1, w_in_l, bd_q, qg, kg, gate_w, gate_b,
                                                      s['seg_len'], rope, L)
            seq = lambda t: t.reshape(B, L, t.shape[-1])
            if s['latent']:
                kc = cache_k[:, l].reshape(DEC_BATCH, PAST_LEN, LANES)
                vc = cache_v[:, l].reshape(DEC_BATCH, PAST_LEN, LANES)
                a_out = attention(seq(q), seq(k), seq(v), attn_sink[l], kc, vc)
                s0 = gla_state_to_blockdiag(state_gla[:, l])
                o_f, o_b, _ = gla(seq(cq), seq(ck), seq(cv), seq(la), s0)
            else:
                a_out = attention(seq(q), seq(k), seq(v), attn_sink[l])
                o_f, o_b, sT = gla(seq(cq), seq(ck), seq(cv), seq(la))
                ks_list.append(k.reshape(B, L, A_KV_HEADS, HEAD_DIM))
                vs_list.append(v.reshape(B, L, A_KV_HEADS, HEAD_DIM))
                st_list.append(gla_state_from_blockdiag(sT))
            h_out = hyena_half(seq(hy), lp, dft[L])
            flat = lambda t: t.reshape(B * L, t.shape[-1])
            x1 = out_proj(s['x'], flat(a_out), [flat(h) for h in h_out], flat(o_f), flat(o_b), cg, mods, w_out_l,
                          bd_q[:C_WIDTH, :C_WIDTH], gg, s['seg_len'])
            if l % 2 == 0:
                s['x'] = ffn_dense(x1, mods, g2, ffn_w1[j].astype(BF16), ffn_w3[j].astype(BF16),
                                   ffn_w2[j].astype(BF16), s['seg_len'])
            else:
                s['x'], s['mods'] = x1, mods
        if l % 2 == 1:
            sa, sb = streams
            sa['x'], sb['x'] = ffn_moe(sa['x'], sb['x'], sa['mods'], sb['mods'], g2, moe_router[j].T,
                                       moe_w1[j].astype(BF16), moe_w3[j].astype(BF16), moe_w2[j].astype(BF16),
                                       sb['seg_len'])
    y_prompt = streams[0]['x'].reshape(BATCH, SEQ, D)
    y_sample = streams[1]['x'].reshape(DEC_BATCH, DEC_SEQ, D)
    new_cache_k = jnp.stack(ks_list, axis=1)
    new_cache_v = jnp.stack(vs_list, axis=1)
    new_state_gla = jnp.stack(st_list, axis=1)
    return (y_prompt, y_sample, new_cache_k, new_cache_v, new_state_gla)
```

```python
import math
import functools
import jax
import jax.numpy as jnp
from jax import lax
from jax.experimental import pallas as pl
from jax.experimental.pallas import tpu as pltpu

D_MODEL = 1024
BATCH = 32
SEQ = 256
DEPTH = 2
DEC_BATCH = 8
DEC_SEQ = 4096
PAST_LEN = 512

GRID_W = 64
HEAD_DIM = 64
A_HEADS = 8
A_KV_HEADS = 2
A_GROUP = A_HEADS // A_KV_HEADS
A_WIDTH = A_HEADS * HEAD_DIM
WINDOW = 128
ROPE_BASE = 10000.0
HY_CH = 256
HY_ORDER = 2
HY_BANDS = 16
HY_EMB = 1 + 2 * HY_BANDS
HY_HID = 64
C_HEADS = 4
C_DK = 64
C_DV = 64
C_WIDTH = C_HEADS * C_DV
GLA_RANK = 16
GLA_TAU = 16.0
GLA_CHUNK = 64
D_FF = 2816
N_EXPERTS = 8
TOP_K = 2
D_FF_EXPERT = 3584

PROJ_SIZES = (A_WIDTH, A_KV_HEADS * HEAD_DIM, A_KV_HEADS * HEAD_DIM, 3 * HY_CH,
              C_HEADS * C_DK, C_HEADS * C_DK, C_WIDTH, C_WIDTH, 2 * GLA_RANK)
D_PROJ = int(sum(PROJ_SIZES))

F32 = jnp.float32
BF16 = jnp.bfloat16
ATT_SCALE = HEAD_DIM ** -0.5
NEG_INF = -1e30
EPS = 1e-6

LANES = 128
SUBLANES = 8
D_PROJ_PAD = -(-D_PROJ // LANES) * LANES
VMEM_LIMIT_BYTES = 56 * 1024 * 1024

MOD_ROWS = SUBLANES
ROW_SHIFT1, ROW_SCALE1, ROW_GATE1, ROW_SHIFT2, ROW_SCALE2, ROW_GATE2 = range(6)

TM_PROJ = 512
TM_MOE = 512
TF_MOE = 1792
TM_COMB = 512


def _cparams(*sem):
    return pltpu.CompilerParams(dimension_semantics=sem, vmem_limit_bytes=VMEM_LIMIT_BYTES)


def _adaln_rows(x, g, shift, scale):
    ms = jnp.mean(x * x, axis=-1, keepdims=True)
    return (x * lax.rsqrt(ms + EPS) * g) * (1.0 + scale) + shift


def _silu(a):
    return a * jax.nn.sigmoid(a)


def _group_mean_sq(x, ones_bd):
    sq = x * x
    hi = sq.astype(BF16)
    lo = (sq - hi.astype(F32)).astype(BF16)
    s = jnp.dot(hi, ones_bd, preferred_element_type=F32) + jnp.dot(lo, ones_bd, preferred_element_type=F32)
    return s * (1.0 / HEAD_DIM)


def _rope_rows(x, cos_t, sin_t):
    q4 = HEAD_DIM // 4
    lane = lax.broadcasted_iota(jnp.int32, x.shape, 1)
    partner = jnp.where((lane % (2 * q4)) < q4, pltpu.roll(x, LANES - q4, 1), pltpu.roll(x, q4, 1))
    return x * cos_t + partner * sin_t


def _log_sigmoid(x):
    return jnp.minimum(x, 0.0) - jnp.log(1.0 + jnp.exp(-jnp.abs(x)))


def _inproj_body(*refs, latent):
    if latent:
        (x_ref, mod_ref, g_ref, w_ref, bd_ref, qg_ref, kg_ref, gw_ref, gb_ref, cos_ref, sin_ref,
         q_ref, k_ref, v_ref, hy_ref, cq_ref, ck_ref, cv_ref, cg_ref, la_ref) = refs
    else:
        (x_ref, mod_ref, g_ref, w_ref, bd_ref, qg_ref, kg_ref, gw_ref, gb_ref,
         q_ref, k_ref, v_ref, hy_ref, cq_ref, ck_ref, cv_ref, cg_ref, la_ref) = refs
    tm = x_ref.shape[0]
    for half in range(2):
        rs = pl.ds(half * (tm // 2), tm // 2)
        h = _adaln_rows(x_ref[rs, :], g_ref[...], mod_ref[0, ROW_SHIFT1:ROW_SHIFT1 + 1, :],
                        mod_ref[0, ROW_SCALE1:ROW_SCALE1 + 1, :])
        acc = jnp.dot(h.astype(BF16), w_ref[...], preferred_element_type=F32)
        o = 0
        q = acc[:, o:o + A_WIDTH]
        o += A_WIDTH
        k = acc[:, o:o + LANES]
        o += LANES
        v_ref[rs, :] = acc[:, o:o + LANES]
        o += LANES
        hy_ref[rs, :] = acc[:, o:o + 3 * HY_CH]
        o += 3 * HY_CH
        cq_ref[rs, :] = acc[:, o:o + C_WIDTH] * (C_DK ** -0.5)
        o += C_WIDTH
        ck_ref[rs, :] = acc[:, o:o + C_WIDTH]
        o += C_WIDTH
        cv_ref[rs, :] = acc[:, o:o + C_WIDTH].astype(cv_ref.dtype)
        o += C_WIDTH
        cg_ref[rs, :] = acc[:, o:o + C_WIDTH]
        o += C_WIDTH
        r = acc[:, o:o + LANES]
        la_ref[rs, :] = _log_sigmoid(jnp.dot(r.astype(BF16), gw_ref[...], preferred_element_type=F32)
                                     + gb_ref[...]) * (1.0 / GLA_TAU)
        q = q * lax.rsqrt(_group_mean_sq(q, bd_ref[...]) + EPS) * qg_ref[...]
        k = k * lax.rsqrt(_group_mean_sq(k, bd_ref[0:LANES, 0:LANES]) + EPS) * kg_ref[...]
        if latent:
            cos_t = cos_ref[rs, :]
            sin_t = sin_ref[rs, :]
            q = jnp.concatenate([_rope_rows(q[:, j * LANES:(j + 1) * LANES], cos_t, sin_t)
                                 for j in range(A_WIDTH // LANES)], axis=1)
            k = _rope_rows(k, cos_t, sin_t)
        q_ref[rs, :] = (q * ATT_SCALE).astype(BF16)
        k_ref[rs, :] = k


def in_proj(x, mods, g, w, bd, qg, kg, gw, gb, seg_len, rope=None, seq_len=None, tm=TM_PROJ):
    T, D = x.shape
    N = w.shape[1]
    latent = rope is not None
    const = lambda shape: pl.BlockSpec(shape, lambda i: (0,) * len(shape))
    in_specs = [
        pl.BlockSpec((tm, D), lambda i: (i, 0)),
        pl.BlockSpec((1, MOD_ROWS, D), lambda i: (i * tm // seg_len, 0, 0)),
        const((1, D)), const((D, N)), const((A_WIDTH, A_WIDTH)), const((1, A_WIDTH)), const((1, LANES)),
        const((LANES, 2 * C_WIDTH)), const((1, 2 * C_WIDTH)),
    ]
    args = [x, mods, g, w, bd, qg, kg, gw, gb]
    if latent:
        tiles_per_seq = seq_len // tm
        in_specs += [pl.BlockSpec((tm, LANES), lambda i: (i % tiles_per_seq, 0))] * 2
        args += list(rope)
    widths = [A_WIDTH, LANES, LANES, 3 * HY_CH, C_WIDTH, C_WIDTH, C_WIDTH, C_WIDTH, 2 * C_WIDTH]
    dtypes = [BF16, F32, F32, F32, F32, F32, BF16, F32, F32]
    return pl.pallas_call(
        functools.partial(_inproj_body, latent=latent),
        grid=(T // tm,),
        in_specs=in_specs,
        out_specs=[pl.BlockSpec((tm, wd), lambda i: (i, 0)) for wd in widths],
        out_shape=[jax.ShapeDtypeStruct((T, wd), dt) for wd, dt in zip(widths, dtypes)],
        compiler_params=_cparams("parallel"),
        name="in_proj_latent" if latent else "in_proj_context",
    )(*args)


def _outproj_body(x_ref, a_ref, h0_ref, h1_ref, of_ref, ob_ref, cg_ref, mod_ref, w_ref, bd_ref, gg_ref, o_ref):
    o = of_ref[...] + ob_ref[...]
    g_out = o * lax.rsqrt(_group_mean_sq(o, bd_ref[...]) + EPS) * gg_ref[...] * _silu(cg_ref[...])
    h0 = A_WIDTH
    g0 = A_WIDTH + HY_CH
    mix = jnp.dot(a_ref[...], w_ref[0:h0, :], preferred_element_type=F32)
    h = jnp.concatenate([h0_ref[...], h1_ref[...]], axis=1)
    mix += jnp.dot(h.astype(BF16), w_ref[h0:g0, :], preferred_element_type=F32)
    mix += jnp.dot(g_out.astype(BF16), w_ref[g0:, :], preferred_element_type=F32)
    o_ref[...] = x_ref[...] + mod_ref[0, ROW_GATE1:ROW_GATE1 + 1, :] * mix


def out_proj(x, a, h, of, ob, cg, mods, w, bd, gg, seg_len, tm=TM_PROJ):
    T, D = x.shape
    row = lambda wd: pl.BlockSpec((tm, wd), lambda i: (i, 0))
    const = lambda shape: pl.BlockSpec(shape, lambda i: (0,) * len(shape))
    return pl.pallas_call(
        _outproj_body,
        grid=(T // tm,),
        in_specs=[
            row(D), row(A_WIDTH), row(LANES), row(LANES), row(C_WIDTH), row(C_WIDTH), row(C_WIDTH),
            pl.BlockSpec((1, MOD_ROWS, D), lambda i: (i * tm // seg_len, 0, 0)),
            const((D, D)), const((C_WIDTH, C_WIDTH)), const((1, C_WIDTH)),
        ],
        out_specs=row(D),
        out_shape=jax.ShapeDtypeStruct((T, D), F32),
        compiler_params=_cparams("parallel"),
        name="out_proj",
    )(x, a, h[0], h[1], of, ob, cg, mods, w, bd, gg)


TQ_ATT = 256
HEADS_PER_STACK = 2
KWIN_ATT = TQ_ATT + 2 * WINDOW


def _dup_heads(x, g):
    lane = lax.broadcasted_iota(jnp.int32, x.shape, 1)
    rolled = pltpu.roll(x, HEAD_DIM, 1)
    keep = (lane < HEAD_DIM) if g == 0 else (lane >= HEAD_DIM)
    return jnp.where(keep, x, rolled)


def _attn_body(*refs, latent, seq_len):
    if latent:
        sink_ref, q_ref, k_ref, v_ref, kc_ref, vc_ref, o_ref = refs
    else:
        sink_ref, q_ref, k_ref, v_ref, o_ref = refs
    tq = q_ref.shape[1]
    i = pl.program_id(1)
    q = q_ref[0]
    lane = lax.broadcasted_iota(jnp.int32, (tq, LANES), 1)
    low = lane < HEAD_DIM
    if latent:
        start = jnp.clip(i * tq - WINDOW, 0, seq_len - KWIN_ATT)
        start = pl.multiple_of(start, WINDOW)
        kl = k_ref[0, pl.ds(start, KWIN_ATT), :]
        vl = v_ref[0, pl.ds(start, KWIN_ATT), :]
        qpos = i * tq + lax.broadcasted_iota(jnp.int32, (tq, KWIN_ATT), 0)
        kpos = start + lax.broadcasted_iota(jnp.int32, (tq, KWIN_ATT), 1)
        bias = jnp.where(jnp.abs(qpos - kpos) <= WINDOW, 0.0, NEG_INF).astype(F32)
        bias = jnp.concatenate([bias] * A_GROUP, axis=0)
        kc = kc_ref[0]
        vc = vc_ref[0]
    else:
        kl = k_ref[0]
        vl = v_ref[0]
    nt = (((1,), (1,)), ((), ()))
    zero = jnp.zeros_like(q[:, :LANES])
    kls = [_dup_heads(kl, g).astype(BF16) for g in range(A_KV_HEADS)]
    vls = [_dup_heads(vl, g).astype(BF16) for g in range(A_KV_HEADS)]
    if latent:
        kcs = [_dup_heads(kc, g).astype(BF16) for g in range(A_KV_HEADS)]
        vcs = [_dup_heads(vc, g).astype(BF16) for g in range(A_KV_HEADS)]

    def run(stacks):
        gs = [g for g, _ in stacks]
        n = range(len(stacks))
        qs = [jnp.concatenate([jnp.where(low if h % 2 == 0 else jnp.logical_not(low),
                                         q[:, (h // 2) * LANES:(h // 2 + 1) * LANES], zero) for h in heads], axis=0)
              for _, heads in stacks]
        sinks = [jnp.concatenate([jnp.full((tq, 1), sink_ref[h], F32) for h in heads], axis=0)
                 for _, heads in stacks]
        s_loc = [lax.dot_general(qs[i], kls[gs[i]], nt, preferred_element_type=F32) for i in n]
        m = sinks
        if latent:
            s_loc = [s + bias[:s.shape[0]] for s in s_loc]
            s_ctx = [lax.dot_general(qs[i], kcs[gs[i]], nt, preferred_element_type=F32) for i in n]
            m = [jnp.maximum(m[i], jnp.max(s_ctx[i], axis=-1, keepdims=True)) for i in n]
        m = [jnp.maximum(m[i], jnp.max(s_loc[i], axis=-1, keepdims=True)) for i in n]
        p_loc = [jnp.exp(s_loc[i] - m[i]) for i in n]
        den = [jnp.exp(sinks[i] - m[i]) + jnp.sum(p_loc[i], axis=-1, keepdims=True) for i in n]
        acc = [jnp.dot(p_loc[i].astype(BF16), vls[gs[i]], preferred_element_type=F32) for i in n]
        if latent:
            p_ctx = [jnp.exp(s_ctx[i] - m[i]) for i in n]
            den = [den[i] + jnp.sum(p_ctx[i], axis=-1, keepdims=True) for i in n]
            acc = [acc[i] + jnp.dot(p_ctx[i].astype(BF16), vcs[gs[i]], preferred_element_type=F32) for i in n]
        outs = []
        for i in n:
            og = acc[i] / den[i]
            for jj in range(len(stacks[i][1]) // 2):
                outs.append(jnp.where(low, og[(2 * jj) * tq:(2 * jj + 1) * tq],
                                      og[(2 * jj + 1) * tq:(2 * jj + 2) * tq]))
        return outs

    stacks = [(g, range(g * A_GROUP + j, g * A_GROUP + j + HEADS_PER_STACK))
              for g in range(A_KV_HEADS) for j in range(0, A_GROUP, HEADS_PER_STACK)]
    if latent:
        outs = [o for st in stacks for o in run([st])]
    else:
        outs = run(stacks)
    o_ref[0] = jnp.concatenate(outs, axis=1).astype(o_ref.dtype)


def attention(q, k, v, sink, kc=None, vc=None, tq=TQ_ATT):
    B, L, _ = q.shape
    latent = kc is not None
    tq = min(tq, L)
    seq = lambda wd: pl.BlockSpec((1, L, wd), lambda b, i: (b, 0, 0))
    in_specs = [
        pl.BlockSpec(memory_space=pltpu.SMEM),
        pl.BlockSpec((1, tq, A_WIDTH), lambda b, i: (b, i, 0)),
        seq(LANES), seq(LANES),
    ]
    args = [sink, q, k, v]
    if latent:
        P = kc.shape[1]
        in_specs += [pl.BlockSpec((1, P, LANES), lambda b, i: (b, 0, 0))] * 2
        args += [kc, vc]
    return pl.pallas_call(
        functools.partial(_attn_body, latent=latent, seq_len=L),
        grid=(B, L // tq),
        in_specs=in_specs,
        out_specs=pl.BlockSpec((1, tq, A_WIDTH), lambda b, i: (b, i, 0)),
        out_shape=jax.ShapeDtypeStruct((B, L, A_WIDTH), BF16),
        compiler_params=_cparams("parallel", "arbitrary"),
        name="attention_latent" if latent else "attention_context",
    )(*args)


def _split3(x):
    hi = x.astype(BF16)
    r = x - hi.astype(F32)
    mid = r.astype(BF16)
    lo = (r - mid.astype(F32)).astype(BF16)
    return hi, mid, lo


def _gla_group(q_ref, k_ref, v_ref, la_ref, o_ref, st_ref, b, d, reverse, n_chunks):
    C = GLA_CHUNK
    W = C_WIDTH
    ti = lax.broadcasted_iota(jnp.int32, (C, C), 0)
    si = lax.broadcasted_iota(jnp.int32, (C, C), 1)
    tri = (si >= ti) if reverse else (si <= ti)
    tri_b = tri.astype(BF16)
    tri4 = jnp.concatenate([tri] * C_HEADS, axis=0)
    r4 = lax.broadcasted_iota(jnp.int32, (C_HEADS * C, W), 0) // C
    c4 = lax.broadcasted_iota(jnp.int32, (C_HEADS * C, W), 1) // C_DK
    same_head = r4 == c4
    nt = (((1,), (1,)), ((), ()))
    tn = (((0,), (0,)), ((), ()))
    chunks = range(n_chunks)
    rows = [pl.ds(c * C, C) for c in chunks]
    vbs = [v_ref[b, r, :].astype(BF16) for r in rows]
    parts = [_split3(la_ref[b, r, :]) for r in rows]
    bsums = [jnp.dot(tri_b, hi, preferred_element_type=F32) + jnp.dot(tri_b, mid, preferred_element_type=F32)
             + jnp.dot(tri_b, lo, preferred_element_type=F32) for hi, mid, lo in parts]
    b_lasts = [s[0:1] if reverse else s[C - 1:C] for s in bsums]
    qgs = [q_ref[b, r, :] * jnp.exp(s) for r, s in zip(rows, bsums)]
    kgs = [(k_ref[b, r, :] * jnp.exp(-s)).astype(BF16) for r, s in zip(rows, bsums)]
    kds = [(k_ref[b, r, :] * jnp.exp(bl - s)).astype(BF16) for r, s, bl in zip(rows, bsums, b_lasts)]
    decays = [jnp.exp(bl) for bl in b_lasts]
    q_bds = [jnp.where(same_head, jnp.concatenate([qg] * C_HEADS, axis=0), 0.0).astype(BF16) for qg in qgs]
    uts = [jnp.where(same_head, lax.dot_general(vb, kd, tn, preferred_element_type=F32), 0.0)
           for vb, kd in zip(vbs, kds)]
    a_s = [jnp.where(tri4, lax.dot_general(qb, kg, nt, preferred_element_type=F32), 0.0).astype(BF16)
           for qb, kg in zip(q_bds, kgs)]
    r_s = [jnp.where(same_head, jnp.dot(a, vb, preferred_element_type=F32), 0.0)
           for a, vb in zip(a_s, vbs)]
    o_intra = [sum([r[h * C:(h + 1) * C] for h in range(1, C_HEADS)], r[0:C]) for r in r_s]
    qgb = [qg.astype(BF16) for qg in qgs]
    st = st_ref[d]
    for c in (reversed(chunks) if reverse else chunks):
        o_ref[b, rows[c], :] = o_intra[c] + lax.dot_general(qgb[c], st.astype(BF16), nt,
                                                            preferred_element_type=F32)
        st = st * decays[c] + uts[c]
    st_ref[d] = st


def _gla_body(*refs, has_state, n_chunks):
    if has_state:
        (qf, kf, vf, lf, qb, kb, vb, lb, s0_ref, of_ref, ob_ref, sT_ref, st) = refs
    else:
        (qf, kf, vf, lf, qb, kb, vb, lb, of_ref, ob_ref, sT_ref, st) = refs
    j = pl.program_id(1)

    nb = qf.shape[0]

    @pl.when(j == 0)
    def _init():
        if has_state:
            st[...] = s0_ref[...].reshape(st.shape)
        else:
            st[...] = jnp.zeros_like(st)

    for b in range(nb):
        _gla_group(qf, kf, vf, lf, of_ref, st, b, 2 * b, False, n_chunks)
        _gla_group(qb, kb, vb, lb, ob_ref, st, b, 2 * b + 1, True, n_chunks)

    @pl.when(j == pl.num_programs(1) - 1)
    def _final():
        sT_ref[...] = st[...].reshape(sT_ref.shape)


NB_GLA = 1


def gla(cq, ck, cv, la, s0=None, rows=512, nb=NB_GLA):
    B, L, W = cq.shape
    assert B % nb == 0
    rows = min(rows, L)
    ng = L // rows
    has_state = s0 is not None
    fwd = lambda: pl.BlockSpec((nb, rows, W), lambda b, j: (b, j, 0))
    bwd = lambda: pl.BlockSpec((nb, rows, W), lambda b, j: (b, ng - 1 - j, 0))
    state = lambda: pl.BlockSpec((nb, 2, W, W), lambda b, j: (b, 0, 0, 0))
    in_specs = [fwd(), fwd(), fwd(), pl.BlockSpec((nb, rows, W), lambda b, j: (b, j, 0)),
                bwd(), bwd(), bwd(), pl.BlockSpec((nb, rows, W), lambda b, j: (b, ng - 1 - j, 1))]
    args = [cq, ck, cv, la, cq, ck, cv, la]
    if has_state:
        in_specs.append(state())
        args.append(s0)
    return pl.pallas_call(
        functools.partial(_gla_body, has_state=has_state, n_chunks=rows // GLA_CHUNK),
        grid=(B // nb, ng),
        in_specs=in_specs,
        out_specs=[fwd(), bwd(), state()],
        out_shape=[jax.ShapeDtypeStruct((B, L, W), F32), jax.ShapeDtypeStruct((B, L, W), F32),
                   jax.ShapeDtypeStruct((B, 2, W, W), F32)],
        scratch_shapes=[pltpu.VMEM((2 * nb, W, W), F32)],
        compiler_params=_cparams("parallel", "arbitrary"),
        name="gla",
    )(*args)


TM_DFT = 512
TM_DFT_INV = 256
BG_HALF = 4


def _freq_weight(i, tm, n):
    k = i * tm + lax.broadcasted_iota(jnp.int32, (tm, 1), 0)
    return jnp.where(k == 0, 1.0 / n, 2.0 / n).astype(F32)


def hyena_time_filters(L, lp):
    t = jnp.arange(L, dtype=F32)
    t_norm = t / max(L - 1, 1)
    w = (2.0 * math.pi / L) * t
    f = jnp.linspace(1e-4, HY_BANDS - 1, HY_BANDS, dtype=F32)
    fw = w[:, None] * f[None, :]
    feat = jnp.concatenate([t_norm[:, None], jnp.cos(fw), -jnp.sin(fw)], axis=-1)
    z = jnp.sin(lp['hy_freq1'] * (feat @ lp['hy_w1'] + lp['hy_b1']))
    z = jnp.sin(lp['hy_freq2'] * (z @ lp['hy_w2'] + lp['hy_b2']))
    hf = (z @ lp['hy_w3']).astype(F32).reshape(L, 2, HY_ORDER, HY_CH)
    hf = hf * jnp.exp(-t_norm[:, None, None, None] * jnp.abs(lp['hy_decay'].astype(F32)))
    return hf / (jnp.sum(jnp.abs(hf), axis=(0, 1), keepdims=True) + EPS)


def _hi_lo(x):
    hi = x.astype(BF16)
    return jnp.stack([hi, (x - hi.astype(F32)).astype(BF16)])


def _shift_rows(x, down):
    n = x.shape[0]
    row = lax.broadcasted_iota(jnp.int32, (n, 1), 0)
    if down:
        return jnp.where(row == 0, 0.0, pltpu.roll(x, 1, 0))
    return jnp.where(row == n - 1, 0.0, pltpu.roll(x, n - 1, 0))


def _conv3_planes_body(x_ref, w_ref, b_ref, u_ref, ub_ref):
    H = x_ref.shape[0] // 2
    e = x_ref[pl.ds(0, H, stride=2), :]
    o = x_ref[pl.ds(1, H, stride=2), :]
    w0, w1, w2 = w_ref[0:1, :], w_ref[1:2, :], w_ref[2:3, :]
    ue = _shift_rows(o, True) * w0 + e * w1 + o * w2 + b_ref[...]
    uo = e * w0 + o * w1 + _shift_rows(e, False) * w2 + b_ref[...]
    u_ref[0, 0] = ue
    u_ref[0, 1] = uo
    ub_ref[0, 0] = ue.astype(BF16)
    ub_ref[0, 1] = uo.astype(BF16)


def hyena_conv3_planes(hy, w, b):
    B, L, C3 = hy.shape
    H = L // 2
    out = pl.BlockSpec((1, 2, H, LANES), lambda b_, j: (b_, 0, 0, j))
    return pl.pallas_call(
        _conv3_planes_body,
        grid=(B, C3 // LANES),
        in_specs=[pl.BlockSpec((None, L, LANES), lambda b_, j: (b_, 0, j)),
                  pl.BlockSpec((SUBLANES, LANES), lambda b_, j: (0, j)),
                  pl.BlockSpec((1, LANES), lambda b_, j: (0, j))],
        out_specs=[out, out],
        out_shape=[jax.ShapeDtypeStruct((B, 2, H, C3), F32), jax.ShapeDtypeStruct((B, 2, H, C3), BF16)],
        compiler_params=_cparams("parallel", "parallel"),
        name="hyena_conv3",
    )(hy, w, b)


def _half_filt_body(fe_ref, fo_ref, sg_ref, hse_ref, hso_ref, hde_ref, hdo_ref,
                    lr_ref, li_ref, hr_ref, hi_ref, sp_ref, *, n):
    i = pl.program_id(0)
    tm = lr_ref.shape[0]
    wk = _freq_weight(i, tm, n)

    def two_pass(m, parts_ref):
        return (jnp.dot(m, parts_ref[0], preferred_element_type=F32)
                + jnp.dot(m, parts_ref[1], preferred_element_type=F32))
    a = two_pass(fe_ref[0, :tm], hse_ref)
    b = two_pass(fo_ref[0, :tm], hso_ref)
    c = two_pass(fe_ref[0, tm:], hde_ref)
    d = two_pass(fo_ref[0, tm:], hdo_ref)
    lr_ref[...] = wk * (a + b)
    hr_ref[...] = wk * (a - b)
    li_ref[...] = wk * (c + d)
    hi_ref[...] = wk * (d - c)

    @pl.when(i == 0)
    def _middle_bin():
        sr = two_pass(sg_ref[...], hse_ref)[0:1, :]
        si = -two_pass(sg_ref[...], hdo_ref)[0:1, :]
        row = lax.broadcasted_iota(jnp.int32, sp_ref.shape, 0)
        sp_ref[...] = jnp.where(row == 0, sr, jnp.where(row == 1, si, 0.0)) * (2.0 / n)


def hyena_half_filter_spectrum(fe, fo, sg, hs, hd, tm):
    H = fe.shape[2]
    W = hs.shape[2]
    planes = lambda h: (h.reshape(2, H, 2, W)[:, :, 0], h.reshape(2, H, 2, W)[:, :, 1])
    hse, hso = planes(hs)
    hde, hdo = planes(hd)
    whole = lambda shape: pl.BlockSpec(shape, lambda i: (0,) * len(shape))
    ftile = pl.BlockSpec((1, 2 * tm, H), lambda i: (i, 0, 0))
    otile = pl.BlockSpec((tm, W), lambda i: (i, 0))
    out = jax.ShapeDtypeStruct((H, W), F32)
    return pl.pallas_call(
        functools.partial(_half_filt_body, n=4 * H),
        grid=(H // tm,),
        in_specs=[ftile, ftile, whole((SUBLANES, H))] + [whole((2, H, W))] * 4,
        out_specs=[otile, otile, otile, otile, whole((SUBLANES, W))],
        out_shape=[out, out, out, out, jax.ShapeDtypeStruct((SUBLANES, W), F32)],
        compiler_params=_cparams("arbitrary"),
        name="hyena_filter_spectrum",
    )(fe, fo, sg, hse, hso, hde, hdo)


def _cmul(xr, xi, hr, hi):
    return xr * hr - xi * hi, xr * hi + xi * hr


def _half_fwd_body(fe_ref, fo_ref, sg_ref, ze_ref, zo_ref, lr_ref, li_ref, hr_ref, hi_ref, sp_ref,
                   per_ref, pei_ref, por_ref, poi_ref, ysp_ref):
    nb = ze_ref.shape[0]
    tm = lr_ref.shape[0]

    def body(b, carry):
        a = jnp.dot(fe_ref[0], ze_ref[b], preferred_element_type=F32)
        o = jnp.dot(fo_ref[0], zo_ref[b], preferred_element_type=F32)
        ylr, yli = _cmul(a[:tm] + o[:tm], a[tm:] + o[tm:], lr_ref[...], li_ref[...])
        yhr, yhi = _cmul(a[:tm] - o[:tm], o[tm:] - a[tm:], hr_ref[...], hi_ref[...])
        per_ref[b] = (ylr + yhr).astype(BF16)
        pei_ref[b] = (yli - yhi).astype(BF16)
        por_ref[b] = (ylr - yhr).astype(BF16)
        poi_ref[b] = (yli + yhi).astype(BF16)
        return carry
    lax.fori_loop(0, nb, body, 0)

    @pl.when(pl.program_id(1) == 0)
    def _middle_bin():
        def mid(b, carry):
            xr = jnp.dot(sg_ref[...], ze_ref[b], preferred_element_type=F32)[0:1, :]
            xi = -jnp.dot(sg_ref[...], zo_ref[b], preferred_element_type=F32)[0:1, :]
            yr, yi = _cmul(xr, xi, sp_ref[0:1, :], sp_ref[1:2, :])
            row = lax.broadcasted_iota(jnp.int32, ysp_ref.shape[1:], 0)
            ysp_ref[b] = jnp.where(row == 0, yr, jnp.where(row == 1, yi, 0.0))
            return carry
        lax.fori_loop(0, nb, mid, 0)


def hyena_half_fwd(fe, fo, sg, zb, zcol, spec, order, tm, bg):
    B, _, H, _ = zb.shape
    C = HY_CH
    lr, li, hr, hi, sp = spec
    ftile = pl.BlockSpec((1, 2 * tm, H), lambda g, i: (i, 0, 0))
    plane = lambda p: pl.BlockSpec((bg, None, H, C), lambda g, i: (g, p, 0, zcol))
    stile = pl.BlockSpec((tm, C), lambda g, i: (i, order))
    ptile = pl.BlockSpec((bg, tm, C), lambda g, i: (g, i, 0))
    pshape = jax.ShapeDtypeStruct((B, H, C), BF16)
    return pl.pallas_call(
        _half_fwd_body,
        grid=(B // bg, H // tm),
        in_specs=[ftile, ftile, pl.BlockSpec((SUBLANES, H), lambda g, i: (0, 0)), plane(0), plane(1),
                  stile, stile, stile, stile, pl.BlockSpec((SUBLANES, C), lambda g, i: (0, order))],
        out_specs=[ptile, ptile, ptile, ptile, pl.BlockSpec((bg, SUBLANES, C), lambda g, i: (g, 0, 0))],
        out_shape=[pshape, pshape, pshape, pshape, jax.ShapeDtypeStruct((B, SUBLANES, C), F32)],
        compiler_params=_cparams("parallel", "arbitrary"),
        name="hyena_dft_fwd",
    )(fe, fo, sg, zb, zb, lr, li, hr, hi, sp)


def _half_inv_body(fer_ref, fei_ref, fotr_ref, foti_ref, per_ref, pei_ref, por_ref, poi_ref, ysp_ref,
                   ze_ref, zo_ref, ge_ref, go_ref, d_ref, *out_refs, natural):
    nb = ze_ref.shape[0]
    tm = fer_ref.shape[0]
    u = pl.program_id(1) * tm + lax.broadcasted_iota(jnp.int32, (tm, 1), 0)
    sign = jnp.where((u & 1) == 0, 1.0, -1.0).astype(F32)
    d = d_ref[...]

    def body(b, carry):
        ce = (jnp.dot(fer_ref[...], per_ref[b], preferred_element_type=F32)
              + jnp.dot(fei_ref[...], pei_ref[b], preferred_element_type=F32) + sign * ysp_ref[b][0:1, :])
        co = (jnp.dot(fotr_ref[...], por_ref[b], preferred_element_type=F32)
              + jnp.dot(foti_ref[...], poi_ref[b], preferred_element_type=F32) - sign * ysp_ref[b][1:2, :])
        zne = ge_ref[b] * (ce + d * ze_ref[b])
        zno = go_ref[b] * (co + d * zo_ref[b])
        if natural:
            bi = jnp.asarray(b, jnp.int32)
            for c, o_ref in enumerate(out_refs):
                o_ref.at[bi][pl.ds(0, tm, stride=2), :] = zne[:, c * LANES:(c + 1) * LANES]
                o_ref.at[bi][pl.ds(1, tm, stride=2), :] = zno[:, c * LANES:(c + 1) * LANES]
        else:
            z_ref, zb_ref = out_refs
            z_ref[b, 0] = zne
            z_ref[b, 1] = zno
            zb_ref[b, 0] = zne.astype(BF16)
            zb_ref[b, 1] = zno.astype(BF16)
        return carry
    lax.fori_loop(0, nb, body, 0)


def hyena_half_inv(mats, ps, ysp, z, zcol, gate, gcol, d, tm, bg, natural=False):
    fer, fei, fotr, foti = mats
    per = ps[0]
    B, H, C = per.shape
    ftile = pl.BlockSpec((tm, H), lambda g, i: (i, 0))
    whole_p = lambda: pl.BlockSpec((bg, H, C), lambda g, i: (g, 0, 0), pipeline_mode=pl.Buffered(1))
    tile = lambda p, col: pl.BlockSpec((bg, None, tm, C), lambda g, i: (g, p, i, col))
    if natural:
        out_specs = [pl.BlockSpec((bg, 2 * tm, LANES), lambda g, i: (g, i, 0))] * (C // LANES)
        out_shape = [jax.ShapeDtypeStruct((B, 2 * H, LANES), F32)] * (C // LANES)
    else:
        out_specs = [pl.BlockSpec((bg, 2, tm, C), lambda g, i: (g, 0, i, 0))] * 2
        out_shape = [jax.ShapeDtypeStruct((B, 2, H, C), F32), jax.ShapeDtypeStruct((B, 2, H, C), BF16)]
    return pl.pallas_call(
        functools.partial(_half_inv_body, natural=natural),
        grid=(B // bg, H // tm),
        in_specs=[ftile, ftile, ftile, ftile, whole_p(), whole_p(), whole_p(), whole_p(),
                  pl.BlockSpec((bg, SUBLANES, C), lambda g, i: (g, 0, 0)),
                  tile(0, zcol), tile(1, zcol), tile(0, gcol), tile(1, gcol),
                  pl.BlockSpec((1, C), lambda g, i: (0, 0))],
        out_specs=out_specs,
        out_shape=out_shape,
        compiler_params=_cparams("parallel", "arbitrary"),
        name="hyena_dft_inv",
    )(fer, fei, fotr, foti, *ps, ysp, z, z, gate, gate, d)


def dft_half_matrices(L, tm):
    H = L // 2
    r = 1 << (max(H.bit_length() - 1, 0) // 2)
    u = jnp.arange(H, dtype=jnp.int32)

    hi = jnp.arange(H // r, dtype=jnp.int32) * r
    lo = jnp.arange(r, dtype=jnp.int32)

    def cos_msin(row_hi, row_lo, col):
        def table(k):
            ang = ((k[:, None] * col[None, :]) % (2 * L)).astype(F32) * (math.pi / L)
            return jnp.cos(ang), jnp.sin(ang)
        ca, sa = table(row_hi)
        cb, sb = table(row_lo)
        c = ca[:, None, :] * cb[None, :, :] - sa[:, None, :] * sb[None, :, :]
        s = sa[:, None, :] * cb[None, :, :] + ca[:, None, :] * sb[None, :, :]
        return c.reshape(H, H).astype(BF16), (-s).reshape(H, H).astype(BF16)
    fre, fie = cos_msin(hi, lo, 2 * u)
    fro, fio = cos_msin(hi, lo, 2 * u + 1)
    frot, fiot = cos_msin(2 * hi, 2 * lo + 1, u)
    stack = lambda a, b: jnp.concatenate([a.reshape(H // tm, tm, H), b.reshape(H // tm, tm, H)], axis=1)
    sg = jnp.zeros((SUBLANES, H), F32).at[0].set(jnp.where(u % 2 == 0, 1.0, -1.0)).astype(BF16)
    return dict(fe=stack(fre, fie), fo=stack(fro, fio), sg=sg, inv=(fre, fie, frot, fiot), tm=tm)


def hyena_half(hy, lp, mats):
    B, L, _ = hy.shape
    tm = mats['tm']
    bg = B if L < TM_DFT else BG_HALF
    tm_inv = min(tm, TM_DFT_INV)
    hf = hyena_time_filters(L, lp)
    h_fwd = hf[:, 0].reshape(L, HY_ORDER * HY_CH)
    h_bwd = hf[:, 1].at[0].set(0.0).reshape(L, HY_ORDER * HY_CH)
    spec = hyena_half_filter_spectrum(mats['fe'], mats['fo'], mats['sg'], _hi_lo(h_fwd + h_bwd),
                                      _hi_lo(h_fwd - h_bwd), tm)
    w = jnp.pad(lp['hy_conv_w'], ((0, SUBLANES - 3), (0, 0)))
    u, ub = hyena_conv3_planes(hy, w, lp['hy_conv_b'][None, :])
    d = lp['hy_d'].astype(F32)
    z, zb, zcol = u, ub, 0
    for o in range(HY_ORDER):
        *ps, ysp = hyena_half_fwd(mats['fe'], mats['fo'], mats['sg'], zb, zcol, spec, o, tm, bg)
        last = o == HY_ORDER - 1
        out = hyena_half_inv(mats['inv'], ps, ysp, z, zcol, u, 1 + o, d[o][None, :], tm_inv, bg, natural=last)
        if last:
            return out
        z, zb = out
        zcol = 0


def _ffn_body(x_ref, mod_ref, g_ref, w1_ref, w3_ref, w2_ref, o_ref):
    x = x_ref[...]
    h = _adaln_rows(x, g_ref[...], mod_ref[0, ROW_SHIFT2:ROW_SHIFT2 + 1, :],
                    mod_ref[0, ROW_SCALE2:ROW_SCALE2 + 1, :]).astype(BF16)
    a = jnp.dot(h, w1_ref[...], preferred_element_type=F32)
    b = jnp.dot(h, w3_ref[...], preferred_element_type=F32)
    act = (_silu(a) * b).astype(BF16)
    ff = jnp.dot(act, w2_ref[...], preferred_element_type=F32)
    o_ref[...] = x + mod_ref[0, ROW_GATE2:ROW_GATE2 + 1, :] * ff


def ffn_dense(x, mods, g, w1, w3, w2, seg_len, tm=TM_PROJ):
    T, D = x.shape
    F = w1.shape[1]
    resident = functools.partial(pl.BlockSpec, pipeline_mode=pl.Buffered(1))
    return pl.pallas_call(
        _ffn_body,
        grid=(T // tm,),
        in_specs=[
            pl.BlockSpec((tm, D), lambda i: (i, 0)),
            pl.BlockSpec((1, MOD_ROWS, D), lambda i: (i * tm // seg_len, 0, 0)),
            pl.BlockSpec((1, D), lambda i: (0, 0)),
            resident((D, F), lambda i: (0, 0)),
            resident((D, F), lambda i: (0, 0)),
            resident((F, D), lambda i: (0, 0)),
        ],
        out_specs=pl.BlockSpec((tm, D), lambda i: (i, 0)),
        out_shape=jax.ShapeDtypeStruct((T, D), F32),
        compiler_params=_cparams("parallel"),
        name="ffn_dense",
    )(x, mods, g, w1, w3, w2)


def _store_token_tiles(ref, x):
    n = x.shape[0]
    for s in range(SUBLANES):
        ref[pl.ds(s, n, stride=SUBLANES), :] = x[:, s * LANES:(s + 1) * LANES]


def _load_token_tiles(ref, n):
    return jnp.concatenate([ref[pl.ds(s, n, stride=SUBLANES), :] for s in range(SUBLANES)], axis=1)


def _router_body(xa_ref, xb_ref, mod_ref, g_ref, wr_ref, h_ref, idx_ref, gw_ref, *, na):
    x = jnp.where(pl.program_id(0) < na, xa_ref[...], xb_ref[...])
    h = _adaln_rows(x, g_ref[...], mod_ref[0, ROW_SHIFT2:ROW_SHIFT2 + 1, :],
                    mod_ref[0, ROW_SCALE2:ROW_SCALE2 + 1, :])
    _store_token_tiles(h_ref, h)
    logits = lax.dot_general(wr_ref[...], h, (((1,), (1,)), ((), ())),
                             precision=lax.Precision.HIGHEST, preferred_element_type=F32)
    eidx = lax.broadcasted_iota(jnp.int32, logits.shape, 0)
    m1 = jnp.max(logits, axis=0, keepdims=True)
    i1 = jnp.min(jnp.where(logits == m1, eidx, N_EXPERTS), axis=0, keepdims=True)
    rest = jnp.where(eidx == i1, -jnp.inf, logits)
    m2 = jnp.max(rest, axis=0, keepdims=True)
    i2 = jnp.min(jnp.where(rest == m2, eidx, N_EXPERTS), axis=0, keepdims=True)
    e2 = jnp.exp(m2 - m1)
    den = 1.0 + e2
    row = lax.broadcasted_iota(jnp.int32, logits.shape, 0)
    idx_ref[...] = jnp.where(row == 0, i1, jnp.where(row == 1, i2, 0))
    gw_ref[...] = jnp.transpose(jnp.where(row == 0, 1.0 / den, jnp.where(row == 1, e2 / den, 0.0)))


def _merged_seg(i, na, tm, seg_len_b):
    return jnp.where(i < na, 0, 1 + jnp.maximum(i - na, 0) * tm // seg_len_b)


def moe_router(xa, xb, mods, g, wr_t, seg_len_b, tm=TM_PROJ):
    D = xa.shape[1]
    na = xa.shape[0] // tm
    T = xa.shape[0] + xb.shape[0]
    return pl.pallas_call(
        functools.partial(_router_body, na=na),
        grid=(T // tm,),
        in_specs=[
            pl.BlockSpec((tm, D), lambda i: (jnp.minimum(i, na - 1), 0)),
            pl.BlockSpec((tm, D), lambda i: (jnp.maximum(i - na, 0), 0)),
            pl.BlockSpec((1, MOD_ROWS, D), lambda i: (_merged_seg(i, na, tm, seg_len_b), 0, 0)),
            pl.BlockSpec((1, D), lambda i: (0, 0)),
            pl.BlockSpec((N_EXPERTS, D), lambda i: (0, 0)),
        ],
        out_specs=[
            pl.BlockSpec((tm * SUBLANES, LANES), lambda i: (i, 0)),
            pl.BlockSpec((N_EXPERTS, tm), lambda i: (0, i)),
            pl.BlockSpec((tm, N_EXPERTS), lambda i: (i, 0)),
        ],
        out_shape=[
            jax.ShapeDtypeStruct((T * SUBLANES, LANES), F32),
            jax.ShapeDtypeStruct((N_EXPERTS, T), jnp.int32),
            jax.ShapeDtypeStruct((T, N_EXPERTS), F32),
        ],
        compiler_params=_cparams("parallel"),
        name="moe_router",
    )(xa, xb, mods, g, wr_t)


def _experts_body(te_ref, nv_ref, src0_ref, srcn_ref, dst_ref, h_hbm, w1_ref, w3_ref, w2_ref,
                  y_hbm, hrows, hb, acc, ybuf, sem_g, sem_s, *, tm, nf, n_slots):
    i = pl.program_id(0)
    f = pl.program_id(1)
    nv = nv_ref[0]
    valid = i < nv
    slot = lax.rem(i, 2)

    def gather_issue(idx_ref, s):
        def body(r, carry):
            src = pl.multiple_of(idx_ref[0, 0, r] * SUBLANES, SUBLANES)
            dst = pl.multiple_of(r * SUBLANES, SUBLANES)
            pltpu.make_async_copy(h_hbm.at[pl.ds(src, SUBLANES)], hrows.at[s, pl.ds(dst, SUBLANES)],
                                  sem_g.at[s]).start()
            return carry
        lax.fori_loop(0, tm, body, 0, unroll=8)

    def gather_wait(s):
        pltpu.make_async_copy(h_hbm.at[pl.ds(0, tm * SUBLANES)], hrows.at[s], sem_g.at[s]).wait()

    def scatter_wait(s):
        pltpu.make_async_copy(ybuf.at[s], y_hbm.at[pl.ds(0, tm * SUBLANES)], sem_s.at[s]).wait()

    @pl.when(jnp.logical_and(i == 0, f == 0))
    def _first():
        ybuf[0] = jnp.zeros(ybuf.shape[1:], ybuf.dtype)
        spare = lambda p: y_hbm.at[pl.ds((n_slots + p * tm) * SUBLANES, tm * SUBLANES)]
        for p in range(2):
            pltpu.make_async_copy(ybuf.at[0], spare(p), sem_s.at[p]).start()
        for p in range(2):
            pltpu.make_async_copy(ybuf.at[0], spare(p), sem_s.at[p]).wait()
        gather_issue(src0_ref, 0)

    @pl.when(jnp.logical_and(valid, f == 0))
    def _stage():
        gather_wait(slot)
        hb[...] = _load_token_tiles(hrows.at[slot], tm).astype(BF16)

        @pl.when(i + 1 < nv)
        def _prefetch():
            gather_issue(srcn_ref, 1 - slot)

    def partial_ffn():
        h = hb[...]
        a = jnp.dot(h, w1_ref[0], preferred_element_type=F32)
        b = jnp.dot(h, w3_ref[0], preferred_element_type=F32)
        act = (_silu(a) * b).astype(BF16)
        return jnp.dot(act, w2_ref[0], preferred_element_type=F32)

    @pl.when(jnp.logical_and(valid, f == 0))
    def _first_slice():
        acc[...] = partial_ffn()

    @pl.when(jnp.logical_and(valid, jnp.logical_and(f > 0, f < nf - 1)))
    def _middle_slice():
        acc[...] += partial_ffn()

    @pl.when(jnp.logical_and(valid, f == nf - 1))
    def _emit():
        y = acc[...] + partial_ffn()

        @pl.when(i >= 2)
        def _reuse():
            scatter_wait(slot)
        _store_token_tiles(ybuf.at[slot], y)

        def body(r, carry):
            src = pl.multiple_of(r * SUBLANES, SUBLANES)
            dst = pl.multiple_of(dst_ref[0, 0, r] * SUBLANES, SUBLANES)
            pltpu.make_async_copy(ybuf.at[slot, pl.ds(src, SUBLANES)], y_hbm.at[pl.ds(dst, SUBLANES)],
                                  sem_s.at[slot]).start()
            return carry
        lax.fori_loop(0, tm, body, 0, unroll=8)

    @pl.when(jnp.logical_and(i == nv, f == 0))
    def _drain():
        scatter_wait(lax.rem(nv + 1, 2))

        @pl.when(nv >= 2)
        def _older():
            scatter_wait(lax.rem(nv, 2))


def moe_experts(h, tile_expert, n_valid, src_rows, dst_rows, w1, w3, w2, tm=TM_MOE, tf=TF_MOE):
    T = h.shape[0] // SUBLANES
    D = w1.shape[1]
    F = w1.shape[2]
    n_tiles = src_rows.shape[0]
    n_slots = TOP_K * T
    nf = F // tf

    def wcol(i, f, te, nv):
        return (te[i], 0, jnp.where(i < nv[0], f, nf - 1))

    def wrow(i, f, te, nv):
        return (te[i], jnp.where(i < nv[0], f, nf - 1), 0)

    smem_tile = lambda fn: pl.BlockSpec((1, 1, tm), fn, memory_space=pltpu.SMEM)
    grid_spec = pltpu.PrefetchScalarGridSpec(
        num_scalar_prefetch=2,
        grid=(n_tiles, nf),
        in_specs=[
            smem_tile(lambda i, f, te, nv: (0, 0, 0)),
            smem_tile(lambda i, f, te, nv: (jnp.minimum(i + 1, n_tiles - 1), 0, 0)),
            smem_tile(lambda i, f, te, nv: (i, 0, 0)),
            pl.BlockSpec(memory_space=pl.ANY),
            pl.BlockSpec((1, D, tf), wcol),
            pl.BlockSpec((1, D, tf), wcol),
            pl.BlockSpec((1, tf, D), wrow),
        ],
        out_specs=pl.BlockSpec(memory_space=pl.ANY),
        scratch_shapes=[
            pltpu.VMEM((2, tm * SUBLANES, LANES), F32),
            pltpu.VMEM((tm, D), BF16),
            pltpu.VMEM((tm, D), F32),
            pltpu.VMEM((2, tm * SUBLANES, LANES), F32),
            pltpu.SemaphoreType.DMA((2,)),
            pltpu.SemaphoreType.DMA((2,)),
        ],
    )
    return pl.pallas_call(
        functools.partial(_experts_body, tm=tm, nf=nf, n_slots=n_slots),
        grid_spec=grid_spec,
        out_shape=jax.ShapeDtypeStruct(((n_slots + 2 * tm) * SUBLANES, LANES), F32),
        compiler_params=_cparams("arbitrary", "arbitrary"),
        name="moe_experts",
    )(tile_expert, n_valid, src_rows, src_rows, dst_rows, h, w1, w3, w2)


def _combine_body(x_ref, mod_ref, gw_ref, y0_ref, y1_ref, o_ref):
    n = x_ref.shape[0]
    gw = gw_ref[...]
    y = gw[:, 0:1] * _load_token_tiles(y0_ref, n) + gw[:, 1:2] * _load_token_tiles(y1_ref, n)
    o_ref[...] = x_ref[...] + mod_ref[0, ROW_GATE2:ROW_GATE2 + 1, :] * y


def moe_combine(x, mods, gw, y, row0, n_tok, seg_len, tm=TM_COMB):
    T, D = x.shape
    off = row0 // tm
    return pl.pallas_call(
        _combine_body,
        grid=(T // tm,),
        in_specs=[
            pl.BlockSpec((tm, D), lambda i: (i, 0)),
            pl.BlockSpec((1, MOD_ROWS, D), lambda i: (i * tm // seg_len, 0, 0)),
            pl.BlockSpec((tm, N_EXPERTS), lambda i: (off + i, 0)),
            pl.BlockSpec((tm * SUBLANES, LANES), lambda i: (off + i, 0)),
            pl.BlockSpec((tm * SUBLANES, LANES), lambda i: (off + n_tok // tm + i, 0)),
        ],
        out_specs=pl.BlockSpec((tm, D), lambda i: (i, 0)),
        out_shape=jax.ShapeDtypeStruct((T, D), F32),
        compiler_params=_cparams("parallel"),
        name="moe_combine",
    )(x, mods, gw, y, y)


def moe_dispatch_plan(idx, tm=TM_MOE):
    T = idx.shape[1]
    n_slots = TOP_K * T
    n_tiles = n_slots // tm + N_EXPERTS
    n_rows = n_tiles * tm
    experts = jnp.arange(N_EXPERTS, dtype=jnp.int32)[None, :]
    e_flat = idx[:TOP_K].reshape(n_slots)
    counts = jnp.sum((e_flat[:, None] == experts).astype(jnp.int32), axis=0)
    padded = (counts + tm - 1) // tm * tm
    ends = jnp.cumsum(padded)
    offs = ends - padded
    slots = jnp.arange(n_slots, dtype=jnp.int32)
    order = jnp.sort(e_flat * n_slots + slots) % n_slots
    cstart = jnp.cumsum(counts) - counts
    rows = jnp.arange(n_rows, dtype=jnp.int32)
    row_e = jnp.minimum(jnp.sum((rows[:, None] >= ends[None, :]).astype(jnp.int32), axis=1), N_EXPERTS - 1)
    pick = lambda table: jnp.sum(jnp.where(row_e[:, None] == experts, table[None, :], 0), axis=1)
    j = rows - pick(offs)
    live = j < pick(counts)
    slot = order[jnp.clip(pick(cstart) + j, 0, n_slots - 1)]
    spare = n_slots + (rows // tm) % 2 * tm + rows % tm
    src_rows = jnp.where(live, slot % T, 0).astype(jnp.int32).reshape(n_tiles, 1, tm)
    dst_rows = jnp.where(live, slot, spare).astype(jnp.int32).reshape(n_tiles, 1, tm)
    tile_start = jnp.arange(n_tiles, dtype=jnp.int32) * tm
    n_valid = (ends[-1] // tm).astype(jnp.int32).reshape(1)
    tile_expert = jnp.minimum(jnp.sum((tile_start[:, None] >= ends[None, :]).astype(jnp.int32), axis=1),
                              N_EXPERTS - 1)
    last_e = tile_expert[jnp.maximum(n_valid[0] - 1, 0)]
    tile_expert = jnp.where(tile_start < ends[-1], tile_expert, last_e).astype(jnp.int32)
    return tile_expert, n_valid, src_rows, dst_rows


def ffn_moe(xa, xb, mods_a, mods_b, g, wr_t, w1, w3, w2, seg_len_b, tm=TM_PROJ, tme=TM_MOE, tmc=TM_COMB,
            tf=TF_MOE):
    na, nb = xa.shape[0], xb.shape[0]
    mods = jnp.concatenate([mods_a, mods_b], axis=0)
    h, idx, gw = moe_router(xa, xb, mods, g, wr_t, seg_len_b, tm)
    tile_expert, n_valid, src_rows, dst_rows = moe_dispatch_plan(idx, tme)
    y = moe_experts(h, tile_expert, n_valid, src_rows, dst_rows, w1, w3, w2, tme, tf)
    return (moe_combine(xa, mods_a, gw, y, 0, na + nb, na, tmc),
            moe_combine(xb, mods_b, gw, y, na, na + nb, seg_len_b, tmc))


def axial_rope(L):
    rows = L // GRID_W
    r = jnp.repeat(jnp.arange(rows, dtype=F32), GRID_W)
    col = jnp.tile(jnp.arange(GRID_W, dtype=F32), rows)
    n = HEAD_DIM // 4
    freqs = ROPE_BASE ** (-jnp.arange(n, dtype=F32) / n)
    ang = jnp.concatenate([r[:, None] * freqs, col[:, None] * freqs], axis=-1)
    return jnp.cos(ang), jnp.sin(ang)


def rope_tables(L):
    cos, sin = axial_rope(L)
    n = HEAD_DIM // 4
    cos_h = jnp.concatenate([cos[:, :n], cos[:, :n], cos[:, n:], cos[:, n:]], axis=1)
    sin_h = jnp.concatenate([-sin[:, :n], sin[:, :n], -sin[:, n:], sin[:, n:]], axis=1)
    reps = LANES // HEAD_DIM
    return jnp.tile(cos_h, (1, reps)), jnp.tile(sin_h, (1, reps))


def block_diag_ones(width, block):
    i = jnp.arange(width) // block
    return (i[:, None] == i[None, :]).astype(BF16)


def gla_gate_params(gw, gb):
    w = jnp.zeros((LANES, 2 * C_WIDTH), F32)
    w = w.at[:GLA_RANK, :C_WIDTH].set(gw[0]).at[GLA_RANK:2 * GLA_RANK, C_WIDTH:].set(gw[1])
    return w.astype(BF16), jnp.concatenate([gb[0], gb[1]])[None, :]


def gla_state_to_blockdiag(st):
    B = st.shape[0]
    out = jnp.zeros((B, 2, C_HEADS, C_DV, C_HEADS, C_DK), F32)
    for h in range(C_HEADS):
        out = out.at[:, :, h, :, h, :].set(jnp.swapaxes(st[:, :, h], -1, -2))
    return out.reshape(B, 2, C_WIDTH, C_WIDTH)


def gla_state_from_blockdiag(sT):
    B = sT.shape[0]
    s6 = sT.reshape(B, 2, C_HEADS, C_DV, C_HEADS, C_DK)
    return jnp.stack([jnp.swapaxes(s6[:, :, h, :, h, :], -1, -2) for h in range(C_HEADS)], axis=2)


def mod_table(cvec, w_ada, b_ada):
    m = jax.nn.silu(cvec) @ w_ada + b_ada
    m = m.reshape(cvec.shape[0], 6, D_MODEL)
    return jnp.pad(m, ((0, 0), (0, MOD_ROWS - 6), (0, 0)))


def kernel(x_prompt, x_sample, cache_k, cache_v, state_gla, c, c_ctx, norm1_g, norm2_g, w_ada, b_ada,
           w_in, w_out, q_norm_g, k_norm_g, attn_sink, hy_conv_w, hy_conv_b, hy_w1, hy_b1, hy_freq1,
           hy_w2, hy_b2, hy_freq2, hy_w3, hy_decay, hy_d, gla_gate_w, gla_gate_b, gla_norm_g,
           ffn_w1, ffn_w3, ffn_w2, moe_router, moe_w1, moe_w3, moe_w2):
    D = D_MODEL
    xp = x_prompt.reshape(BATCH * SEQ, D)
    xs = x_sample.reshape(DEC_BATCH * DEC_SEQ, D)
    streams = [
        dict(x=xp, B=BATCH, L=SEQ, seg_len=BATCH * SEQ, cvec=c_ctx[None, :], latent=False),
        dict(x=xs, B=DEC_BATCH, L=DEC_SEQ, seg_len=DEC_SEQ, cvec=c, latent=True),
    ]
    ks_list, vs_list, st_list = [], [], []
    bd_q = block_diag_ones(A_WIDTH, HEAD_DIM)
    dft = {s['L']: dft_half_matrices(s['L'], min(TM_DFT, s['L'] // 2)) for s in streams}
    for l in range(DEPTH):
        lp = {
            'hy_conv_w': hy_conv_w[l], 'hy_conv_b': hy_conv_b[l],
            'hy_w1': hy_w1[l], 'hy_b1': hy_b1[l], 'hy_freq1': hy_freq1[l], 'hy_w2': hy_w2[l],
            'hy_b2': hy_b2[l], 'hy_freq2': hy_freq2[l], 'hy_w3': hy_w3[l], 'hy_decay': hy_decay[l],
            'hy_d': hy_d[l],
        }
        w_in_l = jnp.pad(w_in[l], ((0, 0), (0, D_PROJ_PAD - D_PROJ))).astype(BF16)
        w_out_l = w_out[l].astype(BF16)
        g1 = norm1_g[l][None, :]
        g2 = norm2_g[l][None, :]
        qg = jnp.tile(q_norm_g[l], A_HEADS)[None, :]
        kg = jnp.tile(k_norm_g[l], A_KV_HEADS)[None, :]
        gg = jnp.tile(gla_norm_g[l], C_HEADS)[None, :]
        gate_w, gate_b = gla_gate_params(gla_gate_w[l], gla_gate_b[l])
        j = l // 2
        for s in streams:
            B, L = s['B'], s['L']
            mods = mod_table(s['cvec'], w_ada[l], b_ada[l])
            rope = rope_tables(L) if s['latent'] else None
            q, k, v, hy, cq, ck, cv, cg, la = in_proj(s['x'], mods, g1, w_in_l, bd_q, qg, kg, gate_w, gate_b,
                                                      s['seg_len'], rope, L)
            seq = lambda t: t.reshape(B, L, t.shape[-1])
            if s['latent']:
                kc = cache_k[:, l].reshape(DEC_BATCH, PAST_LEN, LANES)
                vc = cache_v[:, l].reshape(DEC_BATCH, PAST_LEN, LANES)
                a_out = attention(seq(q), seq(k), seq(v), attn_sink[l], kc, vc)
                s0 = gla_state_to_blockdiag(state_gla[:, l])
                o_f, o_b, _ = gla(seq(cq), seq(ck), seq(cv), seq(la), s0)
            else:
                a_out = attention(seq(q), seq(k), seq(v), attn_sink[l])
                o_f, o_b, sT = gla(seq(cq), seq(ck), seq(cv), seq(la))
                ks_list.append(k.reshape(B, L, A_KV_HEADS, HEAD_DIM))
                vs_list.append(v.reshape(B, L, A_KV_HEADS, HEAD_DIM))
                st_list.append(gla_state_from_blockdiag(sT))
            h_out = hyena_half(seq(hy), lp, dft[L])
            flat = lambda t: t.reshape(B * L, t.shape[-1])
            x1 = out_proj(s['x'], flat(a_out), [flat(h) for h in h_out], flat(o_f), flat(o_b), cg, mods, w_out_l,
                          bd_q[:C_WIDTH, :C_WIDTH], gg, s['seg_len'])
            if l % 2 == 0:
                s['x'] = ffn_dense(x1, mods, g2, ffn_w1[j].astype(BF16), ffn_w3[j].astype(BF16),
                                   ffn_w2[j].astype(BF16), s['seg_len'])
            else:
                s['x'], s['mods'] = x1, mods
        if l % 2 == 1:
            sa, sb = streams
            sa['x'], sb['x'] = ffn_moe(sa['x'], sb['x'], sa['mods'], sb['mods'], g2, moe_router[j].T,
                                       moe_w1[j].astype(BF16), moe_w3[j].astype(BF16), moe_w2[j].astype(BF16),
                                       sb['seg_len'])
    y_prompt = streams[0]['x'].reshape(BATCH, SEQ, D)
    y_sample = streams[1]['x'].reshape(DEC_BATCH, DEC_SEQ, D)
    new_cache_k = jnp.stack(ks_list, axis=1)
    new_cache_v = jnp.stack(vs_list, axis=1)
    new_state_gla = jnp.stack(st_list, axis=1)
    return (y_prompt, y_sample, new_cache_k, new_cache_v, new_state_gla)
```
